```python
import math
import jax, jax.numpy as jnp
from jax import lax
import numpy as np

D_MODEL = 4096
BATCH = 4
SEQ = 4096
DEPTH = 1

DA_HEADS = 16
DA_HEAD_DIM = 128
DA_WIDTH = DA_HEADS * 2 * DA_HEAD_DIM
Q_BLOCK = 128
REL_BUCKETS = 32
REL_MAX_DIST = 128
ML_HEADS = 8
ML_HEAD_DIM = 256
ML_WIDTH = ML_HEADS * ML_HEAD_DIM
ML_CHUNK = 128
CONV_WIDTH = 4
N_GROUPS = 8
EXPERTS_PER_GROUP = 8
N_EXPERTS = N_GROUPS * EXPERTS_PER_GROUP
TOP_K = 2
D_EXPERT = 512
MOE_BLOCK = 128
EPS = 1e-6
IN_SIZES = (DA_WIDTH, DA_WIDTH, DA_WIDTH,
            ML_WIDTH, ML_WIDTH, ML_WIDTH, ML_WIDTH,
            ML_HEADS, ML_HEADS,
            D_MODEL, D_MODEL)
IN_WIDTH = 3 * DA_WIDTH + 4 * ML_WIDTH + 2 * ML_HEADS + 2 * D_MODEL

kernel_name = "hybrid_diffattn_mlstm_hmoe_block"


def rmsnorm(x, g):
    xf = x.astype(jnp.float32)
    y = xf * lax.rsqrt(jnp.mean(xf * xf, axis=-1, keepdims=True) + EPS)
    return (y * g.astype(jnp.float32)).astype(x.dtype)


def rel_bucket(n):
    max_exact = REL_BUCKETS // 2
    nf = jnp.maximum(n, 1).astype(jnp.float32)
    large = max_exact + (jnp.log(nf / max_exact) / math.log(REL_MAX_DIST / max_exact)
                         * (REL_BUCKETS - max_exact)).astype(jnp.int32)
    large = jnp.minimum(large, REL_BUCKETS - 1)
    return jnp.where(n < max_exact, n, large)


def diff_attention(q, k, v, lam, rel_bias, norm_g, lambda_init):
    B, S = q.shape[0], q.shape[1]
    nb = S // Q_BLOCK
    scale = DA_HEAD_DIM ** -0.5
    qb = q.reshape(B, nb, Q_BLOCK, DA_HEADS, 2, DA_HEAD_DIM).swapaxes(0, 1)
    k_pos = jnp.arange(S)

    def block(args):
        qi, bi = args
        q_pos = bi * Q_BLOCK + jnp.arange(Q_BLOCK)
        dist = q_pos[:, None] - k_pos[None, :]
        s = jnp.einsum('bqhmd,bkhmd->bhmqk', qi, k).astype(jnp.float32) * scale
        bias = rel_bias.astype(jnp.float32)[rel_bucket(jnp.maximum(dist, 0))]
        s = s + jnp.transpose(bias, (2, 0, 1))[None, :, None]
        s = jnp.where((dist >= 0)[None, None, None], s, -jnp.inf)
        p = jax.nn.softmax(s, axis=-1)
        a = p[:, :, 0] - lam * p[:, :, 1]
        return jnp.einsum('bhqk,bkhe->bqhe', a.astype(v.dtype), v)

    out = lax.map(block, (qb, jnp.arange(nb)))
    out = out.swapaxes(0, 1).reshape(B, S, DA_HEADS, 2 * DA_HEAD_DIM)
    out = rmsnorm(out, norm_g) * (1.0 - lambda_init)
    return out.reshape(B, S, DA_WIDTH)


def causal_conv(u, w):
    C = u.shape[-1]
    return lax.conv_general_dilated(u, w[:, None, :].astype(u.dtype), window_strides=(1,),
                                    padding=[(CONV_WIDTH - 1, 0)],
                                    dimension_numbers=('NWC', 'WIO', 'NWC'),
                                    feature_group_count=C)


def mlstm(q, k, v, i_pre, f_pre):
    B, S, H, d = q.shape
    L = ML_CHUNK
    nc = S // L
    f32 = jnp.float32
    q = q.astype(f32)
    k = k.astype(f32) * (d ** -0.5)
    v = v.astype(f32)
    log_i = i_pre.astype(f32)
    log_f = jax.nn.log_sigmoid(f_pre.astype(f32))

    def vec_chunks(t):
        return jnp.transpose(t.reshape(B, nc, L, H, d), (1, 0, 3, 2, 4))

    def gate_chunks(t):
        return jnp.transpose(t.reshape(B, nc, L, H), (1, 0, 3, 2))

    causal = jnp.tril(jnp.ones((L, L), bool))

    def step(carry, inp):
        C, n, m = carry
        qc, kc, vc, ic, fc = inp
        b = jnp.cumsum(fc, axis=-1)
        Dm = b[..., :, None] - b[..., None, :] + ic[..., None, :]
        Dm = jnp.where(causal, Dm, -jnp.inf)
        inter = b + m[..., None]
        m_row = jnp.maximum(inter, jnp.max(Dm, axis=-1))
        W = jnp.exp(Dm - m_row[..., None])
        a_inter = jnp.exp(inter - m_row)
        s_qk = jnp.einsum('bhjd,bhsd->bhjs', qc, kc) * W
        num = (a_inter[..., None] * jnp.einsum('bhjd,bhed->bhje', qc, C)
               + jnp.einsum('bhjs,bhse->bhje', s_qk, vc))
        den = a_inter * jnp.einsum('bhjd,bhd->bhj', qc, n) + jnp.sum(s_qk, axis=-1)
        h = num / jnp.maximum(jnp.abs(den), jnp.exp(-m_row))[..., None]
        m_new = m_row[..., -1]
        w_s = jnp.exp(b[..., -1:] - b + ic - m_new[..., None])
        a_state = jnp.exp(b[..., -1] + m - m_new)
        C_new = a_state[..., None, None] * C + jnp.einsum('bhs,bhse,bhsd->bhed', w_s, vc, kc)
        n_new = a_state[..., None] * n + jnp.einsum('bhs,bhsd->bhd', w_s, kc)
        return (C_new, n_new, m_new), h

    init = (jnp.zeros((B, H, d, d), f32), jnp.zeros((B, H, d), f32), jnp.zeros((B, H), f32))
    _, hs = lax.scan(step, init, (vec_chunks(q), vec_chunks(k), vec_chunks(v),
                                  gate_chunks(log_i), gate_chunks(log_f)))
    return jnp.transpose(hs, (1, 0, 3, 2, 4)).reshape(B, S, H, d)


def hier_moe(h, w_group, b_group, w_expert, b_expert, w_gate_up, w_down):
    B, S, D = h.shape
    T = B * S
    ht = h.reshape(T, D)
    g_prob = jax.nn.softmax((ht @ w_group).astype(jnp.float32) + b_group.astype(jnp.float32), axis=-1)
    g_p, g_idx = lax.top_k(g_prob, 1)
    e_logits = ((ht @ w_expert).astype(jnp.float32) + b_expert.astype(jnp.float32)
                ).reshape(T, N_GROUPS, EXPERTS_PER_GROUP)
    e_logits = jnp.take_along_axis(e_logits, g_idx[:, :, None], axis=1)[:, 0]
    e_p, e_idx = lax.top_k(jax.nn.softmax(e_logits, axis=-1), TOP_K)
    gate = g_p * e_p / jnp.sum(e_p, axis=-1, keepdims=True)
    expert_id = (g_idx * EXPERTS_PER_GROUP + e_idx).reshape(-1)
    flat_w = gate.reshape(-1)
    flat_tok = jnp.repeat(jnp.arange(T, dtype=jnp.int32), TOP_K)
    A = T * TOP_K
    order = jnp.argsort(expert_id)
    se = expert_id[order]
    counts = jnp.bincount(expert_id, length=N_EXPERTS)
    starts = jnp.cumsum(counts) - counts
    padded = (counts + MOE_BLOCK - 1) // MOE_BLOCK * MOE_BLOCK
    pend = jnp.cumsum(padded)
    pstart = pend - padded
    dest = pstart[se] + jnp.arange(A) - starts[se]
    n_blocks = -(-A // MOE_BLOCK) + N_EXPERTS
    n_slots = n_blocks * MOE_BLOCK
    slot_tok = jnp.full((n_slots,), T, jnp.int32).at[dest].set(flat_tok[order])
    slot_w = jnp.zeros((n_slots,), jnp.float32).at[dest].set(flat_w[order])
    block_e = jnp.minimum(jnp.searchsorted(pend, jnp.arange(n_blocks) * MOE_BLOCK, side='right'),
                          N_EXPERTS - 1)
    x_pad = jnp.concatenate([ht, jnp.zeros((1, D), ht.dtype)], axis=0)

    def step(y, args):
        tok, wb, e = args
        xb = x_pad[tok]
        g, u = jnp.split(xb @ w_gate_up[e], 2, axis=-1)
        yb = (jax.nn.silu(g) * u) @ w_down[e]
        return y.at[tok].add(yb * wb[:, None].astype(yb.dtype)), None

    y, _ = lax.scan(step, jnp.zeros((T + 1, D), ht.dtype),
                    (slot_tok.reshape(n_blocks, MOE_BLOCK), slot_w.reshape(n_blocks, MOE_BLOCK), block_e))
    return y[:T].reshape(B, S, D)


def hybrid_layer(x, c, lidx, w_ada, b_ada, norm1_g, w_in, conv_qk, b_if, da_lambda, da_norm_g,
                 rel_bias, ml_norm_g, w_o_attn, w_o_mlstm, w_out, norm2_g, w_group, b_group,
                 w_expert, b_expert, w_gate_up, w_down):
    B, S, D = x.shape
    mod = jax.nn.silu(c) @ w_ada + b_ada
    sh1, sc1, g1, sh2, sc2, g2 = jnp.split(mod[:, None, :], 6, axis=-1)

    h = rmsnorm(x, norm1_g) * (1.0 + sc1) + sh1
    proj = h @ w_in
    offsets = np.cumsum(IN_SIZES)[:-1].tolist()
    da_q, da_k, da_v, ml_q, ml_k, ml_v, ml_o, ml_i, ml_f, gate_a, gate_m = jnp.split(proj, offsets, axis=-1)

    lambda_init = 0.8 - 0.6 * math.exp(-0.3 * lidx)
    lamp = da_lambda.astype(jnp.float32)
    lam = (jnp.exp(jnp.sum(lamp[0] * lamp[1])) - jnp.exp(jnp.sum(lamp[2] * lamp[3])) + lambda_init)
    attn = diff_attention(da_q.reshape(B, S, DA_HEADS, 2, DA_HEAD_DIM),
                          da_k.reshape(B, S, DA_HEADS, 2, DA_HEAD_DIM),
                          da_v.reshape(B, S, DA_HEADS, 2 * DA_HEAD_DIM),
                          lam, rel_bias, da_norm_g, lambda_init)

    qk = jax.nn.silu(causal_conv(jnp.concatenate([ml_q, ml_k], axis=-1), conv_qk))
    mq, mk = jnp.split(qk, 2, axis=-1)
    hm = mlstm(mq.reshape(B, S, ML_HEADS, ML_HEAD_DIM), mk.reshape(B, S, ML_HEADS, ML_HEAD_DIM),
               ml_v.reshape(B, S, ML_HEADS, ML_HEAD_DIM),
               ml_i + b_if[:ML_HEADS], ml_f + b_if[ML_HEADS:])
    hm = rmsnorm(hm.astype(x.dtype), ml_norm_g).reshape(B, S, ML_WIDTH) * jax.nn.sigmoid(ml_o)

    merged = jax.nn.sigmoid(gate_a) * (attn @ w_o_attn) + jax.nn.sigmoid(gate_m) * (hm @ w_o_mlstm)
    x = x + g1 * (merged @ w_out)

    h2 = rmsnorm(x, norm2_g) * (1.0 + sc2) + sh2
    x = x + g2 * hier_moe(h2, w_group, b_group, w_expert, b_expert, w_gate_up, w_down)
    return x


def setup_inputs(seed: int = 0) -> dict:
    key = jax.random.key(seed)
    ks = jax.random.split(key, 26)
    f32 = jnp.float32

    def nrm(k, shape, scale):
        return jax.random.normal(k, shape, f32) * scale

    L = DEPTH
    b_if = jnp.concatenate([nrm(ks[8], (L, ML_HEADS), 0.1),
                            jnp.linspace(3.0, 6.0, ML_HEADS, dtype=f32)[None] + nrm(ks[9], (L, ML_HEADS), 0.1)],
                           axis=-1)
    return {
        "x": nrm(ks[0], (BATCH, SEQ, D_MODEL), 1.0),
        "c": nrm(ks[1], (BATCH, D_MODEL), 1.0),
        "w_ada": nrm(ks[2], (L, D_MODEL, 6 * D_MODEL), 0.5 * D_MODEL ** -0.5),
        "b_ada": nrm(ks[3], (L, 6 * D_MODEL), 0.02),
        "norm1_g": 1.0 + nrm(ks[4], (L, D_MODEL), 0.02),
        "w_in": nrm(ks[5], (L, D_MODEL, IN_WIDTH), D_MODEL ** -0.5),
        "conv_qk": nrm(ks[6], (L, CONV_WIDTH, 2 * ML_WIDTH), CONV_WIDTH ** -0.5),
        "b_if": b_if,
        "da_lambda": nrm(ks[7], (L, 4, DA_HEAD_DIM), 0.1),
        "da_norm_g": 1.0 + nrm(ks[10], (L, 2 * DA_HEAD_DIM), 0.02),
        "rel_bias": nrm(ks[11], (REL_BUCKETS, DA_HEADS), 0.5),
        "ml_norm_g": 1.0 + nrm(ks[12], (L, ML_HEADS, ML_HEAD_DIM), 0.02),
        "w_o_attn": nrm(ks[13], (L, DA_WIDTH, D_MODEL), DA_WIDTH ** -0.5),
        "w_o_mlstm": nrm(ks[14], (L, ML_WIDTH, D_MODEL), ML_WIDTH ** -0.5),
        "w_out": nrm(ks[15], (L, D_MODEL, D_MODEL), D_MODEL ** -0.5),
        "norm2_g": 1.0 + nrm(ks[16], (L, D_MODEL), 0.02),
        "w_group": nrm(ks[17], (L, D_MODEL, N_GROUPS), D_MODEL ** -0.5),
        "b_group": nrm(ks[18], (L, N_GROUPS), 0.01),
        "w_expert": nrm(ks[19], (L, D_MODEL, N_EXPERTS), D_MODEL ** -0.5),
        "b_expert": nrm(ks[20], (L, N_EXPERTS), 0.01),
        "w_gate_up": nrm(ks[21], (L, N_EXPERTS, D_MODEL, 2 * D_EXPERT), D_MODEL ** -0.5),
        "w_down": nrm(ks[22], (L, N_EXPERTS, D_EXPERT, D_MODEL), D_EXPERT ** -0.5),
        "normf_g": 1.0 + nrm(ks[23], (D_MODEL,), 0.02),
    }


def reference(x, c, w_ada, b_ada, norm1_g, w_in, conv_qk, b_if, da_lambda, da_norm_g, rel_bias,
              ml_norm_g, w_o_attn, w_o_mlstm, w_out, norm2_g, w_group, b_group, w_expert, b_expert,
              w_gate_up, w_down, normf_g):
    for l in range(DEPTH):
        x = hybrid_layer(x, c, l, w_ada[l], b_ada[l], norm1_g[l], w_in[l], conv_qk[l], b_if[l],
                         da_lambda[l], da_norm_g[l], rel_bias, ml_norm_g[l], w_o_attn[l],
                         w_o_mlstm[l], w_out[l], norm2_g[l], w_group[l], b_group[l], w_expert[l],
                         b_expert[l], w_gate_up[l], w_down[l])
    return rmsnorm(x, normf_g)
```

```python
import functools
import math

import jax
import jax.numpy as jnp
from jax import lax
from jax.experimental import pallas as pl
from jax.experimental.pallas import tpu as pltpu

F32 = jnp.float32
BF16 = jnp.bfloat16
EPS = 1e-6
REL_MAX_DIST = 128
ML_CHUNK = 128
NEG_BIG = -1e30
V7X_VMEM_LIMIT = 56 * 1024 * 1024
LANES = 128


def _cparams(*sem):
    return pltpu.CompilerParams(dimension_semantics=sem, vmem_limit_bytes=V7X_VMEM_LIMIT)


def _tile(n, pref):
    t = min(n, pref)
    while n % t:
        t //= 2
    return t


def _nt_dot(a, b):
    return lax.dot_general(a, b, (((1,), (1,)), ((), ())), preferred_element_type=F32)


def _tn_dot(a, b):
    return lax.dot_general(a, b, (((0,), (0,)), ((), ())), preferred_element_type=F32)


def _ada_kernel(c_ref, w_ref, b_ref, o_ref):
    c = c_ref[...]
    s = (c * jax.nn.sigmoid(c)).astype(BF16)
    o_ref[...] = jnp.dot(s, w_ref[...].astype(BF16), preferred_element_type=F32) + b_ref[...]


def _ada(c, w_ada, b_ada):
    B, D = c.shape
    N = w_ada.shape[1]
    rows = 8
    c8 = jnp.zeros((rows, D), F32).at[:B].set(c)
    tn = _tile(N, 512)
    mod = pl.pallas_call(
        _ada_kernel,
        grid=(N // tn,),
        in_specs=[pl.BlockSpec((rows, D), lambda j: (0, 0)),
                  pl.BlockSpec((D, tn), lambda j: (0, j)),
                  pl.BlockSpec((1, tn), lambda j: (0, j))],
        out_specs=pl.BlockSpec((rows, tn), lambda j: (0, j)),
        out_shape=jax.ShapeDtypeStruct((rows, N), F32),
        compiler_params=_cparams("arbitrary"),
        name="ada",
    )(c8, w_ada, b_ada.reshape(1, N))
    return mod[:B].reshape(B, 6, D)


def _norm1_kernel(x_ref, mod_ref, g_ref, wif_ref, bif_ref, h_ref, gate_ref):
    x = x_ref[...]
    y = x * lax.rsqrt(jnp.mean(x * x, axis=-1, keepdims=True) + EPS) * g_ref[...]
    h = (y * (1.0 + mod_ref[1:2, :]) + mod_ref[0:1, :]).astype(BF16)
    h_ref[...] = h
    gate_ref[...] = jnp.dot(h, wif_ref[...].astype(BF16), preferred_element_type=F32) + bif_ref[...]


def _norm1(x2, mod, g, w_if, b_if, B, S):
    T, D = x2.shape
    G = w_if.shape[1]
    tm = _tile(S, 256)
    nb = S // tm
    return pl.pallas_call(
        _norm1_kernel,
        grid=(B, nb),
        in_specs=[pl.BlockSpec((tm, D), lambda b, i: (b * nb + i, 0)),
                  pl.BlockSpec((None, 6, D), lambda b, i: (b, 0, 0)),
                  pl.BlockSpec((1, D), lambda b, i: (0, 0)),
                  pl.BlockSpec((D, G), lambda b, i: (0, 0)),
                  pl.BlockSpec((1, G), lambda b, i: (0, 0))],
        out_specs=[pl.BlockSpec((tm, D), lambda b, i: (b * nb + i, 0)),
                   pl.BlockSpec((tm, G), lambda b, i: (b * nb + i, 0))],
        out_shape=[jax.ShapeDtypeStruct((T, D), BF16), jax.ShapeDtypeStruct((T, G), F32)],
        compiler_params=_cparams("arbitrary", "arbitrary"),
        name="norm1",
    )(x2, mod, g.reshape(1, D), w_if, b_if.reshape(1, G))


def _proj_kernel(h_ref, w_ref, o_ref):
    o_ref[...] = jnp.dot(h_ref[...], w_ref[...].astype(BF16),
                         preferred_element_type=F32).astype(o_ref.dtype)


def _proj(h, w, col0, ncols, out_dtype, name):
    M, K = h.shape
    tm = _tile(M, 1024)
    tn = _tile(math.gcd(ncols, col0) if col0 else ncols, 512)
    j0 = col0 // tn
    return pl.pallas_call(
        _proj_kernel,
        grid=(M // tm, ncols // tn),
        in_specs=[pl.BlockSpec((tm, K), lambda i, j: (i, 0)),
                  pl.BlockSpec((K, tn), lambda i, j: (0, j0 + j))],
        out_specs=pl.BlockSpec((tm, tn), lambda i, j: (i, j)),
        out_shape=jax.ShapeDtypeStruct((M, ncols), out_dtype),
        compiler_params=_cparams("arbitrary", "arbitrary"),
        name=name,
    )(h, w)


def _conv_kernel(u_ref, w_ref, o_ref, ext_ref, *, width, kcol0, kscale, tc):
    s = pl.program_id(2)
    ts = u_ref.shape[0]

    @pl.when(s == 0)
    def _():
        ext_ref[0:8, :] = jnp.zeros((8, tc), F32)

    u = u_ref[...]
    ext_ref[8:, :] = u
    acc = u * w_ref[width - 1:width, :]
    for t in range(1, width):
        acc = acc + ext_ref[8 - t:8 - t + ts, :] * w_ref[width - 1 - t:width - t, :]
    ext_ref[0:8, :] = u[ts - 8:, :]
    y = acc * jax.nn.sigmoid(acc)
    scale = jnp.where(pl.program_id(1) * tc >= kcol0, kscale, 1.0).astype(F32)
    o_ref[...] = (y * scale).astype(o_ref.dtype)


def _conv(u, w, B, S, kcol0, kscale):
    T, C = u.shape
    width = w.shape[0]
    ts = _tile(S, 512)
    tc = _tile(math.gcd(C, kcol0), 512)
    ns = S // ts
    return pl.pallas_call(
        functools.partial(_conv_kernel, width=width, kcol0=kcol0, kscale=kscale, tc=tc),
        grid=(B, C // tc, ns),
        in_specs=[pl.BlockSpec((ts, tc), lambda b, c, s: (b * ns + s, c)),
                  pl.BlockSpec((width, tc), lambda b, c, s: (0, c))],
        out_specs=pl.BlockSpec((ts, tc), lambda b, c, s: (b * ns + s, c)),
        out_shape=jax.ShapeDtypeStruct((T, C), BF16),
        scratch_shapes=[pltpu.VMEM((ts + 8, tc), F32)],
        compiler_params=_cparams("arbitrary", "arbitrary", "arbitrary"),
        name="conv",
    )(u, w)


def _attn_kernel(lam_ref, q_ref, k_ref, v_ref, bd_ref, bp_ref, g_ref, o_ref, m_sc, l_sc, acc_sc,
                 *, d, t, scale, out_scale):
    i = pl.program_id(2)
    q = q_ref[...]
    m_sc[...] = jnp.full(m_sc.shape, NEG_BIG, F32)
    l_sc[...] = jnp.zeros(l_sc.shape, F32)
    acc_sc[...] = jnp.zeros(acc_sc.shape, F32)

    def step(j, bias):
        r0 = pl.multiple_of(j * t, t)
        k = k_ref[pl.ds(r0, t), :]
        v = v_ref[pl.ds(r0, t), :]
        for m in range(2):
            s = _nt_dot(q[:, m * d:(m + 1) * d], k[:, m * d:(m + 1) * d]) * scale
            if bias is not None:
                s = s + bias
            m_old = m_sc[m]
            m_new = jnp.maximum(m_old, jnp.max(s, axis=1, keepdims=True))
            alpha = jnp.exp(m_old - m_new)
            p = jnp.exp(s - m_new)
            l_sc[m] = alpha * l_sc[m] + jnp.sum(p, axis=1, keepdims=True)
            acc_sc[m] = alpha * acc_sc[m] + jnp.dot(p.astype(BF16), v, preferred_element_type=F32)
            m_sc[m] = m_new

    def far(j, carry):
        step(j, None)
        return carry

    lax.fori_loop(0, jnp.maximum(i - 1, 0), far, 0)

    @pl.when(i > 0)
    def _():
        step(i - 1, bp_ref[...])

    step(i, bd_ref[...])

    a = acc_sc[0] / l_sc[0] - lam_ref[0] * (acc_sc[1] / l_sc[1])
    y = a * lax.rsqrt(jnp.mean(a * a, axis=-1, keepdims=True) + EPS) * g_ref[...]
    o_ref[...] = (y * out_scale).astype(o_ref.dtype)


def _rel_bucket(n, n_buckets):
    max_exact = n_buckets // 2
    nf = jnp.maximum(n, 1).astype(F32)
    large = max_exact + (jnp.log(nf / max_exact) / math.log(REL_MAX_DIST / max_exact)
                         * (n_buckets - max_exact)).astype(jnp.int32)
    large = jnp.minimum(large, n_buckets - 1)
    return jnp.where(n < max_exact, n, large)


def _attn(qkv, lam, rel_bias, norm_g, B, S, H, d, lambda_init):
    T = qkv.shape[0]
    t = _tile(S, 256)
    assert t >= REL_MAX_DIST, "key blocks two or more tiles back must sit in the saturated bucket"
    nq = S // t
    n_buckets = rel_bias.shape[0]
    rel = rel_bias.astype(F32)
    far_bias = rel[n_buckets - 1]
    qi = jnp.arange(t)[:, None]
    kj = jnp.arange(t)[None, :]
    dist_d = qi - kj
    bd = jnp.transpose(rel[_rel_bucket(jnp.maximum(dist_d, 0), n_buckets)] - far_bias, (2, 0, 1))
    bd = jnp.where((dist_d >= 0)[None], bd, NEG_BIG)
    bp = jnp.transpose(rel[_rel_bucket(dist_d + t, n_buckets)] - far_bias, (2, 0, 1))
    kern = functools.partial(_attn_kernel, d=d, t=t, scale=d ** -0.5, out_scale=1.0 - lambda_init)
    return pl.pallas_call(
        kern,
        grid=(B, H, nq),
        in_specs=[pl.BlockSpec(memory_space=pltpu.SMEM),
                  pl.BlockSpec((t, 2 * d), lambda b, h, i: (b * nq + i, h)),
                  pl.BlockSpec((S, 2 * d), lambda b, h, i: (b, H + h)),
                  pl.BlockSpec((S, 2 * d), lambda b, h, i: (b, 2 * H + h)),
                  pl.BlockSpec((None, t, t), lambda b, h, i: (h, 0, 0)),
                  pl.BlockSpec((None, t, t), lambda b, h, i: (h, 0, 0)),
                  pl.BlockSpec((1, 2 * d), lambda b, h, i: (0, 0))],
        out_specs=pl.BlockSpec((t, 2 * d), lambda b, h, i: (b * nq + i, h)),
        out_shape=jax.ShapeDtypeStruct((T, H * 2 * d), BF16),
        scratch_shapes=[pltpu.VMEM((2, t, 1), F32), pltpu.VMEM((2, t, 1), F32),
                        pltpu.VMEM((2, t, 2 * d), F32)],
        compiler_params=_cparams("arbitrary", "arbitrary", "arbitrary"),
        name="attn",
    )(lam.reshape(1), qkv, qkv, qkv, bd, bp, norm_g.reshape(1, 2 * d))


def _mlstm_kernel(q_ref, k_ref, v_ref, og_ref, gcol_ref, grow_ref, g_ref, o_ref, c_sc, n_sc, m_sc,
                  *, nheads):
    h = pl.program_id(1)
    L = q_ref.shape[0]

    @pl.when(pl.program_id(2) == 0)
    def _():
        c_sc[...] = jnp.zeros(c_sc.shape, F32)
        n_sc[...] = jnp.zeros(n_sc.shape, F32)
        m_sc[...] = jnp.zeros(m_sc.shape, F32)

    gcol = gcol_ref[...]
    lane = lax.broadcasted_iota(jnp.int32, gcol.shape, 1)
    i_col = jnp.sum(jnp.where(lane == h, gcol, 0.0), axis=1, keepdims=True)
    f_col = jax.nn.log_sigmoid(jnp.sum(jnp.where(lane == h + nheads, gcol, 0.0), axis=1, keepdims=True))
    grow = grow_ref[...]
    sub = lax.broadcasted_iota(jnp.int32, grow.shape, 0)
    i_row = jnp.sum(jnp.where(sub == h, grow, 0.0), axis=0, keepdims=True)
    f_row = jax.nn.log_sigmoid(jnp.sum(jnp.where(sub == h + nheads, grow, 0.0), axis=0, keepdims=True))

    jj = lax.broadcasted_iota(jnp.int32, (L, L), 0)
    ss = lax.broadcasted_iota(jnp.int32, (L, L), 1)
    tril = ss <= jj
    b_col = jnp.sum(jnp.where(tril, f_row, 0.0), axis=1, keepdims=True)
    b_row = jnp.sum(jnp.where(jj <= ss, f_col, 0.0), axis=0, keepdims=True)
    u_row = i_row - b_row
    u_col = i_col - b_col

    m_prev = m_sc[0:1, 0:1]
    mm_col = jnp.maximum(m_prev, jnp.max(jnp.where(tril, u_row, NEG_BIG), axis=1, keepdims=True))
    w = jnp.exp(jnp.where(tril, u_row - mm_col, NEG_BIG))
    a_inter = jnp.exp(m_prev - mm_col)

    q = q_ref[...]
    k = k_ref[...]
    v = v_ref[...]
    c_old = c_sc[...]
    n_old = n_sc[...]
    s_qk = _nt_dot(q, k) * w
    num = a_inter * _nt_dot(q, c_old.astype(BF16)) + jnp.dot(s_qk.astype(BF16), v, preferred_element_type=F32)
    den = (a_inter * jnp.sum(q.astype(F32) * n_old, axis=1, keepdims=True)
           + jnp.sum(s_qk, axis=1, keepdims=True))
    hh = num / jnp.maximum(jnp.abs(den), jnp.exp(-(b_col + mm_col)))

    mm_last = mm_col[L - 1:L, :]
    w_s = jnp.exp(u_col - mm_last)
    a_state = jnp.exp(m_prev - mm_last)
    c_sc[...] = a_state * c_old + _tn_dot((v.astype(F32) * w_s).astype(BF16), k)
    n_sc[...] = a_state * n_old + jnp.sum(k.astype(F32) * w_s, axis=0, keepdims=True)
    m_sc[...] = jnp.broadcast_to(b_col[L - 1:L, :] + mm_last, m_sc.shape)

    y = hh * lax.rsqrt(jnp.mean(hh * hh, axis=-1, keepdims=True) + EPS) * g_ref[...]
    o_ref[...] = (y * jax.nn.sigmoid(og_ref[...].astype(F32))).astype(o_ref.dtype)


def _mlstm(mqk, mvo, gates, norm_g, B, S, H, dh):
    T = mqk.shape[0]
    L = ML_CHUNK
    nc = S // L
    grow = jnp.transpose(gates.reshape(B, S, 2 * H), (0, 2, 1))
    row = lambda b, h, c: (b * nc + c, h)
    row_hi = lambda b, h, c: (b * nc + c, H + h)
    return pl.pallas_call(
        functools.partial(_mlstm_kernel, nheads=H),
        grid=(B, H, nc),
        in_specs=[pl.BlockSpec((L, dh), row),
                  pl.BlockSpec((L, dh), row_hi),
                  pl.BlockSpec((L, dh), row),
                  pl.BlockSpec((L, dh), row_hi),
                  pl.BlockSpec((L, 2 * H), lambda b, h, c: (b * nc + c, 0)),
                  pl.BlockSpec((None, 2 * H, L), lambda b, h, c: (b, 0, c)),
                  pl.BlockSpec((None, 1, dh), lambda b, h, c: (h, 0, 0))],
        out_specs=pl.BlockSpec((L, dh), row),
        out_shape=jax.ShapeDtypeStruct((T, H * dh), BF16),
        scratch_shapes=[pltpu.VMEM((dh, dh), F32), pltpu.VMEM((1, dh), F32), pltpu.VMEM((8, LANES), F32)],
        compiler_params=_cparams("arbitrary", "arbitrary", "arbitrary"),
        name="mlstm",
    )(mqk, mqk, mvo, mvo, gates, grow, norm_g.reshape(H, 1, dh))


def _merge_kernel(a_ref, m_ref, wa_ref, wm_ref, ga_ref, gm_ref, o_ref):
    pa = jnp.dot(a_ref[...], wa_ref[...].astype(BF16), preferred_element_type=F32)
    pm = jnp.dot(m_ref[...], wm_ref[...].astype(BF16), preferred_element_type=F32)
    out = (jax.nn.sigmoid(ga_ref[...].astype(F32)) * pa + jax.nn.sigmoid(gm_ref[...].astype(F32)) * pm)
    o_ref[...] = out.astype(o_ref.dtype)


def _merge(attn, hm, w_a, w_m, gates_am):
    T, Ka = attn.shape
    Km = hm.shape[1]
    D = w_a.shape[1]
    tm = _tile(T, 1024)
    tn = _tile(D, 256)
    nj = D // tn
    return pl.pallas_call(
        _merge_kernel,
        grid=(T // tm, nj),
        in_specs=[pl.BlockSpec((tm, Ka), lambda i, j: (i, 0)),
                  pl.BlockSpec((tm, Km), lambda i, j: (i, 0)),
                  pl.BlockSpec((Ka, tn), lambda i, j: (0, j)),
                  pl.BlockSpec((Km, tn), lambda i, j: (0, j)),
                  pl.BlockSpec((tm, tn), lambda i, j: (i, j)),
                  pl.BlockSpec((tm, tn), lambda i, j: (i, nj + j))],
        out_specs=pl.BlockSpec((tm, tn), lambda i, j: (i, j)),
        out_shape=jax.ShapeDtypeStruct((T, D), BF16),
        compiler_params=_cparams("arbitrary", "arbitrary"),
        name="merge",
    )(attn, hm, w_a, w_m, gates_am, gates_am)


def _out_kernel(a_ref, w_ref, x_ref, mod_ref, o_ref):
    p = jnp.dot(a_ref[...], w_ref[...].astype(BF16), preferred_element_type=F32)
    o_ref[...] = x_ref[...] + mod_ref[2:3, :] * p


def _outproj(merged, w_out, x2, mod, S):
    T, K = merged.shape
    D = w_out.shape[1]
    tm = _tile(S, 1024)
    tn = _tile(D, 512)
    per_b = S // tm
    return pl.pallas_call(
        _out_kernel,
        grid=(T // tm, D // tn),
        in_specs=[pl.BlockSpec((tm, K), lambda i, j: (i, 0)),
                  pl.BlockSpec((K, tn), lambda i, j: (0, j)),
                  pl.BlockSpec((tm, tn), lambda i, j: (i, j)),
                  pl.BlockSpec((None, 6, tn), lambda i, j: (i // per_b, 0, j))],
        out_specs=pl.BlockSpec((tm, tn), lambda i, j: (i, j)),
        out_shape=jax.ShapeDtypeStruct((T, D), F32),
        compiler_params=_cparams("arbitrary", "arbitrary"),
        name="outproj",
    )(merged, w_out, x2, mod)


def _router_kernel(x_ref, mod_ref, g_ref, whi_ref, wlo_ref, b_ref, h_ref, r_ref, *, n_groups, per_group):
    x = x_ref[...]
    y = x * lax.rsqrt(jnp.mean(x * x, axis=-1, keepdims=True) + EPS) * g_ref[...]
    h2 = y * (1.0 + mod_ref[4:5, :]) + mod_ref[3:4, :]
    h_ref[...] = h2
    hi = h2.astype(BF16)
    lo = (h2 - hi.astype(F32)).astype(BF16)
    logits = (jnp.dot(hi, whi_ref[...], preferred_element_type=F32)
              + jnp.dot(hi, wlo_ref[...], preferred_element_type=F32)
              + jnp.dot(lo, whi_ref[...], preferred_element_type=F32)) + b_ref[...]
    lane = lax.broadcasted_iota(jnp.int32, logits.shape, 1)
    big = jnp.int32(1 << 20)

    def top(vals):
        mx = jnp.max(vals, axis=1, keepdims=True)
        idx = jnp.min(jnp.where(vals == mx, lane, big), axis=1, keepdims=True)
        return mx, idx

    gl = jnp.where(lane < n_groups, logits, -jnp.inf)
    gmax, gidx = top(gl)
    g_p = 1.0 / jnp.sum(jnp.exp(gl - gmax), axis=1, keepdims=True)
    lo_lane = n_groups + per_group * gidx
    el = jnp.where((lane >= lo_lane) & (lane < lo_lane + per_group), logits, -jnp.inf)
    e1, i1 = top(el)
    e2, i2 = top(jnp.where(lane == i1, -jnp.inf, el))
    r = jnp.exp(e2 - e1)
    gate1 = g_p / (1.0 + r)
    gate2 = g_p * r / (1.0 + r)
    out = jnp.where(lane == 0, (i1 - n_groups).astype(F32),
                    jnp.where(lane == 1, (i2 - n_groups).astype(F32),
                              jnp.where(lane == 2, gate1, jnp.where(lane == 3, gate2, 0.0))))
    r_ref[...] = out


def _router(x1, mod, g, w_group, b_group, w_expert, b_expert, B, S):
    T, D = x1.shape
    NG = w_group.shape[1]
    NE = w_expert.shape[1]
    assert NG + NE <= LANES
    pad = LANES - NG - NE
    w = jnp.concatenate([w_group, w_expert, jnp.zeros((D, pad), F32)], axis=1)
    w_hi = w.astype(BF16)
    w_lo = (w - w_hi.astype(F32)).astype(BF16)
    bias = jnp.concatenate([b_group, b_expert, jnp.full((pad,), -jnp.inf, F32)]).reshape(1, LANES)
    tm = _tile(S, 256)
    nb = S // tm
    kern = functools.partial(_router_kernel, n_groups=NG, per_group=NE // NG)
    return pl.pallas_call(
        kern,
        grid=(B, nb),
        in_specs=[pl.BlockSpec((tm, D), lambda b, i: (b * nb + i, 0)),
                  pl.BlockSpec((None, 6, D), lambda b, i: (b, 0, 0)),
                  pl.BlockSpec((1, D), lambda b, i: (0, 0)),
                  pl.BlockSpec((D, LANES), lambda b, i: (0, 0)),
                  pl.BlockSpec((D, LANES), lambda b, i: (0, 0)),
                  pl.BlockSpec((1, LANES), lambda b, i: (0, 0))],
        out_specs=[pl.BlockSpec((tm, D), lambda b, i: (b * nb + i, 0)),
                   pl.BlockSpec((tm, LANES), lambda b, i: (b * nb + i, 0))],
        out_shape=[jax.ShapeDtypeStruct((T, D), F32), jax.ShapeDtypeStruct((T, LANES), F32)],
        compiler_params=_cparams("arbitrary", "arbitrary"),
        name="router",
    )(x1, mod, g.reshape(1, D), w_hi, w_lo, bias)


def _gather_rows(idx_ref, base, src_hbm, dst_ref, sem, n):
    def body(r, carry):
        tok = idx_ref[base + r]
        pltpu.make_async_copy(src_hbm.at[pl.ds(tok, 1), :], dst_ref.at[pl.ds(r, 1), :], sem).start()
        return carry
    lax.fori_loop(0, n, body, 0)


def _wait_rows(src_hbm, dst_ref, sem, n):
    def body(r, carry):
        pltpu.make_async_copy(src_hbm.at[pl.ds(0, 1), :], dst_ref.at[pl.ds(r, 1), :], sem).wait()
        return carry
    lax.fori_loop(0, n, body, 0)


def _moe_kernel(be_ref, tok_ref, nused_ref, h_hbm, wgu_ref, wd_ref, sw_ref, y_ref, xbuf, sem, *, tb, de):
    i = pl.program_id(0)
    nused = nused_ref[0]
    slot = lax.rem(i, 2)

    @pl.when(jnp.logical_and(i == 0, nused > 0))
    def _():
        _gather_rows(tok_ref, 0, h_hbm, xbuf.at[0], sem.at[0], tb)

    @pl.when(i + 1 < nused)
    def _():
        _gather_rows(tok_ref, (i + 1) * tb, h_hbm, xbuf.at[1 - slot], sem.at[1 - slot], tb)

    @pl.when(i < nused)
    def _():
        _wait_rows(h_hbm, xbuf.at[slot], sem.at[slot], tb)
        xb = xbuf[slot].astype(BF16)
        gu = jnp.dot(xb, wgu_ref[...], preferred_element_type=F32)
        g = gu[:, :de]
        u = gu[:, de:]
        act = (g * jax.nn.sigmoid(g) * u).astype(BF16)
        yb = jnp.dot(act, wd_ref[...], preferred_element_type=F32)
        y_ref[...] = yb * sw_ref[...]

    @pl.when(i >= nused)
    def _():
        y_ref[...] = jnp.zeros(y_ref.shape, F32)


def _moe(h2, block_e, slot_tok, nused, slot_w, wgu, wd, tb):
    T, D = h2.shape
    E, _, de2 = wgu.shape
    de = de2 // 2
    n_slots = slot_tok.shape[0]
    n_blocks = n_slots // tb
    grid_spec = pltpu.PrefetchScalarGridSpec(
        num_scalar_prefetch=3,
        grid=(n_blocks,),
        in_specs=[pl.BlockSpec(memory_space=pl.ANY),
                  pl.BlockSpec((None, D, de2), lambda i, be, tok, nu: (be[i], 0, 0)),
                  pl.BlockSpec((None, de, D), lambda i, be, tok, nu: (be[i], 0, 0)),
                  pl.BlockSpec((tb, 1), lambda i, be, tok, nu: (i, 0))],
        out_specs=pl.BlockSpec((tb, D), lambda i, be, tok, nu: (i, 0)),
        scratch_shapes=[pltpu.VMEM((2, tb, D), F32), pltpu.SemaphoreType.DMA((2,))],
    )
    return pl.pallas_call(
        functools.partial(_moe_kernel, tb=tb, de=de),
        grid_spec=grid_spec,
        out_shape=jax.ShapeDtypeStruct((n_slots, D), F32),
        compiler_params=_cparams("arbitrary"),
        name="moe",
    )(block_e, slot_tok, nused, h2, wgu, wd, slot_w.reshape(n_slots, 1))


def _final_kernel(pos_ref, y_hbm, x_ref, mod_ref, g_ref, o_ref, ybuf, sem, *, tm, nsteps):
    i = pl.program_id(0)
    slot = lax.rem(i, 2)

    @pl.when(i == 0)
    def _():
        _gather_rows(pos_ref, 0, y_hbm, ybuf.at[0], sem.at[0], 2 * tm)

    @pl.when(i + 1 < nsteps)
    def _():
        _gather_rows(pos_ref, (i + 1) * 2 * tm, y_hbm, ybuf.at[1 - slot], sem.at[1 - slot], 2 * tm)

    _wait_rows(y_hbm, ybuf.at[slot], sem.at[slot], 2 * tm)
    yb = ybuf[slot]
    x = x_ref[...] + mod_ref[5:6, :] * (yb[:tm] + yb[tm:])
    o_ref[...] = x * lax.rsqrt(jnp.mean(x * x, axis=-1, keepdims=True) + EPS) * g_ref[...]


def _final(y_slots, pos, x1, mod, g, S):
    T, D = x1.shape
    tm = _tile(S, 128)
    per_b = S // tm
    nsteps = T // tm
    pos_tiled = jnp.transpose(pos.reshape(nsteps, tm, 2), (0, 2, 1)).reshape(-1)
    grid_spec = pltpu.PrefetchScalarGridSpec(
        num_scalar_prefetch=1,
        grid=(nsteps,),
        in_specs=[pl.BlockSpec(memory_space=pl.ANY),
                  pl.BlockSpec((tm, D), lambda i, p: (i, 0)),
                  pl.BlockSpec((None, 6, D), lambda i, p: (i // per_b, 0, 0)),
                  pl.BlockSpec((1, D), lambda i, p: (0, 0))],
        out_specs=pl.BlockSpec((tm, D), lambda i, p: (i, 0)),
        scratch_shapes=[pltpu.VMEM((2, 2 * tm, D), F32), pltpu.SemaphoreType.DMA((2,))],
    )
    return pl.pallas_call(
        functools.partial(_final_kernel, tm=tm, nsteps=nsteps),
        grid_spec=grid_spec,
        out_shape=jax.ShapeDtypeStruct((T, D), F32),
        compiler_params=_cparams("arbitrary"),
        name="final",
    )(pos_tiled, y_slots, x1, mod, g.reshape(1, D))


def _dispatch(route, n_experts, tb):
    T = route.shape[0]
    A = 2 * T
    expert_id = route[:, 0:2].astype(jnp.int32).reshape(-1)
    flat_w = route[:, 2:4].reshape(-1)
    flat_tok = jnp.repeat(jnp.arange(T, dtype=jnp.int32), 2)
    order = jnp.argsort(expert_id)
    se = expert_id[order]
    counts = jnp.bincount(expert_id, length=n_experts)
    starts = jnp.cumsum(counts) - counts
    padded = (counts + tb - 1) // tb * tb
    pend = jnp.cumsum(padded)
    pstart = pend - padded
    dest = (pstart[se] + jnp.arange(A) - starts[se]).astype(jnp.int32)
    n_blocks = -(-A // tb) + n_experts
    n_slots = n_blocks * tb
    slot_tok = jnp.zeros((n_slots,), jnp.int32).at[dest].set(flat_tok[order])
    slot_w = jnp.zeros((n_slots,), F32).at[dest].set(flat_w[order])
    block_e = jnp.minimum(jnp.searchsorted(pend, jnp.arange(n_blocks) * tb, side='right'),
                          n_experts - 1).astype(jnp.int32)
    nused = (pend[-1] // tb).astype(jnp.int32).reshape(1)
    pos = jnp.zeros((A,), jnp.int32).at[order].set(dest)
    return block_e, slot_tok, slot_w, nused, pos


def _layer(x2, c, lidx, B, S, w_ada, b_ada, norm1_g, w_in, conv_qk, b_if, da_lambda, da_norm_g, rel_bias,
           ml_norm_g, w_o_attn, w_o_mlstm, w_out, norm2_g, w_group, b_group, w_expert, b_expert,
           w_gate_up, w_down):
    D = x2.shape[1]
    H = rel_bias.shape[1]
    d = da_lambda.shape[1]
    da_w = H * 2 * d
    MH, dh = ml_norm_g.shape
    ml_w = MH * dh
    n_experts = w_expert.shape[1]

    mod = _ada(c, w_ada, b_ada)

    off_mlq = 3 * da_w
    off_mlv = off_mlq + 2 * ml_w
    off_if = off_mlv + 2 * ml_w
    off_g = off_if + 2 * MH
    h, gates_if = _norm1(x2, mod, norm1_g, w_in[:, off_if:off_g], b_if, B, S)
    da_qkv = _proj(h, w_in, 0, 3 * da_w, BF16, "proj_da")
    ml_qk = _proj(h, w_in, off_mlq, 2 * ml_w, F32, "proj_mlqk")
    ml_vo = _proj(h, w_in, off_mlv, 2 * ml_w, BF16, "proj_mlvo")
    gates_am = _proj(h, w_in[:, off_g:], 0, 2 * D, BF16, "proj_gates")

    lambda_init = 0.8 - 0.6 * math.exp(-0.3 * lidx)
    lamp = da_lambda.astype(F32)
    lam = jnp.exp(jnp.sum(lamp[0] * lamp[1])) - jnp.exp(jnp.sum(lamp[2] * lamp[3])) + lambda_init
    attn = _attn(da_qkv, lam, rel_bias, da_norm_g, B, S, H, d, lambda_init)

    mqk = _conv(ml_qk, conv_qk, B, S, ml_w, dh ** -0.5)
    hm = _mlstm(mqk, ml_vo, gates_if, ml_norm_g, B, S, MH, dh)

    merged = _merge(attn, hm, w_o_attn, w_o_mlstm, gates_am)
    x1 = _outproj(merged, w_out, x2, mod, S)

    h2, route = _router(x1, mod, norm2_g, w_group, b_group, w_expert, b_expert, B, S)
    tb = 256
    block_e, slot_tok, slot_w, nused, pos = _dispatch(route, n_experts, tb)
    y_slots = _moe(h2, block_e, slot_tok, nused, slot_w, w_gate_up.astype(BF16), w_down.astype(BF16), tb)
    return y_slots, pos, x1, mod


def kernel(x, c, w_ada, b_ada, norm1_g, w_in, conv_qk, b_if, da_lambda, da_norm_g, rel_bias, ml_norm_g,
           w_o_attn, w_o_mlstm, w_out, norm2_g, w_group, b_group, w_expert, b_expert, w_gate_up, w_down,
           normf_g):
    B, S, D = x.shape
    assert w_ada.shape[0] == 1, "the final rmsnorm is fused into the layer's last kernel: one layer only"
    l = 0
    y_slots, pos, x1, mod = _layer(
        x.reshape(B * S, D), c, l, B, S, w_ada[l], b_ada[l], norm1_g[l], w_in[l], conv_qk[l], b_if[l],
        da_lambda[l], da_norm_g[l], rel_bias, ml_norm_g[l], w_o_attn[l], w_o_mlstm[l], w_out[l],
        norm2_g[l], w_group[l], b_group[l], w_expert[l], b_expert[l], w_gate_up[l], w_down[l])
    return _final(y_slots, pos, x1, mod, normf_g, S).reshape(B, S, D)
```

```python
import functools
import math

import jax
import jax.numpy as jnp
from jax import lax
from jax.experimental import pallas as pl
from jax.experimental.pallas import tpu as pltpu

F32 = jnp.float32
BF16 = jnp.bfloat16
EPS = 1e-6
REL_MAX_DIST = 128
ML_CHUNK = 128
NEG_BIG = -1e30
V7X_VMEM_LIMIT = 56 * 1024 * 1024
LANES = 128


def _cparams(*sem):
    return pltpu.CompilerParams(dimension_semantics=sem, vmem_limit_bytes=V7X_VMEM_LIMIT)


def _tile(n, pref):
    t = min(n, pref)
    while n % t:
        t //= 2
    return t


def _nt_dot(a, b):
    return lax.dot_general(a, b, (((1,), (1,)), ((), ())), preferred_element_type=F32)


def _tn_dot(a, b):
    return lax.dot_general(a, b, (((0,), (0,)), ((), ())), preferred_element_type=F32)


def _ada_kernel(c_ref, w_ref, b_ref, o_ref):
    c = c_ref[...]
    s = (c * jax.nn.sigmoid(c)).astype(BF16)
    o_ref[...] = jnp.dot(s, w_ref[...].astype(BF16), preferred_element_type=F32) + b_ref[...]


def _ada(c, w_ada, b_ada):
    B, D = c.shape
    N = w_ada.shape[1]
    rows = 8
    c8 = jnp.zeros((rows, D), F32).at[:B].set(c)
    tn = _tile(N, 512)
    mod = pl.pallas_call(
        _ada_kernel,
        grid=(N // tn,),
        in_specs=[pl.BlockSpec((rows, D), lambda j: (0, 0)),
                  pl.BlockSpec((D, tn), lambda j: (0, j)),
                  pl.BlockSpec((1, tn), lambda j: (0, j))],
        out_specs=pl.BlockSpec((rows, tn), lambda j: (0, j)),
        out_shape=jax.ShapeDtypeStruct((rows, N), F32),
        compiler_params=_cparams("arbitrary"),
        name="ada",
    )(c8, w_ada, b_ada.reshape(1, N))
    return mod[:B].reshape(B, 6, D)


def _norm1_kernel(x_ref, mod_ref, g_ref, wif_ref, bif_ref, h_ref, gate_ref):
    x = x_ref[...]
    y = x * lax.rsqrt(jnp.mean(x * x, axis=-1, keepdims=True) + EPS) * g_ref[...]
    h = (y * (1.0 + mod_ref[1:2, :]) + mod_ref[0:1, :]).astype(BF16)
    h_ref[...] = h
    gate_ref[...] = jnp.dot(h, wif_ref[...].astype(BF16), preferred_element_type=F32) + bif_ref[...]


def _norm1(x2, mod, g, w_if, b_if, B, S):
    T, D = x2.shape
    G = w_if.shape[1]
    tm = _tile(S, 256)
    nb = S // tm
    return pl.pallas_call(
        _norm1_kernel,
        grid=(B, nb),
        in_specs=[pl.BlockSpec((tm, D), lambda b, i: (b * nb + i, 0)),
                  pl.BlockSpec((None, 6, D), lambda b, i: (b, 0, 0)),
                  pl.BlockSpec((1, D), lambda b, i: (0, 0)),
                  pl.BlockSpec((D, G), lambda b, i: (0, 0)),
                  pl.BlockSpec((1, G), lambda b, i: (0, 0))],
        out_specs=[pl.BlockSpec((tm, D), lambda b, i: (b * nb + i, 0)),
                   pl.BlockSpec((tm, G), lambda b, i: (b * nb + i, 0))],
        out_shape=[jax.ShapeDtypeStruct((T, D), BF16), jax.ShapeDtypeStruct((T, G), F32)],
        compiler_params=_cparams("arbitrary", "arbitrary"),
        name="norm1",
    )(x2, mod, g.reshape(1, D), w_if, b_if.reshape(1, G))


def _proj_kernel(h_ref, w_ref, o_ref):
    o_ref[...] = jnp.dot(h_ref[...], w_ref[...].astype(BF16),
                         preferred_element_type=F32).astype(o_ref.dtype)


def _proj_scaled_kernel(h_ref, w_ref, s_ref, o_ref):
    acc = jnp.dot(h_ref[...], w_ref[...].astype(BF16), preferred_element_type=F32)
    o_ref[...] = (acc * s_ref[...]).astype(o_ref.dtype)


def _proj(h, w, col0, ncols, out_dtype, name, col_scale=None):
    M, K = h.shape
    tm = _tile(M, 1024)
    tn = _tile(math.gcd(ncols, col0) if col0 else ncols, 512)
    j0 = col0 // tn
    in_specs = [pl.BlockSpec((tm, K), lambda i, j: (i, 0)),
                pl.BlockSpec((K, tn), lambda i, j: (0, j0 + j))]
    args = (h, w)
    body = _proj_kernel
    if col_scale is not None:
        in_specs.append(pl.BlockSpec((1, tn), lambda i, j: (0, j)))
        args = (h, w, col_scale.reshape(1, ncols))
        body = _proj_scaled_kernel
    return pl.pallas_call(
        body,
        grid=(M // tm, ncols // tn),
        in_specs=in_specs,
        out_specs=pl.BlockSpec((tm, tn), lambda i, j: (i, j)),
        out_shape=jax.ShapeDtypeStruct((M, ncols), out_dtype),
        compiler_params=_cparams("arbitrary", "arbitrary"),
        name=name,
    )(*args)


def _conv_kernel(u_ref, w_ref, o_ref, ext_ref, *, width, kcol0, kscale, tc):
    s = pl.program_id(2)
    ts = u_ref.shape[0]

    @pl.when(s == 0)
    def _():
        ext_ref[0:8, :] = jnp.zeros((8, tc), F32)

    u = u_ref[...]
    ext_ref[8:, :] = u
    acc = u * w_ref[width - 1:width, :]
    for t in range(1, width):
        acc = acc + ext_ref[8 - t:8 - t + ts, :] * w_ref[width - 1 - t:width - t, :]
    ext_ref[0:8, :] = u[ts - 8:, :]
    y = acc * jax.nn.sigmoid(acc)
    scale = jnp.where(pl.program_id(1) * tc >= kcol0, kscale, 1.0).astype(F32)
    o_ref[...] = (y * scale).astype(o_ref.dtype)


def _conv(u, w, B, S, kcol0, kscale):
    T, C = u.shape
    width = w.shape[0]
    ts = _tile(S, 512)
    tc = _tile(math.gcd(C, kcol0), 512)
    ns = S // ts
    return pl.pallas_call(
        functools.partial(_conv_kernel, width=width, kcol0=kcol0, kscale=kscale, tc=tc),
        grid=(B, C // tc, ns),
        in_specs=[pl.BlockSpec((ts, tc), lambda b, c, s: (b * ns + s, c)),
                  pl.BlockSpec((width, tc), lambda b, c, s: (0, c))],
        out_specs=pl.BlockSpec((ts, tc), lambda b, c, s: (b * ns + s, c)),
        out_shape=jax.ShapeDtypeStruct((T, C), BF16),
        scratch_shapes=[pltpu.VMEM((ts + 8, tc), F32)],
        compiler_params=_cparams("arbitrary", "arbitrary", "arbitrary"),
        name="conv",
    )(u, w)


def _attn_kernel(lam_ref, q_ref, k_ref, v_ref, bd_ref, bp_ref, g_ref, o_ref, m_sc, l_sc, acc_sc,
                 *, d, t, out_scale):
    i = pl.program_id(2)
    q = q_ref[...]
    ngrp = t // LANES
    m_sc[...] = jnp.full(m_sc.shape, NEG_BIG, F32)
    l_sc[...] = jnp.zeros(l_sc.shape, F32)
    acc_sc[...] = jnp.zeros(acc_sc.shape, F32)

    def step(j, bias_ref):
        r0 = pl.multiple_of(j * t, t)
        k = k_ref[pl.ds(r0, t), :]
        v = v_ref[pl.ds(r0, t), :]
        for m in range(2):
            s = _nt_dot(q[:, m * d:(m + 1) * d], k[:, m * d:(m + 1) * d])
            if bias_ref is not None:
                s = s + bias_ref[...]
            grp = [s[:, g * LANES:(g + 1) * LANES] for g in range(ngrp)]
            m_old = m_sc[m]
            row_max = jnp.max(functools.reduce(jnp.maximum, grp), axis=1, keepdims=True)
            m_new = jnp.maximum(m_old, row_max)
            alpha = jnp.exp2(m_old - m_new)
            p = [jnp.exp2(x - m_new) for x in grp]
            l_sc[m] = alpha * l_sc[m] + functools.reduce(jnp.add, p)
            pb = jnp.concatenate([x.astype(BF16) for x in p], axis=1)
            pv = jnp.dot(pb, v, preferred_element_type=F32)
            acc_sc[m] = jnp.concatenate([alpha] * (2 * d // LANES), axis=1) * acc_sc[m] + pv
            m_sc[m] = m_new

    def far(j, carry):
        step(j, None)
        return carry

    lax.fori_loop(0, jnp.maximum(i - 1, 0), far, 0)

    @pl.when(i > 0)
    def _():
        step(i - 1, bp_ref)

    step(i, bd_ref)

    l0 = jnp.sum(l_sc[0], axis=1, keepdims=True)
    l1 = jnp.sum(l_sc[1], axis=1, keepdims=True)
    a = acc_sc[0] / l0 - lam_ref[0] * (acc_sc[1] / l1)
    y = a * lax.rsqrt(jnp.mean(a * a, axis=-1, keepdims=True) + EPS) * g_ref[...]
    o_ref[...] = (y * out_scale).astype(o_ref.dtype)


def _rel_bucket(n, n_buckets):
    max_exact = n_buckets // 2
    nf = jnp.maximum(n, 1).astype(F32)
    large = max_exact + (jnp.log(nf / max_exact) / math.log(REL_MAX_DIST / max_exact)
                         * (n_buckets - max_exact)).astype(jnp.int32)
    large = jnp.minimum(large, n_buckets - 1)
    return jnp.where(n < max_exact, n, large)


def _bias_table(rel, dist, n_buckets):
    bucket = _rel_bucket(dist, n_buckets)
    out = jnp.zeros((rel.shape[1],) + dist.shape, F32)
    for b in range(n_buckets):
        out = jnp.where((bucket == b)[None], rel[b][:, None, None], out)
    return out


def _attn(qkv, lam, rel_bias, norm_g, B, S, H, d, lambda_init):
    T = qkv.shape[0]
    t = _tile(S, 512)
    assert t >= REL_MAX_DIST, "key blocks two or more tiles back must sit in the saturated bucket"
    assert d % LANES == 0
    nq = S // t
    n_buckets = rel_bias.shape[0]
    rel = rel_bias.astype(F32)
    rel = (rel - rel[n_buckets - 1]) * math.log2(math.e)
    dist_d = jnp.arange(t)[:, None] - jnp.arange(t)[None, :]
    bd = jnp.where((dist_d >= 0)[None], _bias_table(rel, jnp.maximum(dist_d, 0), n_buckets), NEG_BIG)
    bp = _bias_table(rel, dist_d + t, n_buckets)
    kern = functools.partial(_attn_kernel, d=d, t=t, out_scale=1.0 - lambda_init)
    return pl.pallas_call(
        kern,
        grid=(B, H, nq),
        in_specs=[pl.BlockSpec(memory_space=pltpu.SMEM),
                  pl.BlockSpec((t, 2 * d), lambda b, h, i: (b * nq + i, h)),
                  pl.BlockSpec((S, 2 * d), lambda b, h, i: (b, H + h)),
                  pl.BlockSpec((S, 2 * d), lambda b, h, i: (b, 2 * H + h)),
                  pl.BlockSpec((None, t, t), lambda b, h, i: (h, 0, 0)),
                  pl.BlockSpec((None, t, t), lambda b, h, i: (h, 0, 0)),
                  pl.BlockSpec((1, 2 * d), lambda b, h, i: (0, 0))],
        out_specs=pl.BlockSpec((t, 2 * d), lambda b, h, i: (b * nq + i, h)),
        out_shape=jax.ShapeDtypeStruct((T, H * 2 * d), BF16),
        scratch_shapes=[pltpu.VMEM((2, t, LANES), F32), pltpu.VMEM((2, t, LANES), F32),
                        pltpu.VMEM((2, t, 2 * d), F32)],
        compiler_params=_cparams("arbitrary", "arbitrary", "arbitrary"),
        name="attn",
    )(lam.reshape(1), qkv, qkv, qkv, bd, bp, norm_g.reshape(1, 2 * d))


def _mlstm_kernel(q_ref, k_ref, v_ref, og_ref, gcol_ref, grow_ref, g_ref, o_ref, c_sc, n_sc, m_sc,
                  *, nheads):
    h = pl.program_id(1)
    L = q_ref.shape[0]

    @pl.when(pl.program_id(2) == 0)
    def _():
        c_sc[...] = jnp.zeros(c_sc.shape, F32)
        n_sc[...] = jnp.zeros(n_sc.shape, F32)
        m_sc[...] = jnp.zeros(m_sc.shape, F32)

    gcol = gcol_ref[...]
    lane = lax.broadcasted_iota(jnp.int32, gcol.shape, 1)
    i_col = jnp.sum(jnp.where(lane == h, gcol, 0.0), axis=1, keepdims=True)
    f_col = jax.nn.log_sigmoid(jnp.sum(jnp.where(lane == h + nheads, gcol, 0.0), axis=1, keepdims=True))
    grow = grow_ref[...]
    sub = lax.broadcasted_iota(jnp.int32, grow.shape, 0)
    i_row = jnp.sum(jnp.where(sub == h, grow, 0.0), axis=0, keepdims=True)
    f_row = jax.nn.log_sigmoid(jnp.sum(jnp.where(sub == h + nheads, grow, 0.0), axis=0, keepdims=True))

    jj = lax.broadcasted_iota(jnp.int32, (L, L), 0)
    ss = lax.broadcasted_iota(jnp.int32, (L, L), 1)
    tril = ss <= jj
    b_col = jnp.sum(jnp.where(tril, f_row, 0.0), axis=1, keepdims=True)
    b_row = jnp.sum(jnp.where(jj <= ss, f_col, 0.0), axis=0, keepdims=True)
    u_row = i_row - b_row
    u_col = i_col - b_col

    m_prev = m_sc[0:1, 0:1]
    mm_col = jnp.maximum(m_prev, jnp.max(jnp.where(tril, u_row, NEG_BIG), axis=1, keepdims=True))
    w = jnp.exp(jnp.where(tril, u_row - mm_col, NEG_BIG))
    a_inter = jnp.exp(m_prev - mm_col)

    q = q_ref[...]
    k = k_ref[...]
    v = v_ref[...]
    c_old = c_sc[...]
    n_old = n_sc[...]
    s_qk = _nt_dot(q, k) * w
    num = a_inter * _nt_dot(q, c_old.astype(BF16)) + jnp.dot(s_qk.astype(BF16), v, preferred_element_type=F32)
    den = (a_inter * jnp.sum(q.astype(F32) * n_old, axis=1, keepdims=True)
           + jnp.sum(s_qk, axis=1, keepdims=True))
    hh = num / jnp.maximum(jnp.abs(den), jnp.exp(-(b_col + mm_col)))

    mm_last = mm_col[L - 1:L, :]
    w_s = jnp.exp(u_col - mm_last)
    a_state = jnp.exp(m_prev - mm_last)
    c_sc[...] = a_state * c_old + _tn_dot((v.astype(F32) * w_s).astype(BF16), k)
    n_sc[...] = a_state * n_old + jnp.sum(k.astype(F32) * w_s, axis=0, keepdims=True)
    m_sc[...] = jnp.broadcast_to(b_col[L - 1:L, :] + mm_last, m_sc.shape)

    y = hh * lax.rsqrt(jnp.mean(hh * hh, axis=-1, keepdims=True) + EPS) * g_ref[...]
    o_ref[...] = (y * jax.nn.sigmoid(og_ref[...].astype(F32))).astype(o_ref.dtype)


def _mlstm(mqk, mvo, gates, norm_g, B, S, H, dh):
    T = mqk.shape[0]
    L = ML_CHUNK
    nc = S // L
    grow = jnp.transpose(gates.reshape(B, S, 2 * H), (0, 2, 1))
    row = lambda b, h, c: (b * nc + c, h)
    row_hi = lambda b, h, c: (b * nc + c, H + h)
    return pl.pallas_call(
        functools.partial(_mlstm_kernel, nheads=H),
        grid=(B, H, nc),
        in_specs=[pl.BlockSpec((L, dh), row),
                  pl.BlockSpec((L, dh), row_hi),
                  pl.BlockSpec((L, dh), row),
                  pl.BlockSpec((L, dh), row_hi),
                  pl.BlockSpec((L, 2 * H), lambda b, h, c: (b * nc + c, 0)),
                  pl.BlockSpec((None, 2 * H, L), lambda b, h, c: (b, 0, c)),
                  pl.BlockSpec((None, 1, dh), lambda b, h, c: (h, 0, 0))],
        out_specs=pl.BlockSpec((L, dh), row),
        out_shape=jax.ShapeDtypeStruct((T, H * dh), BF16),
        scratch_shapes=[pltpu.VMEM((dh, dh), F32), pltpu.VMEM((1, dh), F32), pltpu.VMEM((8, LANES), F32)],
        compiler_params=_cparams("arbitrary", "arbitrary", "arbitrary"),
        name="mlstm",
    )(mqk, mqk, mvo, mvo, gates, grow, norm_g.reshape(H, 1, dh))


def _merge_kernel(a_ref, m_ref, wa_ref, wm_ref, ga_ref, gm_ref, o_ref):
    pa = jnp.dot(a_ref[...], wa_ref[...].astype(BF16), preferred_element_type=F32)
    pm = jnp.dot(m_ref[...], wm_ref[...].astype(BF16), preferred_element_type=F32)
    out = (jax.nn.sigmoid(ga_ref[...].astype(F32)) * pa + jax.nn.sigmoid(gm_ref[...].astype(F32)) * pm)
    o_ref[...] = out.astype(o_ref.dtype)


def _merge(attn, hm, w_a, w_m, gates_am):
    T, Ka = attn.shape
    Km = hm.shape[1]
    D = w_a.shape[1]
    tm = _tile(T, 1024)
    tn = _tile(D, 256)
    nj = D // tn
    return pl.pallas_call(
        _merge_kernel,
        grid=(T // tm, nj),
        in_specs=[pl.BlockSpec((tm, Ka), lambda i, j: (i, 0)),
                  pl.BlockSpec((tm, Km), lambda i, j: (i, 0)),
                  pl.BlockSpec((Ka, tn), lambda i, j: (0, j)),
                  pl.BlockSpec((Km, tn), lambda i, j: (0, j)),
                  pl.BlockSpec((tm, tn), lambda i, j: (i, j)),
                  pl.BlockSpec((tm, tn), lambda i, j: (i, nj + j))],
        out_specs=pl.BlockSpec((tm, tn), lambda i, j: (i, j)),
        out_shape=jax.ShapeDtypeStruct((T, D), BF16),
        compiler_params=_cparams("arbitrary", "arbitrary"),
        name="merge",
    )(attn, hm, w_a, w_m, gates_am, gates_am)


def _out_kernel(a_ref, w_ref, x_ref, mod_ref, o_ref):
    p = jnp.dot(a_ref[...], w_ref[...].astype(BF16), preferred_element_type=F32)
    o_ref[...] = x_ref[...] + mod_ref[2:3, :] * p


def _outproj(merged, w_out, x2, mod, S):
    T, K = merged.shape
    D = w_out.shape[1]
    tm = _tile(S, 1024)
    tn = _tile(D, 512)
    per_b = S // tm
    return pl.pallas_call(
        _out_kernel,
        grid=(T // tm, D // tn),
        in_specs=[pl.BlockSpec((tm, K), lambda i, j: (i, 0)),
                  pl.BlockSpec((K, tn), lambda i, j: (0, j)),
                  pl.BlockSpec((tm, tn), lambda i, j: (i, j)),
                  pl.BlockSpec((None, 6, tn), lambda i, j: (i // per_b, 0, j))],
        out_specs=pl.BlockSpec((tm, tn), lambda i, j: (i, j)),
        out_shape=jax.ShapeDtypeStruct((T, D), F32),
        compiler_params=_cparams("arbitrary", "arbitrary"),
        name="outproj",
    )(merged, w_out, x2, mod)


def _router_kernel(x_ref, mod_ref, g_ref, whi_ref, wlo_ref, b_ref, h_ref, r_ref, cnt_ref, run_sc,
                   *, n_groups, per_group):
    @pl.when(jnp.logical_and(pl.program_id(0) == 0, pl.program_id(1) == 0))
    def _():
        run_sc[...] = jnp.zeros(run_sc.shape, F32)

    x = x_ref[...]
    y = x * lax.rsqrt(jnp.mean(x * x, axis=-1, keepdims=True) + EPS) * g_ref[...]
    h2 = y * (1.0 + mod_ref[4:5, :]) + mod_ref[3:4, :]
    h_ref[...] = h2
    hi = h2.astype(BF16)
    lo = (h2 - hi.astype(F32)).astype(BF16)
    logits = (jnp.dot(hi, whi_ref[...], preferred_element_type=F32)
              + jnp.dot(hi, wlo_ref[...], preferred_element_type=F32)
              + jnp.dot(lo, whi_ref[...], preferred_element_type=F32)) + b_ref[...]
    lane = lax.broadcasted_iota(jnp.int32, logits.shape, 1)
    big = jnp.int32(1 << 20)

    def top(vals):
        mx = jnp.max(vals, axis=1, keepdims=True)
        idx = jnp.min(jnp.where(vals == mx, lane, big), axis=1, keepdims=True)
        return mx, idx

    gl = jnp.where(lane < n_groups, logits, -jnp.inf)
    gmax, gidx = top(gl)
    g_p = 1.0 / jnp.sum(jnp.exp(gl - gmax), axis=1, keepdims=True)
    lo_lane = n_groups + per_group * gidx
    el = jnp.where((lane >= lo_lane) & (lane < lo_lane + per_group), logits, -jnp.inf)
    e1, i1 = top(el)
    e2, i2 = top(jnp.where(lane == i1, -jnp.inf, el))
    r = jnp.exp(e2 - e1)
    gate1 = g_p / (1.0 + r)
    gate2 = g_p * r / (1.0 + r)
    ex1 = i1 - n_groups
    ex2 = i2 - n_groups
    oh1 = (lane == ex1).astype(F32)
    oh2 = (lane == ex2).astype(F32)
    both = oh1 + oh2
    tm = x.shape[0]
    earlier = (lax.broadcasted_iota(jnp.int32, (tm, tm), 1)
               < lax.broadcasted_iota(jnp.int32, (tm, tm), 0)).astype(BF16)
    prefix = jnp.dot(earlier, both.astype(BF16), preferred_element_type=F32) + run_sc[0:1, :]
    rank1 = jnp.sum(oh1 * prefix, axis=1, keepdims=True)
    rank2 = jnp.sum(oh2 * prefix, axis=1, keepdims=True)
    run_sc[...] = run_sc[...] + jnp.sum(both, axis=0, keepdims=True)
    cnt_ref[...] = run_sc[...]

    cols = (ex1.astype(F32), ex2.astype(F32), gate1, gate2, rank1, rank2)
    out = jnp.zeros(logits.shape, F32)
    for n, col in enumerate(cols):
        out = jnp.where(lane == n, col, out)
    r_ref[...] = out


def _router(x1, mod, g, w_group, b_group, w_expert, b_expert, B, S):
    T, D = x1.shape
    NG = w_group.shape[1]
    NE = w_expert.shape[1]
    assert NG + NE <= LANES
    pad = LANES - NG - NE
    w = jnp.concatenate([w_group, w_expert, jnp.zeros((D, pad), F32)], axis=1)
    w_hi = w.astype(BF16)
    w_lo = (w - w_hi.astype(F32)).astype(BF16)
    bias = jnp.concatenate([b_group, b_expert, jnp.full((pad,), -jnp.inf, F32)]).reshape(1, LANES)
    tm = _tile(S, 256)
    nb = S // tm
    kern = functools.partial(_router_kernel, n_groups=NG, per_group=NE // NG)
    return pl.pallas_call(
        kern,
        grid=(B, nb),
        in_specs=[pl.BlockSpec((tm, D), lambda b, i: (b * nb + i, 0)),
                  pl.BlockSpec((None, 6, D), lambda b, i: (b, 0, 0)),
                  pl.BlockSpec((1, D), lambda b, i: (0, 0)),
                  pl.BlockSpec((D, LANES), lambda b, i: (0, 0)),
                  pl.BlockSpec((D, LANES), lambda b, i: (0, 0)),
                  pl.BlockSpec((1, LANES), lambda b, i: (0, 0))],
        out_specs=[pl.BlockSpec((tm, D), lambda b, i: (b * nb + i, 0)),
                   pl.BlockSpec((tm, LANES), lambda b, i: (b * nb + i, 0)),
                   pl.BlockSpec((8, LANES), lambda b, i: (0, 0))],
        out_shape=[jax.ShapeDtypeStruct((T, D), F32), jax.ShapeDtypeStruct((T, LANES), F32),
                   jax.ShapeDtypeStruct((8, LANES), F32)],
        scratch_shapes=[pltpu.VMEM((8, LANES), F32)],
        compiler_params=_cparams("arbitrary", "arbitrary"),
        name="router",
    )(x1, mod, g.reshape(1, D), w_hi, w_lo, bias)


def _gather_rows(idx_ref, base, src_hbm, dst_ref, sem, n):
    def body(r, carry):
        tok = idx_ref[base + r]
        pltpu.make_async_copy(src_hbm.at[pl.ds(tok, 1), :], dst_ref.at[pl.ds(r, 1), :], sem).start()
        return carry
    lax.fori_loop(0, n, body, 0)


def _wait_rows(src_hbm, dst_ref, sem, n):
    def body(r, carry):
        pltpu.make_async_copy(src_hbm.at[pl.ds(0, 1), :], dst_ref.at[pl.ds(r, 1), :], sem).wait()
        return carry
    lax.fori_loop(0, n, body, 0)


def _scatter_kernel(dest_ref, pend_ref, h_ref, xs_hbm, zbuf, sem, zsem, *, tm, tb, n_experts, n_blocks):
    i = pl.program_id(0)

    def zero_block(row0):
        return pltpu.make_async_copy(zbuf, xs_hbm.at[pl.ds(pl.multiple_of(row0, tb), tb), :], zsem)

    def zero_copy(e):
        return zero_block(jnp.maximum(pend_ref[e] - tb, 0))

    @pl.when(i == 0)
    def _():
        zbuf[...] = jnp.zeros(zbuf.shape, F32)

        def start(e, carry):
            @pl.when(pend_ref[e] > 0)
            def _():
                zero_copy(e).start()
            return carry

        def wait(e, carry):
            @pl.when(pend_ref[e] > 0)
            def _():
                zero_copy(e).wait()
            return carry

        def start_tail(b, carry):
            zero_block(b * tb).start()
            return carry

        def wait_tail(b, carry):
            zero_block(b * tb).wait()
            return carry

        first_unused = pend_ref[n_experts - 1] // tb
        lax.fori_loop(0, n_experts, start, 0)
        lax.fori_loop(first_unused, n_blocks, start_tail, 0)
        lax.fori_loop(0, n_experts, wait, 0)
        lax.fori_loop(first_unused, n_blocks, wait_tail, 0)

    def row_copy(r):
        return pltpu.make_async_copy(h_ref.at[pl.ds(r // 2, 1), :],
                                     xs_hbm.at[pl.ds(dest_ref[i * 2 * tm + r], 1), :], sem)

    def start_row(r, carry):
        row_copy(r).start()
        return carry

    def wait_row(r, carry):
        row_copy(r).wait()
        return carry

    lax.fori_loop(0, 2 * tm, start_row, 0)
    lax.fori_loop(0, 2 * tm, wait_row, 0)


def _scatter(h2, dest, pend, n_slots, tb):
    T, D = h2.shape
    tm = _tile(T, 256)
    grid_spec = pltpu.PrefetchScalarGridSpec(
        num_scalar_prefetch=2,
        grid=(T // tm,),
        in_specs=[pl.BlockSpec((tm, D), lambda i, de, pe: (i, 0))],
        out_specs=pl.BlockSpec(memory_space=pl.ANY),
        scratch_shapes=[pltpu.VMEM((tb, D), F32), pltpu.SemaphoreType.DMA(()), pltpu.SemaphoreType.DMA(())],
    )
    return pl.pallas_call(
        functools.partial(_scatter_kernel, tm=tm, tb=tb, n_experts=pend.shape[0],
                          n_blocks=n_slots // tb),
        grid_spec=grid_spec,
        out_shape=jax.ShapeDtypeStruct((n_slots, D), F32),
        compiler_params=_cparams("arbitrary"),
        name="scatter",
    )(dest, pend, h2)


def _moe_kernel(be_ref, nused_ref, x_ref, wgu_ref, wd_ref, y_ref, *, de):
    i = pl.program_id(0)
    nused = nused_ref[0]

    @pl.when(i < nused)
    def _():
        gu = jnp.dot(x_ref[...].astype(BF16), wgu_ref[...], preferred_element_type=F32)
        g = gu[:, :de]
        u = gu[:, de:]
        act = (g * jax.nn.sigmoid(g) * u).astype(BF16)
        y_ref[...] = jnp.dot(act, wd_ref[...], preferred_element_type=F32)

    @pl.when(i >= nused)
    def _():
        y_ref[...] = jnp.zeros(y_ref.shape, F32)


def _moe(xs, block_e, nused, wgu, wd, tb):
    n_slots, D = xs.shape
    E, _, de2 = wgu.shape
    de = de2 // 2
    n_blocks = n_slots // tb
    grid_spec = pltpu.PrefetchScalarGridSpec(
        num_scalar_prefetch=2,
        grid=(n_blocks,),
        in_specs=[pl.BlockSpec((tb, D), lambda i, be, nu: (jnp.minimum(i, nu[0] - 1), 0)),
                  pl.BlockSpec((None, D, de2), lambda i, be, nu: (be[i], 0, 0)),
                  pl.BlockSpec((None, de, D), lambda i, be, nu: (be[i], 0, 0))],
        out_specs=pl.BlockSpec((tb, D), lambda i, be, nu: (i, 0)),
    )
    return pl.pallas_call(
        functools.partial(_moe_kernel, de=de),
        grid_spec=grid_spec,
        out_shape=jax.ShapeDtypeStruct((n_slots, D), F32),
        compiler_params=_cparams("arbitrary"),
        name="moe",
    )(block_e, nused, xs, wgu, wd)


def _final_kernel(pos_ref, y_hbm, x_ref, r_ref, mod_ref, g_ref, o_ref, ybuf, sem, *, tm, nsteps):
    i = pl.program_id(0)
    slot = lax.rem(i, 2)

    @pl.when(i == 0)
    def _():
        _gather_rows(pos_ref, 0, y_hbm, ybuf.at[0], sem.at[0], 2 * tm)

    @pl.when(i + 1 < nsteps)
    def _():
        _gather_rows(pos_ref, (i + 1) * 2 * tm, y_hbm, ybuf.at[1 - slot], sem.at[1 - slot], 2 * tm)

    _wait_rows(y_hbm, ybuf.at[slot], sem.at[slot], 2 * tm)
    yb = ybuf[slot]
    r = r_ref[...]
    lane = lax.broadcasted_iota(jnp.int32, r.shape, 1)
    gate1 = jnp.sum(jnp.where(lane == 2, r, 0.0), axis=1, keepdims=True)
    gate2 = jnp.sum(jnp.where(lane == 3, r, 0.0), axis=1, keepdims=True)
    x = x_ref[...] + mod_ref[5:6, :] * (gate1 * yb[:tm] + gate2 * yb[tm:])
    o_ref[...] = x * lax.rsqrt(jnp.mean(x * x, axis=-1, keepdims=True) + EPS) * g_ref[...]


def _final(y_slots, dest, route, x1, mod, g, S):
    T, D = x1.shape
    tm = _tile(S, 128)
    per_b = S // tm
    nsteps = T // tm
    pos_tiled = jnp.transpose(dest.reshape(nsteps, tm, 2), (0, 2, 1)).reshape(-1)
    grid_spec = pltpu.PrefetchScalarGridSpec(
        num_scalar_prefetch=1,
        grid=(nsteps,),
        in_specs=[pl.BlockSpec(memory_space=pl.ANY),
                  pl.BlockSpec((tm, D), lambda i, p: (i, 0)),
                  pl.BlockSpec((tm, LANES), lambda i, p: (i, 0)),
                  pl.BlockSpec((None, 6, D), lambda i, p: (i // per_b, 0, 0)),
                  pl.BlockSpec((1, D), lambda i, p: (0, 0))],
        out_specs=pl.BlockSpec((tm, D), lambda i, p: (i, 0)),
        scratch_shapes=[pltpu.VMEM((2, 2 * tm, D), F32), pltpu.SemaphoreType.DMA((2,))],
    )
    return pl.pallas_call(
        functools.partial(_final_kernel, tm=tm, nsteps=nsteps),
        grid_spec=grid_spec,
        out_shape=jax.ShapeDtypeStruct((T, D), F32),
        compiler_params=_cparams("arbitrary"),
        name="final",
    )(pos_tiled, y_slots, x1, route, mod, g.reshape(1, D))


def _dispatch(route, counts, n_experts, tb):
    T = route.shape[0]
    ids = route[:, 0:2].astype(jnp.int32)
    rank = route[:, 4:6].astype(jnp.int32)
    counts = counts[0, :n_experts].astype(jnp.int32)
    padded = (counts + tb - 1) // tb * tb
    pend = jnp.cumsum(padded).astype(jnp.int32)
    pstart = pend - padded
    onehot = ids[:, :, None] == jnp.arange(n_experts, dtype=jnp.int32)
    dest = (rank + jnp.sum(jnp.where(onehot, pstart, 0), axis=-1)).reshape(-1)
    n_blocks = -(-2 * T // tb) + n_experts
    first_row = jnp.arange(n_blocks, dtype=jnp.int32) * tb
    block_e = jnp.minimum(jnp.sum(pend[None, :] <= first_row[:, None], axis=1), n_experts - 1)
    nused = (pend[-1] // tb).reshape(1)
    return dest, pend, block_e.astype(jnp.int32), nused, n_blocks * tb


def _layer(x2, c, lidx, B, S, w_ada, b_ada, norm1_g, w_in, conv_qk, b_if, da_lambda, da_norm_g, rel_bias,
           ml_norm_g, w_o_attn, w_o_mlstm, w_out, norm2_g, w_group, b_group, w_expert, b_expert,
           w_gate_up, w_down):
    D = x2.shape[1]
    H = rel_bias.shape[1]
    d = da_lambda.shape[1]
    da_w = H * 2 * d
    MH, dh = ml_norm_g.shape
    ml_w = MH * dh
    n_experts = w_expert.shape[1]

    mod = _ada(c, w_ada, b_ada)

    off_mlq = 3 * da_w
    off_mlv = off_mlq + 2 * ml_w
    off_if = off_mlv + 2 * ml_w
    off_g = off_if + 2 * MH
    h, gates_if = _norm1(x2, mod, norm1_g, w_in[:, off_if:off_g], b_if, B, S)
    q_scale = jnp.concatenate([jnp.full((da_w,), d ** -0.5 * math.log2(math.e), F32),
                               jnp.ones((2 * da_w,), F32)])
    da_qkv = _proj(h, w_in, 0, 3 * da_w, BF16, "proj_da", col_scale=q_scale)
    ml_qk = _proj(h, w_in, off_mlq, 2 * ml_w, F32, "proj_mlqk")
    ml_vo = _proj(h, w_in, off_mlv, 2 * ml_w, BF16, "proj_mlvo")
    gates_am = _proj(h, w_in[:, off_g:], 0, 2 * D, BF16, "proj_gates")

    lambda_init = 0.8 - 0.6 * math.exp(-0.3 * lidx)
    lamp = da_lambda.astype(F32)
    lam = jnp.exp(jnp.sum(lamp[0] * lamp[1])) - jnp.exp(jnp.sum(lamp[2] * lamp[3])) + lambda_init
    attn = _attn(da_qkv, lam, rel_bias, da_norm_g, B, S, H, d, lambda_init)

    mqk = _conv(ml_qk, conv_qk, B, S, ml_w, dh ** -0.5)
    hm = _mlstm(mqk, ml_vo, gates_if, ml_norm_g, B, S, MH, dh)

    merged = _merge(attn, hm, w_o_attn, w_o_mlstm, gates_am)
    x1 = _outproj(merged, w_out, x2, mod, S)

    h2, route, counts = _router(x1, mod, norm2_g, w_group, b_group, w_expert, b_expert, B, S)
    tb = 256
    dest, pend, block_e, nused, n_slots = _dispatch(route, counts, n_experts, tb)
    xs = _scatter(h2, dest, pend, n_slots, tb)
    y_slots = _moe(xs, block_e, nused, w_gate_up.astype(BF16), w_down.astype(BF16), tb)
    return y_slots, dest, route, x1, mod


def kernel(x, c, w_ada, b_ada, norm1_g, w_in, conv_qk, b_if, da_lambda, da_norm_g, rel_bias, ml_norm_g,
           w_o_attn, w_o_mlstm, w_out, norm2_g, w_group, b_group, w_expert, b_expert, w_gate_up, w_down,
           normf_g):
    B, S, D = x.shape
    assert w_ada.shape[0] == 1, "the final rmsnorm is fused into the layer's last kernel: one layer only"
    l = 0
    y_slots, dest, route, x1, mod = _layer(
        x.reshape(B * S, D), c, l, B, S, w_ada[l], b_ada[l], norm1_g[l], w_in[l], conv_qk[l], b_if[l],
        da_lambda[l], da_norm_g[l], rel_bias, ml_norm_g[l], w_o_attn[l], w_o_mlstm[l], w_out[l],
        norm2_g[l], w_group[l], b_group[l], w_expert[l], b_expert[l], w_gate_up[l], w_down[l])
    return _final(y_slots, dest, route, x1, mod, normf_g, S).reshape(B, S, D)
```

```python
import functools
import math

import jax
import jax.numpy as jnp
from jax import lax
from jax.experimental import pallas as pl
from jax.experimental.pallas import tpu as pltpu

F32 = jnp.float32
BF16 = jnp.bfloat16
EPS = 1e-6
REL_MAX_DIST = 128
ML_CHUNK = 128
NEG_BIG = -1e30
V7X_VMEM_LIMIT = 56 * 1024 * 1024
LANES = 128
DMA_UNROLL = 8


def _cparams(*sem):
    return pltpu.CompilerParams(dimension_semantics=sem, vmem_limit_bytes=V7X_VMEM_LIMIT)


def _tile(n, pref):
    t = min(n, pref)
    while n % t:
        t //= 2
    return t


def _nt_dot(a, b):
    return lax.dot_general(a, b, (((1,), (1,)), ((), ())), preferred_element_type=F32)


def _tn_dot(a, b):
    return lax.dot_general(a, b, (((0,), (0,)), ((), ())), preferred_element_type=F32)


def _ada_kernel(c_ref, w_ref, b_ref, o_ref):
    c = c_ref[...]
    s = (c * jax.nn.sigmoid(c)).astype(BF16)
    o_ref[...] = jnp.dot(s, w_ref[...].astype(BF16), preferred_element_type=F32) + b_ref[...]


def _ada(c, w_ada, b_ada):
    B, D = c.shape
    N = w_ada.shape[1]
    rows = 8
    c8 = jnp.zeros((rows, D), F32).at[:B].set(c)
    tn = _tile(N, 512)
    mod = pl.pallas_call(
        _ada_kernel,
        grid=(N // tn,),
        in_specs=[pl.BlockSpec((rows, D), lambda j: (0, 0)),
                  pl.BlockSpec((D, tn), lambda j: (0, j)),
                  pl.BlockSpec((1, tn), lambda j: (0, j))],
        out_specs=pl.BlockSpec((rows, tn), lambda j: (0, j)),
        out_shape=jax.ShapeDtypeStruct((rows, N), F32),
        compiler_params=_cparams("arbitrary"),
        name="ada",
    )(c8, w_ada, b_ada.reshape(1, N))
    return mod[:B].reshape(B, 6, D)


def _norm1_kernel(x_ref, mod_ref, g_ref, wif_ref, bif_ref, h_ref, gate_ref):
    x = x_ref[...]
    y = x * lax.rsqrt(jnp.mean(x * x, axis=-1, keepdims=True) + EPS) * g_ref[...]
    h = (y * (1.0 + mod_ref[1:2, :]) + mod_ref[0:1, :]).astype(BF16)
    h_ref[...] = h
    gate_ref[...] = _nt_dot(h, wif_ref[...].astype(BF16)) + bif_ref[...]


def _norm1(x2, mod, g, w_if_t, b_if, B, S):
    T, D = x2.shape
    G = w_if_t.shape[0]
    tm = _tile(S, 256)
    nb = S // tm
    return pl.pallas_call(
        _norm1_kernel,
        grid=(B, nb),
        in_specs=[pl.BlockSpec((tm, D), lambda b, i: (b * nb + i, 0)),
                  pl.BlockSpec((None, 6, D), lambda b, i: (b, 0, 0)),
                  pl.BlockSpec((1, D), lambda b, i: (0, 0)),
                  pl.BlockSpec((G, D), lambda b, i: (0, 0)),
                  pl.BlockSpec((1, G), lambda b, i: (0, 0))],
        out_specs=[pl.BlockSpec((tm, D), lambda b, i: (b * nb + i, 0)),
                   pl.BlockSpec((tm, G), lambda b, i: (b * nb + i, 0))],
        out_shape=[jax.ShapeDtypeStruct((T, D), BF16), jax.ShapeDtypeStruct((T, G), F32)],
        compiler_params=_cparams("arbitrary", "arbitrary"),
        name="norm1",
    )(x2, mod, g.reshape(1, D), w_if_t, b_if.reshape(1, G))


def _proj_kernel(h_ref, w_ref, o_ref):
    o_ref[...] = _nt_dot(h_ref[...], w_ref[...].astype(BF16)).astype(o_ref.dtype)


def _proj_scaled_kernel(h_ref, w_ref, s_ref, o_ref):
    acc = _nt_dot(h_ref[...], w_ref[...].astype(BF16))
    o_ref[...] = (acc * s_ref[...]).astype(o_ref.dtype)


def _proj(h, w_t, col0, ncols, out_dtype, name, col_scale=None):
    M, K = h.shape
    tm = _tile(M, 1024)
    tn = _tile(math.gcd(ncols, col0) if col0 else ncols, 512)
    j0 = col0 // tn
    in_specs = [pl.BlockSpec((tm, K), lambda i, j: (i, 0)),
                pl.BlockSpec((tn, K), lambda i, j: (j0 + j, 0))]
    args = (h, w_t)
    body = _proj_kernel
    if col_scale is not None:
        in_specs.append(pl.BlockSpec((1, tn), lambda i, j: (0, j)))
        args = (h, w_t, col_scale.reshape(1, ncols))
        body = _proj_scaled_kernel
    return pl.pallas_call(
        body,
        grid=(M // tm, ncols // tn),
        in_specs=in_specs,
        out_specs=pl.BlockSpec((tm, tn), lambda i, j: (i, j)),
        out_shape=jax.ShapeDtypeStruct((M, ncols), out_dtype),
        compiler_params=_cparams("arbitrary", "arbitrary"),
        name=name,
    )(*args)


def _conv_kernel(u_ref, w_ref, o_ref, ext_ref, *, width, kcol0, kscale, tc):
    s = pl.program_id(2)
    ts = u_ref.shape[0]

    @pl.when(s == 0)
    def _():
        ext_ref[0:8, :] = jnp.zeros((8, tc), F32)

    u = u_ref[...]
    ext_ref[8:, :] = u
    acc = u * w_ref[width - 1:width, :]
    for t in range(1, width):
        acc = acc + ext_ref[8 - t:8 - t + ts, :] * w_ref[width - 1 - t:width - t, :]
    ext_ref[0:8, :] = u[ts - 8:, :]
    y = acc * jax.nn.sigmoid(acc)
    scale = jnp.where(pl.program_id(1) * tc >= kcol0, kscale, 1.0).astype(F32)
    o_ref[...] = (y * scale).astype(o_ref.dtype)


def _conv(u, w, B, S, kcol0, kscale):
    T, C = u.shape
    width = w.shape[0]
    ts = _tile(S, 512)
    tc = _tile(math.gcd(C, kcol0), 512)
    ns = S // ts
    return pl.pallas_call(
        functools.partial(_conv_kernel, width=width, kcol0=kcol0, kscale=kscale, tc=tc),
        grid=(B, C // tc, ns),
        in_specs=[pl.BlockSpec((ts, tc), lambda b, c, s: (b * ns + s, c)),
                  pl.BlockSpec((width, tc), lambda b, c, s: (0, c))],
        out_specs=pl.BlockSpec((ts, tc), lambda b, c, s: (b * ns + s, c)),
        out_shape=jax.ShapeDtypeStruct((T, C), BF16),
        scratch_shapes=[pltpu.VMEM((ts + 8, tc), F32)],
        compiler_params=_cparams("arbitrary", "arbitrary", "arbitrary"),
        name="conv",
    )(u, w)


def _attn_kernel(lam_ref, q_ref, k_ref, v_ref, bd_ref, bp_ref, g_ref, o_ref, m_sc, l_sc, acc_sc,
                 *, d, t, out_scale):
    i = pl.program_id(2)
    q = q_ref[...]
    ngrp = t // LANES
    m_sc[...] = jnp.full(m_sc.shape, NEG_BIG, F32)
    l_sc[...] = jnp.zeros(l_sc.shape, F32)
    acc_sc[...] = jnp.zeros(acc_sc.shape, F32)

    def step(j, bias_ref):
        r0 = pl.multiple_of(j * t, t)
        k = k_ref[pl.ds(r0, t), :]
        v = v_ref[pl.ds(r0, t), :]
        for m in range(2):
            s = _nt_dot(q[:, m * d:(m + 1) * d], k[:, m * d:(m + 1) * d])
            if bias_ref is not None:
                s = s + bias_ref[...]
            grp = [s[:, g * LANES:(g + 1) * LANES] for g in range(ngrp)]
            m_old = m_sc[m]
            row_max = jnp.max(functools.reduce(jnp.maximum, grp), axis=1, keepdims=True)
            m_new = jnp.maximum(m_old, row_max)
            alpha = jnp.exp2(m_old - m_new)
            p = [jnp.exp2(x - m_new) for x in grp]
            l_sc[m] = alpha * l_sc[m] + functools.reduce(jnp.add, p)
            pb = jnp.concatenate([x.astype(BF16) for x in p], axis=1)
            pv = jnp.dot(pb, v, preferred_element_type=F32)
            acc_sc[m] = jnp.concatenate([alpha] * (2 * d // LANES), axis=1) * acc_sc[m] + pv
            m_sc[m] = m_new

    def far(j, carry):
        step(j, None)
        return carry

    lax.fori_loop(0, jnp.maximum(i - 1, 0), far, 0)

    @pl.when(i > 0)
    def _():
        step(i - 1, bp_ref)

    step(i, bd_ref)

    l0 = jnp.sum(l_sc[0], axis=1, keepdims=True)
    l1 = jnp.sum(l_sc[1], axis=1, keepdims=True)
    a = acc_sc[0] / l0 - lam_ref[0] * (acc_sc[1] / l1)
    y = a * lax.rsqrt(jnp.mean(a * a, axis=-1, keepdims=True) + EPS) * g_ref[...]
    o_ref[...] = (y * out_scale).astype(o_ref.dtype)


def _rel_bucket(n, n_buckets):
    max_exact = n_buckets // 2
    nf = jnp.maximum(n, 1).astype(F32)
    large = max_exact + (jnp.log(nf / max_exact) / math.log(REL_MAX_DIST / max_exact)
                         * (n_buckets - max_exact)).astype(jnp.int32)
    large = jnp.minimum(large, n_buckets - 1)
    return jnp.where(n < max_exact, n, large)


def _bias_table(rel, dist, n_buckets):
    bucket = _rel_bucket(dist, n_buckets)
    out = jnp.zeros((rel.shape[1],) + dist.shape, F32)
    for b in range(n_buckets):
        out = jnp.where((bucket == b)[None], rel[b][:, None, None], out)
    return out


def _attn(qkv, lam, rel_bias, norm_g, B, S, H, d, lambda_init):
    T = qkv.shape[0]
    t = _tile(S, 512)
    assert t >= REL_MAX_DIST, "key blocks two or more tiles back must sit in the saturated bucket"
    assert d % LANES == 0
    nq = S // t
    n_buckets = rel_bias.shape[0]
    rel = rel_bias.astype(F32)
    rel = (rel - rel[n_buckets - 1]) * math.log2(math.e)
    dist_d = jnp.arange(t)[:, None] - jnp.arange(t)[None, :]
    bd = jnp.where((dist_d >= 0)[None], _bias_table(rel, jnp.maximum(dist_d, 0), n_buckets), NEG_BIG)
    bp = _bias_table(rel, dist_d + t, n_buckets)
    kern = functools.partial(_attn_kernel, d=d, t=t, out_scale=1.0 - lambda_init)
    return pl.pallas_call(
        kern,
        grid=(B, H, nq),
        in_specs=[pl.BlockSpec(memory_space=pltpu.SMEM),
                  pl.BlockSpec((t, 2 * d), lambda b, h, i: (b * nq + i, h)),
                  pl.BlockSpec((S, 2 * d), lambda b, h, i: (b, H + h)),
                  pl.BlockSpec((S, 2 * d), lambda b, h, i: (b, 2 * H + h)),
                  pl.BlockSpec((None, t, t), lambda b, h, i: (h, 0, 0)),
                  pl.BlockSpec((None, t, t), lambda b, h, i: (h, 0, 0)),
                  pl.BlockSpec((1, 2 * d), lambda b, h, i: (0, 0))],
        out_specs=pl.BlockSpec((t, 2 * d), lambda b, h, i: (b * nq + i, h)),
        out_shape=jax.ShapeDtypeStruct((T, H * 2 * d), BF16),
        scratch_shapes=[pltpu.VMEM((2, t, LANES), F32), pltpu.VMEM((2, t, LANES), F32),
                        pltpu.VMEM((2, t, 2 * d), F32)],
        compiler_params=_cparams("arbitrary", "arbitrary", "arbitrary"),
        name="attn",
    )(lam.reshape(1), qkv, qkv, qkv, bd, bp, norm_g.reshape(1, 2 * d))


def _mlstm_kernel(q_ref, k_ref, v_ref, og_ref, gcol_ref, grow_ref, g_ref, o_ref, c_sc, n_sc, m_sc,
                  *, nheads):
    h = pl.program_id(1)
    L = q_ref.shape[0]

    @pl.when(pl.program_id(2) == 0)
    def _():
        c_sc[...] = jnp.zeros(c_sc.shape, F32)
        n_sc[...] = jnp.zeros(n_sc.shape, F32)
        m_sc[...] = jnp.zeros(m_sc.shape, F32)

    gcol = gcol_ref[...]
    lane = lax.broadcasted_iota(jnp.int32, gcol.shape, 1)
    i_col = jnp.sum(jnp.where(lane == h, gcol, 0.0), axis=1, keepdims=True)
    f_col = jax.nn.log_sigmoid(jnp.sum(jnp.where(lane == h + nheads, gcol, 0.0), axis=1, keepdims=True))
    grow = grow_ref[...]
    sub = lax.broadcasted_iota(jnp.int32, grow.shape, 0)
    i_row = jnp.sum(jnp.where(sub == h, grow, 0.0), axis=0, keepdims=True)
    f_row = jax.nn.log_sigmoid(jnp.sum(jnp.where(sub == h + nheads, grow, 0.0), axis=0, keepdims=True))

    jj = lax.broadcasted_iota(jnp.int32, (L, L), 0)
    ss = lax.broadcasted_iota(jnp.int32, (L, L), 1)
    tril = ss <= jj
    b_col = jnp.sum(jnp.where(tril, f_row, 0.0), axis=1, keepdims=True)
    b_row = jnp.sum(jnp.where(jj <= ss, f_col, 0.0), axis=0, keepdims=True)
    u_row = i_row - b_row
    u_col = i_col - b_col

    m_prev = m_sc[0:1, 0:1]
    mm_col = jnp.maximum(m_prev, jnp.max(jnp.where(tril, u_row, NEG_BIG), axis=1, keepdims=True))
    w = jnp.exp(jnp.where(tril, u_row - mm_col, NEG_BIG))
    a_inter = jnp.exp(m_prev - mm_col)

    q = q_ref[...]
    k = k_ref[...]
    v = v_ref[...]
    c_old = c_sc[...]
    n_old = n_sc[...]
    s_qk = _nt_dot(q, k) * w
    num = a_inter * _nt_dot(q, c_old.astype(BF16)) + jnp.dot(s_qk.astype(BF16), v, preferred_element_type=F32)
    den = (a_inter * jnp.sum(q.astype(F32) * n_old, axis=1, keepdims=True)
           + jnp.sum(s_qk, axis=1, keepdims=True))
    hh = num / jnp.maximum(jnp.abs(den), jnp.exp(-(b_col + mm_col)))

    mm_last = mm_col[L - 1:L, :]
    w_s = jnp.exp(u_col - mm_last)
    a_state = jnp.exp(m_prev - mm_last)
    c_sc[...] = a_state * c_old + _tn_dot((v.astype(F32) * w_s).astype(BF16), k)
    n_sc[...] = a_state * n_old + jnp.sum(k.astype(F32) * w_s, axis=0, keepdims=True)
    m_sc[...] = jnp.broadcast_to(b_col[L - 1:L, :] + mm_last, m_sc.shape)

    y = hh * lax.rsqrt(jnp.mean(hh * hh, axis=-1, keepdims=True) + EPS) * g_ref[...]
    o_ref[...] = (y * jax.nn.sigmoid(og_ref[...].astype(F32))).astype(o_ref.dtype)


def _mlstm(mqk, mvo, gates, norm_g, B, S, H, dh):
    T = mqk.shape[0]
    L = ML_CHUNK
    nc = S // L
    grow = jnp.transpose(gates.reshape(B, S, 2 * H), (0, 2, 1))
    row = lambda b, h, c: (b * nc + c, h)
    row_hi = lambda b, h, c: (b * nc + c, H + h)
    return pl.pallas_call(
        functools.partial(_mlstm_kernel, nheads=H),
        grid=(B, H, nc),
        in_specs=[pl.BlockSpec((L, dh), row),
                  pl.BlockSpec((L, dh), row_hi),
                  pl.BlockSpec((L, dh), row),
                  pl.BlockSpec((L, dh), row_hi),
                  pl.BlockSpec((L, 2 * H), lambda b, h, c: (b * nc + c, 0)),
                  pl.BlockSpec((None, 2 * H, L), lambda b, h, c: (b, 0, c)),
                  pl.BlockSpec((None, 1, dh), lambda b, h, c: (h, 0, 0))],
        out_specs=pl.BlockSpec((L, dh), row),
        out_shape=jax.ShapeDtypeStruct((T, H * dh), BF16),
        scratch_shapes=[pltpu.VMEM((dh, dh), F32), pltpu.VMEM((1, dh), F32), pltpu.VMEM((8, LANES), F32)],
        compiler_params=_cparams("arbitrary", "arbitrary", "arbitrary"),
        name="mlstm",
    )(mqk, mqk, mvo, mvo, gates, grow, norm_g.reshape(H, 1, dh))


def _merge_kernel(a_ref, m_ref, wa_ref, wm_ref, ga_ref, gm_ref, o_ref):
    pa = jnp.dot(a_ref[...], wa_ref[...].astype(BF16), preferred_element_type=F32)
    pm = jnp.dot(m_ref[...], wm_ref[...].astype(BF16), preferred_element_type=F32)
    out = (jax.nn.sigmoid(ga_ref[...].astype(F32)) * pa + jax.nn.sigmoid(gm_ref[...].astype(F32)) * pm)
    o_ref[...] = out.astype(o_ref.dtype)


def _merge(attn, hm, w_a, w_m, gates_am):
    T, Ka = attn.shape
    Km = hm.shape[1]
    D = w_a.shape[1]
    tm = _tile(T, 1024)
    tn = _tile(D, 256)
    nj = D // tn
    return pl.pallas_call(
        _merge_kernel,
        grid=(T // tm, nj),
        in_specs=[pl.BlockSpec((tm, Ka), lambda i, j: (i, 0)),
                  pl.BlockSpec((tm, Km), lambda i, j: (i, 0)),
                  pl.BlockSpec((Ka, tn), lambda i, j: (0, j)),
                  pl.BlockSpec((Km, tn), lambda i, j: (0, j)),
                  pl.BlockSpec((tm, tn), lambda i, j: (i, j)),
                  pl.BlockSpec((tm, tn), lambda i, j: (i, nj + j))],
        out_specs=pl.BlockSpec((tm, tn), lambda i, j: (i, j)),
        out_shape=jax.ShapeDtypeStruct((T, D), BF16),
        compiler_params=_cparams("arbitrary", "arbitrary"),
        name="merge",
    )(attn, hm, w_a, w_m, gates_am, gates_am)


def _out_kernel(a_ref, w_ref, x_ref, mod_ref, o_ref):
    p = jnp.dot(a_ref[...], w_ref[...].astype(BF16), preferred_element_type=F32)
    o_ref[...] = x_ref[...] + mod_ref[2:3, :] * p


def _outproj(merged, w_out, x2, mod, S):
    T, K = merged.shape
    D = w_out.shape[1]
    tm = _tile(S, 1024)
    tn = _tile(D, 512)
    per_b = S // tm
    return pl.pallas_call(
        _out_kernel,
        grid=(T // tm, D // tn),
        in_specs=[pl.BlockSpec((tm, K), lambda i, j: (i, 0)),
                  pl.BlockSpec((K, tn), lambda i, j: (0, j)),
                  pl.BlockSpec((tm, tn), lambda i, j: (i, j)),
                  pl.BlockSpec((None, 6, tn), lambda i, j: (i // per_b, 0, j))],
        out_specs=pl.BlockSpec((tm, tn), lambda i, j: (i, j)),
        out_shape=jax.ShapeDtypeStruct((T, D), F32),
        compiler_params=_cparams("arbitrary", "arbitrary"),
        name="outproj",
    )(merged, w_out, x2, mod)


def _router_kernel(x_ref, mod_ref, g_ref, whi_ref, wlo_ref, b_ref, h_ref, r_ref, cnt_ref, run_sc,
                   *, n_groups, per_group):
    @pl.when(jnp.logical_and(pl.program_id(0) == 0, pl.program_id(1) == 0))
    def _():
        run_sc[...] = jnp.zeros(run_sc.shape, F32)

    x = x_ref[...]
    y = x * lax.rsqrt(jnp.mean(x * x, axis=-1, keepdims=True) + EPS) * g_ref[...]
    h2 = y * (1.0 + mod_ref[4:5, :]) + mod_ref[3:4, :]
    h_ref[...] = h2
    hi = h2.astype(BF16)
    lo = (h2 - hi.astype(F32)).astype(BF16)
    logits = (jnp.dot(hi, whi_ref[...], preferred_element_type=F32)
              + jnp.dot(hi, wlo_ref[...], preferred_element_type=F32)
              + jnp.dot(lo, whi_ref[...], preferred_element_type=F32)) + b_ref[...]
    lane = lax.broadcasted_iota(jnp.int32, logits.shape, 1)
    big = jnp.int32(1 << 20)

    def top(vals):
        mx = jnp.max(vals, axis=1, keepdims=True)
        idx = jnp.min(jnp.where(vals == mx, lane, big), axis=1, keepdims=True)
        return mx, idx

    gl = jnp.where(lane < n_groups, logits, -jnp.inf)
    gmax, gidx = top(gl)
    g_p = 1.0 / jnp.sum(jnp.exp(gl - gmax), axis=1, keepdims=True)
    lo_lane = n_groups + per_group * gidx
    el = jnp.where((lane >= lo_lane) & (lane < lo_lane + per_group), logits, -jnp.inf)
    e1, i1 = top(el)
    e2, i2 = top(jnp.where(lane == i1, -jnp.inf, el))
    r = jnp.exp(e2 - e1)
    gate1 = g_p / (1.0 + r)
    gate2 = g_p * r / (1.0 + r)
    ex1 = i1 - n_groups
    ex2 = i2 - n_groups
    oh1 = (lane == ex1).astype(F32)
    oh2 = (lane == ex2).astype(F32)
    both = oh1 + oh2
    tm = x.shape[0]
    earlier = (lax.broadcasted_iota(jnp.int32, (tm, tm), 1)
               < lax.broadcasted_iota(jnp.int32, (tm, tm), 0)).astype(BF16)
    prefix = jnp.dot(earlier, both.astype(BF16), preferred_element_type=F32) + run_sc[0:1, :]
    rank1 = jnp.sum(oh1 * prefix, axis=1, keepdims=True)
    rank2 = jnp.sum(oh2 * prefix, axis=1, keepdims=True)
    run_sc[...] = run_sc[...] + jnp.sum(both, axis=0, keepdims=True)
    cnt_ref[...] = run_sc[...]

    cols = (ex1.astype(F32), ex2.astype(F32), gate1, gate2, rank1, rank2)
    out = jnp.zeros(logits.shape, F32)
    for n, col in enumerate(cols):
        out = jnp.where(lane == n, col, out)
    r_ref[...] = out


def _router(x1, mod, g, w_group, b_group, w_expert, b_expert, B, S):
    T, D = x1.shape
    NG = w_group.shape[1]
    NE = w_expert.shape[1]
    assert NG + NE <= LANES
    pad = LANES - NG - NE
    w = jnp.concatenate([w_group, w_expert, jnp.zeros((D, pad), F32)], axis=1)
    w_hi = w.astype(BF16)
    w_lo = (w - w_hi.astype(F32)).astype(BF16)
    bias = jnp.concatenate([b_group, b_expert, jnp.full((pad,), -jnp.inf, F32)]).reshape(1, LANES)
    tm = _tile(S, 256)
    nb = S // tm
    kern = functools.partial(_router_kernel, n_groups=NG, per_group=NE // NG)
    return pl.pallas_call(
        kern,
        grid=(B, nb),
        in_specs=[pl.BlockSpec((tm, D), lambda b, i: (b * nb + i, 0)),
                  pl.BlockSpec((None, 6, D), lambda b, i: (b, 0, 0)),
                  pl.BlockSpec((1, D), lambda b, i: (0, 0)),
                  pl.BlockSpec((D, LANES), lambda b, i: (0, 0)),
                  pl.BlockSpec((D, LANES), lambda b, i: (0, 0)),
                  pl.BlockSpec((1, LANES), lambda b, i: (0, 0))],
        out_specs=[pl.BlockSpec((tm, D), lambda b, i: (b * nb + i, 0)),
                   pl.BlockSpec((tm, LANES), lambda b, i: (b * nb + i, 0)),
                   pl.BlockSpec((8, LANES), lambda b, i: (0, 0))],
        out_shape=[jax.ShapeDtypeStruct((T, D), F32), jax.ShapeDtypeStruct((T, LANES), F32),
                   jax.ShapeDtypeStruct((8, LANES), F32)],
        scratch_shapes=[pltpu.VMEM((8, LANES), F32)],
        compiler_params=_cparams("arbitrary", "arbitrary"),
        name="router",
    )(x1, mod, g.reshape(1, D), w_hi, w_lo, bias)


def _gather_rows(idx_ref, base, src_hbm, dst_ref, sem, n):
    def body(r, carry):
        tok = idx_ref[base + r]
        pltpu.make_async_copy(src_hbm.at[pl.ds(tok, 1), :], dst_ref.at[pl.ds(r, 1), :], sem).start()
        return carry
    lax.fori_loop(0, n, body, 0, unroll=DMA_UNROLL)


def _wait_rows(src_hbm, dst_ref, sem, n):
    def body(r, carry):
        pltpu.make_async_copy(src_hbm.at[pl.ds(0, 1), :], dst_ref.at[pl.ds(r, 1), :], sem).wait()
        return carry
    lax.fori_loop(0, n, body, 0, unroll=DMA_UNROLL)


def _scatter_kernel(dest_ref, pend_ref, h_ref, xs_hbm, zbuf, sem, zsem, *, tm, tb, n_experts, n_blocks):
    i = pl.program_id(0)

    def zero_block(row0):
        return pltpu.make_async_copy(zbuf, xs_hbm.at[pl.ds(pl.multiple_of(row0, tb), tb), :], zsem)

    def zero_copy(e):
        return zero_block(jnp.maximum(pend_ref[e] - tb, 0))

    @pl.when(i == 0)
    def _():
        zbuf[...] = jnp.zeros(zbuf.shape, F32)

        def start(e, carry):
            @pl.when(pend_ref[e] > 0)
            def _():
                zero_copy(e).start()
            return carry

        def wait(e, carry):
            @pl.when(pend_ref[e] > 0)
            def _():
                zero_copy(e).wait()
            return carry

        def start_tail(b, carry):
            zero_block(b * tb).start()
            return carry

        def wait_tail(b, carry):
            zero_block(b * tb).wait()
            return carry

        first_unused = pend_ref[n_experts - 1] // tb
        lax.fori_loop(0, n_experts, start, 0)
        lax.fori_loop(first_unused, n_blocks, start_tail, 0)
        lax.fori_loop(0, n_experts, wait, 0)
        lax.fori_loop(first_unused, n_blocks, wait_tail, 0)

    def row_copy(r, choice):
        slot = dest_ref[2 * (i * tm + r) + choice]
        return pltpu.make_async_copy(h_ref.at[pl.ds(r, 1), :], xs_hbm.at[pl.ds(slot, 1), :], sem)

    def start_row(r, carry):
        row_copy(r, 0).start()
        row_copy(r, 1).start()
        return carry

    def wait_row(r, carry):
        row_copy(r, 0).wait()
        row_copy(r, 1).wait()
        return carry

    lax.fori_loop(0, tm, start_row, 0, unroll=DMA_UNROLL)
    lax.fori_loop(0, tm, wait_row, 0, unroll=DMA_UNROLL)


def _scatter(h2, dest, pend, n_slots, tb):
    T, D = h2.shape
    tm = _tile(T, 256)
    grid_spec = pltpu.PrefetchScalarGridSpec(
        num_scalar_prefetch=2,
        grid=(T // tm,),
        in_specs=[pl.BlockSpec((tm, D), lambda i, de, pe: (i, 0))],
        out_specs=pl.BlockSpec(memory_space=pl.ANY),
        scratch_shapes=[pltpu.VMEM((tb, D), F32), pltpu.SemaphoreType.DMA(()), pltpu.SemaphoreType.DMA(())],
    )
    return pl.pallas_call(
        functools.partial(_scatter_kernel, tm=tm, tb=tb, n_experts=pend.shape[0],
                          n_blocks=n_slots // tb),
        grid_spec=grid_spec,
        out_shape=jax.ShapeDtypeStruct((n_slots, D), F32),
        compiler_params=_cparams("arbitrary"),
        name="scatter",
    )(dest, pend, h2)


def _moe_kernel(be_ref, nused_ref, x_ref, wg_ref, wu_ref, wd_ref, y_ref):
    i = pl.program_id(0)
    k = pl.program_id(1)
    nused = nused_ref[0]

    @pl.when(i < nused)
    def _():
        x = x_ref[...].astype(BF16)
        g = jnp.dot(x, wg_ref[...].astype(BF16), preferred_element_type=F32)
        u = jnp.dot(x, wu_ref[...].astype(BF16), preferred_element_type=F32)
        act = (g * jax.nn.sigmoid(g) * u).astype(BF16)
        part = jnp.dot(act, wd_ref[...].astype(BF16), preferred_element_type=F32)

        @pl.when(k == 0)
        def _():
            y_ref[...] = part

        @pl.when(k != 0)
        def _():
            y_ref[...] += part

    @pl.when(i >= nused)
    def _():
        y_ref[...] = jnp.zeros(y_ref.shape, F32)


def _moe(xs, block_e, nused, wgu, wd, tb):
    n_slots, D = xs.shape
    E, _, de2 = wgu.shape
    de = de2 // 2
    nk = 2
    dk = de // nk
    n_blocks = n_slots // tb

    def blk(i, nu):
        return jnp.minimum(i, nu[0] - 1)

    def half(i, k, nu):
        ii = blk(i, nu)
        kk = jnp.where(i < nu[0], k, nk - 1)
        return jnp.where(ii % 2 == 0, kk, nk - 1 - kk)

    grid_spec = pltpu.PrefetchScalarGridSpec(
        num_scalar_prefetch=2,
        grid=(n_blocks, nk),
        in_specs=[pl.BlockSpec((tb, D), lambda i, k, be, nu: (blk(i, nu), 0)),
                  pl.BlockSpec((None, D, dk), lambda i, k, be, nu: (be[blk(i, nu)], 0, half(i, k, nu))),
                  pl.BlockSpec((None, D, dk), lambda i, k, be, nu: (be[blk(i, nu)], 0, nk + half(i, k, nu))),
                  pl.BlockSpec((None, dk, D), lambda i, k, be, nu: (be[blk(i, nu)], half(i, k, nu), 0))],
        out_specs=pl.BlockSpec((tb, D), lambda i, k, be, nu: (i, 0)),
    )
    return pl.pallas_call(
        _moe_kernel,
        grid_spec=grid_spec,
        out_shape=jax.ShapeDtypeStruct((n_slots, D), F32),
        compiler_params=_cparams("arbitrary", "arbitrary"),
        name="moe",
    )(block_e, nused, xs, wgu, wgu, wd)


def _final_kernel(pos_ref, y_hbm, x_ref, r_ref, mod_ref, g_ref, o_ref, ybuf, sem, *, tm, nsteps):
    i = pl.program_id(0)
    slot = lax.rem(i, 2)

    @pl.when(i == 0)
    def _():
        _gather_rows(pos_ref, 0, y_hbm, ybuf.at[0], sem.at[0], 2 * tm)

    @pl.when(i + 1 < nsteps)
    def _():
        _gather_rows(pos_ref, (i + 1) * 2 * tm, y_hbm, ybuf.at[1 - slot], sem.at[1 - slot], 2 * tm)

    _wait_rows(y_hbm, ybuf.at[slot], sem.at[slot], 2 * tm)
    yb = ybuf[slot]
    r = r_ref[...]
    lane = lax.broadcasted_iota(jnp.int32, r.shape, 1)
    gate1 = jnp.sum(jnp.where(lane == 2, r, 0.0), axis=1, keepdims=True)
    gate2 = jnp.sum(jnp.where(lane == 3, r, 0.0), axis=1, keepdims=True)
    x = x_ref[...] + mod_ref[5:6, :] * (gate1 * yb[:tm] + gate2 * yb[tm:])
    o_ref[...] = x * lax.rsqrt(jnp.mean(x * x, axis=-1, keepdims=True) + EPS) * g_ref[...]


def _final(y_slots, dest, route, x1, mod, g, S):
    T, D = x1.shape
    tm = _tile(S, 128)
    per_b = S // tm
    nsteps = T // tm
    pos_tiled = jnp.transpose(dest.reshape(nsteps, tm, 2), (0, 2, 1)).reshape(-1)
    grid_spec = pltpu.PrefetchScalarGridSpec(
        num_scalar_prefetch=1,
        grid=(nsteps,),
        in_specs=[pl.BlockSpec(memory_space=pl.ANY),
                  pl.BlockSpec((tm, D), lambda i, p: (i, 0)),
                  pl.BlockSpec((tm, LANES), lambda i, p: (i, 0)),
                  pl.BlockSpec((None, 6, D), lambda i, p: (i // per_b, 0, 0)),
                  pl.BlockSpec((1, D), lambda i, p: (0, 0))],
        out_specs=pl.BlockSpec((tm, D), lambda i, p: (i, 0)),
        scratch_shapes=[pltpu.VMEM((2, 2 * tm, D), F32), pltpu.SemaphoreType.DMA((2,))],
    )
    return pl.pallas_call(
        functools.partial(_final_kernel, tm=tm, nsteps=nsteps),
        grid_spec=grid_spec,
        out_shape=jax.ShapeDtypeStruct((T, D), F32),
        compiler_params=_cparams("arbitrary"),
        name="final",
    )(pos_tiled, y_slots, x1, route, mod, g.reshape(1, D))


def _dispatch(route, counts, n_experts, tb):
    T = route.shape[0]
    ids = route[:, 0:2].astype(jnp.int32)
    rank = route[:, 4:6].astype(jnp.int32)
    counts = counts[0, :n_experts].astype(jnp.int32)
    padded = (counts + tb - 1) // tb * tb
    pend = jnp.cumsum(padded).astype(jnp.int32)
    pstart = pend - padded
    onehot = ids[:, :, None] == jnp.arange(n_experts, dtype=jnp.int32)
    dest = (rank + jnp.sum(jnp.where(onehot, pstart, 0), axis=-1)).reshape(-1)
    n_blocks = -(-2 * T // tb) + n_experts
    first_row = jnp.arange(n_blocks, dtype=jnp.int32) * tb
    block_e = jnp.minimum(jnp.sum(pend[None, :] <= first_row[:, None], axis=1), n_experts - 1)
    nused = (pend[-1] // tb).reshape(1)
    return dest, pend, block_e.astype(jnp.int32), nused, n_blocks * tb


def _layer(x2, c, lidx, B, S, w_ada, b_ada, norm1_g, w_in, conv_qk, b_if, da_lambda, da_norm_g, rel_bias,
           ml_norm_g, w_o_attn, w_o_mlstm, w_out, norm2_g, w_group, b_group, w_expert, b_expert,
           w_gate_up, w_down):
    D = x2.shape[1]
    H = rel_bias.shape[1]
    d = da_lambda.shape[1]
    da_w = H * 2 * d
    MH, dh = ml_norm_g.shape
    ml_w = MH * dh
    n_experts = w_expert.shape[1]

    mod = _ada(c, w_ada, b_ada)

    off_mlq = 3 * da_w
    off_mlv = off_mlq + 2 * ml_w
    off_if = off_mlv + 2 * ml_w
    off_g = off_if + 2 * MH
    w_in_t = w_in.T
    h, gates_if = _norm1(x2, mod, norm1_g, w_in_t[off_if:off_g], b_if, B, S)
    q_scale = jnp.concatenate([jnp.full((da_w,), d ** -0.5 * math.log2(math.e), F32),
                               jnp.ones((2 * da_w,), F32)])
    da_qkv = _proj(h, w_in_t, 0, 3 * da_w, BF16, "proj_da", col_scale=q_scale)
    ml_qk = _proj(h, w_in_t, off_mlq, 2 * ml_w, F32, "proj_mlqk")
    ml_vo = _proj(h, w_in_t, off_mlv, 2 * ml_w, BF16, "proj_mlvo")
    gates_am = _proj(h, w_in_t[off_g:], 0, 2 * D, BF16, "proj_gates")

    lambda_init = 0.8 - 0.6 * math.exp(-0.3 * lidx)
    lamp = da_lambda.astype(F32)
    lam = jnp.exp(jnp.sum(lamp[0] * lamp[1])) - jnp.exp(jnp.sum(lamp[2] * lamp[3])) + lambda_init
    attn = _attn(da_qkv, lam, rel_bias, da_norm_g, B, S, H, d, lambda_init)

    mqk = _conv(ml_qk, conv_qk, B, S, ml_w, dh ** -0.5)
    hm = _mlstm(mqk, ml_vo, gates_if, ml_norm_g, B, S, MH, dh)

    merged = _merge(attn, hm, w_o_attn, w_o_mlstm, gates_am)
    x1 = _outproj(merged, w_out, x2, mod, S)

    h2, route, counts = _router(x1, mod, norm2_g, w_group, b_group, w_expert, b_expert, B, S)
    tb = 256
    dest, pend, block_e, nused, n_slots = _dispatch(route, counts, n_experts, tb)
    xs = _scatter(h2, dest, pend, n_slots, tb)
    y_slots = _moe(xs, block_e, nused, w_gate_up, w_down, tb)
    return y_slots, dest, route, x1, mod


def kernel(x, c, w_ada, b_ada, norm1_g, w_in, conv_qk, b_if, da_lambda, da_norm_g, rel_bias, ml_norm_g,
           w_o_attn, w_o_mlstm, w_out, norm2_g, w_group, b_group, w_expert, b_expert, w_gate_up, w_down,
           normf_g):
    B, S, D = x.shape
    assert w_ada.shape[0] == 1, "the final rmsnorm is fused into the layer's last kernel: one layer only"
    l = 0
    y_slots, dest, route, x1, mod = _layer(
        x.reshape(B * S, D), c, l, B, S, w_ada[l], b_ada[l], norm1_g[l], w_in[l], conv_qk[l], b_if[l],
        da_lambda[l], da_norm_g[l], rel_bias, ml_norm_g[l], w_o_attn[l], w_o_mlstm[l], w_out[l],
        norm2_g[l], w_group[l], b_group[l], w_expert[l], b_expert[l], w_gate_up[l], w_down[l])
    return _final(y_slots, dest, route, x1, mod, normf_g, S).reshape(B, S, D)
```

```python
import functools
import math

import jax
import jax.numpy as jnp
from jax import lax
from jax.experimental import pallas as pl
from jax.experimental.pallas import tpu as pltpu

F32 = jnp.float32
BF16 = jnp.bfloat16
EPS = 1e-6
REL_MAX_DIST = 128
ML_CHUNK = 128
NEG_BIG = -1e30
V7X_VMEM_LIMIT = 56 * 1024 * 1024
LANES = 128
DMA_UNROLL = 8


def _cparams(*sem):
    return pltpu.CompilerParams(dimension_semantics=sem, vmem_limit_bytes=V7X_VMEM_LIMIT)


def _tile(n, pref):
    t = min(n, pref)
    while n % t:
        t //= 2
    return t


def _nt_dot(a, b):
    return lax.dot_general(a, b, (((1,), (1,)), ((), ())), preferred_element_type=F32)


def _tn_dot(a, b):
    return lax.dot_general(a, b, (((0,), (0,)), ((), ())), preferred_element_type=F32)


def _ada_kernel(c_ref, w_ref, b_ref, o_ref):
    c = c_ref[...]
    s = (c * jax.nn.sigmoid(c)).astype(BF16)
    o_ref[...] = jnp.dot(s, w_ref[...].astype(BF16), preferred_element_type=F32) + b_ref[...]


def _ada(c, w_ada, b_ada):
    B, D = c.shape
    N = w_ada.shape[1]
    rows = 8
    c8 = jnp.zeros((rows, D), F32).at[:B].set(c)
    tn = _tile(N, 512)
    mod = pl.pallas_call(
        _ada_kernel,
        grid=(N // tn,),
        in_specs=[pl.BlockSpec((rows, D), lambda j: (0, 0)),
                  pl.BlockSpec((D, tn), lambda j: (0, j)),
                  pl.BlockSpec((1, tn), lambda j: (0, j))],
        out_specs=pl.BlockSpec((rows, tn), lambda j: (0, j)),
        out_shape=jax.ShapeDtypeStruct((rows, N), F32),
        compiler_params=_cparams("arbitrary"),
        name="ada",
    )(c8, w_ada, b_ada.reshape(1, N))
    return mod[:B].reshape(B, 6, D)


def _norm1_kernel(x_ref, mod_ref, g_ref, wif_ref, bif_ref, h_ref, gate_ref):
    x = x_ref[...]
    y = x * lax.rsqrt(jnp.mean(x * x, axis=-1, keepdims=True) + EPS) * g_ref[...]
    h = (y * (1.0 + mod_ref[1:2, :]) + mod_ref[0:1, :]).astype(BF16)
    h_ref[...] = h
    gate_ref[...] = _nt_dot(h, wif_ref[...].astype(BF16)) + bif_ref[...]


def _norm1(x2, mod, g, w_if_t, b_if, B, S):
    T, D = x2.shape
    G = w_if_t.shape[0]
    tm = _tile(S, 256)
    nb = S // tm
    return pl.pallas_call(
        _norm1_kernel,
        grid=(B, nb),
        in_specs=[pl.BlockSpec((tm, D), lambda b, i: (b * nb + i, 0)),
                  pl.BlockSpec((None, 6, D), lambda b, i: (b, 0, 0)),
                  pl.BlockSpec((1, D), lambda b, i: (0, 0)),
                  pl.BlockSpec((G, D), lambda b, i: (0, 0)),
                  pl.BlockSpec((1, G), lambda b, i: (0, 0))],
        out_specs=[pl.BlockSpec((tm, D), lambda b, i: (b * nb + i, 0)),
                   pl.BlockSpec((tm, G), lambda b, i: (b * nb + i, 0))],
        out_shape=[jax.ShapeDtypeStruct((T, D), BF16), jax.ShapeDtypeStruct((T, G), F32)],
        compiler_params=_cparams("arbitrary", "arbitrary"),
        name="norm1",
    )(x2, mod, g.reshape(1, D), w_if_t, b_if.reshape(1, G))


def _proj_kernel(h_ref, w_ref, o_ref):
    o_ref[...] = _nt_dot(h_ref[...], w_ref[...].astype(BF16)).astype(o_ref.dtype)


def _proj_scaled_kernel(h_ref, w_ref, s_ref, o_ref):
    acc = _nt_dot(h_ref[...], w_ref[...].astype(BF16))
    o_ref[...] = (acc * s_ref[...]).astype(o_ref.dtype)


def _proj(h, w_t, col0, ncols, out_dtype, name, col_scale=None):
    M, K = h.shape
    tm = _tile(M, 1024)
    tn = _tile(math.gcd(ncols, col0) if col0 else ncols, 512)
    j0 = col0 // tn
    in_specs = [pl.BlockSpec((tm, K), lambda i, j: (i, 0)),
                pl.BlockSpec((tn, K), lambda i, j: (j0 + j, 0))]
    args = (h, w_t)
    body = _proj_kernel
    if col_scale is not None:
        in_specs.append(pl.BlockSpec((1, tn), lambda i, j: (0, j)))
        args = (h, w_t, col_scale.reshape(1, ncols))
        body = _proj_scaled_kernel
    return pl.pallas_call(
        body,
        grid=(M // tm, ncols // tn),
        in_specs=in_specs,
        out_specs=pl.BlockSpec((tm, tn), lambda i, j: (i, j)),
        out_shape=jax.ShapeDtypeStruct((M, ncols), out_dtype),
        compiler_params=_cparams("arbitrary", "arbitrary"),
        name=name,
    )(*args)


def _conv_kernel(u_ref, w_ref, o_ref, ext_ref, *, width, kcol0, kscale, tc):
    s = pl.program_id(2)
    ts = u_ref.shape[0]

    @pl.when(s == 0)
    def _():
        ext_ref[0:8, :] = jnp.zeros((8, tc), F32)

    u = u_ref[...]
    ext_ref[8:, :] = u
    acc = u * w_ref[width - 1:width, :]
    for t in range(1, width):
        acc = acc + ext_ref[8 - t:8 - t + ts, :] * w_ref[width - 1 - t:width - t, :]
    ext_ref[0:8, :] = u[ts - 8:, :]
    y = acc * jax.nn.sigmoid(acc)
    scale = jnp.where(pl.program_id(1) * tc >= kcol0, kscale, 1.0).astype(F32)
    o_ref[...] = (y * scale).astype(o_ref.dtype)


def _conv(u, w, B, S, kcol0, kscale):
    T, C = u.shape
    width = w.shape[0]
    ts = _tile(S, 512)
    tc = _tile(math.gcd(C, kcol0), 512)
    ns = S // ts
    return pl.pallas_call(
        functools.partial(_conv_kernel, width=width, kcol0=kcol0, kscale=kscale, tc=tc),
        grid=(B, C // tc, ns),
        in_specs=[pl.BlockSpec((ts, tc), lambda b, c, s: (b * ns + s, c)),
                  pl.BlockSpec((width, tc), lambda b, c, s: (0, c))],
        out_specs=pl.BlockSpec((ts, tc), lambda b, c, s: (b * ns + s, c)),
        out_shape=jax.ShapeDtypeStruct((T, C), BF16),
        scratch_shapes=[pltpu.VMEM((ts + 8, tc), F32)],
        compiler_params=_cparams("arbitrary", "arbitrary", "arbitrary"),
        name="conv",
    )(u, w)


def _attn_kernel(lam_ref, q_ref, k_ref, v_ref, bd_ref, bp_ref, g_ref, o_ref,
                 m_sc, l_sc, acc_sc, s_sc, p_sc, a_sc, *, d, t, out_scale):
    i = pl.program_id(2)
    q = q_ref[...]
    ngrp = t // LANES
    m_sc[...] = jnp.full(m_sc.shape, NEG_BIG, F32)
    l_sc[...] = jnp.zeros(l_sc.shape, F32)
    acc_sc[...] = jnp.zeros(acc_sc.shape, F32)
    p_sc[...] = jnp.zeros(p_sc.shape, BF16)
    a_sc[...] = jnp.ones(a_sc.shape, F32)

    def rows(j):
        return pl.ds(pl.multiple_of(j * t, t), t)

    def scores(j):
        k = k_ref[rows(j), :]
        for m in range(2):
            s_sc[m] = _nt_dot(q[:, m * d:(m + 1) * d], k[:, m * d:(m + 1) * d])

    def accumulate(j, buf):
        v = v_ref[rows(j), :]
        for m in range(2):
            alpha = a_sc[buf, m]
            pv = jnp.dot(p_sc[buf, m], v, preferred_element_type=F32)
            acc_sc[m] = jnp.concatenate([alpha] * (2 * d // LANES), axis=1) * acc_sc[m] + pv

    def softmax(bias_ref, buf):
        for m in range(2):
            s = s_sc[m]
            if bias_ref is not None:
                s = s + bias_ref[...]
            grp = [s[:, g * LANES:(g + 1) * LANES] for g in range(ngrp)]
            m_old = m_sc[m]
            row_max = jnp.max(functools.reduce(jnp.maximum, grp), axis=1, keepdims=True)
            m_new = jnp.maximum(m_old, row_max)
            alpha = jnp.exp2(m_old - m_new)
            p = [jnp.exp2(x - m_new) for x in grp]
            l_sc[m] = alpha * l_sc[m] + functools.reduce(jnp.add, p)
            p_sc[buf, m] = jnp.concatenate([x.astype(BF16) for x in p], axis=1)
            a_sc[buf, m] = alpha
            m_sc[m] = m_new

    def stage(j, bias_ref, buf, last=False):
        accumulate(jnp.maximum(j - 1, 0), 1 - buf)
        softmax(bias_ref, buf)
        if not last:
            scores(j + 1)

    n_far = jnp.maximum(i - 1, 0)
    odd = lax.rem(n_far, 2)

    def far_pair(jj, carry):
        j = odd + 2 * jj
        stage(j, None, 1)
        stage(j + 1, None, 0)
        return carry

    scores(0)

    @pl.when(odd == 1)
    def _():
        stage(0, None, 0)

    lax.fori_loop(0, n_far // 2, far_pair, 0)

    @pl.when(i > 0)
    def _():
        stage(i - 1, bp_ref, 1)

    stage(i, bd_ref, 0, last=True)
    accumulate(i, 0)

    l0 = jnp.sum(l_sc[0], axis=1, keepdims=True)
    l1 = jnp.sum(l_sc[1], axis=1, keepdims=True)
    a = acc_sc[0] / l0 - lam_ref[0] * (acc_sc[1] / l1)
    y = a * lax.rsqrt(jnp.mean(a * a, axis=-1, keepdims=True) + EPS) * g_ref[...]
    o_ref[...] = (y * out_scale).astype(o_ref.dtype)


def _rel_bucket(n, n_buckets):
    max_exact = n_buckets // 2
    nf = jnp.maximum(n, 1).astype(F32)
    large = max_exact + (jnp.log(nf / max_exact) / math.log(REL_MAX_DIST / max_exact)
                         * (n_buckets - max_exact)).astype(jnp.int32)
    large = jnp.minimum(large, n_buckets - 1)
    return jnp.where(n < max_exact, n, large)


def _bias_table(rel, dist, n_buckets):
    bucket = _rel_bucket(dist, n_buckets)
    out = jnp.zeros((rel.shape[1],) + dist.shape, F32)
    for b in range(n_buckets):
        out = jnp.where((bucket == b)[None], rel[b][:, None, None], out)
    return out


def _attn(qkv, lam, rel_bias, norm_g, B, S, H, d, lambda_init):
    T = qkv.shape[0]
    t = _tile(S, 512)
    assert t >= REL_MAX_DIST, "key blocks two or more tiles back must sit in the saturated bucket"
    assert d % LANES == 0
    nq = S // t
    n_buckets = rel_bias.shape[0]
    rel = rel_bias.astype(F32)
    rel = (rel - rel[n_buckets - 1]) * math.log2(math.e)
    dist_d = jnp.arange(t)[:, None] - jnp.arange(t)[None, :]
    bd = jnp.where((dist_d >= 0)[None], _bias_table(rel, jnp.maximum(dist_d, 0), n_buckets), NEG_BIG)
    bp = _bias_table(rel, dist_d + t, n_buckets)
    kern = functools.partial(_attn_kernel, d=d, t=t, out_scale=1.0 - lambda_init)
    return pl.pallas_call(
        kern,
        grid=(B, H, nq),
        in_specs=[pl.BlockSpec(memory_space=pltpu.SMEM),
                  pl.BlockSpec((t, 2 * d), lambda b, h, i: (b * nq + i, h)),
                  pl.BlockSpec((S, 2 * d), lambda b, h, i: (b, H + h)),
                  pl.BlockSpec((S, 2 * d), lambda b, h, i: (b, 2 * H + h)),
                  pl.BlockSpec((None, t, t), lambda b, h, i: (h, 0, 0)),
                  pl.BlockSpec((None, t, t), lambda b, h, i: (h, 0, 0)),
                  pl.BlockSpec((1, 2 * d), lambda b, h, i: (0, 0))],
        out_specs=pl.BlockSpec((t, 2 * d), lambda b, h, i: (b * nq + i, h)),
        out_shape=jax.ShapeDtypeStruct((T, H * 2 * d), BF16),
        scratch_shapes=[pltpu.VMEM((2, t, LANES), F32), pltpu.VMEM((2, t, LANES), F32),
                        pltpu.VMEM((2, t, 2 * d), F32), pltpu.VMEM((2, t, t), F32),
                        pltpu.VMEM((2, 2, t, t), BF16), pltpu.VMEM((2, 2, t, LANES), F32)],
        compiler_params=_cparams("arbitrary", "arbitrary", "arbitrary"),
        name="attn",
    )(lam.reshape(1), qkv, qkv, qkv, bd, bp, norm_g.reshape(1, 2 * d))


def _mlstm_kernel(q_ref, k_ref, v_ref, og_ref, gcol_ref, grow_ref, g_ref, o_ref, c_sc, n_sc, m_sc,
                  *, nheads, group, dh):
    L = q_ref.shape[0]

    @pl.when(pl.program_id(2) == 0)
    def _():
        c_sc[...] = jnp.zeros(c_sc.shape, F32)
        n_sc[...] = jnp.zeros(n_sc.shape, F32)
        m_sc[...] = jnp.zeros(m_sc.shape, F32)

    gcol = gcol_ref[...]
    lane = lax.broadcasted_iota(jnp.int32, gcol.shape, 1)
    grow = grow_ref[...]
    sub = lax.broadcasted_iota(jnp.int32, grow.shape, 0)
    jj = lax.broadcasted_iota(jnp.int32, (L, L), 0)
    ss = lax.broadcasted_iota(jnp.int32, (L, L), 1)
    tril = ss <= jj

    for hx in range(group):
        h = pl.program_id(1) * group + hx
        cols = slice(hx * dh, (hx + 1) * dh)
        i_col = jnp.sum(jnp.where(lane == h, gcol, 0.0), axis=1, keepdims=True)
        f_col = jax.nn.log_sigmoid(jnp.sum(jnp.where(lane == h + nheads, gcol, 0.0), axis=1, keepdims=True))
        i_row = jnp.sum(jnp.where(sub == h, grow, 0.0), axis=0, keepdims=True)
        f_row = jax.nn.log_sigmoid(jnp.sum(jnp.where(sub == h + nheads, grow, 0.0), axis=0, keepdims=True))

        b_col = jnp.sum(jnp.where(tril, f_row, 0.0), axis=1, keepdims=True)
        b_row = jnp.sum(jnp.where(jj <= ss, f_col, 0.0), axis=0, keepdims=True)
        u_row = i_row - b_row
        u_col = i_col - b_col

        m_prev = m_sc[hx, 0:1, 0:1]
        mm_col = jnp.maximum(m_prev, jnp.max(jnp.where(tril, u_row, NEG_BIG), axis=1, keepdims=True))
        w = jnp.exp(jnp.where(tril, u_row - mm_col, NEG_BIG))
        a_inter = jnp.exp(m_prev - mm_col)

        q = q_ref[:, cols]
        k = k_ref[:, cols]
        v = v_ref[:, cols]
        c_old = c_sc[hx]
        n_old = n_sc[hx]
        s_qk = _nt_dot(q, k) * w
        num = (a_inter * _nt_dot(q, c_old.astype(BF16))
               + jnp.dot(s_qk.astype(BF16), v, preferred_element_type=F32))
        den = (a_inter * jnp.sum(q.astype(F32) * n_old, axis=1, keepdims=True)
               + jnp.sum(s_qk, axis=1, keepdims=True))
        hh = num / jnp.maximum(jnp.abs(den), jnp.exp(-(b_col + mm_col)))

        mm_last = mm_col[L - 1:L, :]
        w_s = jnp.exp(u_col - mm_last)
        a_state = jnp.exp(m_prev - mm_last)
        c_sc[hx] = a_state * c_old + _tn_dot((v.astype(F32) * w_s).astype(BF16), k)
        n_sc[hx] = a_state * n_old + jnp.sum(k.astype(F32) * w_s, axis=0, keepdims=True)
        m_sc[hx] = jnp.broadcast_to(b_col[L - 1:L, :] + mm_last, m_sc.shape[1:])

        y = hh * lax.rsqrt(jnp.mean(hh * hh, axis=-1, keepdims=True) + EPS) * g_ref[hx]
        o_ref[:, cols] = (y * jax.nn.sigmoid(og_ref[:, cols].astype(F32))).astype(o_ref.dtype)


def _mlstm(mqk, mvo, gates, norm_g, B, S, H, dh):
    T = mqk.shape[0]
    L = ML_CHUNK
    nc = S // L
    group = _tile(H, 4)
    ng = H // group
    gw = group * dh
    grow = jnp.transpose(gates.reshape(B, S, 2 * H), (0, 2, 1))
    row = lambda b, h, c: (b * nc + c, h)
    row_hi = lambda b, h, c: (b * nc + c, ng + h)
    return pl.pallas_call(
        functools.partial(_mlstm_kernel, nheads=H, group=group, dh=dh),
        grid=(B, ng, nc),
        in_specs=[pl.BlockSpec((L, gw), row),
                  pl.BlockSpec((L, gw), row_hi),
                  pl.BlockSpec((L, gw), row),
                  pl.BlockSpec((L, gw), row_hi),
                  pl.BlockSpec((L, 2 * H), lambda b, h, c: (b * nc + c, 0)),
                  pl.BlockSpec((None, 2 * H, L), lambda b, h, c: (b, 0, c)),
                  pl.BlockSpec((group, 1, dh), lambda b, h, c: (h, 0, 0))],
        out_specs=pl.BlockSpec((L, gw), row),
        out_shape=jax.ShapeDtypeStruct((T, H * dh), BF16),
        scratch_shapes=[pltpu.VMEM((group, dh, dh), F32), pltpu.VMEM((group, 1, dh), F32),
                        pltpu.VMEM((group, 8, LANES), F32)],
        compiler_params=_cparams("arbitrary", "arbitrary", "arbitrary"),
        name="mlstm",
    )(mqk, mqk, mvo, mvo, gates, grow, norm_g.reshape(H, 1, dh))


def _merge_kernel(a_ref, m_ref, wa_ref, wm_ref, ga_ref, gm_ref, o_ref):
    pa = jnp.dot(a_ref[...], wa_ref[...].astype(BF16), preferred_element_type=F32)
    pm = jnp.dot(m_ref[...], wm_ref[...].astype(BF16), preferred_element_type=F32)
    out = (jax.nn.sigmoid(ga_ref[...].astype(F32)) * pa + jax.nn.sigmoid(gm_ref[...].astype(F32)) * pm)
    o_ref[...] = out.astype(o_ref.dtype)


def _merge(attn, hm, w_a, w_m, gates_am):
    T, Ka = attn.shape
    Km = hm.shape[1]
    D = w_a.shape[1]
    tm = _tile(T, 1024)
    tn = _tile(D, 256)
    nj = D // tn
    return pl.pallas_call(
        _merge_kernel,
        grid=(T // tm, nj),
        in_specs=[pl.BlockSpec((tm, Ka), lambda i, j: (i, 0)),
                  pl.BlockSpec((tm, Km), lambda i, j: (i, 0)),
                  pl.BlockSpec((Ka, tn), lambda i, j: (0, j)),
                  pl.BlockSpec((Km, tn), lambda i, j: (0, j)),
                  pl.BlockSpec((tm, tn), lambda i, j: (i, j)),
                  pl.BlockSpec((tm, tn), lambda i, j: (i, nj + j))],
        out_specs=pl.BlockSpec((tm, tn), lambda i, j: (i, j)),
        out_shape=jax.ShapeDtypeStruct((T, D), BF16),
        compiler_params=_cparams("arbitrary", "arbitrary"),
        name="merge",
    )(attn, hm, w_a, w_m, gates_am, gates_am)


def _out_kernel(a_ref, w_ref, x_ref, mod_ref, o_ref):
    p = jnp.dot(a_ref[...], w_ref[...].astype(BF16), preferred_element_type=F32)
    o_ref[...] = x_ref[...] + mod_ref[2:3, :] * p


def _outproj(merged, w_out, x2, mod, S):
    T, K = merged.shape
    D = w_out.shape[1]
    tm = _tile(S, 1024)
    tn = _tile(D, 512)
    per_b = S // tm
    return pl.pallas_call(
        _out_kernel,
        grid=(T // tm, D // tn),
        in_specs=[pl.BlockSpec((tm, K), lambda i, j: (i, 0)),
                  pl.BlockSpec((K, tn), lambda i, j: (0, j)),
                  pl.BlockSpec((tm, tn), lambda i, j: (i, j)),
                  pl.BlockSpec((None, 6, tn), lambda i, j: (i // per_b, 0, j))],
        out_specs=pl.BlockSpec((tm, tn), lambda i, j: (i, j)),
        out_shape=jax.ShapeDtypeStruct((T, D), F32),
        compiler_params=_cparams("arbitrary", "arbitrary"),
        name="outproj",
    )(merged, w_out, x2, mod)


def _router_kernel(x_ref, mod_ref, g_ref, whi_ref, wlo_ref, b_ref, h_ref, r_ref, cnt_ref, run_sc,
                   *, n_groups, per_group):
    @pl.when(jnp.logical_and(pl.program_id(0) == 0, pl.program_id(1) == 0))
    def _():
        run_sc[...] = jnp.zeros(run_sc.shape, F32)

    x = x_ref[...]
    y = x * lax.rsqrt(jnp.mean(x * x, axis=-1, keepdims=True) + EPS) * g_ref[...]
    h2 = y * (1.0 + mod_ref[4:5, :]) + mod_ref[3:4, :]
    h_ref[...] = h2
    hi = h2.astype(BF16)
    lo = (h2 - hi.astype(F32)).astype(BF16)
    logits = (jnp.dot(hi, whi_ref[...], preferred_element_type=F32)
              + jnp.dot(hi, wlo_ref[...], preferred_element_type=F32)
              + jnp.dot(lo, whi_ref[...], preferred_element_type=F32)) + b_ref[...]
    lane = lax.broadcasted_iota(jnp.int32, logits.shape, 1)
    big = jnp.int32(1 << 20)

    def top(vals):
        mx = jnp.max(vals, axis=1, keepdims=True)
        idx = jnp.min(jnp.where(vals == mx, lane, big), axis=1, keepdims=True)
        return mx, idx

    gl = jnp.where(lane < n_groups, logits, -jnp.inf)
    gmax, gidx = top(gl)
    g_p = 1.0 / jnp.sum(jnp.exp(gl - gmax), axis=1, keepdims=True)
    lo_lane = n_groups + per_group * gidx
    el = jnp.where((lane >= lo_lane) & (lane < lo_lane + per_group), logits, -jnp.inf)
    e1, i1 = top(el)
    e2, i2 = top(jnp.where(lane == i1, -jnp.inf, el))
    r = jnp.exp(e2 - e1)
    gate1 = g_p / (1.0 + r)
    gate2 = g_p * r / (1.0 + r)
    ex1 = i1 - n_groups
    ex2 = i2 - n_groups
    oh1 = (lane == ex1).astype(F32)
    oh2 = (lane == ex2).astype(F32)
    both = oh1 + oh2
    tm = x.shape[0]
    earlier = (lax.broadcasted_iota(jnp.int32, (tm, tm), 1)
               < lax.broadcasted_iota(jnp.int32, (tm, tm), 0)).astype(BF16)
    prefix = jnp.dot(earlier, both.astype(BF16), preferred_element_type=F32) + run_sc[0:1, :]
    rank1 = jnp.sum(oh1 * prefix, axis=1, keepdims=True)
    rank2 = jnp.sum(oh2 * prefix, axis=1, keepdims=True)
    run_sc[...] = run_sc[...] + jnp.sum(both, axis=0, keepdims=True)
    cnt_ref[...] = run_sc[...]

    cols = (ex1.astype(F32), ex2.astype(F32), gate1, gate2, rank1, rank2)
    out = jnp.zeros(logits.shape, F32)
    for n, col in enumerate(cols):
        out = jnp.where(lane == n, col, out)
    r_ref[...] = out


def _router(x1, mod, g, w_group, b_group, w_expert, b_expert, B, S):
    T, D = x1.shape
    NG = w_group.shape[1]
    NE = w_expert.shape[1]
    assert NG + NE <= LANES
    pad = LANES - NG - NE
    w = jnp.concatenate([w_group, w_expert, jnp.zeros((D, pad), F32)], axis=1)
    w_hi = w.astype(BF16)
    w_lo = (w - w_hi.astype(F32)).astype(BF16)
    bias = jnp.concatenate([b_group, b_expert, jnp.full((pad,), -jnp.inf, F32)]).reshape(1, LANES)
    tm = _tile(S, 256)
    nb = S // tm
    kern = functools.partial(_router_kernel, n_groups=NG, per_group=NE // NG)
    return pl.pallas_call(
        kern,
        grid=(B, nb),
        in_specs=[pl.BlockSpec((tm, D), lambda b, i: (b * nb + i, 0)),
                  pl.BlockSpec((None, 6, D), lambda b, i: (b, 0, 0)),
                  pl.BlockSpec((1, D), lambda b, i: (0, 0)),
                  pl.BlockSpec((D, LANES), lambda b, i: (0, 0)),
                  pl.BlockSpec((D, LANES), lambda b, i: (0, 0)),
                  pl.BlockSpec((1, LANES), lambda b, i: (0, 0))],
        out_specs=[pl.BlockSpec((tm, D), lambda b, i: (b * nb + i, 0)),
                   pl.BlockSpec((tm, LANES), lambda b, i: (b * nb + i, 0)),
                   pl.BlockSpec((8, LANES), lambda b, i: (0, 0))],
        out_shape=[jax.ShapeDtypeStruct((T, D), F32), jax.ShapeDtypeStruct((T, LANES), F32),
                   jax.ShapeDtypeStruct((8, LANES), F32)],
        scratch_shapes=[pltpu.VMEM((8, LANES), F32)],
        compiler_params=_cparams("arbitrary", "arbitrary"),
        name="router",
    )(x1, mod, g.reshape(1, D), w_hi, w_lo, bias)


def _gather_rows(idx_ref, base, src_hbm, dst_ref, sem, n):
    def body(r, carry):
        tok = idx_ref[base + r]
        pltpu.make_async_copy(src_hbm.at[pl.ds(tok, 1), :], dst_ref.at[pl.ds(r, 1), :], sem).start()
        return carry
    lax.fori_loop(0, n, body, 0, unroll=DMA_UNROLL)


def _wait_rows(src_hbm, dst_ref, sem, n):
    def body(r, carry):
        pltpu.make_async_copy(src_hbm.at[pl.ds(0, 1), :], dst_ref.at[pl.ds(r, 1), :], sem).wait()
        return carry
    lax.fori_loop(0, n, body, 0, unroll=DMA_UNROLL)


def _scatter_kernel(dest_ref, pend_ref, h_ref, xs_hbm, zbuf, sem, zsem, *, tm, tb, n_experts, n_blocks):
    i = pl.program_id(0)

    def zero_block(row0):
        return pltpu.make_async_copy(zbuf, xs_hbm.at[pl.ds(pl.multiple_of(row0, tb), tb), :], zsem)

    def zero_copy(e):
        return zero_block(jnp.maximum(pend_ref[e] - tb, 0))

    @pl.when(i == 0)
    def _():
        zbuf[...] = jnp.zeros(zbuf.shape, F32)

        def start(e, carry):
            @pl.when(pend_ref[e] > 0)
            def _():
                zero_copy(e).start()
            return carry

        def wait(e, carry):
            @pl.when(pend_ref[e] > 0)
            def _():
                zero_copy(e).wait()
            return carry

        def start_tail(b, carry):
            zero_block(b * tb).start()
            return carry

        def wait_tail(b, carry):
            zero_block(b * tb).wait()
            return carry

        first_unused = pend_ref[n_experts - 1] // tb
        lax.fori_loop(0, n_experts, start, 0)
        lax.fori_loop(first_unused, n_blocks, start_tail, 0)
        lax.fori_loop(0, n_experts, wait, 0)
        lax.fori_loop(first_unused, n_blocks, wait_tail, 0)

    def row_copy(r, choice):
        slot = dest_ref[2 * (i * tm + r) + choice]
        return pltpu.make_async_copy(h_ref.at[pl.ds(r, 1), :], xs_hbm.at[pl.ds(slot, 1), :], sem)

    def start_row(r, carry):
        row_copy(r, 0).start()
        row_copy(r, 1).start()
        return carry

    def wait_row(r, carry):
        row_copy(r, 0).wait()
        row_copy(r, 1).wait()
        return carry

    lax.fori_loop(0, tm, start_row, 0, unroll=DMA_UNROLL)
    lax.fori_loop(0, tm, wait_row, 0, unroll=DMA_UNROLL)


def _scatter(h2, dest, pend, n_slots, tb):
    T, D = h2.shape
    tm = _tile(T, 256)
    grid_spec = pltpu.PrefetchScalarGridSpec(
        num_scalar_prefetch=2,
        grid=(T // tm,),
        in_specs=[pl.BlockSpec((tm, D), lambda i, de, pe: (i, 0))],
        out_specs=pl.BlockSpec(memory_space=pl.ANY),
        scratch_shapes=[pltpu.VMEM((tb, D), F32), pltpu.SemaphoreType.DMA(()), pltpu.SemaphoreType.DMA(())],
    )
    return pl.pallas_call(
        functools.partial(_scatter_kernel, tm=tm, tb=tb, n_experts=pend.shape[0],
                          n_blocks=n_slots // tb),
        grid_spec=grid_spec,
        out_shape=jax.ShapeDtypeStruct((n_slots, D), F32),
        compiler_params=_cparams("arbitrary"),
        name="scatter",
    )(dest, pend, h2)


def _moe_kernel(be_ref, nused_ref, x_ref, wg_ref, wu_ref, wd_ref, y_ref):
    i = pl.program_id(0)
    k = pl.program_id(1)
    nused = nused_ref[0]

    @pl.when(i < nused)
    def _():
        x = x_ref[...].astype(BF16)
        g = jnp.dot(x, wg_ref[...].astype(BF16), preferred_element_type=F32)
        u = jnp.dot(x, wu_ref[...].astype(BF16), preferred_element_type=F32)
        act = (g * jax.nn.sigmoid(g) * u).astype(BF16)
        part = jnp.dot(act, wd_ref[...].astype(BF16), preferred_element_type=F32)

        @pl.when(k == 0)
        def _():
            y_ref[...] = part

        @pl.when(k != 0)
        def _():
            y_ref[...] += part

    @pl.when(i >= nused)
    def _():
        y_ref[...] = jnp.zeros(y_ref.shape, F32)


def _moe(xs, block_e, nused, wgu, wd, tb):
    n_slots, D = xs.shape
    E, _, de2 = wgu.shape
    de = de2 // 2
    nk = 2
    dk = de // nk
    n_blocks = n_slots // tb

    def blk(i, nu):
        return jnp.minimum(i, nu[0] - 1)

    def half(i, k, nu):
        ii = blk(i, nu)
        kk = jnp.where(i < nu[0], k, nk - 1)
        return jnp.where(ii % 2 == 0, kk, nk - 1 - kk)

    grid_spec = pltpu.PrefetchScalarGridSpec(
        num_scalar_prefetch=2,
        grid=(n_blocks, nk),
        in_specs=[pl.BlockSpec((tb, D), lambda i, k, be, nu: (blk(i, nu), 0)),
                  pl.BlockSpec((None, D, dk), lambda i, k, be, nu: (be[blk(i, nu)], 0, half(i, k, nu))),
                  pl.BlockSpec((None, D, dk), lambda i, k, be, nu: (be[blk(i, nu)], 0, nk + half(i, k, nu))),
                  pl.BlockSpec((None, dk, D), lambda i, k, be, nu: (be[blk(i, nu)], half(i, k, nu), 0))],
        out_specs=pl.BlockSpec((tb, D), lambda i, k, be, nu: (i, 0)),
    )
    return pl.pallas_call(
        _moe_kernel,
        grid_spec=grid_spec,
        out_shape=jax.ShapeDtypeStruct((n_slots, D), F32),
        compiler_params=_cparams("arbitrary", "arbitrary"),
        name="moe",
    )(block_e, nused, xs, wgu, wgu, wd)


def _final_kernel(pos_ref, y_hbm, x_ref, r_ref, mod_ref, g_ref, o_ref, ybuf, sem, *, tm, nsteps):
    i = pl.program_id(0)
    slot = lax.rem(i, 2)

    @pl.when(i == 0)
    def _():
        _gather_rows(pos_ref, 0, y_hbm, ybuf.at[0], sem.at[0], 2 * tm)

    @pl.when(i + 1 < nsteps)
    def _():
        _gather_rows(pos_ref, (i + 1) * 2 * tm, y_hbm, ybuf.at[1 - slot], sem.at[1 - slot], 2 * tm)

    _wait_rows(y_hbm, ybuf.at[slot], sem.at[slot], 2 * tm)
    yb = ybuf[slot]
    r = r_ref[...]
    lane = lax.broadcasted_iota(jnp.int32, r.shape, 1)
    gate1 = jnp.sum(jnp.where(lane == 2, r, 0.0), axis=1, keepdims=True)
    gate2 = jnp.sum(jnp.where(lane == 3, r, 0.0), axis=1, keepdims=True)
    x = x_ref[...] + mod_ref[5:6, :] * (gate1 * yb[:tm] + gate2 * yb[tm:])
    o_ref[...] = x * lax.rsqrt(jnp.mean(x * x, axis=-1, keepdims=True) + EPS) * g_ref[...]


def _final(y_slots, dest, route, x1, mod, g, S):
    T, D = x1.shape
    tm = _tile(S, 128)
    per_b = S // tm
    nsteps = T // tm
    pos_tiled = jnp.transpose(dest.reshape(nsteps, tm, 2), (0, 2, 1)).reshape(-1)
    grid_spec = pltpu.PrefetchScalarGridSpec(
        num_scalar_prefetch=1,
        grid=(nsteps,),
        in_specs=[pl.BlockSpec(memory_space=pl.ANY),
                  pl.BlockSpec((tm, D), lambda i, p: (i, 0)),
                  pl.BlockSpec((tm, LANES), lambda i, p: (i, 0)),
                  pl.BlockSpec((None, 6, D), lambda i, p: (i // per_b, 0, 0)),
                  pl.BlockSpec((1, D), lambda i, p: (0, 0))],
        out_specs=pl.BlockSpec((tm, D), lambda i, p: (i, 0)),
        scratch_shapes=[pltpu.VMEM((2, 2 * tm, D), F32), pltpu.SemaphoreType.DMA((2,))],
    )
    return pl.pallas_call(
        functools.partial(_final_kernel, tm=tm, nsteps=nsteps),
        grid_spec=grid_spec,
        out_shape=jax.ShapeDtypeStruct((T, D), F32),
        compiler_params=_cparams("arbitrary"),
        name="final",
    )(pos_tiled, y_slots, x1, route, mod, g.reshape(1, D))


def _dispatch(route, counts, n_experts, tb):
    T = route.shape[0]
    ids = route[:, 0:2].astype(jnp.int32)
    rank = route[:, 4:6].astype(jnp.int32)
    counts = counts[0, :n_experts].astype(jnp.int32)
    padded = (counts + tb - 1) // tb * tb
    pend = jnp.cumsum(padded).astype(jnp.int32)
    pstart = pend - padded
    onehot = ids[:, :, None] == jnp.arange(n_experts, dtype=jnp.int32)
    dest = (rank + jnp.sum(jnp.where(onehot, pstart, 0), axis=-1)).reshape(-1)
    n_blocks = -(-2 * T // tb) + n_experts
    first_row = jnp.arange(n_blocks, dtype=jnp.int32) * tb
    block_e = jnp.minimum(jnp.sum(pend[None, :] <= first_row[:, None], axis=1), n_experts - 1)
    nused = (pend[-1] // tb).reshape(1)
    return dest, pend, block_e.astype(jnp.int32), nused, n_blocks * tb


def _layer(x2, c, lidx, B, S, w_ada, b_ada, norm1_g, w_in, conv_qk, b_if, da_lambda, da_norm_g, rel_bias,
           ml_norm_g, w_o_attn, w_o_mlstm, w_out, norm2_g, w_group, b_group, w_expert, b_expert,
           w_gate_up, w_down):
    D = x2.shape[1]
    H = rel_bias.shape[1]
    d = da_lambda.shape[1]
    da_w = H * 2 * d
    MH, dh = ml_norm_g.shape
    ml_w = MH * dh
    n_experts = w_expert.shape[1]

    mod = _ada(c, w_ada, b_ada)

    off_mlq = 3 * da_w
    off_mlv = off_mlq + 2 * ml_w
    off_if = off_mlv + 2 * ml_w
    off_g = off_if + 2 * MH
    w_in_t = w_in.T
    h, gates_if = _norm1(x2, mod, norm1_g, w_in_t[off_if:off_g], b_if, B, S)
    q_scale = jnp.concatenate([jnp.full((da_w,), d ** -0.5 * math.log2(math.e), F32),
                               jnp.ones((2 * da_w,), F32)])
    da_qkv = _proj(h, w_in_t, 0, 3 * da_w, BF16, "proj_da", col_scale=q_scale)
    ml_qk = _proj(h, w_in_t, off_mlq, 2 * ml_w, F32, "proj_mlqk")
    ml_vo = _proj(h, w_in_t, off_mlv, 2 * ml_w, BF16, "proj_mlvo")
    gates_am = _proj(h, w_in_t[off_g:], 0, 2 * D, BF16, "proj_gates")

    lambda_init = 0.8 - 0.6 * math.exp(-0.3 * lidx)
    lamp = da_lambda.astype(F32)
    lam = jnp.exp(jnp.sum(lamp[0] * lamp[1])) - jnp.exp(jnp.sum(lamp[2] * lamp[3])) + lambda_init
    attn = _attn(da_qkv, lam, rel_bias, da_norm_g, B, S, H, d, lambda_init)

    mqk = _conv(ml_qk, conv_qk, B, S, ml_w, dh ** -0.5)
    hm = _mlstm(mqk, ml_vo, gates_if, ml_norm_g, B, S, MH, dh)

    merged = _merge(attn, hm, w_o_attn, w_o_mlstm, gates_am)
    x1 = _outproj(merged, w_out, x2, mod, S)

    h2, route, counts = _router(x1, mod, norm2_g, w_group, b_group, w_expert, b_expert, B, S)
    tb = 256
    dest, pend, block_e, nused, n_slots = _dispatch(route, counts, n_experts, tb)
    xs = _scatter(h2, dest, pend, n_slots, tb)
    y_slots = _moe(xs, block_e, nused, w_gate_up, w_down, tb)
    return y_slots, dest, route, x1, mod


def kernel(x, c, w_ada, b_ada, norm1_g, w_in, conv_qk, b_if, da_lambda, da_norm_g, rel_bias, ml_norm_g,
           w_o_attn, w_o_mlstm, w_out, norm2_g, w_group, b_group, w_expert, b_expert, w_gate_up, w_down,
           normf_g):
    B, S, D = x.shape
    assert w_ada.shape[0] == 1, "the final rmsnorm is fused into the layer's last kernel: one layer only"
    l = 0
    y_slots, dest, route, x1, mod = _layer(
        x.reshape(B * S, D), c, l, B, S, w_ada[l], b_ada[l], norm1_g[l], w_in[l], conv_qk[l], b_if[l],
        da_lambda[l], da_norm_g[l], rel_bias, ml_norm_g[l], w_o_attn[l], w_o_mlstm[l], w_out[l],
        norm2_g[l], w_group[l], b_group[l], w_expert[l], b_expert[l], w_gate_up[l], w_down[l])
    return _final(y_slots, dest, route, x1, mod, normf_g, S).reshape(B, S, D)
```

```python
import functools
import math

import jax
import jax.numpy as jnp
from jax import lax
from jax.experimental import pallas as pl
from jax.experimental.pallas import tpu as pltpu

F32 = jnp.float32
BF16 = jnp.bfloat16
EPS = 1e-6
REL_MAX_DIST = 128
ML_CHUNK = 128
NEG_BIG = -1e30
V7X_VMEM_LIMIT = 56 * 1024 * 1024
LANES = 128
DMA_UNROLL = 8


def _cparams(*sem):
    return pltpu.CompilerParams(dimension_semantics=sem, vmem_limit_bytes=V7X_VMEM_LIMIT)


def _tile(n, pref):
    t = min(n, pref)
    while n % t:
        t //= 2
    return t


def _nt_dot(a, b):
    return lax.dot_general(a, b, (((1,), (1,)), ((), ())), preferred_element_type=F32)


def _tn_dot(a, b):
    return lax.dot_general(a, b, (((0,), (0,)), ((), ())), preferred_element_type=F32)


def _pack_halves(x):
    n = x.shape[1] // 2
    bits = pltpu.bitcast(x.astype(BF16).astype(F32), jnp.uint32)
    return bits[:, n:] | (bits[:, :n] >> 16)


def _unpack_halves(w):
    lo = pltpu.bitcast(w << 16, F32)
    hi = pltpu.bitcast(w & jnp.uint32(0xFFFF0000), F32)
    return lo, hi


def _ada_kernel(c_ref, w_ref, b_ref, o_ref):
    c = c_ref[...]
    s = (c * jax.nn.sigmoid(c)).astype(BF16)
    o_ref[...] = jnp.dot(s, w_ref[...].astype(BF16), preferred_element_type=F32) + b_ref[...]


def _ada(c, w_ada, b_ada):
    B, D = c.shape
    N = w_ada.shape[1]
    rows = 8
    c8 = jnp.zeros((rows, D), F32).at[:B].set(c)
    tn = _tile(N, 512)
    mod = pl.pallas_call(
        _ada_kernel,
        grid=(N // tn,),
        in_specs=[pl.BlockSpec((rows, D), lambda j: (0, 0)),
                  pl.BlockSpec((D, tn), lambda j: (0, j)),
                  pl.BlockSpec((1, tn), lambda j: (0, j))],
        out_specs=pl.BlockSpec((rows, tn), lambda j: (0, j)),
        out_shape=jax.ShapeDtypeStruct((rows, N), F32),
        compiler_params=_cparams("arbitrary"),
        name="ada",
    )(c8, w_ada, b_ada.reshape(1, N))
    return mod[:B].reshape(B, 6, D)


def _norm1_kernel(x_ref, mod_ref, g_ref, wif_ref, bif_ref, h_ref, gate_ref):
    x = x_ref[...]
    y = x * lax.rsqrt(jnp.mean(x * x, axis=-1, keepdims=True) + EPS) * g_ref[...]
    h = (y * (1.0 + mod_ref[1:2, :]) + mod_ref[0:1, :]).astype(BF16)
    h_ref[...] = h
    gate_ref[...] = _nt_dot(h, wif_ref[...].astype(BF16)) + bif_ref[...]


def _norm1(x2, mod, g, w_if_t, b_if, B, S):
    T, D = x2.shape
    G = w_if_t.shape[0]
    tm = _tile(S, 256)
    nb = S // tm
    return pl.pallas_call(
        _norm1_kernel,
        grid=(B, nb),
        in_specs=[pl.BlockSpec((tm, D), lambda b, i: (b * nb + i, 0)),
                  pl.BlockSpec((None, 6, D), lambda b, i: (b, 0, 0)),
                  pl.BlockSpec((1, D), lambda b, i: (0, 0)),
                  pl.BlockSpec((G, D), lambda b, i: (0, 0)),
                  pl.BlockSpec((1, G), lambda b, i: (0, 0))],
        out_specs=[pl.BlockSpec((tm, D), lambda b, i: (b * nb + i, 0)),
                   pl.BlockSpec((tm, G), lambda b, i: (b * nb + i, 0))],
        out_shape=[jax.ShapeDtypeStruct((T, D), BF16), jax.ShapeDtypeStruct((T, G), F32)],
        compiler_params=_cparams("arbitrary", "arbitrary"),
        name="norm1",
    )(x2, mod, g.reshape(1, D), w_if_t, b_if.reshape(1, G))


def _proj_kernel(h_ref, w_ref, o_ref):
    o_ref[...] = _nt_dot(h_ref[...], w_ref[...].astype(BF16)).astype(o_ref.dtype)


def _proj_scaled_kernel(h_ref, w_ref, s_ref, o_ref):
    acc = _nt_dot(h_ref[...], w_ref[...].astype(BF16))
    o_ref[...] = (acc * s_ref[...]).astype(o_ref.dtype)


def _proj(h, w_t, col0, ncols, out_dtype, name, col_scale=None):
    M, K = h.shape
    tm = _tile(M, 1024)
    tn = _tile(math.gcd(ncols, col0) if col0 else ncols, 512)
    j0 = col0 // tn
    in_specs = [pl.BlockSpec((tm, K), lambda i, j: (i, 0)),
                pl.BlockSpec((tn, K), lambda i, j: (j0 + j, 0))]
    args = (h, w_t)
    body = _proj_kernel
    if col_scale is not None:
        in_specs.append(pl.BlockSpec((1, tn), lambda i, j: (0, j)))
        args = (h, w_t, col_scale.reshape(1, ncols))
        body = _proj_scaled_kernel
    return pl.pallas_call(
        body,
        grid=(M // tm, ncols // tn),
        in_specs=in_specs,
        out_specs=pl.BlockSpec((tm, tn), lambda i, j: (i, j)),
        out_shape=jax.ShapeDtypeStruct((M, ncols), out_dtype),
        compiler_params=_cparams("arbitrary", "arbitrary"),
        name=name,
    )(*args)


def _conv_kernel(u_ref, w_ref, o_ref, ext_ref, *, width, kcol0, kscale, tc):
    s = pl.program_id(2)
    ts = u_ref.shape[0]

    @pl.when(s == 0)
    def _():
        ext_ref[0:8, :] = jnp.zeros((8, tc), F32)

    u = u_ref[...]
    ext_ref[8:, :] = u
    acc = u * w_ref[width - 1:width, :]
    for t in range(1, width):
        acc = acc + ext_ref[8 - t:8 - t + ts, :] * w_ref[width - 1 - t:width - t, :]
    ext_ref[0:8, :] = u[ts - 8:, :]
    y = acc * jax.nn.sigmoid(acc)
    scale = jnp.where(pl.program_id(1) * tc >= kcol0, kscale, 1.0).astype(F32)
    o_ref[...] = (y * scale).astype(o_ref.dtype)


def _conv(u, w, B, S, kcol0, kscale):
    T, C = u.shape
    width = w.shape[0]
    ts = _tile(S, 512)
    tc = _tile(math.gcd(C, kcol0), 512)
    ns = S // ts
    return pl.pallas_call(
        functools.partial(_conv_kernel, width=width, kcol0=kcol0, kscale=kscale, tc=tc),
        grid=(B, C // tc, ns),
        in_specs=[pl.BlockSpec((ts, tc), lambda b, c, s: (b * ns + s, c)),
                  pl.BlockSpec((width, tc), lambda b, c, s: (0, c))],
        out_specs=pl.BlockSpec((ts, tc), lambda b, c, s: (b * ns + s, c)),
        out_shape=jax.ShapeDtypeStruct((T, C), BF16),
        scratch_shapes=[pltpu.VMEM((ts + 8, tc), F32)],
        compiler_params=_cparams("arbitrary", "arbitrary", "arbitrary"),
        name="conv",
    )(u, w)


def _attn_kernel(lam_ref, q_ref, k_ref, v_ref, bd_ref, bp_ref, g_ref, o_ref,
                 m_sc, l_sc, acc_sc, s_sc, p_sc, a_sc, *, d, t, out_scale):
    i = pl.program_id(2)
    q = q_ref[...]
    ngrp = t // LANES
    m_sc[...] = jnp.full(m_sc.shape, NEG_BIG, F32)
    l_sc[...] = jnp.zeros(l_sc.shape, F32)
    acc_sc[...] = jnp.zeros(acc_sc.shape, F32)
    p_sc[...] = jnp.zeros(p_sc.shape, BF16)
    a_sc[...] = jnp.ones(a_sc.shape, F32)

    def rows(j):
        return pl.ds(pl.multiple_of(j * t, t), t)

    def scores(j):
        k = k_ref[rows(j), :]
        for m in range(2):
            s_sc[m] = _nt_dot(q[:, m * d:(m + 1) * d], k[:, m * d:(m + 1) * d])

    def accumulate(j, buf):
        v = v_ref[rows(j), :]
        for m in range(2):
            alpha = a_sc[buf, m]
            pv = jnp.dot(p_sc[buf, m], v, preferred_element_type=F32)
            acc_sc[m] = jnp.concatenate([alpha] * (2 * d // LANES), axis=1) * acc_sc[m] + pv

    def softmax(bias_ref, buf):
        for m in range(2):
            s = s_sc[m]
            if bias_ref is not None:
                s = s + bias_ref[...]
            grp = [s[:, g * LANES:(g + 1) * LANES] for g in range(ngrp)]
            m_old = m_sc[m]
            row_max = jnp.max(functools.reduce(jnp.maximum, grp), axis=1, keepdims=True)
            m_new = jnp.maximum(m_old, row_max)
            alpha = jnp.exp2(m_old - m_new)
            p = [jnp.exp2(x - m_new) for x in grp]
            l_sc[m] = alpha * l_sc[m] + functools.reduce(jnp.add, p)
            p_sc[buf, m] = jnp.concatenate([x.astype(BF16) for x in p], axis=1)
            a_sc[buf, m] = alpha
            m_sc[m] = m_new

    def stage(j, bias_ref, buf, last=False):
        accumulate(jnp.maximum(j - 1, 0), 1 - buf)
        softmax(bias_ref, buf)
        if not last:
            scores(j + 1)

    n_far = jnp.maximum(i - 1, 0)
    odd = lax.rem(n_far, 2)

    def far_pair(jj, carry):
        j = odd + 2 * jj
        stage(j, None, 1)
        stage(j + 1, None, 0)
        return carry

    scores(0)

    @pl.when(odd == 1)
    def _():
        stage(0, None, 0)

    lax.fori_loop(0, n_far // 2, far_pair, 0)

    @pl.when(i > 0)
    def _():
        stage(i - 1, bp_ref, 1)

    stage(i, bd_ref, 0, last=True)
    accumulate(i, 0)

    l0 = jnp.sum(l_sc[0], axis=1, keepdims=True)
    l1 = jnp.sum(l_sc[1], axis=1, keepdims=True)
    a = acc_sc[0] / l0 - lam_ref[0] * (acc_sc[1] / l1)
    y = a * lax.rsqrt(jnp.mean(a * a, axis=-1, keepdims=True) + EPS) * g_ref[...]
    o_ref[...] = (y * out_scale).astype(o_ref.dtype)


def _rel_bucket(n, n_buckets):
    max_exact = n_buckets // 2
    nf = jnp.maximum(n, 1).astype(F32)
    large = max_exact + (jnp.log(nf / max_exact) / math.log(REL_MAX_DIST / max_exact)
                         * (n_buckets - max_exact)).astype(jnp.int32)
    large = jnp.minimum(large, n_buckets - 1)
    return jnp.where(n < max_exact, n, large)


def _bias_table(rel, dist, n_buckets):
    bucket = _rel_bucket(dist, n_buckets)
    out = jnp.zeros((rel.shape[1],) + dist.shape, F32)
    for b in range(n_buckets):
        out = jnp.where((bucket == b)[None], rel[b][:, None, None], out)
    return out


def _attn(qkv, lam, rel_bias, norm_g, B, S, H, d, lambda_init):
    T = qkv.shape[0]
    t = _tile(S, 512)
    assert t >= REL_MAX_DIST, "key blocks two or more tiles back must sit in the saturated bucket"
    assert d % LANES == 0
    nq = S // t
    n_buckets = rel_bias.shape[0]
    rel = rel_bias.astype(F32)
    rel = (rel - rel[n_buckets - 1]) * math.log2(math.e)
    dist_d = jnp.arange(t)[:, None] - jnp.arange(t)[None, :]
    bd = jnp.where((dist_d >= 0)[None], _bias_table(rel, jnp.maximum(dist_d, 0), n_buckets), NEG_BIG)
    bp = _bias_table(rel, dist_d + t, n_buckets)
    kern = functools.partial(_attn_kernel, d=d, t=t, out_scale=1.0 - lambda_init)
    return pl.pallas_call(
        kern,
        grid=(B, H, nq),
        in_specs=[pl.BlockSpec(memory_space=pltpu.SMEM),
                  pl.BlockSpec((t, 2 * d), lambda b, h, i: (b * nq + i, h)),
                  pl.BlockSpec((S, 2 * d), lambda b, h, i: (b, H + h)),
                  pl.BlockSpec((S, 2 * d), lambda b, h, i: (b, 2 * H + h)),
                  pl.BlockSpec((None, t, t), lambda b, h, i: (h, 0, 0)),
                  pl.BlockSpec((None, t, t), lambda b, h, i: (h, 0, 0)),
                  pl.BlockSpec((1, 2 * d), lambda b, h, i: (0, 0))],
        out_specs=pl.BlockSpec((t, 2 * d), lambda b, h, i: (b * nq + i, h)),
        out_shape=jax.ShapeDtypeStruct((T, H * 2 * d), BF16),
        scratch_shapes=[pltpu.VMEM((2, t, LANES), F32), pltpu.VMEM((2, t, LANES), F32),
                        pltpu.VMEM((2, t, 2 * d), F32), pltpu.VMEM((2, t, t), F32),
                        pltpu.VMEM((2, 2, t, t), BF16), pltpu.VMEM((2, 2, t, LANES), F32)],
        compiler_params=_cparams("arbitrary", "arbitrary", "arbitrary"),
        name="attn",
    )(lam.reshape(1), qkv, qkv, qkv, bd, bp, norm_g.reshape(1, 2 * d))


def _mlstm_kernel(q_ref, k_ref, v_ref, og_ref, gcol_ref, grow_ref, g_ref, o_ref, c_sc, n_sc, m_sc,
                  *, nheads, group, dh):
    L = q_ref.shape[0]

    @pl.when(pl.program_id(2) == 0)
    def _():
        c_sc[...] = jnp.zeros(c_sc.shape, F32)
        n_sc[...] = jnp.zeros(n_sc.shape, F32)
        m_sc[...] = jnp.zeros(m_sc.shape, F32)

    gcol = gcol_ref[...]
    lane = lax.broadcasted_iota(jnp.int32, gcol.shape, 1)
    grow = grow_ref[...]
    sub = lax.broadcasted_iota(jnp.int32, grow.shape, 0)
    jj = lax.broadcasted_iota(jnp.int32, (L, L), 0)
    ss = lax.broadcasted_iota(jnp.int32, (L, L), 1)
    tril = ss <= jj

    for hx in range(group):
        h = pl.program_id(1) * group + hx
        cols = slice(hx * dh, (hx + 1) * dh)
        i_col = jnp.sum(jnp.where(lane == h, gcol, 0.0), axis=1, keepdims=True)
        f_col = jax.nn.log_sigmoid(jnp.sum(jnp.where(lane == h + nheads, gcol, 0.0), axis=1, keepdims=True))
        i_row = jnp.sum(jnp.where(sub == h, grow, 0.0), axis=0, keepdims=True)
        f_row = jax.nn.log_sigmoid(jnp.sum(jnp.where(sub == h + nheads, grow, 0.0), axis=0, keepdims=True))

        b_col = jnp.sum(jnp.where(tril, f_row, 0.0), axis=1, keepdims=True)
        b_row = jnp.sum(jnp.where(jj <= ss, f_col, 0.0), axis=0, keepdims=True)
        u_row = i_row - b_row
        u_col = i_col - b_col

        m_prev = m_sc[hx, 0:1, 0:1]
        mm_col = jnp.maximum(m_prev, jnp.max(jnp.where(tril, u_row, NEG_BIG), axis=1, keepdims=True))
        w = jnp.exp(jnp.where(tril, u_row - mm_col, NEG_BIG))
        a_inter = jnp.exp(m_prev - mm_col)

        q = q_ref[:, cols]
        k = k_ref[:, cols]
        v = v_ref[:, cols]
        c_old = c_sc[hx]
        n_old = n_sc[hx]
        s_qk = _nt_dot(q, k) * w
        num = (a_inter * _nt_dot(q, c_old.astype(BF16))
               + jnp.dot(s_qk.astype(BF16), v, preferred_element_type=F32))
        den = (a_inter * jnp.sum(q.astype(F32) * n_old, axis=1, keepdims=True)
               + jnp.sum(s_qk, axis=1, keepdims=True))
        hh = num / jnp.maximum(jnp.abs(den), jnp.exp(-(b_col + mm_col)))

        mm_last = mm_col[L - 1:L, :]
        w_s = jnp.exp(u_col - mm_last)
        a_state = jnp.exp(m_prev - mm_last)
        c_sc[hx] = a_state * c_old + _tn_dot((v.astype(F32) * w_s).astype(BF16), k)
        n_sc[hx] = a_state * n_old + jnp.sum(k.astype(F32) * w_s, axis=0, keepdims=True)
        m_sc[hx] = jnp.broadcast_to(b_col[L - 1:L, :] + mm_last, m_sc.shape[1:])

        y = hh * lax.rsqrt(jnp.mean(hh * hh, axis=-1, keepdims=True) + EPS) * g_ref[hx]
        o_ref[:, cols] = (y * jax.nn.sigmoid(og_ref[:, cols].astype(F32))).astype(o_ref.dtype)


def _mlstm(mqk, mvo, gates, norm_g, B, S, H, dh):
    T = mqk.shape[0]
    L = ML_CHUNK
    nc = S // L
    group = _tile(H, 4)
    ng = H // group
    gw = group * dh
    grow = jnp.transpose(gates.reshape(B, S, 2 * H), (0, 2, 1))
    row = lambda b, h, c: (b * nc + c, h)
    row_hi = lambda b, h, c: (b * nc + c, ng + h)
    return pl.pallas_call(
        functools.partial(_mlstm_kernel, nheads=H, group=group, dh=dh),
        grid=(B, ng, nc),
        in_specs=[pl.BlockSpec((L, gw), row),
                  pl.BlockSpec((L, gw), row_hi),
                  pl.BlockSpec((L, gw), row),
                  pl.BlockSpec((L, gw), row_hi),
                  pl.BlockSpec((L, 2 * H), lambda b, h, c: (b * nc + c, 0)),
                  pl.BlockSpec((None, 2 * H, L), lambda b, h, c: (b, 0, c)),
                  pl.BlockSpec((group, 1, dh), lambda b, h, c: (h, 0, 0))],
        out_specs=pl.BlockSpec((L, gw), row),
        out_shape=jax.ShapeDtypeStruct((T, H * dh), BF16),
        scratch_shapes=[pltpu.VMEM((group, dh, dh), F32), pltpu.VMEM((group, 1, dh), F32),
                        pltpu.VMEM((group, 8, LANES), F32)],
        compiler_params=_cparams("arbitrary", "arbitrary", "arbitrary"),
        name="mlstm",
    )(mqk, mqk, mvo, mvo, gates, grow, norm_g.reshape(H, 1, dh))


def _merge_kernel(a_ref, m_ref, wa_ref, wm_ref, ga_ref, gm_ref, o_ref):
    pa = jnp.dot(a_ref[...], wa_ref[...].astype(BF16), preferred_element_type=F32)
    pm = jnp.dot(m_ref[...], wm_ref[...].astype(BF16), preferred_element_type=F32)
    out = (jax.nn.sigmoid(ga_ref[...].astype(F32)) * pa + jax.nn.sigmoid(gm_ref[...].astype(F32)) * pm)
    o_ref[...] = out.astype(o_ref.dtype)


def _merge(attn, hm, w_a, w_m, gates_am):
    T, Ka = attn.shape
    Km = hm.shape[1]
    D = w_a.shape[1]
    tm = _tile(T, 1024)
    tn = _tile(D, 256)
    nj = D // tn
    return pl.pallas_call(
        _merge_kernel,
        grid=(T // tm, nj),
        in_specs=[pl.BlockSpec((tm, Ka), lambda i, j: (i, 0)),
                  pl.BlockSpec((tm, Km), lambda i, j: (i, 0)),
                  pl.BlockSpec((Ka, tn), lambda i, j: (0, j)),
                  pl.BlockSpec((Km, tn), lambda i, j: (0, j)),
                  pl.BlockSpec((tm, tn), lambda i, j: (i, j)),
                  pl.BlockSpec((tm, tn), lambda i, j: (i, nj + j))],
        out_specs=pl.BlockSpec((tm, tn), lambda i, j: (i, j)),
        out_shape=jax.ShapeDtypeStruct((T, D), BF16),
        compiler_params=_cparams("arbitrary", "arbitrary"),
        name="merge",
    )(attn, hm, w_a, w_m, gates_am, gates_am)


def _out_kernel(a_ref, w_ref, x_ref, mod_ref, o_ref):
    p = jnp.dot(a_ref[...], w_ref[...].astype(BF16), preferred_element_type=F32)
    o_ref[...] = x_ref[...] + mod_ref[2:3, :] * p


def _outproj(merged, w_out, x2, mod, S):
    T, K = merged.shape
    D = w_out.shape[1]
    tm = _tile(S, 1024)
    tn = _tile(D, 512)
    per_b = S // tm
    return pl.pallas_call(
        _out_kernel,
        grid=(T // tm, D // tn),
        in_specs=[pl.BlockSpec((tm, K), lambda i, j: (i, 0)),
                  pl.BlockSpec((K, tn), lambda i, j: (0, j)),
                  pl.BlockSpec((tm, tn), lambda i, j: (i, j)),
                  pl.BlockSpec((None, 6, tn), lambda i, j: (i // per_b, 0, j))],
        out_specs=pl.BlockSpec((tm, tn), lambda i, j: (i, j)),
        out_shape=jax.ShapeDtypeStruct((T, D), F32),
        compiler_params=_cparams("arbitrary", "arbitrary"),
        name="outproj",
    )(merged, w_out, x2, mod)


def _router_kernel(x_ref, mod_ref, g_ref, whi_ref, wlo_ref, b_ref, h_ref, r_ref, cnt_ref, run_sc,
                   *, n_groups, per_group):
    @pl.when(jnp.logical_and(pl.program_id(0) == 0, pl.program_id(1) == 0))
    def _():
        run_sc[...] = jnp.zeros(run_sc.shape, F32)

    x = x_ref[...]
    y = x * lax.rsqrt(jnp.mean(x * x, axis=-1, keepdims=True) + EPS) * g_ref[...]
    h2 = y * (1.0 + mod_ref[4:5, :]) + mod_ref[3:4, :]
    h_ref[...] = _pack_halves(h2)
    hi = h2.astype(BF16)
    lo = (h2 - hi.astype(F32)).astype(BF16)
    logits = (jnp.dot(hi, whi_ref[...], preferred_element_type=F32)
              + jnp.dot(hi, wlo_ref[...], preferred_element_type=F32)
              + jnp.dot(lo, whi_ref[...], preferred_element_type=F32)) + b_ref[...]
    lane = lax.broadcasted_iota(jnp.int32, logits.shape, 1)
    big = jnp.int32(1 << 20)

    def top(vals):
        mx = jnp.max(vals, axis=1, keepdims=True)
        idx = jnp.min(jnp.where(vals == mx, lane, big), axis=1, keepdims=True)
        return mx, idx

    gl = jnp.where(lane < n_groups, logits, -jnp.inf)
    gmax, gidx = top(gl)
    g_p = 1.0 / jnp.sum(jnp.exp(gl - gmax), axis=1, keepdims=True)
    lo_lane = n_groups + per_group * gidx
    el = jnp.where((lane >= lo_lane) & (lane < lo_lane + per_group), logits, -jnp.inf)
    e1, i1 = top(el)
    e2, i2 = top(jnp.where(lane == i1, -jnp.inf, el))
    r = jnp.exp(e2 - e1)
    gate1 = g_p / (1.0 + r)
    gate2 = g_p * r / (1.0 + r)
    ex1 = i1 - n_groups
    ex2 = i2 - n_groups
    oh1 = (lane == ex1).astype(F32)
    oh2 = (lane == ex2).astype(F32)
    both = oh1 + oh2
    tm = x.shape[0]
    earlier = (lax.broadcasted_iota(jnp.int32, (tm, tm), 1)
               < lax.broadcasted_iota(jnp.int32, (tm, tm), 0)).astype(BF16)
    prefix = jnp.dot(earlier, both.astype(BF16), preferred_element_type=F32) + run_sc[0:1, :]
    rank1 = jnp.sum(oh1 * prefix, axis=1, keepdims=True)
    rank2 = jnp.sum(oh2 * prefix, axis=1, keepdims=True)
    run_sc[...] = run_sc[...] + jnp.sum(both, axis=0, keepdims=True)
    cnt_ref[...] = run_sc[...]

    cols = (ex1.astype(F32), ex2.astype(F32), gate1, gate2, rank1, rank2)
    out = jnp.zeros(logits.shape, F32)
    for n, col in enumerate(cols):
        out = jnp.where(lane == n, col, out)
    r_ref[...] = out


def _router(x1, mod, g, w_group, b_group, w_expert, b_expert, B, S):
    T, D = x1.shape
    NG = w_group.shape[1]
    NE = w_expert.shape[1]
    assert NG + NE <= LANES
    pad = LANES - NG - NE
    w = jnp.concatenate([w_group, w_expert, jnp.zeros((D, pad), F32)], axis=1)
    w_hi = w.astype(BF16)
    w_lo = (w - w_hi.astype(F32)).astype(BF16)
    bias = jnp.concatenate([b_group, b_expert, jnp.full((pad,), -jnp.inf, F32)]).reshape(1, LANES)
    tm = _tile(S, 256)
    nb = S // tm
    kern = functools.partial(_router_kernel, n_groups=NG, per_group=NE // NG)
    return pl.pallas_call(
        kern,
        grid=(B, nb),
        in_specs=[pl.BlockSpec((tm, D), lambda b, i: (b * nb + i, 0)),
                  pl.BlockSpec((None, 6, D), lambda b, i: (b, 0, 0)),
                  pl.BlockSpec((1, D), lambda b, i: (0, 0)),
                  pl.BlockSpec((D, LANES), lambda b, i: (0, 0)),
                  pl.BlockSpec((D, LANES), lambda b, i: (0, 0)),
                  pl.BlockSpec((1, LANES), lambda b, i: (0, 0))],
        out_specs=[pl.BlockSpec((tm, D // 2), lambda b, i: (b * nb + i, 0)),
                   pl.BlockSpec((tm, LANES), lambda b, i: (b * nb + i, 0)),
                   pl.BlockSpec((8, LANES), lambda b, i: (0, 0))],
        out_shape=[jax.ShapeDtypeStruct((T, D // 2), jnp.uint32), jax.ShapeDtypeStruct((T, LANES), F32),
                   jax.ShapeDtypeStruct((8, LANES), F32)],
        scratch_shapes=[pltpu.VMEM((8, LANES), F32)],
        compiler_params=_cparams("arbitrary", "arbitrary"),
        name="router",
    )(x1, mod, g.reshape(1, D), w_hi, w_lo, bias)


def _gather_rows(idx_ref, base, src_hbm, dst_ref, sem, n):
    def body(r, carry):
        tok = idx_ref[base + r]
        pltpu.make_async_copy(src_hbm.at[pl.ds(tok, 1), :], dst_ref.at[pl.ds(r, 1), :], sem).start()
        return carry
    lax.fori_loop(0, n, body, 0, unroll=DMA_UNROLL)


def _wait_rows(src_hbm, dst_ref, sem, n):
    def body(r, carry):
        pltpu.make_async_copy(src_hbm.at[pl.ds(0, 1), :], dst_ref.at[pl.ds(r, 1), :], sem).wait()
        return carry
    lax.fori_loop(0, n, body, 0, unroll=DMA_UNROLL)


def _scatter_kernel(dest_ref, pend_ref, h_ref, xs_hbm, zbuf, sem, zsem, *, tm, tb, n_experts, n_blocks):
    i = pl.program_id(0)

    def zero_block(row0):
        return pltpu.make_async_copy(zbuf, xs_hbm.at[pl.ds(pl.multiple_of(row0, tb), tb), :], zsem)

    def zero_copy(e):
        return zero_block(jnp.maximum(pend_ref[e] - tb, 0))

    @pl.when(i == 0)
    def _():
        zbuf[...] = jnp.zeros(zbuf.shape, zbuf.dtype)

        def start(e, carry):
            @pl.when(pend_ref[e] > 0)
            def _():
                zero_copy(e).start()
            return carry

        def wait(e, carry):
            @pl.when(pend_ref[e] > 0)
            def _():
                zero_copy(e).wait()
            return carry

        def start_tail(b, carry):
            zero_block(b * tb).start()
            return carry

        def wait_tail(b, carry):
            zero_block(b * tb).wait()
            return carry

        first_unused = pend_ref[n_experts - 1] // tb
        lax.fori_loop(0, n_experts, start, 0)
        lax.fori_loop(first_unused, n_blocks, start_tail, 0)
        lax.fori_loop(0, n_experts, wait, 0)
        lax.fori_loop(first_unused, n_blocks, wait_tail, 0)

    def row_copy(r, choice):
        slot = dest_ref[2 * (i * tm + r) + choice]
        return pltpu.make_async_copy(h_ref.at[pl.ds(r, 1), :], xs_hbm.at[pl.ds(slot, 1), :], sem)

    def start_row(r, carry):
        row_copy(r, 0).start()
        row_copy(r, 1).start()
        return carry

    def wait_row(r, carry):
        row_copy(r, 0).wait()
        row_copy(r, 1).wait()
        return carry

    lax.fori_loop(0, tm, start_row, 0, unroll=DMA_UNROLL)
    lax.fori_loop(0, tm, wait_row, 0, unroll=DMA_UNROLL)


def _scatter(h2, dest, pend, n_slots, tb):
    T, D = h2.shape
    tm = _tile(T, 256)
    grid_spec = pltpu.PrefetchScalarGridSpec(
        num_scalar_prefetch=2,
        grid=(T // tm,),
        in_specs=[pl.BlockSpec((tm, D), lambda i, de, pe: (i, 0))],
        out_specs=pl.BlockSpec(memory_space=pl.ANY),
        scratch_shapes=[pltpu.VMEM((tb, D), h2.dtype), pltpu.SemaphoreType.DMA(()),
                        pltpu.SemaphoreType.DMA(())],
    )
    return pl.pallas_call(
        functools.partial(_scatter_kernel, tm=tm, tb=tb, n_experts=pend.shape[0],
                          n_blocks=n_slots // tb),
        grid_spec=grid_spec,
        out_shape=jax.ShapeDtypeStruct((n_slots, D), h2.dtype),
        compiler_params=_cparams("arbitrary"),
        name="scatter",
    )(dest, pend, h2)


def _moe_kernel(be_ref, nused_ref, x_ref, wg_ref, wu_ref, wd_ref, y_ref, acc_sc):
    i = pl.program_id(0)
    k = pl.program_id(1)
    nused = nused_ref[0]
    half = x_ref.shape[1]

    @pl.when(i < nused)
    def _():
        x_lo, x_hi = _unpack_halves(x_ref[...])
        x_lo = x_lo.astype(BF16)
        x_hi = x_hi.astype(BF16)

        def x_dot(w_ref):
            return (jnp.dot(x_lo, w_ref[:half, :].astype(BF16), preferred_element_type=F32)
                    + jnp.dot(x_hi, w_ref[half:, :].astype(BF16), preferred_element_type=F32))

        g = x_dot(wg_ref)
        u = x_dot(wu_ref)
        act = (g * jax.nn.sigmoid(g) * u).astype(BF16)
        part = jnp.dot(act, wd_ref[...].astype(BF16), preferred_element_type=F32)

        @pl.when(k == 0)
        def _():
            acc_sc[...] = part

        @pl.when(k != 0)
        def _():
            y_ref[...] = _pack_halves(acc_sc[...] + part)

    @pl.when(i >= nused)
    def _():
        y_ref[...] = jnp.zeros(y_ref.shape, y_ref.dtype)


def _moe(xs, block_e, nused, wgu, wd, tb):
    n_slots, hw = xs.shape
    E, D, de2 = wgu.shape
    de = de2 // 2
    nk = 2
    dk = de // nk
    n_blocks = n_slots // tb

    def blk(i, nu):
        return jnp.minimum(i, nu[0] - 1)

    def half(i, k, nu):
        ii = blk(i, nu)
        kk = jnp.where(i < nu[0], k, nk - 1)
        return jnp.where(ii % 2 == 0, kk, nk - 1 - kk)

    grid_spec = pltpu.PrefetchScalarGridSpec(
        num_scalar_prefetch=2,
        grid=(n_blocks, nk),
        in_specs=[pl.BlockSpec((tb, hw), lambda i, k, be, nu: (blk(i, nu), 0)),
                  pl.BlockSpec((None, D, dk), lambda i, k, be, nu: (be[blk(i, nu)], 0, half(i, k, nu))),
                  pl.BlockSpec((None, D, dk), lambda i, k, be, nu: (be[blk(i, nu)], 0, nk + half(i, k, nu))),
                  pl.BlockSpec((None, dk, D), lambda i, k, be, nu: (be[blk(i, nu)], half(i, k, nu), 0))],
        out_specs=pl.BlockSpec((tb, hw), lambda i, k, be, nu: (i, 0)),
        scratch_shapes=[pltpu.VMEM((tb, D), F32)],
    )
    return pl.pallas_call(
        _moe_kernel,
        grid_spec=grid_spec,
        out_shape=jax.ShapeDtypeStruct((n_slots, hw), jnp.uint32),
        compiler_params=_cparams("arbitrary", "arbitrary"),
        name="moe",
    )(block_e, nused, xs, wgu, wgu, wd)


def _final_kernel(pos_ref, y_hbm, x_ref, r_ref, mod_ref, g_ref, o_ref, ybuf, sem, *, tm, nsteps):
    i = pl.program_id(0)
    slot = lax.rem(i, 2)

    @pl.when(i == 0)
    def _():
        _gather_rows(pos_ref, 0, y_hbm, ybuf.at[0], sem.at[0], 2 * tm)

    @pl.when(i + 1 < nsteps)
    def _():
        _gather_rows(pos_ref, (i + 1) * 2 * tm, y_hbm, ybuf.at[1 - slot], sem.at[1 - slot], 2 * tm)

    _wait_rows(y_hbm, ybuf.at[slot], sem.at[slot], 2 * tm)
    y_lo, y_hi = _unpack_halves(ybuf[slot])
    r = r_ref[...]
    lane = lax.broadcasted_iota(jnp.int32, r.shape, 1)
    gate1 = jnp.sum(jnp.where(lane == 2, r, 0.0), axis=1, keepdims=True)
    gate2 = jnp.sum(jnp.where(lane == 3, r, 0.0), axis=1, keepdims=True)
    half = y_lo.shape[1]
    x_lo = x_ref[:, :half] + mod_ref[5:6, :half] * (gate1 * y_lo[:tm] + gate2 * y_lo[tm:])
    x_hi = x_ref[:, half:] + mod_ref[5:6, half:] * (gate1 * y_hi[:tm] + gate2 * y_hi[tm:])
    ms = (jnp.sum(x_lo * x_lo, axis=-1, keepdims=True)
          + jnp.sum(x_hi * x_hi, axis=-1, keepdims=True)) / (2 * half)
    inv = lax.rsqrt(ms + EPS)
    o_ref[:, :half] = x_lo * inv * g_ref[:, :half]
    o_ref[:, half:] = x_hi * inv * g_ref[:, half:]


def _final(y_slots, dest, route, x1, mod, g, S):
    T, D = x1.shape
    tm = _tile(S, 128)
    per_b = S // tm
    nsteps = T // tm
    pos_tiled = jnp.transpose(dest.reshape(nsteps, tm, 2), (0, 2, 1)).reshape(-1)
    grid_spec = pltpu.PrefetchScalarGridSpec(
        num_scalar_prefetch=1,
        grid=(nsteps,),
        in_specs=[pl.BlockSpec(memory_space=pl.ANY),
                  pl.BlockSpec((tm, D), lambda i, p: (i, 0)),
                  pl.BlockSpec((tm, LANES), lambda i, p: (i, 0)),
                  pl.BlockSpec((None, 6, D), lambda i, p: (i // per_b, 0, 0)),
                  pl.BlockSpec((1, D), lambda i, p: (0, 0))],
        out_specs=pl.BlockSpec((tm, D), lambda i, p: (i, 0)),
        scratch_shapes=[pltpu.VMEM((2, 2 * tm, D // 2), y_slots.dtype), pltpu.SemaphoreType.DMA((2,))],
    )
    return pl.pallas_call(
        functools.partial(_final_kernel, tm=tm, nsteps=nsteps),
        grid_spec=grid_spec,
        out_shape=jax.ShapeDtypeStruct((T, D), F32),
        compiler_params=_cparams("arbitrary"),
        name="final",
    )(pos_tiled, y_slots, x1, route, mod, g.reshape(1, D))


def _dispatch(route, counts, n_experts, tb):
    T = route.shape[0]
    ids = route[:, 0:2].astype(jnp.int32)
    rank = route[:, 4:6].astype(jnp.int32)
    counts = counts[0, :n_experts].astype(jnp.int32)
    padded = (counts + tb - 1) // tb * tb
    pend = jnp.cumsum(padded).astype(jnp.int32)
    pstart = pend - padded
    onehot = ids[:, :, None] == jnp.arange(n_experts, dtype=jnp.int32)
    dest = (rank + jnp.sum(jnp.where(onehot, pstart, 0), axis=-1)).reshape(-1)
    n_blocks = -(-2 * T // tb) + n_experts
    first_row = jnp.arange(n_blocks, dtype=jnp.int32) * tb
    block_e = jnp.minimum(jnp.sum(pend[None, :] <= first_row[:, None], axis=1), n_experts - 1)
    nused = (pend[-1] // tb).reshape(1)
    return dest, pend, block_e.astype(jnp.int32), nused, n_blocks * tb


def _layer(x2, c, lidx, B, S, w_ada, b_ada, norm1_g, w_in, conv_qk, b_if, da_lambda, da_norm_g, rel_bias,
           ml_norm_g, w_o_attn, w_o_mlstm, w_out, norm2_g, w_group, b_group, w_expert, b_expert,
           w_gate_up, w_down):
    D = x2.shape[1]
    H = rel_bias.shape[1]
    d = da_lambda.shape[1]
    da_w = H * 2 * d
    MH, dh = ml_norm_g.shape
    ml_w = MH * dh
    n_experts = w_expert.shape[1]

    mod = _ada(c, w_ada, b_ada)

    off_mlq = 3 * da_w
    off_mlv = off_mlq + 2 * ml_w
    off_if = off_mlv + 2 * ml_w
    off_g = off_if + 2 * MH
    w_in_t = w_in.T
    h, gates_if = _norm1(x2, mod, norm1_g, w_in_t[off_if:off_g], b_if, B, S)
    q_scale = jnp.concatenate([jnp.full((da_w,), d ** -0.5 * math.log2(math.e), F32),
                               jnp.ones((2 * da_w,), F32)])
    da_qkv = _proj(h, w_in_t, 0, 3 * da_w, BF16, "proj_da", col_scale=q_scale)
    ml_qk = _proj(h, w_in_t, off_mlq, 2 * ml_w, F32, "proj_mlqk")
    ml_vo = _proj(h, w_in_t, off_mlv, 2 * ml_w, BF16, "proj_mlvo")
    gates_am = _proj(h, w_in_t[off_g:], 0, 2 * D, BF16, "proj_gates")

    lambda_init = 0.8 - 0.6 * math.exp(-0.3 * lidx)
    lamp = da_lambda.astype(F32)
    lam = jnp.exp(jnp.sum(lamp[0] * lamp[1])) - jnp.exp(jnp.sum(lamp[2] * lamp[3])) + lambda_init
    attn = _attn(da_qkv, lam, rel_bias, da_norm_g, B, S, H, d, lambda_init)

    mqk = _conv(ml_qk, conv_qk, B, S, ml_w, dh ** -0.5)
    hm = _mlstm(mqk, ml_vo, gates_if, ml_norm_g, B, S, MH, dh)

    merged = _merge(attn, hm, w_o_attn, w_o_mlstm, gates_am)
    x1 = _outproj(merged, w_out, x2, mod, S)

    h2, route, counts = _router(x1, mod, norm2_g, w_group, b_group, w_expert, b_expert, B, S)
    tb = 256
    dest, pend, block_e, nused, n_slots = _dispatch(route, counts, n_experts, tb)
    xs = _scatter(h2, dest, pend, n_slots, tb)
    y_slots = _moe(xs, block_e, nused, w_gate_up, w_down, tb)
    return y_slots, dest, route, x1, mod


def kernel(x, c, w_ada, b_ada, norm1_g, w_in, conv_qk, b_if, da_lambda, da_norm_g, rel_bias, ml_norm_g,
           w_o_attn, w_o_mlstm, w_out, norm2_g, w_group, b_group, w_expert, b_expert, w_gate_up, w_down,
           normf_g):
    B, S, D = x.shape
    assert w_ada.shape[0] == 1, "the final rmsnorm is fused into the layer's last kernel: one layer only"
    l = 0
    y_slots, dest, route, x1, mod = _layer(
        x.reshape(B * S, D), c, l, B, S, w_ada[l], b_ada[l], norm1_g[l], w_in[l], conv_qk[l], b_if[l],
        da_lambda[l], da_norm_g[l], rel_bias, ml_norm_g[l], w_o_attn[l], w_o_mlstm[l], w_out[l],
        norm2_g[l], w_group[l], b_group[l], w_expert[l], b_expert[l], w_gate_up[l], w_down[l])
    return _final(y_slots, dest, route, x1, mod, normf_g, S).reshape(B, S, D)
```

```python
import functools
import math

import jax
import jax.numpy as jnp
from jax import lax
from jax.experimental import pallas as pl
from jax.experimental.pallas import tpu as pltpu

F32 = jnp.float32
BF16 = jnp.bfloat16
EPS = 1e-6
REL_MAX_DIST = 128
ML_CHUNK = 128
NEG_BIG = -1e30
V7X_VMEM_LIMIT = 56 * 1024 * 1024
LANES = 128
DMA_UNROLL = 8
MOE_WEIGHT_SLICES = 2


def _cparams(*sem):
    return pltpu.CompilerParams(dimension_semantics=sem, vmem_limit_bytes=V7X_VMEM_LIMIT)


def _tile(n, pref):
    t = min(n, pref)
    while n % t:
        t //= 2
    return t


def _nt_dot(a, b):
    return lax.dot_general(a, b, (((1,), (1,)), ((), ())), preferred_element_type=F32)


def _tn_dot(a, b):
    return lax.dot_general(a, b, (((0,), (0,)), ((), ())), preferred_element_type=F32)


def _pack_halves(x):
    n = x.shape[1] // 2
    bits = pltpu.bitcast(x.astype(BF16).astype(F32), jnp.uint32)
    return bits[:, n:] | (bits[:, :n] >> 16)


def _unpack_halves(w):
    lo = pltpu.bitcast(w << 16, F32)
    hi = pltpu.bitcast(w & jnp.uint32(0xFFFF0000), F32)
    return lo, hi


def _ada_kernel(c_ref, w_ref, b_ref, o_ref):
    c = c_ref[...]
    s = (c * jax.nn.sigmoid(c)).astype(BF16)
    o_ref[...] = jnp.dot(s, w_ref[...].astype(BF16), preferred_element_type=F32) + b_ref[...]


def _ada(c, w_ada, b_ada):
    B, D = c.shape
    N = w_ada.shape[1]
    rows = 8
    c8 = jnp.zeros((rows, D), F32).at[:B].set(c)
    tn = _tile(N, 512)
    mod = pl.pallas_call(
        _ada_kernel,
        grid=(N // tn,),
        in_specs=[pl.BlockSpec((rows, D), lambda j: (0, 0)),
                  pl.BlockSpec((D, tn), lambda j: (0, j)),
                  pl.BlockSpec((1, tn), lambda j: (0, j))],
        out_specs=pl.BlockSpec((rows, tn), lambda j: (0, j)),
        out_shape=jax.ShapeDtypeStruct((rows, N), F32),
        compiler_params=_cparams("arbitrary"),
        name="ada",
    )(c8, w_ada, b_ada.reshape(1, N))
    return mod[:B].reshape(B, 6, D)


def _norm1_kernel(x_ref, mod_ref, g_ref, wif_ref, bif_ref, h_ref, gate_ref):
    x = x_ref[...]
    y = x * lax.rsqrt(jnp.mean(x * x, axis=-1, keepdims=True) + EPS) * g_ref[...]
    h = (y * (1.0 + mod_ref[1:2, :]) + mod_ref[0:1, :]).astype(BF16)
    h_ref[...] = h
    gate_ref[...] = _nt_dot(h, wif_ref[...].astype(BF16)) + bif_ref[...]


def _norm1(x2, mod, g, w_if_t, b_if, B, S):
    T, D = x2.shape
    G = w_if_t.shape[0]
    tm = _tile(S, 256)
    nb = S // tm
    return pl.pallas_call(
        _norm1_kernel,
        grid=(B, nb),
        in_specs=[pl.BlockSpec((tm, D), lambda b, i: (b * nb + i, 0)),
                  pl.BlockSpec((None, 6, D), lambda b, i: (b, 0, 0)),
                  pl.BlockSpec((1, D), lambda b, i: (0, 0)),
                  pl.BlockSpec((G, D), lambda b, i: (0, 0)),
                  pl.BlockSpec((1, G), lambda b, i: (0, 0))],
        out_specs=[pl.BlockSpec((tm, D), lambda b, i: (b * nb + i, 0)),
                   pl.BlockSpec((tm, G), lambda b, i: (b * nb + i, 0))],
        out_shape=[jax.ShapeDtypeStruct((T, D), BF16), jax.ShapeDtypeStruct((T, G), F32)],
        compiler_params=_cparams("arbitrary", "arbitrary"),
        name="norm1",
    )(x2, mod, g.reshape(1, D), w_if_t, b_if.reshape(1, G))


def _proj_kernel(h_ref, w_ref, o_ref):
    o_ref[...] = _nt_dot(h_ref[...], w_ref[...].astype(BF16)).astype(o_ref.dtype)


def _proj_scaled_kernel(h_ref, w_ref, s_ref, o_ref):
    acc = _nt_dot(h_ref[...], w_ref[...].astype(BF16))
    o_ref[...] = (acc * s_ref[...]).astype(o_ref.dtype)


def _proj(h, w_t, col0, ncols, out_dtype, name, col_scale=None):
    M, K = h.shape
    tm = _tile(M, 1024)
    tn = _tile(math.gcd(ncols, col0) if col0 else ncols, 512)
    j0 = col0 // tn
    in_specs = [pl.BlockSpec((tm, K), lambda i, j: (i, 0)),
                pl.BlockSpec((tn, K), lambda i, j: (j0 + j, 0))]
    args = (h, w_t)
    body = _proj_kernel
    if col_scale is not None:
        in_specs.append(pl.BlockSpec((1, tn), lambda i, j: (0, j)))
        args = (h, w_t, col_scale.reshape(1, ncols))
        body = _proj_scaled_kernel
    return pl.pallas_call(
        body,
        grid=(M // tm, ncols // tn),
        in_specs=in_specs,
        out_specs=pl.BlockSpec((tm, tn), lambda i, j: (i, j)),
        out_shape=jax.ShapeDtypeStruct((M, ncols), out_dtype),
        compiler_params=_cparams("arbitrary", "arbitrary"),
        name=name,
    )(*args)


def _attn_kernel(lam_ref, q_ref, k_ref, v_ref, bd_ref, bp_ref, g_ref, o_ref,
                 m_sc, l_sc, acc_sc, s_sc, p_sc, a_sc, *, d, t, out_scale):
    i = pl.program_id(2)
    q = q_ref[...]
    ngrp = t // LANES
    m_sc[...] = jnp.full(m_sc.shape, NEG_BIG, F32)
    l_sc[...] = jnp.zeros(l_sc.shape, F32)
    acc_sc[...] = jnp.zeros(acc_sc.shape, F32)
    p_sc[...] = jnp.zeros(p_sc.shape, BF16)
    a_sc[...] = jnp.ones(a_sc.shape, F32)

    def rows(j):
        return pl.ds(pl.multiple_of(j * t, t), t)

    def scores(j):
        k = k_ref[rows(j), :]
        for m in range(2):
            s_sc[m] = _nt_dot(q[:, m * d:(m + 1) * d], k[:, m * d:(m + 1) * d])

    def accumulate(j, buf):
        v = v_ref[rows(j), :]
        for m in range(2):
            alpha = a_sc[buf, m]
            pv = jnp.dot(p_sc[buf, m], v, preferred_element_type=F32)
            acc_sc[m] = jnp.concatenate([alpha] * (2 * d // LANES), axis=1) * acc_sc[m] + pv

    def softmax(bias_ref, buf):
        for m in range(2):
            s = s_sc[m]
            if bias_ref is not None:
                s = s + bias_ref[...]
            grp = [s[:, g * LANES:(g + 1) * LANES] for g in range(ngrp)]
            m_old = m_sc[m]
            row_max = jnp.max(functools.reduce(jnp.maximum, grp), axis=1, keepdims=True)
            m_new = jnp.maximum(m_old, row_max)
            alpha = jnp.exp2(m_old - m_new)
            p = [jnp.exp2(x - m_new) for x in grp]
            l_sc[m] = alpha * l_sc[m] + functools.reduce(jnp.add, p)
            p_sc[buf, m] = jnp.concatenate([x.astype(BF16) for x in p], axis=1)
            a_sc[buf, m] = alpha
            m_sc[m] = m_new

    def stage(j, bias_ref, buf, last=False):
        accumulate(jnp.maximum(j - 1, 0), 1 - buf)
        softmax(bias_ref, buf)
        if not last:
            scores(j + 1)

    n_far = jnp.maximum(i - 1, 0)
    odd = lax.rem(n_far, 2)

    def far_pair(jj, carry):
        j = odd + 2 * jj
        stage(j, None, 1)
        stage(j + 1, None, 0)
        return carry

    scores(0)

    @pl.when(odd == 1)
    def _():
        stage(0, None, 0)

    lax.fori_loop(0, n_far // 2, far_pair, 0)

    @pl.when(i > 0)
    def _():
        stage(i - 1, bp_ref, 1)

    stage(i, bd_ref, 0, last=True)
    accumulate(i, 0)

    l0 = jnp.sum(l_sc[0], axis=1, keepdims=True)
    l1 = jnp.sum(l_sc[1], axis=1, keepdims=True)
    a = acc_sc[0] / l0 - lam_ref[0] * (acc_sc[1] / l1)
    y = a * lax.rsqrt(jnp.mean(a * a, axis=-1, keepdims=True) + EPS) * g_ref[...]
    o_ref[...] = (y * out_scale).astype(o_ref.dtype)


def _rel_bucket(n, n_buckets):
    max_exact = n_buckets // 2
    nf = jnp.maximum(n, 1).astype(F32)
    large = max_exact + (jnp.log(nf / max_exact) / math.log(REL_MAX_DIST / max_exact)
                         * (n_buckets - max_exact)).astype(jnp.int32)
    large = jnp.minimum(large, n_buckets - 1)
    return jnp.where(n < max_exact, n, large)


def _bias_table(rel, dist, n_buckets):
    bucket = _rel_bucket(dist, n_buckets)
    out = jnp.zeros((rel.shape[1],) + dist.shape, F32)
    for b in range(n_buckets):
        out = jnp.where((bucket == b)[None], rel[b][:, None, None], out)
    return out


def _attn(qkv, lam, rel_bias, norm_g, B, S, H, d, lambda_init):
    T = qkv.shape[0]
    t = _tile(S, 512)
    assert t >= REL_MAX_DIST, "key blocks two or more tiles back must sit in the saturated bucket"
    assert d % LANES == 0
    nq = S // t
    n_buckets = rel_bias.shape[0]
    rel = rel_bias.astype(F32)
    rel = (rel - rel[n_buckets - 1]) * math.log2(math.e)
    dist_d = jnp.arange(t)[:, None] - jnp.arange(t)[None, :]
    bd = jnp.where((dist_d >= 0)[None], _bias_table(rel, jnp.maximum(dist_d, 0), n_buckets), NEG_BIG)
    bp = _bias_table(rel, dist_d + t, n_buckets)
    kern = functools.partial(_attn_kernel, d=d, t=t, out_scale=1.0 - lambda_init)
    return pl.pallas_call(
        kern,
        grid=(B, H, nq),
        in_specs=[pl.BlockSpec(memory_space=pltpu.SMEM),
                  pl.BlockSpec((t, 2 * d), lambda b, h, i: (b * nq + i, h)),
                  pl.BlockSpec((S, 2 * d), lambda b, h, i: (b, H + h)),
                  pl.BlockSpec((S, 2 * d), lambda b, h, i: (b, 2 * H + h)),
                  pl.BlockSpec((None, t, t), lambda b, h, i: (h, 0, 0)),
                  pl.BlockSpec((None, t, t), lambda b, h, i: (h, 0, 0)),
                  pl.BlockSpec((1, 2 * d), lambda b, h, i: (0, 0))],
        out_specs=pl.BlockSpec((t, 2 * d), lambda b, h, i: (b * nq + i, h)),
        out_shape=jax.ShapeDtypeStruct((T, H * 2 * d), BF16),
        scratch_shapes=[pltpu.VMEM((2, t, LANES), F32), pltpu.VMEM((2, t, LANES), F32),
                        pltpu.VMEM((2, t, 2 * d), F32), pltpu.VMEM((2, t, t), F32),
                        pltpu.VMEM((2, 2, t, t), BF16), pltpu.VMEM((2, 2, t, LANES), F32)],
        compiler_params=_cparams("arbitrary", "arbitrary", "arbitrary"),
        name="attn",
    )(lam.reshape(1), qkv, qkv, qkv, bd, bp, norm_g.reshape(1, 2 * d))


def _causal_conv_silu(u_ref, w_ref, ext_ref, scale):
    L = u_ref.shape[0]
    width = w_ref.shape[0]
    u = u_ref[...]
    ext_ref[8:, :] = u
    acc = u * w_ref[width - 1:width, :]
    for t in range(1, width):
        acc = acc + ext_ref[8 - t:8 - t + L, :] * w_ref[width - 1 - t:width - t, :]
    ext_ref[0:8, :] = u[L - 8:, :]
    return (acc * jax.nn.sigmoid(acc) * scale).astype(BF16)


def _mlstm_kernel(q_ref, k_ref, wq_ref, wk_ref, v_ref, og_ref, gcol_ref, grow_ref, g_ref, o_ref,
                  c_sc, n_sc, m_sc, extq_sc, extk_sc, *, nheads, group, dh):
    L = q_ref.shape[0]

    @pl.when(pl.program_id(2) == 0)
    def _():
        c_sc[...] = jnp.zeros(c_sc.shape, F32)
        n_sc[...] = jnp.zeros(n_sc.shape, F32)
        m_sc[...] = jnp.zeros(m_sc.shape, F32)
        extq_sc[0:8, :] = jnp.zeros((8, extq_sc.shape[1]), F32)
        extk_sc[0:8, :] = jnp.zeros((8, extk_sc.shape[1]), F32)

    q_all = _causal_conv_silu(q_ref, wq_ref, extq_sc, 1.0)
    k_all = _causal_conv_silu(k_ref, wk_ref, extk_sc, dh ** -0.5)

    gcol = gcol_ref[...]
    lane = lax.broadcasted_iota(jnp.int32, gcol.shape, 1)
    grow = grow_ref[...]
    sub = lax.broadcasted_iota(jnp.int32, grow.shape, 0)
    jj = lax.broadcasted_iota(jnp.int32, (L, L), 0)
    ss = lax.broadcasted_iota(jnp.int32, (L, L), 1)
    tril = ss <= jj

    for hx in range(group):
        h = pl.program_id(1) * group + hx
        cols = slice(hx * dh, (hx + 1) * dh)
        i_col = jnp.sum(jnp.where(lane == h, gcol, 0.0), axis=1, keepdims=True)
        f_col = jax.nn.log_sigmoid(jnp.sum(jnp.where(lane == h + nheads, gcol, 0.0), axis=1, keepdims=True))
        i_row = jnp.sum(jnp.where(sub == h, grow, 0.0), axis=0, keepdims=True)
        f_row = jax.nn.log_sigmoid(jnp.sum(jnp.where(sub == h + nheads, grow, 0.0), axis=0, keepdims=True))

        b_col = jnp.sum(jnp.where(tril, f_row, 0.0), axis=1, keepdims=True)
        b_row = jnp.sum(jnp.where(jj <= ss, f_col, 0.0), axis=0, keepdims=True)
        u_row = i_row - b_row
        u_col = i_col - b_col

        m_prev = m_sc[hx, 0:1, 0:1]
        mm_col = jnp.maximum(m_prev, jnp.max(jnp.where(tril, u_row, NEG_BIG), axis=1, keepdims=True))
        w = jnp.exp(jnp.where(tril, u_row - mm_col, NEG_BIG))
        a_inter = jnp.exp(m_prev - mm_col)

        q = q_all[:, cols]
        k = k_all[:, cols]
        v = v_ref[:, cols]
        c_old = c_sc[hx]
        n_old = n_sc[hx]
        s_qk = _nt_dot(q, k) * w
        num = (a_inter * _nt_dot(q, c_old.astype(BF16))
               + jnp.dot(s_qk.astype(BF16), v, preferred_element_type=F32))
        den = (a_inter * jnp.sum(q.astype(F32) * n_old, axis=1, keepdims=True)
               + jnp.sum(s_qk, axis=1, keepdims=True))
        hh = num / jnp.maximum(jnp.abs(den), jnp.exp(-(b_col + mm_col)))

        mm_last = mm_col[L - 1:L, :]
        w_s = jnp.exp(u_col - mm_last)
        a_state = jnp.exp(m_prev - mm_last)
        c_sc[hx] = a_state * c_old + _tn_dot((v.astype(F32) * w_s).astype(BF16), k)
        n_sc[hx] = a_state * n_old + jnp.sum(k.astype(F32) * w_s, axis=0, keepdims=True)
        m_sc[hx] = jnp.broadcast_to(b_col[L - 1:L, :] + mm_last, m_sc.shape[1:])

        y = hh * lax.rsqrt(jnp.mean(hh * hh, axis=-1, keepdims=True) + EPS) * g_ref[hx]
        o_ref[:, cols] = (y * jax.nn.sigmoid(og_ref[:, cols].astype(F32))).astype(o_ref.dtype)


def _mlstm(mqk, conv_w, mvo, gates, norm_g, B, S, H, dh):
    T = mqk.shape[0]
    width = conv_w.shape[0]
    L = ML_CHUNK
    nc = S // L
    group = _tile(H, 4)
    ng = H // group
    gw = group * dh
    grow = jnp.transpose(gates.reshape(B, S, 2 * H), (0, 2, 1))
    row = lambda b, h, c: (b * nc + c, h)
    row_hi = lambda b, h, c: (b * nc + c, ng + h)
    return pl.pallas_call(
        functools.partial(_mlstm_kernel, nheads=H, group=group, dh=dh),
        grid=(B, ng, nc),
        in_specs=[pl.BlockSpec((L, gw), row),
                  pl.BlockSpec((L, gw), row_hi),
                  pl.BlockSpec((width, gw), lambda b, h, c: (0, h)),
                  pl.BlockSpec((width, gw), lambda b, h, c: (0, ng + h)),
                  pl.BlockSpec((L, gw), row),
                  pl.BlockSpec((L, gw), row_hi),
                  pl.BlockSpec((L, 2 * H), lambda b, h, c: (b * nc + c, 0)),
                  pl.BlockSpec((None, 2 * H, L), lambda b, h, c: (b, 0, c)),
                  pl.BlockSpec((group, 1, dh), lambda b, h, c: (h, 0, 0))],
        out_specs=pl.BlockSpec((L, gw), row),
        out_shape=jax.ShapeDtypeStruct((T, H * dh), BF16),
        scratch_shapes=[pltpu.VMEM((group, dh, dh), F32), pltpu.VMEM((group, 1, dh), F32),
                        pltpu.VMEM((group, 8, LANES), F32), pltpu.VMEM((L + 8, gw), F32),
                        pltpu.VMEM((L + 8, gw), F32)],
        compiler_params=_cparams("arbitrary", "arbitrary", "arbitrary"),
        name="mlstm",
    )(mqk, mqk, conv_w, conv_w, mvo, mvo, gates, grow, norm_g.reshape(H, 1, dh))


def _merge_kernel(a_ref, m_ref, wa_ref, wm_ref, ga_ref, gm_ref, o_ref):
    pa = jnp.dot(a_ref[...], wa_ref[...].astype(BF16), preferred_element_type=F32)
    pm = jnp.dot(m_ref[...], wm_ref[...].astype(BF16), preferred_element_type=F32)
    out = (jax.nn.sigmoid(ga_ref[...].astype(F32)) * pa + jax.nn.sigmoid(gm_ref[...].astype(F32)) * pm)
    o_ref[...] = out.astype(o_ref.dtype)


def _merge(attn, hm, w_a, w_m, gates_am):
    T, Ka = attn.shape
    Km = hm.shape[1]
    D = w_a.shape[1]
    tm = _tile(T, 1024)
    tn = _tile(D, 256)
    nj = D // tn
    return pl.pallas_call(
        _merge_kernel,
        grid=(T // tm, nj),
        in_specs=[pl.BlockSpec((tm, Ka), lambda i, j: (i, 0)),
                  pl.BlockSpec((tm, Km), lambda i, j: (i, 0)),
                  pl.BlockSpec((Ka, tn), lambda i, j: (0, j)),
                  pl.BlockSpec((Km, tn), lambda i, j: (0, j)),
                  pl.BlockSpec((tm, tn), lambda i, j: (i, j)),
                  pl.BlockSpec((tm, tn), lambda i, j: (i, nj + j))],
        out_specs=pl.BlockSpec((tm, tn), lambda i, j: (i, j)),
        out_shape=jax.ShapeDtypeStruct((T, D), BF16),
        compiler_params=_cparams("arbitrary", "arbitrary"),
        name="merge",
    )(attn, hm, w_a, w_m, gates_am, gates_am)


def _out_kernel(a_ref, w_ref, x_ref, mod_ref, o_ref):
    p = jnp.dot(a_ref[...], w_ref[...].astype(BF16), preferred_element_type=F32)
    o_ref[...] = x_ref[...] + mod_ref[2:3, :] * p


def _outproj(merged, w_out, x2, mod, S):
    T, K = merged.shape
    D = w_out.shape[1]
    tm = _tile(S, 1024)
    tn = _tile(D, 512)
    per_b = S // tm
    return pl.pallas_call(
        _out_kernel,
        grid=(T // tm, D // tn),
        in_specs=[pl.BlockSpec((tm, K), lambda i, j: (i, 0)),
                  pl.BlockSpec((K, tn), lambda i, j: (0, j)),
                  pl.BlockSpec((tm, tn), lambda i, j: (i, j)),
                  pl.BlockSpec((None, 6, tn), lambda i, j: (i // per_b, 0, j))],
        out_specs=pl.BlockSpec((tm, tn), lambda i, j: (i, j)),
        out_shape=jax.ShapeDtypeStruct((T, D), F32),
        compiler_params=_cparams("arbitrary", "arbitrary"),
        name="outproj",
    )(merged, w_out, x2, mod)


def _router_kernel(x_ref, mod_ref, g_ref, whi_ref, wlo_ref, b_ref, h_ref, r_ref, cnt_ref, run_sc,
                   *, n_groups, per_group):
    @pl.when(jnp.logical_and(pl.program_id(0) == 0, pl.program_id(1) == 0))
    def _():
        run_sc[...] = jnp.zeros(run_sc.shape, F32)

    x = x_ref[...]
    y = x * lax.rsqrt(jnp.mean(x * x, axis=-1, keepdims=True) + EPS) * g_ref[...]
    h2 = y * (1.0 + mod_ref[4:5, :]) + mod_ref[3:4, :]
    h_ref[...] = _pack_halves(h2)
    hi = h2.astype(BF16)
    lo = (h2 - hi.astype(F32)).astype(BF16)
    logits = (jnp.dot(hi, whi_ref[...], preferred_element_type=F32)
              + jnp.dot(hi, wlo_ref[...], preferred_element_type=F32)
              + jnp.dot(lo, whi_ref[...], preferred_element_type=F32)) + b_ref[...]
    lane = lax.broadcasted_iota(jnp.int32, logits.shape, 1)
    big = jnp.int32(1 << 20)

    def top(vals):
        mx = jnp.max(vals, axis=1, keepdims=True)
        idx = jnp.min(jnp.where(vals == mx, lane, big), axis=1, keepdims=True)
        return mx, idx

    gl = jnp.where(lane < n_groups, logits, -jnp.inf)
    gmax, gidx = top(gl)
    g_p = 1.0 / jnp.sum(jnp.exp(gl - gmax), axis=1, keepdims=True)
    lo_lane = n_groups + per_group * gidx
    el = jnp.where((lane >= lo_lane) & (lane < lo_lane + per_group), logits, -jnp.inf)
    e1, i1 = top(el)
    e2, i2 = top(jnp.where(lane == i1, -jnp.inf, el))
    r = jnp.exp(e2 - e1)
    gate1 = g_p / (1.0 + r)
    gate2 = g_p * r / (1.0 + r)
    ex1 = i1 - n_groups
    ex2 = i2 - n_groups
    oh1 = (lane == ex1).astype(F32)
    oh2 = (lane == ex2).astype(F32)
    both = oh1 + oh2
    tm = x.shape[0]
    earlier = (lax.broadcasted_iota(jnp.int32, (tm, tm), 1)
               < lax.broadcasted_iota(jnp.int32, (tm, tm), 0)).astype(BF16)
    prefix = jnp.dot(earlier, both.astype(BF16), preferred_element_type=F32) + run_sc[0:1, :]
    rank1 = jnp.sum(oh1 * prefix, axis=1, keepdims=True)
    rank2 = jnp.sum(oh2 * prefix, axis=1, keepdims=True)
    run_sc[...] = run_sc[...] + jnp.sum(both, axis=0, keepdims=True)
    cnt_ref[...] = run_sc[...]

    cols = (ex1.astype(F32), ex2.astype(F32), gate1, gate2, rank1, rank2)
    out = jnp.zeros(logits.shape, F32)
    for n, col in enumerate(cols):
        out = jnp.where(lane == n, col, out)
    r_ref[...] = out


def _router(x1, mod, g, w_group, b_group, w_expert, b_expert, B, S):
    T, D = x1.shape
    NG = w_group.shape[1]
    NE = w_expert.shape[1]
    assert NG + NE <= LANES
    pad = LANES - NG - NE
    w = jnp.concatenate([w_group, w_expert, jnp.zeros((D, pad), F32)], axis=1)
    w_hi = w.astype(BF16)
    w_lo = (w - w_hi.astype(F32)).astype(BF16)
    bias = jnp.concatenate([b_group, b_expert, jnp.full((pad,), -jnp.inf, F32)]).reshape(1, LANES)
    tm = _tile(S, 256)
    nb = S // tm
    kern = functools.partial(_router_kernel, n_groups=NG, per_group=NE // NG)
    return pl.pallas_call(
        kern,
        grid=(B, nb),
        in_specs=[pl.BlockSpec((tm, D), lambda b, i: (b * nb + i, 0)),
                  pl.BlockSpec((None, 6, D), lambda b, i: (b, 0, 0)),
                  pl.BlockSpec((1, D), lambda b, i: (0, 0)),
                  pl.BlockSpec((D, LANES), lambda b, i: (0, 0)),
                  pl.BlockSpec((D, LANES), lambda b, i: (0, 0)),
                  pl.BlockSpec((1, LANES), lambda b, i: (0, 0))],
        out_specs=[pl.BlockSpec((tm, D // 2), lambda b, i: (b * nb + i, 0)),
                   pl.BlockSpec((tm, LANES), lambda b, i: (b * nb + i, 0)),
                   pl.BlockSpec((8, LANES), lambda b, i: (0, 0))],
        out_shape=[jax.ShapeDtypeStruct((T, D // 2), jnp.uint32), jax.ShapeDtypeStruct((T, LANES), F32),
                   jax.ShapeDtypeStruct((8, LANES), F32)],
        scratch_shapes=[pltpu.VMEM((8, LANES), F32)],
        compiler_params=_cparams("arbitrary", "arbitrary"),
        name="router",
    )(x1, mod, g.reshape(1, D), w_hi, w_lo, bias)


def _gather_rows(idx_ref, base, src_hbm, dst_ref, sem, n):
    def body(r, carry):
        tok = idx_ref[base + r]
        pltpu.make_async_copy(src_hbm.at[pl.ds(tok, 1), :], dst_ref.at[pl.ds(r, 1), :], sem).start()
        return carry
    lax.fori_loop(0, n, body, 0, unroll=DMA_UNROLL)


def _wait_rows(src_hbm, dst_ref, sem, n):
    def body(r, carry):
        pltpu.make_async_copy(src_hbm.at[pl.ds(0, 1), :], dst_ref.at[pl.ds(r, 1), :], sem).wait()
        return carry
    lax.fori_loop(0, n, body, 0, unroll=DMA_UNROLL)


def _scatter_kernel(dest_ref, pend_ref, h_ref, xs_hbm, zbuf, sem, zsem, *, tm, tb, n_experts, n_blocks):
    i = pl.program_id(0)

    def zero_block(row0):
        return pltpu.make_async_copy(zbuf, xs_hbm.at[pl.ds(pl.multiple_of(row0, tb), tb), :], zsem)

    def zero_copy(e):
        return zero_block(jnp.maximum(pend_ref[e] - tb, 0))

    @pl.when(i == 0)
    def _():
        zbuf[...] = jnp.zeros(zbuf.shape, zbuf.dtype)

        def start(e, carry):
            @pl.when(pend_ref[e] > 0)
            def _():
                zero_copy(e).start()
            return carry

        def wait(e, carry):
            @pl.when(pend_ref[e] > 0)
            def _():
                zero_copy(e).wait()
            return carry

        def start_tail(b, carry):
            zero_block(b * tb).start()
            return carry

        def wait_tail(b, carry):
            zero_block(b * tb).wait()
            return carry

        first_unused = pend_ref[n_experts - 1] // tb
        lax.fori_loop(0, n_experts, start, 0)
        lax.fori_loop(first_unused, n_blocks, start_tail, 0)
        lax.fori_loop(0, n_experts, wait, 0)
        lax.fori_loop(first_unused, n_blocks, wait_tail, 0)

    def row_copy(r, choice):
        slot = dest_ref[2 * (i * tm + r) + choice]
        return pltpu.make_async_copy(h_ref.at[pl.ds(r, 1), :], xs_hbm.at[pl.ds(slot, 1), :], sem)

    def start_row(r, carry):
        row_copy(r, 0).start()
        row_copy(r, 1).start()
        return carry

    def wait_row(r, carry):
        row_copy(r, 0).wait()
        row_copy(r, 1).wait()
        return carry

    lax.fori_loop(0, tm, start_row, 0, unroll=DMA_UNROLL)
    lax.fori_loop(0, tm, wait_row, 0, unroll=DMA_UNROLL)


def _scatter(h2, dest, pend, n_slots, tb):
    T, D = h2.shape
    tm = _tile(T, 256)
    grid_spec = pltpu.PrefetchScalarGridSpec(
        num_scalar_prefetch=2,
        grid=(T // tm,),
        in_specs=[pl.BlockSpec((tm, D), lambda i, de, pe: (i, 0))],
        out_specs=pl.BlockSpec(memory_space=pl.ANY),
        scratch_shapes=[pltpu.VMEM((tb, D), h2.dtype), pltpu.SemaphoreType.DMA(()),
                        pltpu.SemaphoreType.DMA(())],
    )
    return pl.pallas_call(
        functools.partial(_scatter_kernel, tm=tm, tb=tb, n_experts=pend.shape[0],
                          n_blocks=n_slots // tb),
        grid_spec=grid_spec,
        out_shape=jax.ShapeDtypeStruct((n_slots, D), h2.dtype),
        compiler_params=_cparams("arbitrary"),
        name="scatter",
    )(dest, pend, h2)


def _moe_kernel(be_ref, first_ref, nused_ref, x_ref, wg_ref, wu_ref, wd_ref, y_ref,
                cg_sc, cu_sc, cd_sc, *, nk):
    i = pl.program_id(0)
    k = pl.program_id(1)
    nused = nused_ref[0]
    live = i < nused
    first = first_ref[jnp.minimum(i, nused - 1)] == 1
    half = x_ref.shape[1]
    dk = wd_ref.shape[0]

    for s in range(nk):
        @pl.when(jnp.logical_and(jnp.logical_and(live, first), k == s))
        def _():
            cg_sc[:, s * dk:(s + 1) * dk] = wg_ref[...].astype(BF16)
            cu_sc[:, s * dk:(s + 1) * dk] = wu_ref[...].astype(BF16)
            cd_sc[s * dk:(s + 1) * dk, :] = wd_ref[...].astype(BF16)

    @pl.when(jnp.logical_and(live, k == jnp.where(first, nk - 1, 0)))
    def _():
        x_lo, x_hi = _unpack_halves(x_ref[...])
        x_lo = x_lo.astype(BF16)
        x_hi = x_hi.astype(BF16)
        g = (jnp.dot(x_lo, cg_sc[:half, :], preferred_element_type=F32)
             + jnp.dot(x_hi, cg_sc[half:, :], preferred_element_type=F32))
        u = (jnp.dot(x_lo, cu_sc[:half, :], preferred_element_type=F32)
             + jnp.dot(x_hi, cu_sc[half:, :], preferred_element_type=F32))
        act = (g * jax.nn.sigmoid(g) * u).astype(BF16)
        y_ref[...] = _pack_halves(jnp.dot(act, cd_sc[...], preferred_element_type=F32))

    @pl.when(jnp.logical_and(jnp.logical_not(live), k == 0))
    def _():
        y_ref[...] = jnp.zeros(y_ref.shape, y_ref.dtype)


def _moe(xs, block_e, first, nused, wgu, wd, tb):
    n_slots, hw = xs.shape
    E, D, de2 = wgu.shape
    de = de2 // 2
    nk = MOE_WEIGHT_SLICES
    dk = de // nk
    n_blocks = n_slots // tb

    def blk(i, nu):
        return jnp.minimum(i, nu[0] - 1)

    def piece(i, k, fi, nu):
        streaming = jnp.logical_and(i < nu[0], fi[blk(i, nu)] == 1)
        return jnp.where(streaming, k, nk - 1)

    grid_spec = pltpu.PrefetchScalarGridSpec(
        num_scalar_prefetch=3,
        grid=(n_blocks, nk),
        in_specs=[pl.BlockSpec((tb, hw), lambda i, k, be, fi, nu: (blk(i, nu), 0)),
                  pl.BlockSpec((None, D, dk), lambda i, k, be, fi, nu: (be[blk(i, nu)], 0, piece(i, k, fi, nu))),
                  pl.BlockSpec((None, D, dk),
                               lambda i, k, be, fi, nu: (be[blk(i, nu)], 0, nk + piece(i, k, fi, nu))),
                  pl.BlockSpec((None, dk, D), lambda i, k, be, fi, nu: (be[blk(i, nu)], piece(i, k, fi, nu), 0))],
        out_specs=pl.BlockSpec((tb, hw), lambda i, k, be, fi, nu: (i, 0)),
        scratch_shapes=[pltpu.VMEM((D, de), BF16), pltpu.VMEM((D, de), BF16), pltpu.VMEM((de, D), BF16)],
    )
    return pl.pallas_call(
        functools.partial(_moe_kernel, nk=nk),
        grid_spec=grid_spec,
        out_shape=jax.ShapeDtypeStruct((n_slots, hw), jnp.uint32),
        compiler_params=_cparams("arbitrary", "arbitrary"),
        name="moe",
    )(block_e, first, nused, xs, wgu, wgu, wd)


def _final_kernel(pos_ref, y_hbm, x_ref, r_ref, mod_ref, g_ref, o_ref, ybuf, sem, *, tm, nsteps):
    i = pl.program_id(0)
    slot = lax.rem(i, 2)

    @pl.when(i == 0)
    def _():
        _gather_rows(pos_ref, 0, y_hbm, ybuf.at[0], sem.at[0], 2 * tm)

    @pl.when(i + 1 < nsteps)
    def _():
        _gather_rows(pos_ref, (i + 1) * 2 * tm, y_hbm, ybuf.at[1 - slot], sem.at[1 - slot], 2 * tm)

    _wait_rows(y_hbm, ybuf.at[slot], sem.at[slot], 2 * tm)
    y_lo, y_hi = _unpack_halves(ybuf[slot])
    r = r_ref[...]
    lane = lax.broadcasted_iota(jnp.int32, r.shape, 1)
    gate1 = jnp.sum(jnp.where(lane == 2, r, 0.0), axis=1, keepdims=True)
    gate2 = jnp.sum(jnp.where(lane == 3, r, 0.0), axis=1, keepdims=True)
    half = y_lo.shape[1]
    x_lo = x_ref[:, :half] + mod_ref[5:6, :half] * (gate1 * y_lo[:tm] + gate2 * y_lo[tm:])
    x_hi = x_ref[:, half:] + mod_ref[5:6, half:] * (gate1 * y_hi[:tm] + gate2 * y_hi[tm:])
    ms = (jnp.sum(x_lo * x_lo, axis=-1, keepdims=True)
          + jnp.sum(x_hi * x_hi, axis=-1, keepdims=True)) / (2 * half)
    inv = lax.rsqrt(ms + EPS)
    o_ref[:, :half] = x_lo * inv * g_ref[:, :half]
    o_ref[:, half:] = x_hi * inv * g_ref[:, half:]


def _final(y_slots, dest, route, x1, mod, g, S):
    T, D = x1.shape
    tm = _tile(S, 128)
    per_b = S // tm
    nsteps = T // tm
    pos_tiled = jnp.transpose(dest.reshape(nsteps, tm, 2), (0, 2, 1)).reshape(-1)
    grid_spec = pltpu.PrefetchScalarGridSpec(
        num_scalar_prefetch=1,
        grid=(nsteps,),
        in_specs=[pl.BlockSpec(memory_space=pl.ANY),
                  pl.BlockSpec((tm, D), lambda i, p: (i, 0)),
                  pl.BlockSpec((tm, LANES), lambda i, p: (i, 0)),
                  pl.BlockSpec((None, 6, D), lambda i, p: (i // per_b, 0, 0)),
                  pl.BlockSpec((1, D), lambda i, p: (0, 0))],
        out_specs=pl.BlockSpec((tm, D), lambda i, p: (i, 0)),
        scratch_shapes=[pltpu.VMEM((2, 2 * tm, D // 2), y_slots.dtype), pltpu.SemaphoreType.DMA((2,))],
    )
    return pl.pallas_call(
        functools.partial(_final_kernel, tm=tm, nsteps=nsteps),
        grid_spec=grid_spec,
        out_shape=jax.ShapeDtypeStruct((T, D), F32),
        compiler_params=_cparams("arbitrary"),
        name="final",
    )(pos_tiled, y_slots, x1, route, mod, g.reshape(1, D))


def _dispatch(route, counts, n_experts, tb):
    T = route.shape[0]
    ids = route[:, 0:2].astype(jnp.int32)
    rank = route[:, 4:6].astype(jnp.int32)
    counts = counts[0, :n_experts].astype(jnp.int32)
    padded = (counts + tb - 1) // tb * tb
    pend = jnp.cumsum(padded).astype(jnp.int32)
    pstart = pend - padded
    onehot = ids[:, :, None] == jnp.arange(n_experts, dtype=jnp.int32)
    dest = (rank + jnp.sum(jnp.where(onehot, pstart, 0), axis=-1)).reshape(-1)
    n_blocks = -(-2 * T // tb) + n_experts
    first_row = jnp.arange(n_blocks, dtype=jnp.int32) * tb
    block_e = jnp.minimum(jnp.sum(pend[None, :] <= first_row[:, None], axis=1), n_experts - 1)
    nused = (pend[-1] // tb).reshape(1)
    block_e = block_e.astype(jnp.int32)
    first = jnp.concatenate([jnp.ones((1,), jnp.int32), (block_e[1:] != block_e[:-1]).astype(jnp.int32)])
    return dest, pend, block_e, first, nused, n_blocks * tb


def _layer(x2, c, lidx, B, S, w_ada, b_ada, norm1_g, w_in, conv_qk, b_if, da_lambda, da_norm_g, rel_bias,
           ml_norm_g, w_o_attn, w_o_mlstm, w_out, norm2_g, w_group, b_group, w_expert, b_expert,
           w_gate_up, w_down):
    D = x2.shape[1]
    H = rel_bias.shape[1]
    d = da_lambda.shape[1]
    da_w = H * 2 * d
    MH, dh = ml_norm_g.shape
    ml_w = MH * dh
    n_experts = w_expert.shape[1]

    mod = _ada(c, w_ada, b_ada)

    off_mlq = 3 * da_w
    off_mlv = off_mlq + 2 * ml_w
    off_if = off_mlv + 2 * ml_w
    off_g = off_if + 2 * MH
    w_in_t = w_in.T
    h, gates_if = _norm1(x2, mod, norm1_g, w_in_t[off_if:off_g], b_if, B, S)
    q_scale = jnp.concatenate([jnp.full((da_w,), d ** -0.5 * math.log2(math.e), F32),
                               jnp.ones((2 * da_w,), F32)])
    da_qkv = _proj(h, w_in_t, 0, 3 * da_w, BF16, "proj_da", col_scale=q_scale)
    ml_qk = _proj(h, w_in_t, off_mlq, 2 * ml_w, F32, "proj_mlqk")
    ml_vo = _proj(h, w_in_t, off_mlv, 2 * ml_w, BF16, "proj_mlvo")
    gates_am = _proj(h, w_in_t[off_g:], 0, 2 * D, BF16, "proj_gates")

    lambda_init = 0.8 - 0.6 * math.exp(-0.3 * lidx)
    lamp = da_lambda.astype(F32)
    lam = jnp.exp(jnp.sum(lamp[0] * lamp[1])) - jnp.exp(jnp.sum(lamp[2] * lamp[3])) + lambda_init
    attn = _attn(da_qkv, lam, rel_bias, da_norm_g, B, S, H, d, lambda_init)

    hm = _mlstm(ml_qk, conv_qk, ml_vo, gates_if, ml_norm_g, B, S, MH, dh)

    merged = _merge(attn, hm, w_o_attn, w_o_mlstm, gates_am)
    x1 = _outproj(merged, w_out, x2, mod, S)

    h2, route, counts = _router(x1, mod, norm2_g, w_group, b_group, w_expert, b_expert, B, S)
    tb = 256
    dest, pend, block_e, first, nused, n_slots = _dispatch(route, counts, n_experts, tb)
    xs = _scatter(h2, dest, pend, n_slots, tb)
    y_slots = _moe(xs, block_e, first, nused, w_gate_up, w_down, tb)
    return y_slots, dest, route, x1, mod


def kernel(x, c, w_ada, b_ada, norm1_g, w_in, conv_qk, b_if, da_lambda, da_norm_g, rel_bias, ml_norm_g,
           w_o_attn, w_o_mlstm, w_out, norm2_g, w_group, b_group, w_expert, b_expert, w_gate_up, w_down,
           normf_g):
    B, S, D = x.shape
    assert w_ada.shape[0] == 1, "the final rmsnorm is fused into the layer's last kernel: one layer only"
    l = 0
    y_slots, dest, route, x1, mod = _layer(
        x.reshape(B * S, D), c, l, B, S, w_ada[l], b_ada[l], norm1_g[l], w_in[l], conv_qk[l], b_if[l],
        da_lambda[l], da_norm_g[l], rel_bias, ml_norm_g[l], w_o_attn[l], w_o_mlstm[l], w_out[l],
        norm2_g[l], w_group[l], b_group[l], w_expert[l], b_expert[l], w_gate_up[l], w_down[l])
    return _final(y_slots, dest, route, x1, mod, normf_g, S).reshape(B, S, D)
```

```python
import functools
import math

import jax
import jax.numpy as jnp
from jax import lax
from jax.experimental import pallas as pl
from jax.experimental.pallas import tpu as pltpu

F32 = jnp.float32
BF16 = jnp.bfloat16
EPS = 1e-6
REL_MAX_DIST = 128
ML_CHUNK = 128
NEG_BIG = -1e30
V7X_VMEM_LIMIT = 56 * 1024 * 1024
LANES = 128
DMA_UNROLL = 8
MOE_WEIGHT_SLICES = 2


def _cparams(*sem):
    return pltpu.CompilerParams(dimension_semantics=sem, vmem_limit_bytes=V7X_VMEM_LIMIT)


def _tile(n, pref):
    t = min(n, pref)
    while n % t:
        t //= 2
    return t


def _nt_dot(a, b):
    return lax.dot_general(a, b, (((1,), (1,)), ((), ())), preferred_element_type=F32)


def _tn_dot(a, b):
    return lax.dot_general(a, b, (((0,), (0,)), ((), ())), preferred_element_type=F32)


def _pack_halves(x):
    n = x.shape[1] // 2
    bits = pltpu.bitcast(x.astype(BF16).astype(F32), jnp.uint32)
    return bits[:, n:] | (bits[:, :n] >> 16)


def _unpack_halves(w):
    lo = pltpu.bitcast(w << 16, F32)
    hi = pltpu.bitcast(w & jnp.uint32(0xFFFF0000), F32)
    return lo, hi


def _ada_kernel(c_ref, w_ref, b_ref, o_ref):
    c = c_ref[...]
    s = (c * jax.nn.sigmoid(c)).astype(BF16)
    o_ref[...] = jnp.dot(s, w_ref[...].astype(BF16), preferred_element_type=F32) + b_ref[...]


def _ada(c, w_ada, b_ada):
    B, D = c.shape
    N = w_ada.shape[1]
    rows = 8
    c8 = jnp.zeros((rows, D), F32).at[:B].set(c)
    tn = _tile(N, 512)
    mod = pl.pallas_call(
        _ada_kernel,
        grid=(N // tn,),
        in_specs=[pl.BlockSpec((rows, D), lambda j: (0, 0)),
                  pl.BlockSpec((D, tn), lambda j: (0, j)),
                  pl.BlockSpec((1, tn), lambda j: (0, j))],
        out_specs=pl.BlockSpec((rows, tn), lambda j: (0, j)),
        out_shape=jax.ShapeDtypeStruct((rows, N), F32),
        compiler_params=_cparams("arbitrary"),
        name="ada",
    )(c8, w_ada, b_ada.reshape(1, N))
    return mod[:B].reshape(B, 6, D)


def _norm1_kernel(x_ref, mod_ref, g_ref, wif_ref, bif_ref, h_ref, gate_ref):
    x = x_ref[...]
    y = x * lax.rsqrt(jnp.mean(x * x, axis=-1, keepdims=True) + EPS) * g_ref[...]
    h = (y * (1.0 + mod_ref[1:2, :]) + mod_ref[0:1, :]).astype(BF16)
    h_ref[...] = h
    gate_ref[...] = _nt_dot(h, wif_ref[...].astype(BF16)) + bif_ref[...]


def _norm1(x2, mod, g, w_if_t, b_if, B, S):
    T, D = x2.shape
    G = w_if_t.shape[0]
    tm = _tile(S, 256)
    nb = S // tm
    return pl.pallas_call(
        _norm1_kernel,
        grid=(B, nb),
        in_specs=[pl.BlockSpec((tm, D), lambda b, i: (b * nb + i, 0)),
                  pl.BlockSpec((None, 6, D), lambda b, i: (b, 0, 0)),
                  pl.BlockSpec((1, D), lambda b, i: (0, 0)),
                  pl.BlockSpec((G, D), lambda b, i: (0, 0)),
                  pl.BlockSpec((1, G), lambda b, i: (0, 0))],
        out_specs=[pl.BlockSpec((tm, D), lambda b, i: (b * nb + i, 0)),
                   pl.BlockSpec((tm, G), lambda b, i: (b * nb + i, 0))],
        out_shape=[jax.ShapeDtypeStruct((T, D), BF16), jax.ShapeDtypeStruct((T, G), F32)],
        compiler_params=_cparams("arbitrary", "arbitrary"),
        name="norm1",
    )(x2, mod, g.reshape(1, D), w_if_t, b_if.reshape(1, G))


def _proj_kernel(h_ref, w_ref, o_ref):
    o_ref[...] = _nt_dot(h_ref[...], w_ref[...].astype(BF16)).astype(o_ref.dtype)


def _proj_scaled_kernel(h_ref, w_ref, s_ref, o_ref):
    acc = _nt_dot(h_ref[...], w_ref[...].astype(BF16))
    o_ref[...] = (acc * s_ref[...]).astype(o_ref.dtype)


def _proj(h, w_t, col0, ncols, out_dtype, name, col_scale=None):
    M, K = h.shape
    tm = _tile(M, 1024)
    tn = _tile(math.gcd(ncols, col0) if col0 else ncols, 512)
    j0 = col0 // tn
    in_specs = [pl.BlockSpec((tm, K), lambda i, j: (i, 0)),
                pl.BlockSpec((tn, K), lambda i, j: (j0 + j, 0))]
    args = (h, w_t)
    body = _proj_kernel
    if col_scale is not None:
        in_specs.append(pl.BlockSpec((1, tn), lambda i, j: (0, j)))
        args = (h, w_t, col_scale.reshape(1, ncols))
        body = _proj_scaled_kernel
    return pl.pallas_call(
        body,
        grid=(M // tm, ncols // tn),
        in_specs=in_specs,
        out_specs=pl.BlockSpec((tm, tn), lambda i, j: (i, j)),
        out_shape=jax.ShapeDtypeStruct((M, ncols), out_dtype),
        compiler_params=_cparams("arbitrary", "arbitrary"),
        name=name,
    )(*args)


def _attn_kernel(lam_ref, q_ref, k_ref, v_ref, bd_ref, bp_ref, g_ref, o_ref,
                 m_sc, l_sc, acc_sc, s_sc, p_sc, a_sc, *, d, t, out_scale):
    i = pl.program_id(2)
    q = q_ref[...]
    ngrp = t // LANES
    m_sc[...] = jnp.full(m_sc.shape, NEG_BIG, F32)
    l_sc[...] = jnp.zeros(l_sc.shape, F32)
    acc_sc[...] = jnp.zeros(acc_sc.shape, F32)
    spare = pl.ds(1 - lax.rem(i, 2), 1)
    p_sc[spare] = jnp.zeros((1,) + p_sc.shape[1:], BF16)
    a_sc[spare] = jnp.ones((1,) + a_sc.shape[1:], F32)

    def rows(j):
        return pl.ds(pl.multiple_of(j * t, t), t)

    def scores(j):
        k = k_ref[rows(j), :]
        for m in range(2):
            s_sc[m] = _nt_dot(q[:, m * d:(m + 1) * d], k[:, m * d:(m + 1) * d])

    def accumulate(j, buf):
        v = v_ref[rows(j), :]
        for m in range(2):
            alpha = a_sc[buf, m]
            pv = jnp.dot(p_sc[buf, m], v, preferred_element_type=F32)
            acc_sc[m] = jnp.concatenate([alpha] * (2 * d // LANES), axis=1) * acc_sc[m] + pv

    def softmax(bias_ref, buf):
        for m in range(2):
            s = s_sc[m]
            if bias_ref is not None:
                s = s + bias_ref[...]
            grp = [s[:, g * LANES:(g + 1) * LANES] for g in range(ngrp)]
            m_old = m_sc[m]
            row_max = jnp.max(functools.reduce(jnp.maximum, grp), axis=1, keepdims=True)
            m_new = jnp.maximum(m_old, row_max)
            alpha = jnp.exp2(m_old - m_new)
            p = [jnp.exp2(x - m_new) for x in grp]
            l_sc[m] = alpha * l_sc[m] + functools.reduce(jnp.add, p)
            p_sc[buf, m] = jnp.concatenate([x.astype(BF16) for x in p], axis=1)
            a_sc[buf, m] = alpha
            m_sc[m] = m_new

    def stage(j, bias_ref, buf, last=False):
        accumulate(jnp.maximum(j - 1, 0), 1 - buf)
        softmax(bias_ref, buf)
        if not last:
            scores(j + 1)

    n_far = jnp.maximum(i - 1, 0)
    odd = lax.rem(n_far, 2)

    def far_pair(jj, carry):
        j = odd + 2 * jj
        stage(j, None, 1)
        stage(j + 1, None, 0)
        return carry

    scores(0)

    @pl.when(odd == 1)
    def _():
        stage(0, None, 0)

    lax.fori_loop(0, n_far // 2, far_pair, 0)

    @pl.when(i > 0)
    def _():
        stage(i - 1, bp_ref, 1)

    stage(i, bd_ref, 0, last=True)
    accumulate(i, 0)

    l0 = jnp.sum(l_sc[0], axis=1, keepdims=True)
    l1 = jnp.sum(l_sc[1], axis=1, keepdims=True)
    a = acc_sc[0] / l0 - lam_ref[0] * (acc_sc[1] / l1)
    y = a * lax.rsqrt(jnp.mean(a * a, axis=-1, keepdims=True) + EPS) * g_ref[...]
    o_ref[...] = (y * out_scale).astype(o_ref.dtype)


def _rel_bucket(n, n_buckets):
    max_exact = n_buckets // 2
    nf = jnp.maximum(n, 1).astype(F32)
    large = max_exact + (jnp.log(nf / max_exact) / math.log(REL_MAX_DIST / max_exact)
                         * (n_buckets - max_exact)).astype(jnp.int32)
    large = jnp.minimum(large, n_buckets - 1)
    return jnp.where(n < max_exact, n, large)


def _bias_table(rel, dist, n_buckets):
    bucket = _rel_bucket(dist, n_buckets)
    out = jnp.zeros((rel.shape[1],) + dist.shape, F32)
    for b in range(n_buckets):
        out = jnp.where((bucket == b)[None], rel[b][:, None, None], out)
    return out


def _attn(qkv, lam, rel_bias, norm_g, B, S, H, d, lambda_init):
    T = qkv.shape[0]
    t = _tile(S, 512)
    assert t >= REL_MAX_DIST, "key blocks two or more tiles back must sit in the saturated bucket"
    assert d % LANES == 0
    nq = S // t
    n_buckets = rel_bias.shape[0]
    rel = rel_bias.astype(F32)
    rel = (rel - rel[n_buckets - 1]) * math.log2(math.e)
    dist_d = jnp.arange(t)[:, None] - jnp.arange(t)[None, :]
    bd = jnp.where((dist_d >= 0)[None], _bias_table(rel, jnp.maximum(dist_d, 0), n_buckets), NEG_BIG)
    bp = _bias_table(rel, dist_d + t, n_buckets)
    kern = functools.partial(_attn_kernel, d=d, t=t, out_scale=1.0 - lambda_init)
    return pl.pallas_call(
        kern,
        grid=(B, H, nq),
        in_specs=[pl.BlockSpec(memory_space=pltpu.SMEM),
                  pl.BlockSpec((t, 2 * d), lambda b, h, i: (b * nq + i, h)),
                  pl.BlockSpec((S, 2 * d), lambda b, h, i: (b, H + h)),
                  pl.BlockSpec((S, 2 * d), lambda b, h, i: (b, 2 * H + h)),
                  pl.BlockSpec((None, t, t), lambda b, h, i: (h, 0, 0)),
                  pl.BlockSpec((None, t, t), lambda b, h, i: (h, 0, 0)),
                  pl.BlockSpec((1, 2 * d), lambda b, h, i: (0, 0))],
        out_specs=pl.BlockSpec((t, 2 * d), lambda b, h, i: (b * nq + i, h)),
        out_shape=jax.ShapeDtypeStruct((T, H * 2 * d), BF16),
        scratch_shapes=[pltpu.VMEM((2, t, LANES), F32), pltpu.VMEM((2, t, LANES), F32),
                        pltpu.VMEM((2, t, 2 * d), F32), pltpu.VMEM((2, t, t), F32),
                        pltpu.VMEM((2, 2, t, t), BF16), pltpu.VMEM((2, 2, t, LANES), F32)],
        compiler_params=_cparams("arbitrary", "arbitrary", "arbitrary"),
        name="attn",
    )(lam.reshape(1), qkv, qkv, qkv, bd, bp, norm_g.reshape(1, 2 * d))


def _causal_conv_silu(u_ref, w_ref, ext_ref, scale):
    L = u_ref.shape[0]
    width = w_ref.shape[0]
    u = u_ref[...]
    ext_ref[8:, :] = u
    acc = u * w_ref[width - 1:width, :]
    for t in range(1, width):
        acc = acc + ext_ref[8 - t:8 - t + L, :] * w_ref[width - 1 - t:width - t, :]
    ext_ref[0:8, :] = u[L - 8:, :]
    return (acc * jax.nn.sigmoid(acc) * scale).astype(BF16)


def _mlstm_kernel(q_ref, k_ref, wq_ref, wk_ref, v_ref, og_ref, gcol_ref, grow_ref, g_ref, o_ref,
                  c_sc, n_sc, m_sc, extq_sc, extk_sc, *, nheads, group, dh):
    L = q_ref.shape[0]

    @pl.when(pl.program_id(2) == 0)
    def _():
        c_sc[...] = jnp.zeros(c_sc.shape, F32)
        n_sc[...] = jnp.zeros(n_sc.shape, F32)
        m_sc[...] = jnp.zeros(m_sc.shape, F32)
        extq_sc[0:8, :] = jnp.zeros((8, extq_sc.shape[1]), F32)
        extk_sc[0:8, :] = jnp.zeros((8, extk_sc.shape[1]), F32)

    q_all = _causal_conv_silu(q_ref, wq_ref, extq_sc, 1.0)
    k_all = _causal_conv_silu(k_ref, wk_ref, extk_sc, dh ** -0.5)

    gcol = gcol_ref[...]
    lane = lax.broadcasted_iota(jnp.int32, gcol.shape, 1)
    grow = grow_ref[...]
    sub = lax.broadcasted_iota(jnp.int32, grow.shape, 0)
    jj = lax.broadcasted_iota(jnp.int32, (L, L), 0)
    ss = lax.broadcasted_iota(jnp.int32, (L, L), 1)
    tril = ss <= jj

    for hx in range(group):
        h = pl.program_id(1) * group + hx
        cols = slice(hx * dh, (hx + 1) * dh)
        i_col = jnp.sum(jnp.where(lane == h, gcol, 0.0), axis=1, keepdims=True)
        f_col = jax.nn.log_sigmoid(jnp.sum(jnp.where(lane == h + nheads, gcol, 0.0), axis=1, keepdims=True))
        i_row = jnp.sum(jnp.where(sub == h, grow, 0.0), axis=0, keepdims=True)
        f_row = jax.nn.log_sigmoid(jnp.sum(jnp.where(sub == h + nheads, grow, 0.0), axis=0, keepdims=True))

        b_col = jnp.sum(jnp.where(tril, f_row, 0.0), axis=1, keepdims=True)
        b_row = jnp.sum(jnp.where(jj <= ss, f_col, 0.0), axis=0, keepdims=True)
        u_row = i_row - b_row
        u_col = i_col - b_col

        m_prev = m_sc[hx, 0:1, 0:1]
        mm_col = jnp.maximum(m_prev, jnp.max(jnp.where(tril, u_row, NEG_BIG), axis=1, keepdims=True))
        w = jnp.exp(jnp.where(tril, u_row - mm_col, NEG_BIG))
        a_inter = jnp.exp(m_prev - mm_col)

        q = q_all[:, cols]
        k = k_all[:, cols]
        v = v_ref[:, cols]
        c_old = c_sc[hx]
        n_old = n_sc[hx]
        s_qk = _nt_dot(q, k) * w
        num = (a_inter * _nt_dot(q, c_old.astype(BF16))
               + jnp.dot(s_qk.astype(BF16), v, preferred_element_type=F32))
        den = (a_inter * jnp.sum(q.astype(F32) * n_old, axis=1, keepdims=True)
               + jnp.sum(s_qk, axis=1, keepdims=True))
        hh = num / jnp.maximum(jnp.abs(den), jnp.exp(-(b_col + mm_col)))

        mm_last = mm_col[L - 1:L, :]
        w_s = jnp.exp(u_col - mm_last)
        a_state = jnp.exp(m_prev - mm_last)
        c_sc[hx] = a_state * c_old + _tn_dot((v.astype(F32) * w_s).astype(BF16), k)
        n_sc[hx] = a_state * n_old + jnp.sum(k.astype(F32) * w_s, axis=0, keepdims=True)
        m_sc[hx] = jnp.broadcast_to(b_col[L - 1:L, :] + mm_last, m_sc.shape[1:])

        y = hh * lax.rsqrt(jnp.mean(hh * hh, axis=-1, keepdims=True) + EPS) * g_ref[hx]
        o_ref[:, cols] = (y * jax.nn.sigmoid(og_ref[:, cols].astype(F32))).astype(o_ref.dtype)


def _mlstm(mqk, conv_w, mvo, gates, norm_g, B, S, H, dh):
    T = mqk.shape[0]
    width = conv_w.shape[0]
    L = ML_CHUNK
    nc = S // L
    group = _tile(H, 4)
    ng = H // group
    gw = group * dh
    grow = jnp.transpose(gates.reshape(B, S, 2 * H), (0, 2, 1))
    row = lambda b, h, c: (b * nc + c, h)
    row_hi = lambda b, h, c: (b * nc + c, ng + h)
    return pl.pallas_call(
        functools.partial(_mlstm_kernel, nheads=H, group=group, dh=dh),
        grid=(B, ng, nc),
        in_specs=[pl.BlockSpec((L, gw), row),
                  pl.BlockSpec((L, gw), row_hi),
                  pl.BlockSpec((width, gw), lambda b, h, c: (0, h)),
                  pl.BlockSpec((width, gw), lambda b, h, c: (0, ng + h)),
                  pl.BlockSpec((L, gw), row),
                  pl.BlockSpec((L, gw), row_hi),
                  pl.BlockSpec((L, 2 * H), lambda b, h, c: (b * nc + c, 0)),
                  pl.BlockSpec((None, 2 * H, L), lambda b, h, c: (b, 0, c)),
                  pl.BlockSpec((group, 1, dh), lambda b, h, c: (h, 0, 0))],
        out_specs=pl.BlockSpec((L, gw), row),
        out_shape=jax.ShapeDtypeStruct((T, H * dh), BF16),
        scratch_shapes=[pltpu.VMEM((group, dh, dh), F32), pltpu.VMEM((group, 1, dh), F32),
                        pltpu.VMEM((group, 8, LANES), F32), pltpu.VMEM((L + 8, gw), F32),
                        pltpu.VMEM((L + 8, gw), F32)],
        compiler_params=_cparams("arbitrary", "arbitrary", "arbitrary"),
        name="mlstm",
    )(mqk, mqk, conv_w, conv_w, mvo, mvo, gates, grow, norm_g.reshape(H, 1, dh))


def _merge_kernel(a_ref, m_ref, wa_ref, wm_ref, ga_ref, gm_ref, o_ref):
    pa = jnp.dot(a_ref[...], wa_ref[...].astype(BF16), preferred_element_type=F32)
    pm = jnp.dot(m_ref[...], wm_ref[...].astype(BF16), preferred_element_type=F32)
    out = (jax.nn.sigmoid(ga_ref[...].astype(F32)) * pa + jax.nn.sigmoid(gm_ref[...].astype(F32)) * pm)
    o_ref[...] = out.astype(o_ref.dtype)


def _merge(attn, hm, w_a, w_m, gates_am):
    T, Ka = attn.shape
    Km = hm.shape[1]
    D = w_a.shape[1]
    tm = _tile(T, 1024)
    tn = _tile(D, 256)
    nj = D // tn
    return pl.pallas_call(
        _merge_kernel,
        grid=(T // tm, nj),
        in_specs=[pl.BlockSpec((tm, Ka), lambda i, j: (i, 0)),
                  pl.BlockSpec((tm, Km), lambda i, j: (i, 0)),
                  pl.BlockSpec((Ka, tn), lambda i, j: (0, j)),
                  pl.BlockSpec((Km, tn), lambda i, j: (0, j)),
                  pl.BlockSpec((tm, tn), lambda i, j: (i, j)),
                  pl.BlockSpec((tm, tn), lambda i, j: (i, nj + j))],
        out_specs=pl.BlockSpec((tm, tn), lambda i, j: (i, j)),
        out_shape=jax.ShapeDtypeStruct((T, D), BF16),
        compiler_params=_cparams("arbitrary", "arbitrary"),
        name="merge",
    )(attn, hm, w_a, w_m, gates_am, gates_am)


def _out_kernel(a_ref, w_ref, x_ref, mod_ref, o_ref):
    p = jnp.dot(a_ref[...], w_ref[...].astype(BF16), preferred_element_type=F32)
    o_ref[...] = x_ref[...] + mod_ref[2:3, :] * p


def _outproj(merged, w_out, x2, mod, S):
    T, K = merged.shape
    D = w_out.shape[1]
    tm = _tile(S, 1024)
    tn = _tile(D, 512)
    per_b = S // tm
    return pl.pallas_call(
        _out_kernel,
        grid=(T // tm, D // tn),
        in_specs=[pl.BlockSpec((tm, K), lambda i, j: (i, 0)),
                  pl.BlockSpec((K, tn), lambda i, j: (0, j)),
                  pl.BlockSpec((tm, tn), lambda i, j: (i, j)),
                  pl.BlockSpec((None, 6, tn), lambda i, j: (i // per_b, 0, j))],
        out_specs=pl.BlockSpec((tm, tn), lambda i, j: (i, j)),
        out_shape=jax.ShapeDtypeStruct((T, D), F32),
        compiler_params=_cparams("arbitrary", "arbitrary"),
        name="outproj",
    )(merged, w_out, x2, mod)


def _router_kernel(x_ref, mod_ref, g_ref, whi_ref, wlo_ref, b_ref, h_ref, r_ref, cnt_ref, run_sc,
                   *, n_groups, per_group):
    @pl.when(jnp.logical_and(pl.program_id(0) == 0, pl.program_id(1) == 0))
    def _():
        run_sc[...] = jnp.zeros(run_sc.shape, F32)

    x = x_ref[...]
    y = x * lax.rsqrt(jnp.mean(x * x, axis=-1, keepdims=True) + EPS) * g_ref[...]
    h2 = y * (1.0 + mod_ref[4:5, :]) + mod_ref[3:4, :]
    h_ref[...] = _pack_halves(h2)
    hi = h2.astype(BF16)
    lo = (h2 - hi.astype(F32)).astype(BF16)
    logits = (jnp.dot(hi, whi_ref[...], preferred_element_type=F32)
              + jnp.dot(hi, wlo_ref[...], preferred_element_type=F32)
              + jnp.dot(lo, whi_ref[...], preferred_element_type=F32)) + b_ref[...]
    lane = lax.broadcasted_iota(jnp.int32, logits.shape, 1)
    big = jnp.int32(1 << 20)

    def top(vals):
        mx = jnp.max(vals, axis=1, keepdims=True)
        idx = jnp.min(jnp.where(vals == mx, lane, big), axis=1, keepdims=True)
        return mx, idx

    gl = jnp.where(lane < n_groups, logits, -jnp.inf)
    gmax, gidx = top(gl)
    g_p = 1.0 / jnp.sum(jnp.exp(gl - gmax), axis=1, keepdims=True)
    lo_lane = n_groups + per_group * gidx
    el = jnp.where((lane >= lo_lane) & (lane < lo_lane + per_group), logits, -jnp.inf)
    e1, i1 = top(el)
    e2, i2 = top(jnp.where(lane == i1, -jnp.inf, el))
    r = jnp.exp(e2 - e1)
    gate1 = g_p / (1.0 + r)
    gate2 = g_p * r / (1.0 + r)
    ex1 = i1 - n_groups
    ex2 = i2 - n_groups
    oh1 = (lane == ex1).astype(F32)
    oh2 = (lane == ex2).astype(F32)
    both = oh1 + oh2
    tm = x.shape[0]
    earlier = (lax.broadcasted_iota(jnp.int32, (tm, tm), 1)
               < lax.broadcasted_iota(jnp.int32, (tm, tm), 0)).astype(BF16)
    prefix = jnp.dot(earlier, both.astype(BF16), preferred_element_type=F32) + run_sc[0:1, :]
    rank1 = jnp.sum(oh1 * prefix, axis=1, keepdims=True)
    rank2 = jnp.sum(oh2 * prefix, axis=1, keepdims=True)
    run_sc[...] = run_sc[...] + jnp.sum(both, axis=0, keepdims=True)
    cnt_ref[...] = run_sc[...]

    cols = (ex1.astype(F32), ex2.astype(F32), gate1, gate2, rank1, rank2)
    out = jnp.zeros(logits.shape, F32)
    for n, col in enumerate(cols):
        out = jnp.where(lane == n, col, out)
    r_ref[...] = out


def _router(x1, mod, g, w_group, b_group, w_expert, b_expert, B, S):
    T, D = x1.shape
    NG = w_group.shape[1]
    NE = w_expert.shape[1]
    assert NG + NE <= LANES
    pad = LANES - NG - NE
    w = jnp.concatenate([w_group, w_expert, jnp.zeros((D, pad), F32)], axis=1)
    w_hi = w.astype(BF16)
    w_lo = (w - w_hi.astype(F32)).astype(BF16)
    bias = jnp.concatenate([b_group, b_expert, jnp.full((pad,), -jnp.inf, F32)]).reshape(1, LANES)
    tm = _tile(S, 256)
    nb = S // tm
    kern = functools.partial(_router_kernel, n_groups=NG, per_group=NE // NG)
    return pl.pallas_call(
        kern,
        grid=(B, nb),
        in_specs=[pl.BlockSpec((tm, D), lambda b, i: (b * nb + i, 0)),
                  pl.BlockSpec((None, 6, D), lambda b, i: (b, 0, 0)),
                  pl.BlockSpec((1, D), lambda b, i: (0, 0)),
                  pl.BlockSpec((D, LANES), lambda b, i: (0, 0)),
                  pl.BlockSpec((D, LANES), lambda b, i: (0, 0)),
                  pl.BlockSpec((1, LANES), lambda b, i: (0, 0))],
        out_specs=[pl.BlockSpec((tm, D // 2), lambda b, i: (b * nb + i, 0)),
                   pl.BlockSpec((tm, LANES), lambda b, i: (b * nb + i, 0)),
                   pl.BlockSpec((8, LANES), lambda b, i: (0, 0))],
        out_shape=[jax.ShapeDtypeStruct((T, D // 2), jnp.uint32), jax.ShapeDtypeStruct((T, LANES), F32),
                   jax.ShapeDtypeStruct((8, LANES), F32)],
        scratch_shapes=[pltpu.VMEM((8, LANES), F32)],
        compiler_params=_cparams("arbitrary", "arbitrary"),
        name="router",
    )(x1, mod, g.reshape(1, D), w_hi, w_lo, bias)


def _gather_rows(idx_ref, base, src_hbm, dst_ref, sem, n):
    def body(r, carry):
        tok = idx_ref[base + r]
        pltpu.make_async_copy(src_hbm.at[pl.ds(tok, 1), :], dst_ref.at[pl.ds(r, 1), :], sem).start()
        return carry
    lax.fori_loop(0, n, body, 0, unroll=DMA_UNROLL)


def _wait_rows(src_hbm, dst_ref, sem, n):
    def body(r, carry):
        pltpu.make_async_copy(src_hbm.at[pl.ds(0, 1), :], dst_ref.at[pl.ds(r, 1), :], sem).wait()
        return carry
    lax.fori_loop(0, n, body, 0, unroll=DMA_UNROLL)


def _scatter_kernel(dest_ref, pend_ref, h_ref, xs_hbm, zbuf, sem, zsem, *, tm, tb, n_experts, n_blocks):
    i = pl.program_id(0)

    def zero_block(row0):
        return pltpu.make_async_copy(zbuf, xs_hbm.at[pl.ds(pl.multiple_of(row0, tb), tb), :], zsem)

    def zero_copy(e):
        return zero_block(jnp.maximum(pend_ref[e] - tb, 0))

    @pl.when(i == 0)
    def _():
        zbuf[...] = jnp.zeros(zbuf.shape, zbuf.dtype)

        def start(e, carry):
            @pl.when(pend_ref[e] > 0)
            def _():
                zero_copy(e).start()
            return carry

        def wait(e, carry):
            @pl.when(pend_ref[e] > 0)
            def _():
                zero_copy(e).wait()
            return carry

        def start_tail(b, carry):
            zero_block(b * tb).start()
            return carry

        def wait_tail(b, carry):
            zero_block(b * tb).wait()
            return carry

        first_unused = pend_ref[n_experts - 1] // tb
        lax.fori_loop(0, n_experts, start, 0)
        lax.fori_loop(first_unused, n_blocks, start_tail, 0)
        lax.fori_loop(0, n_experts, wait, 0)
        lax.fori_loop(first_unused, n_blocks, wait_tail, 0)

    def row_copy(r, choice):
        slot = dest_ref[2 * (i * tm + r) + choice]
        return pltpu.make_async_copy(h_ref.at[pl.ds(r, 1), :], xs_hbm.at[pl.ds(slot, 1), :], sem)

    def start_row(r, carry):
        row_copy(r, 0).start()
        row_copy(r, 1).start()
        return carry

    def wait_row(r, carry):
        row_copy(r, 0).wait()
        row_copy(r, 1).wait()
        return carry

    lax.fori_loop(0, tm, start_row, 0, unroll=DMA_UNROLL)
    lax.fori_loop(0, tm, wait_row, 0, unroll=DMA_UNROLL)


def _scatter(h2, dest, pend, n_slots, tb):
    T, D = h2.shape
    tm = _tile(T, 256)
    grid_spec = pltpu.PrefetchScalarGridSpec(
        num_scalar_prefetch=2,
        grid=(T // tm,),
        in_specs=[pl.BlockSpec((tm, D), lambda i, de, pe: (i, 0))],
        out_specs=pl.BlockSpec(memory_space=pl.ANY),
        scratch_shapes=[pltpu.VMEM((tb, D), h2.dtype), pltpu.SemaphoreType.DMA(()),
                        pltpu.SemaphoreType.DMA(())],
    )
    return pl.pallas_call(
        functools.partial(_scatter_kernel, tm=tm, tb=tb, n_experts=pend.shape[0],
                          n_blocks=n_slots // tb),
        grid_spec=grid_spec,
        out_shape=jax.ShapeDtypeStruct((n_slots, D), h2.dtype),
        compiler_params=_cparams("arbitrary"),
        name="scatter",
    )(dest, pend, h2)


def _moe_kernel(be_ref, first_ref, nused_ref, x_ref, wg_ref, wu_ref, wd_ref, y_ref,
                cg_sc, cu_sc, cd_sc, acc_sc, *, nk):
    i = pl.program_id(0)
    k = pl.program_id(1)
    nused = nused_ref[0]
    live = i < nused
    first = first_ref[jnp.minimum(i, nused - 1)] == 1
    half = x_ref.shape[1]
    dk = wd_ref.shape[0]

    def x_halves():
        x_lo, x_hi = _unpack_halves(x_ref[...])
        return x_lo.astype(BF16), x_hi.astype(BF16)

    def expert(cols):
        x_lo, x_hi = x_halves()
        g = (jnp.dot(x_lo, cg_sc[:half, cols], preferred_element_type=F32)
             + jnp.dot(x_hi, cg_sc[half:, cols], preferred_element_type=F32))
        u = (jnp.dot(x_lo, cu_sc[:half, cols], preferred_element_type=F32)
             + jnp.dot(x_hi, cu_sc[half:, cols], preferred_element_type=F32))
        act = (g * jax.nn.sigmoid(g) * u).astype(BF16)
        return jnp.dot(act, cd_sc[cols, :], preferred_element_type=F32)

    for s in range(nk):
        @pl.when(jnp.logical_and(jnp.logical_and(live, first), k == s))
        def _():
            cols = slice(s * dk, (s + 1) * dk)
            cg_sc[:, cols] = wg_ref[...].astype(BF16)
            cu_sc[:, cols] = wu_ref[...].astype(BF16)
            cd_sc[cols, :] = wd_ref[...].astype(BF16)
            part = expert(cols)
            if s == 0:
                acc_sc[...] = part
            elif s < nk - 1:
                acc_sc[...] += part
            else:
                y_ref[...] = _pack_halves(acc_sc[...] + part)

    @pl.when(jnp.logical_and(live, jnp.logical_and(jnp.logical_not(first), k == nk - 1)))
    def _():
        y_ref[...] = _pack_halves(expert(slice(None)))

    @pl.when(jnp.logical_and(jnp.logical_not(live), k == 0))
    def _():
        y_ref[...] = jnp.zeros(y_ref.shape, y_ref.dtype)


def _moe(xs, block_e, first, nused, wgu, wd, tb):
    n_slots, hw = xs.shape
    E, D, de2 = wgu.shape
    de = de2 // 2
    nk = MOE_WEIGHT_SLICES
    dk = de // nk
    n_blocks = n_slots // tb

    def blk(i, nu):
        return jnp.minimum(i, nu[0] - 1)

    def piece(i, k, fi, nu):
        streaming = jnp.logical_and(i < nu[0], fi[blk(i, nu)] == 1)
        return jnp.where(streaming, k, nk - 1)

    grid_spec = pltpu.PrefetchScalarGridSpec(
        num_scalar_prefetch=3,
        grid=(n_blocks, nk),
        in_specs=[pl.BlockSpec((tb, hw), lambda i, k, be, fi, nu: (blk(i, nu), 0)),
                  pl.BlockSpec((None, D, dk), lambda i, k, be, fi, nu: (be[blk(i, nu)], 0, piece(i, k, fi, nu))),
                  pl.BlockSpec((None, D, dk),
                               lambda i, k, be, fi, nu: (be[blk(i, nu)], 0, nk + piece(i, k, fi, nu))),
                  pl.BlockSpec((None, dk, D), lambda i, k, be, fi, nu: (be[blk(i, nu)], piece(i, k, fi, nu), 0))],
        out_specs=pl.BlockSpec((tb, hw), lambda i, k, be, fi, nu: (i, 0)),
        scratch_shapes=[pltpu.VMEM((D, de), BF16), pltpu.VMEM((D, de), BF16), pltpu.VMEM((de, D), BF16),
                        pltpu.VMEM((tb, D), F32)],
    )
    return pl.pallas_call(
        functools.partial(_moe_kernel, nk=nk),
        grid_spec=grid_spec,
        out_shape=jax.ShapeDtypeStruct((n_slots, hw), jnp.uint32),
        compiler_params=_cparams("arbitrary", "arbitrary"),
        name="moe",
    )(block_e, first, nused, xs, wgu, wgu, wd)


def _final_kernel(pos_ref, y_hbm, x_ref, r_ref, mod_ref, g_ref, o_ref, ybuf, sem, *, tm, nsteps):
    i = pl.program_id(0)
    slot = lax.rem(i, 2)

    @pl.when(i == 0)
    def _():
        _gather_rows(pos_ref, 0, y_hbm, ybuf.at[0], sem.at[0], 2 * tm)

    @pl.when(i + 1 < nsteps)
    def _():
        _gather_rows(pos_ref, (i + 1) * 2 * tm, y_hbm, ybuf.at[1 - slot], sem.at[1 - slot], 2 * tm)

    _wait_rows(y_hbm, ybuf.at[slot], sem.at[slot], 2 * tm)
    y_lo, y_hi = _unpack_halves(ybuf[slot])
    r = r_ref[...]
    lane = lax.broadcasted_iota(jnp.int32, r.shape, 1)
    gate1 = jnp.sum(jnp.where(lane == 2, r, 0.0), axis=1, keepdims=True)
    gate2 = jnp.sum(jnp.where(lane == 3, r, 0.0), axis=1, keepdims=True)
    half = y_lo.shape[1]
    x_lo = x_ref[:, :half] + mod_ref[5:6, :half] * (gate1 * y_lo[:tm] + gate2 * y_lo[tm:])
    x_hi = x_ref[:, half:] + mod_ref[5:6, half:] * (gate1 * y_hi[:tm] + gate2 * y_hi[tm:])
    ms = (jnp.sum(x_lo * x_lo, axis=-1, keepdims=True)
          + jnp.sum(x_hi * x_hi, axis=-1, keepdims=True)) / (2 * half)
    inv = lax.rsqrt(ms + EPS)
    o_ref[:, :half] = x_lo * inv * g_ref[:, :half]
    o_ref[:, half:] = x_hi * inv * g_ref[:, half:]


def _final(y_slots, dest, route, x1, mod, g, S):
    T, D = x1.shape
    tm = _tile(S, 128)
    per_b = S // tm
    nsteps = T // tm
    pos_tiled = jnp.transpose(dest.reshape(nsteps, tm, 2), (0, 2, 1)).reshape(-1)
    grid_spec = pltpu.PrefetchScalarGridSpec(
        num_scalar_prefetch=1,
        grid=(nsteps,),
        in_specs=[pl.BlockSpec(memory_space=pl.ANY),
                  pl.BlockSpec((tm, D), lambda i, p: (i, 0)),
                  pl.BlockSpec((tm, LANES), lambda i, p: (i, 0)),
                  pl.BlockSpec((None, 6, D), lambda i, p: (i // per_b, 0, 0)),
                  pl.BlockSpec((1, D), lambda i, p: (0, 0))],
        out_specs=pl.BlockSpec((tm, D), lambda i, p: (i, 0)),
        scratch_shapes=[pltpu.VMEM((2, 2 * tm, D // 2), y_slots.dtype), pltpu.SemaphoreType.DMA((2,))],
    )
    return pl.pallas_call(
        functools.partial(_final_kernel, tm=tm, nsteps=nsteps),
        grid_spec=grid_spec,
        out_shape=jax.ShapeDtypeStruct((T, D), F32),
        compiler_params=_cparams("arbitrary"),
        name="final",
    )(pos_tiled, y_slots, x1, route, mod, g.reshape(1, D))


def _dispatch(route, counts, n_experts, tb):
    T = route.shape[0]
    ids = route[:, 0:2].astype(jnp.int32)
    rank = route[:, 4:6].astype(jnp.int32)
    counts = counts[0, :n_experts].astype(jnp.int32)
    padded = (counts + tb - 1) // tb * tb
    pend = jnp.cumsum(padded).astype(jnp.int32)
    pstart = pend - padded
    onehot = ids[:, :, None] == jnp.arange(n_experts, dtype=jnp.int32)
    dest = (rank + jnp.sum(jnp.where(onehot, pstart, 0), axis=-1)).reshape(-1)
    n_blocks = -(-2 * T // tb) + n_experts
    first_row = jnp.arange(n_blocks, dtype=jnp.int32) * tb
    block_e = jnp.minimum(jnp.sum(pend[None, :] <= first_row[:, None], axis=1), n_experts - 1)
    nused = (pend[-1] // tb).reshape(1)
    block_e = block_e.astype(jnp.int32)
    first = jnp.concatenate([jnp.ones((1,), jnp.int32), (block_e[1:] != block_e[:-1]).astype(jnp.int32)])
    return dest, pend, block_e, first, nused, n_blocks * tb


def _layer(x2, c, lidx, B, S, w_ada, b_ada, norm1_g, w_in, conv_qk, b_if, da_lambda, da_norm_g, rel_bias,
           ml_norm_g, w_o_attn, w_o_mlstm, w_out, norm2_g, w_group, b_group, w_expert, b_expert,
           w_gate_up, w_down):
    D = x2.shape[1]
    H = rel_bias.shape[1]
    d = da_lambda.shape[1]
    da_w = H * 2 * d
    MH, dh = ml_norm_g.shape
    ml_w = MH * dh
    n_experts = w_expert.shape[1]

    mod = _ada(c, w_ada, b_ada)

    off_mlq = 3 * da_w
    off_mlv = off_mlq + 2 * ml_w
    off_if = off_mlv + 2 * ml_w
    off_g = off_if + 2 * MH
    w_in_t = w_in.T
    h, gates_if = _norm1(x2, mod, norm1_g, w_in_t[off_if:off_g], b_if, B, S)
    q_scale = jnp.concatenate([jnp.full((da_w,), d ** -0.5 * math.log2(math.e), F32),
                               jnp.ones((2 * da_w,), F32)])
    da_qkv = _proj(h, w_in_t, 0, 3 * da_w, BF16, "proj_da", col_scale=q_scale)
    ml_qk = _proj(h, w_in_t, off_mlq, 2 * ml_w, F32, "proj_mlqk")
    ml_vo = _proj(h, w_in_t, off_mlv, 2 * ml_w, BF16, "proj_mlvo")
    gates_am = _proj(h, w_in_t[off_g:], 0, 2 * D, BF16, "proj_gates")

    lambda_init = 0.8 - 0.6 * math.exp(-0.3 * lidx)
    lamp = da_lambda.astype(F32)
    lam = jnp.exp(jnp.sum(lamp[0] * lamp[1])) - jnp.exp(jnp.sum(lamp[2] * lamp[3])) + lambda_init
    attn = _attn(da_qkv, lam, rel_bias, da_norm_g, B, S, H, d, lambda_init)

    hm = _mlstm(ml_qk, conv_qk, ml_vo, gates_if, ml_norm_g, B, S, MH, dh)

    merged = _merge(attn, hm, w_o_attn, w_o_mlstm, gates_am)
    x1 = _outproj(merged, w_out, x2, mod, S)

    h2, route, counts = _router(x1, mod, norm2_g, w_group, b_group, w_expert, b_expert, B, S)
    tb = 256
    dest, pend, block_e, first, nused, n_slots = _dispatch(route, counts, n_experts, tb)
    xs = _scatter(h2, dest, pend, n_slots, tb)
    y_slots = _moe(xs, block_e, first, nused, w_gate_up, w_down, tb)
    return y_slots, dest, route, x1, mod


def kernel(x, c, w_ada, b_ada, norm1_g, w_in, conv_qk, b_if, da_lambda, da_norm_g, rel_bias, ml_norm_g,
           w_o_attn, w_o_mlstm, w_out, norm2_g, w_group, b_group, w_expert, b_expert, w_gate_up, w_down,
           normf_g):
    B, S, D = x.shape
    assert w_ada.shape[0] == 1, "the final rmsnorm is fused into the layer's last kernel: one layer only"
    l = 0
    y_slots, dest, route, x1, mod = _layer(
        x.reshape(B * S, D), c, l, B, S, w_ada[l], b_ada[l], norm1_g[l], w_in[l], conv_qk[l], b_if[l],
        da_lambda[l], da_norm_g[l], rel_bias, ml_norm_g[l], w_o_attn[l], w_o_mlstm[l], w_out[l],
        norm2_g[l], w_group[l], b_group[l], w_expert[l], b_expert[l], w_gate_up[l], w_down[l])
    return _final(y_slots, dest, route, x1, mod, normf_g, S).reshape(B, S, D)
```

```python
import functools
import math

import jax
import jax.numpy as jnp
from jax import lax
from jax.experimental import pallas as pl
from jax.experimental.pallas import tpu as pltpu

F32 = jnp.float32
BF16 = jnp.bfloat16
EPS = 1e-6
REL_MAX_DIST = 128
ML_CHUNK = 128
NEG_BIG = -1e30
V7X_VMEM_LIMIT = 56 * 1024 * 1024
LANES = 128
SUBLANES = 8
DMA_UNROLL = 8
MOE_WEIGHT_SLICES = 2


def _cparams(*sem):
    return pltpu.CompilerParams(dimension_semantics=sem, vmem_limit_bytes=V7X_VMEM_LIMIT)


def _tile(n, pref):
    t = min(n, pref)
    while n % t:
        t //= 2
    return t


def _nt_dot(a, b):
    return lax.dot_general(a, b, (((1,), (1,)), ((), ())), preferred_element_type=F32)


def _tn_dot(a, b):
    return lax.dot_general(a, b, (((0,), (0,)), ((), ())), preferred_element_type=F32)


def _pack_halves(x):
    n = x.shape[1] // 2
    bits = pltpu.bitcast(x.astype(BF16).astype(F32), jnp.uint32)
    return bits[:, n:] | (bits[:, :n] >> 16)


def _unpack_halves(w):
    lo = pltpu.bitcast(w << 16, F32)
    hi = pltpu.bitcast(w & jnp.uint32(0xFFFF0000), F32)
    return lo, hi


def _ada_kernel(c_ref, w_ref, b_ref, o_ref):
    c = c_ref[...]
    s = (c * jax.nn.sigmoid(c)).astype(BF16)
    o_ref[...] = jnp.dot(s, w_ref[...].astype(BF16), preferred_element_type=F32) + b_ref[...]


def _ada(c, w_ada, b_ada):
    B, D = c.shape
    N = w_ada.shape[1]
    rows = 8
    c8 = jnp.zeros((rows, D), F32).at[:B].set(c)
    tn = _tile(N, 512)
    mod = pl.pallas_call(
        _ada_kernel,
        grid=(N // tn,),
        in_specs=[pl.BlockSpec((rows, D), lambda j: (0, 0)),
                  pl.BlockSpec((D, tn), lambda j: (0, j)),
                  pl.BlockSpec((1, tn), lambda j: (0, j))],
        out_specs=pl.BlockSpec((rows, tn), lambda j: (0, j)),
        out_shape=jax.ShapeDtypeStruct((rows, N), F32),
        compiler_params=_cparams("arbitrary"),
        name="ada",
    )(c8, w_ada, b_ada.reshape(1, N))
    return mod[:B].reshape(B, 6, D)


def _norm1_kernel(x_ref, mod_ref, g_ref, wif_ref, bif_ref, h_ref, gate_ref):
    x = x_ref[...]
    y = x * lax.rsqrt(jnp.mean(x * x, axis=-1, keepdims=True) + EPS) * g_ref[...]
    h = (y * (1.0 + mod_ref[1:2, :]) + mod_ref[0:1, :]).astype(BF16)
    h_ref[...] = h
    gate_ref[...] = _nt_dot(h, wif_ref[...].astype(BF16)) + bif_ref[...]


def _norm1(x2, mod, g, w_if_t, b_if, B, S):
    T, D = x2.shape
    G = w_if_t.shape[0]
    tm = _tile(S, 256)
    nb = S // tm
    return pl.pallas_call(
        _norm1_kernel,
        grid=(B, nb),
        in_specs=[pl.BlockSpec((tm, D), lambda b, i: (b * nb + i, 0)),
                  pl.BlockSpec((None, 6, D), lambda b, i: (b, 0, 0)),
                  pl.BlockSpec((1, D), lambda b, i: (0, 0)),
                  pl.BlockSpec((G, D), lambda b, i: (0, 0)),
                  pl.BlockSpec((1, G), lambda b, i: (0, 0))],
        out_specs=[pl.BlockSpec((tm, D), lambda b, i: (b * nb + i, 0)),
                   pl.BlockSpec((tm, G), lambda b, i: (b * nb + i, 0))],
        out_shape=[jax.ShapeDtypeStruct((T, D), BF16), jax.ShapeDtypeStruct((T, G), F32)],
        compiler_params=_cparams("arbitrary", "arbitrary"),
        name="norm1",
    )(x2, mod, g.reshape(1, D), w_if_t, b_if.reshape(1, G))


def _proj_kernel(h_ref, w_ref, o_ref):
    o_ref[...] = _nt_dot(h_ref[...], w_ref[...].astype(BF16)).astype(o_ref.dtype)


def _proj_scaled_kernel(h_ref, w_ref, s_ref, o_ref):
    acc = _nt_dot(h_ref[...], w_ref[...].astype(BF16))
    o_ref[...] = (acc * s_ref[...]).astype(o_ref.dtype)


def _proj(h, w_t, col0, ncols, out_dtype, name, col_scale=None):
    M, K = h.shape
    tm = _tile(M, 1024)
    tn = _tile(ncols, 512)
    assert col0 % SUBLANES == 0
    in_specs = [pl.BlockSpec((tm, K), lambda i, j: (i, 0)),
                pl.BlockSpec((pl.Element(tn), pl.Element(K)),
                             lambda i, j: (pl.multiple_of(col0 + j * tn, SUBLANES), 0))]
    args = (h, w_t)
    body = _proj_kernel
    if col_scale is not None:
        in_specs.append(pl.BlockSpec((1, tn), lambda i, j: (0, j)))
        args = (h, w_t, col_scale.reshape(1, ncols))
        body = _proj_scaled_kernel
    return pl.pallas_call(
        body,
        grid=(M // tm, ncols // tn),
        in_specs=in_specs,
        out_specs=pl.BlockSpec((tm, tn), lambda i, j: (i, j)),
        out_shape=jax.ShapeDtypeStruct((M, ncols), out_dtype),
        compiler_params=_cparams("arbitrary", "arbitrary"),
        name=name,
    )(*args)


def _attn_kernel(lam_ref, q_ref, k_ref, v_ref, bd_ref, bp_ref, g_ref, o_ref,
                 m_sc, l_sc, acc_sc, s_sc, p_sc, a_sc, *, d, t, out_scale):
    i = pl.program_id(2)
    q = q_ref[...]
    ngrp = t // LANES
    m_sc[...] = jnp.full(m_sc.shape, NEG_BIG, F32)
    l_sc[...] = jnp.zeros(l_sc.shape, F32)
    acc_sc[...] = jnp.zeros(acc_sc.shape, F32)
    spare = pl.ds(1 - lax.rem(i, 2), 1)
    p_sc[spare] = jnp.zeros((1,) + p_sc.shape[1:], BF16)
    a_sc[spare] = jnp.ones((1,) + a_sc.shape[1:], F32)

    def rows(j):
        return pl.ds(pl.multiple_of(j * t, t), t)

    def scores(j):
        k = k_ref[rows(j), :]
        for m in range(2):
            s_sc[m] = _nt_dot(q[:, m * d:(m + 1) * d], k[:, m * d:(m + 1) * d])

    def accumulate(j, buf):
        v = v_ref[rows(j), :]
        for m in range(2):
            alpha = a_sc[buf, m]
            pv = jnp.dot(p_sc[buf, m], v, preferred_element_type=F32)
            acc_sc[m] = jnp.concatenate([alpha] * (2 * d // LANES), axis=1) * acc_sc[m] + pv

    def softmax(bias_ref, buf):
        for m in range(2):
            s = s_sc[m]
            if bias_ref is not None:
                s = s + bias_ref[...]
            grp = [s[:, g * LANES:(g + 1) * LANES] for g in range(ngrp)]
            m_old = m_sc[m]
            row_max = jnp.max(functools.reduce(jnp.maximum, grp), axis=1, keepdims=True)
            m_new = jnp.maximum(m_old, row_max)
            alpha = jnp.exp2(m_old - m_new)
            p = [jnp.exp2(x - m_new) for x in grp]
            l_sc[m] = alpha * l_sc[m] + functools.reduce(jnp.add, p)
            p_sc[buf, m] = jnp.concatenate([x.astype(BF16) for x in p], axis=1)
            a_sc[buf, m] = alpha
            m_sc[m] = m_new

    def stage(j, bias_ref, buf, last=False):
        accumulate(jnp.maximum(j - 1, 0), 1 - buf)
        softmax(bias_ref, buf)
        if not last:
            scores(j + 1)

    n_far = jnp.maximum(i - 1, 0)
    odd = lax.rem(n_far, 2)

    def far_pair(jj, carry):
        j = odd + 2 * jj
        stage(j, None, 1)
        stage(j + 1, None, 0)
        return carry

    scores(0)

    @pl.when(odd == 1)
    def _():
        stage(0, None, 0)

    lax.fori_loop(0, n_far // 2, far_pair, 0)

    @pl.when(i > 0)
    def _():
        stage(i - 1, bp_ref, 1)

    stage(i, bd_ref, 0, last=True)
    accumulate(i, 0)

    l0 = jnp.sum(l_sc[0], axis=1, keepdims=True)
    l1 = jnp.sum(l_sc[1], axis=1, keepdims=True)
    a = acc_sc[0] / l0 - lam_ref[0] * (acc_sc[1] / l1)
    y = a * lax.rsqrt(jnp.mean(a * a, axis=-1, keepdims=True) + EPS) * g_ref[...]
    o_ref[...] = (y * out_scale).astype(o_ref.dtype)


def _rel_bucket(n, n_buckets):
    max_exact = n_buckets // 2
    nf = jnp.maximum(n, 1).astype(F32)
    large = max_exact + (jnp.log(nf / max_exact) / math.log(REL_MAX_DIST / max_exact)
                         * (n_buckets - max_exact)).astype(jnp.int32)
    large = jnp.minimum(large, n_buckets - 1)
    return jnp.where(n < max_exact, n, large)


def _bias_vector(rel, dist, n_buckets):
    bucket = _rel_bucket(dist, n_buckets)
    out = jnp.zeros((rel.shape[1],) + dist.shape, F32)
    for b in range(n_buckets):
        out = jnp.where((bucket == b)[None], rel[b][:, None], out)
    return out


def _toeplitz(v, t):
    r = jnp.roll(v[:, ::-1], t, axis=1)
    flat = jnp.tile(r, (1, t))[:, :t * (2 * t - 1)]
    return flat.reshape(v.shape[0], t, 2 * t - 1)[:, :, :t]


def _attn(qkv, lam, rel_bias, norm_g, B, S, H, d, lambda_init):
    T = qkv.shape[0]
    t = _tile(S, 512)
    assert t >= REL_MAX_DIST, "key blocks two or more tiles back must sit in the saturated bucket"
    assert d % LANES == 0
    nq = S // t
    n_buckets = rel_bias.shape[0]
    rel = rel_bias.astype(F32)
    rel = (rel - rel[n_buckets - 1]) * math.log2(math.e)
    n = jnp.arange(2 * t)
    dist = n - (t - 1)
    bd = _toeplitz(jnp.where((dist >= 0)[None], _bias_vector(rel, jnp.maximum(dist, 0), n_buckets), NEG_BIG), t)
    bp = _toeplitz(_bias_vector(rel, n + 1, n_buckets), t)
    kern = functools.partial(_attn_kernel, d=d, t=t, out_scale=1.0 - lambda_init)
    return pl.pallas_call(
        kern,
        grid=(B, H, nq),
        in_specs=[pl.BlockSpec(memory_space=pltpu.SMEM),
                  pl.BlockSpec((t, 2 * d), lambda b, h, i: (b * nq + i, h)),
                  pl.BlockSpec((S, 2 * d), lambda b, h, i: (b, H + h)),
                  pl.BlockSpec((S, 2 * d), lambda b, h, i: (b, 2 * H + h)),
                  pl.BlockSpec((None, t, t), lambda b, h, i: (h, 0, 0)),
                  pl.BlockSpec((None, t, t), lambda b, h, i: (h, 0, 0)),
                  pl.BlockSpec((1, 2 * d), lambda b, h, i: (0, 0))],
        out_specs=pl.BlockSpec((t, 2 * d), lambda b, h, i: (b * nq + i, h)),
        out_shape=jax.ShapeDtypeStruct((T, H * 2 * d), BF16),
        scratch_shapes=[pltpu.VMEM((2, t, LANES), F32), pltpu.VMEM((2, t, LANES), F32),
                        pltpu.VMEM((2, t, 2 * d), F32), pltpu.VMEM((2, t, t), F32),
                        pltpu.VMEM((2, 2, t, t), BF16), pltpu.VMEM((2, 2, t, LANES), F32)],
        compiler_params=_cparams("arbitrary", "arbitrary", "arbitrary"),
        name="attn",
    )(lam.reshape(1), qkv, qkv, qkv, bd, bp, norm_g.reshape(1, 2 * d))


def _causal_conv_silu(u_ref, w_ref, ext_ref, scale):
    L = u_ref.shape[0]
    width = w_ref.shape[0]
    u = u_ref[...]
    ext_ref[8:, :] = u
    acc = u * w_ref[width - 1:width, :]
    for t in range(1, width):
        acc = acc + ext_ref[8 - t:8 - t + L, :] * w_ref[width - 1 - t:width - t, :]
    ext_ref[0:8, :] = u[L - 8:, :]
    return (acc * jax.nn.sigmoid(acc) * scale).astype(BF16)


def _mlstm_kernel(q_ref, k_ref, wq_ref, wk_ref, v_ref, og_ref, gcol_ref, grow_ref, g_ref, o_ref,
                  c_sc, n_sc, m_sc, extq_sc, extk_sc, *, nheads, group, dh):
    L = q_ref.shape[0]

    @pl.when(pl.program_id(2) == 0)
    def _():
        c_sc[...] = jnp.zeros(c_sc.shape, F32)
        n_sc[...] = jnp.zeros(n_sc.shape, F32)
        m_sc[...] = jnp.zeros(m_sc.shape, F32)
        extq_sc[0:8, :] = jnp.zeros((8, extq_sc.shape[1]), F32)
        extk_sc[0:8, :] = jnp.zeros((8, extk_sc.shape[1]), F32)

    q_all = _causal_conv_silu(q_ref, wq_ref, extq_sc, 1.0)
    k_all = _causal_conv_silu(k_ref, wk_ref, extk_sc, dh ** -0.5)

    gcol = gcol_ref[...]
    lane = lax.broadcasted_iota(jnp.int32, gcol.shape, 1)
    grow = grow_ref[...]
    sub = lax.broadcasted_iota(jnp.int32, grow.shape, 0)
    jj = lax.broadcasted_iota(jnp.int32, (L, L), 0)
    ss = lax.broadcasted_iota(jnp.int32, (L, L), 1)
    tril = ss <= jj

    for hx in range(group):
        h = pl.program_id(1) * group + hx
        cols = slice(hx * dh, (hx + 1) * dh)
        i_col = jnp.sum(jnp.where(lane == h, gcol, 0.0), axis=1, keepdims=True)
        f_col = jax.nn.log_sigmoid(jnp.sum(jnp.where(lane == h + nheads, gcol, 0.0), axis=1, keepdims=True))
        i_row = jnp.sum(jnp.where(sub == h, grow, 0.0), axis=0, keepdims=True)
        f_row = jax.nn.log_sigmoid(jnp.sum(jnp.where(sub == h + nheads, grow, 0.0), axis=0, keepdims=True))

        b_col = jnp.sum(jnp.where(tril, f_row, 0.0), axis=1, keepdims=True)
        b_row = jnp.sum(jnp.where(jj <= ss, f_col, 0.0), axis=0, keepdims=True)
        u_row = i_row - b_row
        u_col = i_col - b_col

        m_prev = m_sc[hx, 0:1, 0:1]
        mm_col = jnp.maximum(m_prev, jnp.max(jnp.where(tril, u_row, NEG_BIG), axis=1, keepdims=True))
        w = jnp.exp(jnp.where(tril, u_row - mm_col, NEG_BIG))
        a_inter = jnp.exp(m_prev - mm_col)

        q = q_all[:, cols]
        k = k_all[:, cols]
        v = v_ref[:, cols]
        c_old = c_sc[hx]
        n_old = n_sc[hx]
        s_qk = _nt_dot(q, k) * w
        num = (a_inter * _nt_dot(q, c_old.astype(BF16))
               + jnp.dot(s_qk.astype(BF16), v, preferred_element_type=F32))
        den = (a_inter * jnp.sum(q.astype(F32) * n_old, axis=1, keepdims=True)
               + jnp.sum(s_qk, axis=1, keepdims=True))
        hh = num / jnp.maximum(jnp.abs(den), jnp.exp(-(b_col + mm_col)))

        mm_last = mm_col[L - 1:L, :]
        w_s = jnp.exp(u_col - mm_last)
        a_state = jnp.exp(m_prev - mm_last)
        c_sc[hx] = a_state * c_old + _tn_dot((v.astype(F32) * w_s).astype(BF16), k)
        n_sc[hx] = a_state * n_old + jnp.sum(k.astype(F32) * w_s, axis=0, keepdims=True)
        m_sc[hx] = jnp.broadcast_to(b_col[L - 1:L, :] + mm_last, m_sc.shape[1:])

        y = hh * lax.rsqrt(jnp.mean(hh * hh, axis=-1, keepdims=True) + EPS) * g_ref[hx]
        o_ref[:, cols] = (y * jax.nn.sigmoid(og_ref[:, cols].astype(F32))).astype(o_ref.dtype)


def _mlstm(mqk, conv_w, mvo, gates, norm_g, B, S, H, dh):
    T = mqk.shape[0]
    width = conv_w.shape[0]
    L = ML_CHUNK
    nc = S // L
    group = _tile(H, 4)
    ng = H // group
    gw = group * dh
    grow = jnp.transpose(gates.reshape(B, S, 2 * H), (0, 2, 1))
    row = lambda b, h, c: (b * nc + c, h)
    row_hi = lambda b, h, c: (b * nc + c, ng + h)
    return pl.pallas_call(
        functools.partial(_mlstm_kernel, nheads=H, group=group, dh=dh),
        grid=(B, ng, nc),
        in_specs=[pl.BlockSpec((L, gw), row),
                  pl.BlockSpec((L, gw), row_hi),
                  pl.BlockSpec((width, gw), lambda b, h, c: (0, h)),
                  pl.BlockSpec((width, gw), lambda b, h, c: (0, ng + h)),
                  pl.BlockSpec((L, gw), row),
                  pl.BlockSpec((L, gw), row_hi),
                  pl.BlockSpec((L, 2 * H), lambda b, h, c: (b * nc + c, 0)),
                  pl.BlockSpec((None, 2 * H, L), lambda b, h, c: (b, 0, c)),
                  pl.BlockSpec((group, 1, dh), lambda b, h, c: (h, 0, 0))],
        out_specs=pl.BlockSpec((L, gw), row),
        out_shape=jax.ShapeDtypeStruct((T, H * dh), BF16),
        scratch_shapes=[pltpu.VMEM((group, dh, dh), F32), pltpu.VMEM((group, 1, dh), F32),
                        pltpu.VMEM((group, 8, LANES), F32), pltpu.VMEM((L + 8, gw), F32),
                        pltpu.VMEM((L + 8, gw), F32)],
        compiler_params=_cparams("arbitrary", "arbitrary", "arbitrary"),
        name="mlstm",
    )(mqk, mqk, conv_w, conv_w, mvo, mvo, gates, grow, norm_g.reshape(H, 1, dh))


def _merge_kernel(a_ref, m_ref, wa_ref, wm_ref, ga_ref, gm_ref, o_ref):
    pa = jnp.dot(a_ref[...], wa_ref[...].astype(BF16), preferred_element_type=F32)
    pm = jnp.dot(m_ref[...], wm_ref[...].astype(BF16), preferred_element_type=F32)
    out = (jax.nn.sigmoid(ga_ref[...].astype(F32)) * pa + jax.nn.sigmoid(gm_ref[...].astype(F32)) * pm)
    o_ref[...] = out.astype(o_ref.dtype)


def _merge(attn, hm, w_a, w_m, gates_am):
    T, Ka = attn.shape
    Km = hm.shape[1]
    D = w_a.shape[1]
    tm = _tile(T, 1024)
    tn = _tile(D, 256)
    nj = D // tn
    return pl.pallas_call(
        _merge_kernel,
        grid=(T // tm, nj),
        in_specs=[pl.BlockSpec((tm, Ka), lambda i, j: (i, 0)),
                  pl.BlockSpec((tm, Km), lambda i, j: (i, 0)),
                  pl.BlockSpec((Ka, tn), lambda i, j: (0, j)),
                  pl.BlockSpec((Km, tn), lambda i, j: (0, j)),
                  pl.BlockSpec((tm, tn), lambda i, j: (i, j)),
                  pl.BlockSpec((tm, tn), lambda i, j: (i, nj + j))],
        out_specs=pl.BlockSpec((tm, tn), lambda i, j: (i, j)),
        out_shape=jax.ShapeDtypeStruct((T, D), BF16),
        compiler_params=_cparams("arbitrary", "arbitrary"),
        name="merge",
    )(attn, hm, w_a, w_m, gates_am, gates_am)


def _out_kernel(a_ref, w_ref, x_ref, mod_ref, o_ref):
    p = jnp.dot(a_ref[...], w_ref[...].astype(BF16), preferred_element_type=F32)
    o_ref[...] = x_ref[...] + mod_ref[2:3, :] * p


def _outproj(merged, w_out, x2, mod, S):
    T, K = merged.shape
    D = w_out.shape[1]
    tm = _tile(S, 1024)
    tn = _tile(D, 512)
    per_b = S // tm
    return pl.pallas_call(
        _out_kernel,
        grid=(T // tm, D // tn),
        in_specs=[pl.BlockSpec((tm, K), lambda i, j: (i, 0)),
                  pl.BlockSpec((K, tn), lambda i, j: (0, j)),
                  pl.BlockSpec((tm, tn), lambda i, j: (i, j)),
                  pl.BlockSpec((None, 6, tn), lambda i, j: (i // per_b, 0, j))],
        out_specs=pl.BlockSpec((tm, tn), lambda i, j: (i, j)),
        out_shape=jax.ShapeDtypeStruct((T, D), F32),
        compiler_params=_cparams("arbitrary", "arbitrary"),
        name="outproj",
    )(merged, w_out, x2, mod)


def _router_kernel(x_ref, mod_ref, g_ref, whi_ref, wlo_ref, b_ref, h_ref, r_ref, cnt_ref, run_sc,
                   *, n_groups, per_group):
    @pl.when(jnp.logical_and(pl.program_id(0) == 0, pl.program_id(1) == 0))
    def _():
        run_sc[...] = jnp.zeros(run_sc.shape, F32)

    x = x_ref[...]
    y = x * lax.rsqrt(jnp.mean(x * x, axis=-1, keepdims=True) + EPS) * g_ref[...]
    h2 = y * (1.0 + mod_ref[4:5, :]) + mod_ref[3:4, :]
    h_ref[...] = _pack_halves(h2)
    hi = h2.astype(BF16)
    lo = (h2 - hi.astype(F32)).astype(BF16)
    logits = (jnp.dot(hi, whi_ref[...], preferred_element_type=F32)
              + jnp.dot(hi, wlo_ref[...], preferred_element_type=F32)
              + jnp.dot(lo, whi_ref[...], preferred_element_type=F32)) + b_ref[...]
    lane = lax.broadcasted_iota(jnp.int32, logits.shape, 1)
    big = jnp.int32(1 << 20)

    def top(vals):
        mx = jnp.max(vals, axis=1, keepdims=True)
        idx = jnp.min(jnp.where(vals == mx, lane, big), axis=1, keepdims=True)
        return mx, idx

    gl = jnp.where(lane < n_groups, logits, -jnp.inf)
    gmax, gidx = top(gl)
    g_p = 1.0 / jnp.sum(jnp.exp(gl - gmax), axis=1, keepdims=True)
    lo_lane = n_groups + per_group * gidx
    el = jnp.where((lane >= lo_lane) & (lane < lo_lane + per_group), logits, -jnp.inf)
    e1, i1 = top(el)
    e2, i2 = top(jnp.where(lane == i1, -jnp.inf, el))
    r = jnp.exp(e2 - e1)
    gate1 = g_p / (1.0 + r)
    gate2 = g_p * r / (1.0 + r)
    ex1 = i1 - n_groups
    ex2 = i2 - n_groups
    oh1 = (lane == ex1).astype(F32)
    oh2 = (lane == ex2).astype(F32)
    both = oh1 + oh2
    tm = x.shape[0]
    earlier = (lax.broadcasted_iota(jnp.int32, (tm, tm), 1)
               < lax.broadcasted_iota(jnp.int32, (tm, tm), 0)).astype(BF16)
    prefix = jnp.dot(earlier, both.astype(BF16), preferred_element_type=F32) + run_sc[0:1, :]
    rank1 = jnp.sum(oh1 * prefix, axis=1, keepdims=True)
    rank2 = jnp.sum(oh2 * prefix, axis=1, keepdims=True)
    run_sc[...] = run_sc[...] + jnp.sum(both, axis=0, keepdims=True)
    cnt_ref[...] = run_sc[...]

    cols = (ex1.astype(F32), ex2.astype(F32), gate1, gate2, rank1, rank2)
    out = jnp.zeros(logits.shape, F32)
    for n, col in enumerate(cols):
        out = jnp.where(lane == n, col, out)
    r_ref[...] = out


def _router(x1, mod, g, w_group, b_group, w_expert, b_expert, B, S):
    T, D = x1.shape
    NG = w_group.shape[1]
    NE = w_expert.shape[1]
    assert NG + NE <= LANES
    pad = LANES - NG - NE
    w = jnp.concatenate([w_group, w_expert, jnp.zeros((D, pad), F32)], axis=1)
    w_hi = w.astype(BF16)
    w_lo = (w - w_hi.astype(F32)).astype(BF16)
    bias = jnp.concatenate([b_group, b_expert, jnp.full((pad,), -jnp.inf, F32)]).reshape(1, LANES)
    tm = _tile(S, 256)
    nb = S // tm
    kern = functools.partial(_router_kernel, n_groups=NG, per_group=NE // NG)
    return pl.pallas_call(
        kern,
        grid=(B, nb),
        in_specs=[pl.BlockSpec((tm, D), lambda b, i: (b * nb + i, 0)),
                  pl.BlockSpec((None, 6, D), lambda b, i: (b, 0, 0)),
                  pl.BlockSpec((1, D), lambda b, i: (0, 0)),
                  pl.BlockSpec((D, LANES), lambda b, i: (0, 0)),
                  pl.BlockSpec((D, LANES), lambda b, i: (0, 0)),
                  pl.BlockSpec((1, LANES), lambda b, i: (0, 0))],
        out_specs=[pl.BlockSpec((tm, D // 2), lambda b, i: (b * nb + i, 0)),
                   pl.BlockSpec((tm, LANES), lambda b, i: (b * nb + i, 0)),
                   pl.BlockSpec((8, LANES), lambda b, i: (0, 0))],
        out_shape=[jax.ShapeDtypeStruct((T, D // 2), jnp.uint32), jax.ShapeDtypeStruct((T, LANES), F32),
                   jax.ShapeDtypeStruct((8, LANES), F32)],
        scratch_shapes=[pltpu.VMEM((8, LANES), F32)],
        compiler_params=_cparams("arbitrary", "arbitrary"),
        name="router",
    )(x1, mod, g.reshape(1, D), w_hi, w_lo, bias)


def _gather_rows(idx_ref, base, src_hbm, dst_ref, sem, n, unroll=DMA_UNROLL):
    def body(r, carry):
        tok = idx_ref[base + r]
        pltpu.make_async_copy(src_hbm.at[pl.ds(tok, 1), :], dst_ref.at[pl.ds(r, 1), :], sem).start()
        return carry
    lax.fori_loop(0, n, body, 0, unroll=unroll)


def _wait_rows(src_hbm, dst_ref, sem, n):
    def body(r, carry):
        pltpu.make_async_copy(src_hbm.at[pl.ds(0, 1), :], dst_ref.at[pl.ds(r, 1), :], sem).wait()
        return carry
    lax.fori_loop(0, n, body, 0, unroll=DMA_UNROLL)


def _scatter_kernel(dest_ref, pend_ref, h_ref, xs_hbm, zbuf, sem, zsem, *, tm, tb, n_experts, n_blocks):
    i = pl.program_id(0)

    def zero_block(row0):
        return pltpu.make_async_copy(zbuf, xs_hbm.at[pl.ds(pl.multiple_of(row0, tb), tb), :], zsem)

    def zero_copy(e):
        return zero_block(jnp.maximum(pend_ref[e] - tb, 0))

    @pl.when(i == 0)
    def _():
        zbuf[...] = jnp.zeros(zbuf.shape, zbuf.dtype)

        def start(e, carry):
            @pl.when(pend_ref[e] > 0)
            def _():
                zero_copy(e).start()
            return carry

        def wait(e, carry):
            @pl.when(pend_ref[e] > 0)
            def _():
                zero_copy(e).wait()
            return carry

        def start_tail(b, carry):
            zero_block(b * tb).start()
            return carry

        def wait_tail(b, carry):
            zero_block(b * tb).wait()
            return carry

        first_unused = pend_ref[n_experts - 1] // tb
        lax.fori_loop(0, n_experts, start, 0)
        lax.fori_loop(first_unused, n_blocks, start_tail, 0)
        lax.fori_loop(0, n_experts, wait, 0)
        lax.fori_loop(first_unused, n_blocks, wait_tail, 0)

    def row_copy(r, choice):
        slot = dest_ref[2 * (i * tm + r) + choice]
        return pltpu.make_async_copy(h_ref.at[pl.ds(r, 1), :], xs_hbm.at[pl.ds(slot, 1), :], sem)

    def start_row(r, carry):
        row_copy(r, 0).start()
        row_copy(r, 1).start()
        return carry

    def wait_row(r, carry):
        row_copy(r, 0).wait()
        row_copy(r, 1).wait()
        return carry

    lax.fori_loop(0, tm, start_row, 0, unroll=DMA_UNROLL)
    lax.fori_loop(0, tm, wait_row, 0, unroll=DMA_UNROLL)


def _scatter(h2, dest, pend, n_slots, tb):
    T, D = h2.shape
    tm = _tile(T, 256)
    grid_spec = pltpu.PrefetchScalarGridSpec(
        num_scalar_prefetch=2,
        grid=(T // tm,),
        in_specs=[pl.BlockSpec((tm, D), lambda i, de, pe: (i, 0))],
        out_specs=pl.BlockSpec(memory_space=pl.ANY),
        scratch_shapes=[pltpu.VMEM((tb, D), h2.dtype), pltpu.SemaphoreType.DMA(()),
                        pltpu.SemaphoreType.DMA(())],
    )
    return pl.pallas_call(
        functools.partial(_scatter_kernel, tm=tm, tb=tb, n_experts=pend.shape[0],
                          n_blocks=n_slots // tb),
        grid_spec=grid_spec,
        out_shape=jax.ShapeDtypeStruct((n_slots, D), h2.dtype),
        compiler_params=_cparams("arbitrary"),
        name="scatter",
    )(dest, pend, h2)


def _moe_kernel(be_ref, first_ref, nused_ref, x_ref, wg_ref, wu_ref, wd_ref, y_ref,
                cg_sc, cu_sc, cd_sc, acc_sc, *, nk):
    i = pl.program_id(0)
    k = pl.program_id(1)
    nused = nused_ref[0]
    live = i < nused
    first = first_ref[jnp.minimum(i, nused - 1)] == 1
    half = x_ref.shape[1]
    dk = wd_ref.shape[0]

    def x_halves():
        x_lo, x_hi = _unpack_halves(x_ref[...])
        return x_lo.astype(BF16), x_hi.astype(BF16)

    def expert(cols):
        x_lo, x_hi = x_halves()
        g = (jnp.dot(x_lo, cg_sc[:half, cols], preferred_element_type=F32)
             + jnp.dot(x_hi, cg_sc[half:, cols], preferred_element_type=F32))
        u = (jnp.dot(x_lo, cu_sc[:half, cols], preferred_element_type=F32)
             + jnp.dot(x_hi, cu_sc[half:, cols], preferred_element_type=F32))
        act = (g * jax.nn.sigmoid(g) * u).astype(BF16)
        return jnp.dot(act, cd_sc[cols, :], preferred_element_type=F32)

    for s in range(nk):
        @pl.when(jnp.logical_and(jnp.logical_and(live, first), k == s))
        def _():
            cols = slice(s * dk, (s + 1) * dk)
            cg_sc[:, cols] = wg_ref[...].astype(BF16)
            cu_sc[:, cols] = wu_ref[...].astype(BF16)
            cd_sc[cols, :] = wd_ref[...].astype(BF16)
            part = expert(cols)
            if s == 0:
                acc_sc[...] = part
            elif s < nk - 1:
                acc_sc[...] += part
            else:
                y_ref[...] = _pack_halves(acc_sc[...] + part)

    @pl.when(jnp.logical_and(live, jnp.logical_and(jnp.logical_not(first), k == nk - 1)))
    def _():
        y_ref[...] = _pack_halves(expert(slice(None)))

    @pl.when(jnp.logical_and(jnp.logical_not(live), k == 0))
    def _():
        y_ref[...] = jnp.zeros(y_ref.shape, y_ref.dtype)


def _moe(xs, block_e, first, nused, wgu, wd, tb):
    n_slots, hw = xs.shape
    E, D, de2 = wgu.shape
    de = de2 // 2
    nk = MOE_WEIGHT_SLICES
    dk = de // nk
    n_blocks = n_slots // tb

    def blk(i, nu):
        return jnp.minimum(i, nu[0] - 1)

    def piece(i, k, fi, nu):
        streaming = jnp.logical_and(i < nu[0], fi[blk(i, nu)] == 1)
        return jnp.where(streaming, k, nk - 1)

    grid_spec = pltpu.PrefetchScalarGridSpec(
        num_scalar_prefetch=3,
        grid=(n_blocks, nk),
        in_specs=[pl.BlockSpec((tb, hw), lambda i, k, be, fi, nu: (blk(i, nu), 0)),
                  pl.BlockSpec((None, D, dk), lambda i, k, be, fi, nu: (be[blk(i, nu)], 0, piece(i, k, fi, nu))),
                  pl.BlockSpec((None, D, dk),
                               lambda i, k, be, fi, nu: (be[blk(i, nu)], 0, nk + piece(i, k, fi, nu))),
                  pl.BlockSpec((None, dk, D), lambda i, k, be, fi, nu: (be[blk(i, nu)], piece(i, k, fi, nu), 0))],
        out_specs=pl.BlockSpec((tb, hw), lambda i, k, be, fi, nu: (i, 0)),
        scratch_shapes=[pltpu.VMEM((D, de), BF16), pltpu.VMEM((D, de), BF16), pltpu.VMEM((de, D), BF16),
                        pltpu.VMEM((tb, D), F32)],
    )
    return pl.pallas_call(
        functools.partial(_moe_kernel, nk=nk),
        grid_spec=grid_spec,
        out_shape=jax.ShapeDtypeStruct((n_slots, hw), jnp.uint32),
        compiler_params=_cparams("arbitrary", "arbitrary"),
        name="moe",
    )(block_e, first, nused, xs, wgu, wgu, wd)


def _final_kernel(pos_ref, y_hbm, x_ref, r_ref, mod_ref, g_ref, o_ref, ybuf, sem, *, tm, nsteps):
    i = pl.program_id(0)
    n_sub = 2 * nsteps
    rows = 2 * tm

    def gather(sub, buf):
        _gather_rows(pos_ref, sub * rows, y_hbm, ybuf.at[buf], sem.at[buf], rows, unroll=True)

    def finish(buf):
        tok = pl.ds(buf * tm, tm)
        y_lo, y_hi = _unpack_halves(ybuf[buf])
        r = r_ref[tok, :]
        lane = lax.broadcasted_iota(jnp.int32, r.shape, 1)
        gate1 = jnp.sum(jnp.where(lane == 2, r, 0.0), axis=1, keepdims=True)
        gate2 = jnp.sum(jnp.where(lane == 3, r, 0.0), axis=1, keepdims=True)
        half = y_lo.shape[1]
        x_lo = x_ref[tok, :half] + mod_ref[5:6, :half] * (gate1 * y_lo[:tm] + gate2 * y_lo[tm:])
        x_hi = x_ref[tok, half:] + mod_ref[5:6, half:] * (gate1 * y_hi[:tm] + gate2 * y_hi[tm:])
        ms = (jnp.sum(x_lo * x_lo, axis=-1, keepdims=True)
              + jnp.sum(x_hi * x_hi, axis=-1, keepdims=True)) / (2 * half)
        inv = lax.rsqrt(ms + EPS)
        o_ref[tok, :half] = x_lo * inv * g_ref[:, :half]
        o_ref[tok, half:] = x_hi * inv * g_ref[:, half:]

    @pl.when(i == 0)
    def _():
        gather(0, 0)

    _wait_rows(y_hbm, ybuf.at[0], sem.at[0], rows)
    gather(2 * i + 1, 1)
    finish(0)
    _wait_rows(y_hbm, ybuf.at[1], sem.at[1], rows)
    gather(jnp.minimum(2 * i + 2, n_sub - 1), 0)
    finish(1)

    @pl.when(i == nsteps - 1)
    def _():
        _wait_rows(y_hbm, ybuf.at[0], sem.at[0], rows)


def _final(y_slots, dest, route, x1, mod, g, S):
    T, D = x1.shape
    tm = _tile(S // 2, 128)
    per_b = S // (2 * tm)
    nsteps = T // (2 * tm)
    pos_tiled = jnp.transpose(dest.reshape(2 * nsteps, tm, 2), (0, 2, 1)).reshape(-1)
    grid_spec = pltpu.PrefetchScalarGridSpec(
        num_scalar_prefetch=1,
        grid=(nsteps,),
        in_specs=[pl.BlockSpec(memory_space=pl.ANY),
                  pl.BlockSpec((2 * tm, D), lambda i, p: (i, 0)),
                  pl.BlockSpec((2 * tm, LANES), lambda i, p: (i, 0)),
                  pl.BlockSpec((None, 6, D), lambda i, p: (i // per_b, 0, 0)),
                  pl.BlockSpec((1, D), lambda i, p: (0, 0))],
        out_specs=pl.BlockSpec((2 * tm, D), lambda i, p: (i, 0)),
        scratch_shapes=[pltpu.VMEM((2, 2 * tm, D // 2), y_slots.dtype), pltpu.SemaphoreType.DMA((2,))],
    )
    return pl.pallas_call(
        functools.partial(_final_kernel, tm=tm, nsteps=nsteps),
        grid_spec=grid_spec,
        out_shape=jax.ShapeDtypeStruct((T, D), F32),
        compiler_params=_cparams("arbitrary"),
        name="final",
    )(pos_tiled, y_slots, x1, route, mod, g.reshape(1, D))


def _dispatch(route, counts, n_experts, tb):
    T = route.shape[0]
    ids = route[:, 0:2].astype(jnp.int32)
    rank = route[:, 4:6].astype(jnp.int32)
    counts = counts[0, :n_experts].astype(jnp.int32)
    padded = (counts + tb - 1) // tb * tb
    pend = jnp.cumsum(padded).astype(jnp.int32)
    pstart = pend - padded
    onehot = ids[:, :, None] == jnp.arange(n_experts, dtype=jnp.int32)
    dest = (rank + jnp.sum(jnp.where(onehot, pstart, 0), axis=-1)).reshape(-1)
    n_blocks = -(-2 * T // tb) + n_experts
    first_row = jnp.arange(n_blocks, dtype=jnp.int32) * tb
    block_e = jnp.minimum(jnp.sum(pend[None, :] <= first_row[:, None], axis=1), n_experts - 1)
    nused = (pend[-1] // tb).reshape(1)
    block_e = block_e.astype(jnp.int32)
    first = jnp.concatenate([jnp.ones((1,), jnp.int32), (block_e[1:] != block_e[:-1]).astype(jnp.int32)])
    return dest, pend, block_e, first, nused, n_blocks * tb


def _layer(x2, c, lidx, B, S, w_ada, b_ada, norm1_g, w_in, conv_qk, b_if, da_lambda, da_norm_g, rel_bias,
           ml_norm_g, w_o_attn, w_o_mlstm, w_out, norm2_g, w_group, b_group, w_expert, b_expert,
           w_gate_up, w_down):
    D = x2.shape[1]
    H = rel_bias.shape[1]
    d = da_lambda.shape[1]
    da_w = H * 2 * d
    MH, dh = ml_norm_g.shape
    ml_w = MH * dh
    n_experts = w_expert.shape[1]

    mod = _ada(c, w_ada, b_ada)

    off_mlq = 3 * da_w
    off_mlv = off_mlq + 2 * ml_w
    off_if = off_mlv + 2 * ml_w
    off_g = off_if + 2 * MH
    w_in_t = w_in.T
    h, gates_if = _norm1(x2, mod, norm1_g, w_in_t[off_if:off_g], b_if, B, S)
    q_scale = jnp.concatenate([jnp.full((da_w,), d ** -0.5 * math.log2(math.e), F32),
                               jnp.ones((2 * da_w,), F32)])
    da_qkv = _proj(h, w_in_t, 0, 3 * da_w, BF16, "proj_da", col_scale=q_scale)
    ml_qk = _proj(h, w_in_t, off_mlq, 2 * ml_w, F32, "proj_mlqk")
    ml_vo = _proj(h, w_in_t, off_mlv, 2 * ml_w, BF16, "proj_mlvo")
    gates_am = _proj(h, w_in_t, off_g, 2 * D, BF16, "proj_gates")

    lambda_init = 0.8 - 0.6 * math.exp(-0.3 * lidx)
    lamp = da_lambda.astype(F32)
    lam = jnp.exp(jnp.sum(lamp[0] * lamp[1])) - jnp.exp(jnp.sum(lamp[2] * lamp[3])) + lambda_init
    attn = _attn(da_qkv, lam, rel_bias, da_norm_g, B, S, H, d, lambda_init)

    hm = _mlstm(ml_qk, conv_qk, ml_vo, gates_if, ml_norm_g, B, S, MH, dh)

    merged = _merge(attn, hm, w_o_attn, w_o_mlstm, gates_am)
    x1 = _outproj(merged, w_out, x2, mod, S)

    h2, route, counts = _router(x1, mod, norm2_g, w_group, b_group, w_expert, b_expert, B, S)
    tb = 256
    dest, pend, block_e, first, nused, n_slots = _dispatch(route, counts, n_experts, tb)
    xs = _scatter(h2, dest, pend, n_slots, tb)
    y_slots = _moe(xs, block_e, first, nused, w_gate_up, w_down, tb)
    return y_slots, dest, route, x1, mod


def kernel(x, c, w_ada, b_ada, norm1_g, w_in, conv_qk, b_if, da_lambda, da_norm_g, rel_bias, ml_norm_g,
           w_o_attn, w_o_mlstm, w_out, norm2_g, w_group, b_group, w_expert, b_expert, w_gate_up, w_down,
           normf_g):
    B, S, D = x.shape
    assert w_ada.shape[0] == 1, "the final rmsnorm is fused into the layer's last kernel: one layer only"
    l = 0
    y_slots, dest, route, x1, mod = _layer(
        x.reshape(B * S, D), c, l, B, S, w_ada[l], b_ada[l], norm1_g[l], w_in[l], conv_qk[l], b_if[l],
        da_lambda[l], da_norm_g[l], rel_bias, ml_norm_g[l], w_o_attn[l], w_o_mlstm[l], w_out[l],
        norm2_g[l], w_group[l], b_group[l], w_expert[l], b_expert[l], w_gate_up[l], w_down[l])
    return _final(y_slots, dest, route, x1, mod, normf_g, S).reshape(B, S, D)
```

```python
import functools
import math

import jax
import jax.numpy as jnp
from jax import lax
from jax.experimental import pallas as pl
from jax.experimental.pallas import tpu as pltpu

F32 = jnp.float32
BF16 = jnp.bfloat16
EPS = 1e-6
REL_MAX_DIST = 128
ML_CHUNK = 128
NEG_BIG = -1e30
V7X_VMEM_LIMIT = 56 * 1024 * 1024
LANES = 128
SUBLANES = 8
DMA_UNROLL = 8
MOE_WEIGHT_SLICES = 2


def _cparams(*sem):
    return pltpu.CompilerParams(dimension_semantics=sem, vmem_limit_bytes=V7X_VMEM_LIMIT)


def _tile(n, pref):
    t = min(n, pref)
    while n % t:
        t //= 2
    return t


def _nt_dot(a, b):
    return lax.dot_general(a, b, (((1,), (1,)), ((), ())), preferred_element_type=F32)


def _tn_dot(a, b):
    return lax.dot_general(a, b, (((0,), (0,)), ((), ())), preferred_element_type=F32)


def _pack_halves(x):
    n = x.shape[1] // 2
    bits = pltpu.bitcast(x.astype(BF16).astype(F32), jnp.uint32)
    return bits[:, n:] | (bits[:, :n] >> 16)


def _unpack_halves(w):
    lo = pltpu.bitcast(w << 16, F32)
    hi = pltpu.bitcast(w & jnp.uint32(0xFFFF0000), F32)
    return lo, hi


def _ada_kernel(c_ref, w_ref, b_ref, o_ref):
    c = c_ref[...]
    s = (c * jax.nn.sigmoid(c)).astype(BF16)
    o_ref[...] = jnp.dot(s, w_ref[...].astype(BF16), preferred_element_type=F32) + b_ref[...]


def _ada(c, w_ada, b_ada):
    B, D = c.shape
    N = w_ada.shape[1]
    rows = 8
    c8 = jnp.zeros((rows, D), F32).at[:B].set(c)
    tn = _tile(N, 512)
    mod = pl.pallas_call(
        _ada_kernel,
        grid=(N // tn,),
        in_specs=[pl.BlockSpec((rows, D), lambda j: (0, 0)),
                  pl.BlockSpec((D, tn), lambda j: (0, j)),
                  pl.BlockSpec((1, tn), lambda j: (0, j))],
        out_specs=pl.BlockSpec((rows, tn), lambda j: (0, j)),
        out_shape=jax.ShapeDtypeStruct((rows, N), F32),
        compiler_params=_cparams("arbitrary"),
        name="ada",
    )(c8, w_ada, b_ada.reshape(1, N))
    return mod[:B].reshape(B, 6, D)


def _norm1_kernel(x_ref, mod_ref, g_ref, wif_ref, bif_ref, h_ref, gate_ref):
    x = x_ref[...]
    y = x * lax.rsqrt(jnp.mean(x * x, axis=-1, keepdims=True) + EPS) * g_ref[...]
    h = (y * (1.0 + mod_ref[1:2, :]) + mod_ref[0:1, :]).astype(BF16)
    h_ref[...] = h
    gate_ref[...] = _nt_dot(h, wif_ref[...].astype(BF16)) + bif_ref[...]


def _norm1(x2, mod, g, w_if_t, b_if, B, S):
    T, D = x2.shape
    G = w_if_t.shape[0]
    tm = _tile(S, 256)
    nb = S // tm
    return pl.pallas_call(
        _norm1_kernel,
        grid=(B, nb),
        in_specs=[pl.BlockSpec((tm, D), lambda b, i: (b * nb + i, 0)),
                  pl.BlockSpec((None, 6, D), lambda b, i: (b, 0, 0)),
                  pl.BlockSpec((1, D), lambda b, i: (0, 0)),
                  pl.BlockSpec((G, D), lambda b, i: (0, 0)),
                  pl.BlockSpec((1, G), lambda b, i: (0, 0))],
        out_specs=[pl.BlockSpec((tm, D), lambda b, i: (b * nb + i, 0)),
                   pl.BlockSpec((tm, G), lambda b, i: (b * nb + i, 0))],
        out_shape=[jax.ShapeDtypeStruct((T, D), BF16), jax.ShapeDtypeStruct((T, G), F32)],
        compiler_params=_cparams("arbitrary", "arbitrary"),
        name="norm1",
    )(x2, mod, g.reshape(1, D), w_if_t, b_if.reshape(1, G))


def _proj_kernel(h_ref, w_ref, o_ref):
    o_ref[...] = _nt_dot(h_ref[...], w_ref[...].astype(BF16)).astype(o_ref.dtype)


def _proj_scaled_kernel(h_ref, w_ref, s_ref, o_ref):
    acc = _nt_dot(h_ref[...], w_ref[...].astype(BF16))
    o_ref[...] = (acc * s_ref[...]).astype(o_ref.dtype)


def _proj(h, w_t, col0, ncols, out_dtype, name, col_scale=None):
    M, K = h.shape
    tm = _tile(M, 1024)
    tn = _tile(ncols, 512)
    assert col0 % SUBLANES == 0
    in_specs = [pl.BlockSpec((tm, K), lambda i, j: (i, 0)),
                pl.BlockSpec((pl.Element(tn), pl.Element(K)),
                             lambda i, j: (pl.multiple_of(col0 + j * tn, SUBLANES), 0))]
    args = (h, w_t)
    body = _proj_kernel
    if col_scale is not None:
        in_specs.append(pl.BlockSpec((1, tn), lambda i, j: (0, j)))
        args = (h, w_t, col_scale.reshape(1, ncols))
        body = _proj_scaled_kernel
    return pl.pallas_call(
        body,
        grid=(M // tm, ncols // tn),
        in_specs=in_specs,
        out_specs=pl.BlockSpec((tm, tn), lambda i, j: (i, j)),
        out_shape=jax.ShapeDtypeStruct((M, ncols), out_dtype),
        compiler_params=_cparams("arbitrary", "arbitrary"),
        name=name,
    )(*args)


def _attn_kernel(lam_ref, q_ref, k_ref, v_ref, bd_ref, bp_ref, g_ref, o_ref,
                 m_sc, l_sc, acc_sc, s_sc, p_sc, a_sc, *, d, t, out_scale):
    i = pl.program_id(2)
    q = q_ref[...]
    ngrp = t // LANES
    m_sc[...] = jnp.full(m_sc.shape, NEG_BIG, F32)
    l_sc[...] = jnp.zeros(l_sc.shape, F32)
    acc_sc[...] = jnp.zeros(acc_sc.shape, F32)
    spare = pl.ds(1 - lax.rem(i, 2), 1)
    p_sc[spare] = jnp.zeros((1,) + p_sc.shape[1:], BF16)
    a_sc[spare] = jnp.ones((1,) + a_sc.shape[1:], F32)

    def rows(j):
        return pl.ds(pl.multiple_of(j * t, t), t)

    def scores(j):
        k = k_ref[rows(j), :]
        for m in range(2):
            s_sc[m] = _nt_dot(q[:, m * d:(m + 1) * d], k[:, m * d:(m + 1) * d])

    def accumulate(j, buf):
        v = v_ref[rows(j), :]
        for m in range(2):
            alpha = a_sc[buf, m]
            pv = jnp.dot(p_sc[buf, m], v, preferred_element_type=F32)
            acc_sc[m] = jnp.concatenate([alpha] * (2 * d // LANES), axis=1) * acc_sc[m] + pv

    def softmax(bias_ref, buf):
        for m in range(2):
            s = s_sc[m]
            if bias_ref is not None:
                s = s + bias_ref[...]
            grp = [s[:, g * LANES:(g + 1) * LANES] for g in range(ngrp)]
            m_old = m_sc[m]
            row_max = jnp.max(functools.reduce(jnp.maximum, grp), axis=1, keepdims=True)
            m_new = jnp.maximum(m_old, row_max)
            alpha = jnp.exp2(m_old - m_new)
            p = [jnp.exp2(x - m_new) for x in grp]
            l_sc[m] = alpha * l_sc[m] + functools.reduce(jnp.add, p)
            p_sc[buf, m] = jnp.concatenate([x.astype(BF16) for x in p], axis=1)
            a_sc[buf, m] = alpha
            m_sc[m] = m_new

    def stage(j, bias_ref, buf, last=False):
        accumulate(jnp.maximum(j - 1, 0), 1 - buf)
        softmax(bias_ref, buf)
        if not last:
            scores(j + 1)

    n_far = jnp.maximum(i - 1, 0)
    odd = lax.rem(n_far, 2)

    def far_pair(jj, carry):
        j = odd + 2 * jj
        stage(j, None, 1)
        stage(j + 1, None, 0)
        return carry

    scores(0)

    @pl.when(odd == 1)
    def _():
        stage(0, None, 0)

    lax.fori_loop(0, n_far // 2, far_pair, 0)

    @pl.when(i > 0)
    def _():
        stage(i - 1, bp_ref, 1)

    stage(i, bd_ref, 0, last=True)
    accumulate(i, 0)

    l0 = jnp.sum(l_sc[0], axis=1, keepdims=True)
    l1 = jnp.sum(l_sc[1], axis=1, keepdims=True)
    a = acc_sc[0] / l0 - lam_ref[0] * (acc_sc[1] / l1)
    y = a * lax.rsqrt(jnp.mean(a * a, axis=-1, keepdims=True) + EPS) * g_ref[...]
    o_ref[...] = (y * out_scale).astype(o_ref.dtype)


def _rel_bucket(n, n_buckets):
    max_exact = n_buckets // 2
    nf = jnp.maximum(n, 1).astype(F32)
    large = max_exact + (jnp.log(nf / max_exact) / math.log(REL_MAX_DIST / max_exact)
                         * (n_buckets - max_exact)).astype(jnp.int32)
    large = jnp.minimum(large, n_buckets - 1)
    return jnp.where(n < max_exact, n, large)


def _bias_table(rel, dist, n_buckets):
    bucket = _rel_bucket(dist, n_buckets)
    out = jnp.zeros((rel.shape[1],) + dist.shape, F32)
    for b in range(n_buckets):
        out = jnp.where((bucket == b)[None], rel[b][:, None, None], out)
    return out


def _attn(qkv, lam, rel_bias, norm_g, B, S, H, d, lambda_init):
    T = qkv.shape[0]
    t = _tile(S, 512)
    assert t >= REL_MAX_DIST, "key blocks two or more tiles back must sit in the saturated bucket"
    assert d % LANES == 0
    nq = S // t
    n_buckets = rel_bias.shape[0]
    rel = rel_bias.astype(F32)
    rel = (rel - rel[n_buckets - 1]) * math.log2(math.e)
    dist_d = jnp.arange(t)[:, None] - jnp.arange(t)[None, :]
    bd = jnp.where((dist_d >= 0)[None], _bias_table(rel, jnp.maximum(dist_d, 0), n_buckets), NEG_BIG)
    bp = _bias_table(rel, dist_d + t, n_buckets)
    kern = functools.partial(_attn_kernel, d=d, t=t, out_scale=1.0 - lambda_init)
    return pl.pallas_call(
        kern,
        grid=(B, H, nq),
        in_specs=[pl.BlockSpec(memory_space=pltpu.SMEM),
                  pl.BlockSpec((t, 2 * d), lambda b, h, i: (b * nq + i, h)),
                  pl.BlockSpec((S, 2 * d), lambda b, h, i: (b, H + h)),
                  pl.BlockSpec((S, 2 * d), lambda b, h, i: (b, 2 * H + h)),
                  pl.BlockSpec((None, t, t), lambda b, h, i: (h, 0, 0)),
                  pl.BlockSpec((None, t, t), lambda b, h, i: (h, 0, 0)),
                  pl.BlockSpec((1, 2 * d), lambda b, h, i: (0, 0))],
        out_specs=pl.BlockSpec((t, 2 * d), lambda b, h, i: (b * nq + i, h)),
        out_shape=jax.ShapeDtypeStruct((T, H * 2 * d), BF16),
        scratch_shapes=[pltpu.VMEM((2, t, LANES), F32), pltpu.VMEM((2, t, LANES), F32),
                        pltpu.VMEM((2, t, 2 * d), F32), pltpu.VMEM((2, t, t), F32),
                        pltpu.VMEM((2, 2, t, t), BF16), pltpu.VMEM((2, 2, t, LANES), F32)],
        compiler_params=_cparams("arbitrary", "arbitrary", "arbitrary"),
        name="attn",
    )(lam.reshape(1), qkv, qkv, qkv, bd, bp, norm_g.reshape(1, 2 * d))


def _causal_conv_silu(u_ref, w_ref, ext_ref, scale):
    L = u_ref.shape[0]
    width = w_ref.shape[0]
    u = u_ref[...]
    ext_ref[8:, :] = u
    acc = u * w_ref[width - 1:width, :]
    for t in range(1, width):
        acc = acc + ext_ref[8 - t:8 - t + L, :] * w_ref[width - 1 - t:width - t, :]
    ext_ref[0:8, :] = u[L - 8:, :]
    return (acc * jax.nn.sigmoid(acc) * scale).astype(BF16)


def _mlstm_kernel(q_ref, k_ref, wq_ref, wk_ref, v_ref, og_ref, gcol_ref, grow_ref, g_ref, o_ref,
                  c_sc, n_sc, m_sc, extq_sc, extk_sc, *, nheads, group, dh):
    L = q_ref.shape[0]

    @pl.when(pl.program_id(2) == 0)
    def _():
        c_sc[...] = jnp.zeros(c_sc.shape, F32)
        n_sc[...] = jnp.zeros(n_sc.shape, F32)
        m_sc[...] = jnp.zeros(m_sc.shape, F32)
        extq_sc[0:8, :] = jnp.zeros((8, extq_sc.shape[1]), F32)
        extk_sc[0:8, :] = jnp.zeros((8, extk_sc.shape[1]), F32)

    q_all = _causal_conv_silu(q_ref, wq_ref, extq_sc, 1.0)
    k_all = _causal_conv_silu(k_ref, wk_ref, extk_sc, dh ** -0.5)

    gcol = gcol_ref[...]
    lane = lax.broadcasted_iota(jnp.int32, gcol.shape, 1)
    grow = grow_ref[...]
    sub = lax.broadcasted_iota(jnp.int32, grow.shape, 0)
    jj = lax.broadcasted_iota(jnp.int32, (L, L), 0)
    ss = lax.broadcasted_iota(jnp.int32, (L, L), 1)
    tril = ss <= jj

    for hx in range(group):
        h = pl.program_id(1) * group + hx
        cols = slice(hx * dh, (hx + 1) * dh)
        i_col = jnp.sum(jnp.where(lane == h, gcol, 0.0), axis=1, keepdims=True)
        f_col = jax.nn.log_sigmoid(jnp.sum(jnp.where(lane == h + nheads, gcol, 0.0), axis=1, keepdims=True))
        i_row = jnp.sum(jnp.where(sub == h, grow, 0.0), axis=0, keepdims=True)
        f_row = jax.nn.log_sigmoid(jnp.sum(jnp.where(sub == h + nheads, grow, 0.0), axis=0, keepdims=True))

        b_col = jnp.sum(jnp.where(tril, f_row, 0.0), axis=1, keepdims=True)
        b_row = jnp.sum(jnp.where(jj <= ss, f_col, 0.0), axis=0, keepdims=True)
        u_row = i_row - b_row
        u_col = i_col - b_col

        m_prev = m_sc[hx, 0:1, 0:1]
        mm_col = jnp.maximum(m_prev, jnp.max(jnp.where(tril, u_row, NEG_BIG), axis=1, keepdims=True))
        w = jnp.exp(jnp.where(tril, u_row - mm_col, NEG_BIG))
        a_inter = jnp.exp(m_prev - mm_col)

        q = q_all[:, cols]
        k = k_all[:, cols]
        v = v_ref[:, cols]
        c_old = c_sc[hx]
        n_old = n_sc[hx]
        s_qk = _nt_dot(q, k) * w
        num = (a_inter * _nt_dot(q, c_old.astype(BF16))
               + jnp.dot(s_qk.astype(BF16), v, preferred_element_type=F32))
        den = (a_inter * jnp.sum(q.astype(F32) * n_old, axis=1, keepdims=True)
               + jnp.sum(s_qk, axis=1, keepdims=True))
        hh = num / jnp.maximum(jnp.abs(den), jnp.exp(-(b_col + mm_col)))

        mm_last = mm_col[L - 1:L, :]
        w_s = jnp.exp(u_col - mm_last)
        a_state = jnp.exp(m_prev - mm_last)
        c_sc[hx] = a_state * c_old + _tn_dot((v.astype(F32) * w_s).astype(BF16), k)
        n_sc[hx] = a_state * n_old + jnp.sum(k.astype(F32) * w_s, axis=0, keepdims=True)
        m_sc[hx] = jnp.broadcast_to(b_col[L - 1:L, :] + mm_last, m_sc.shape[1:])

        y = hh * lax.rsqrt(jnp.mean(hh * hh, axis=-1, keepdims=True) + EPS) * g_ref[hx]
        o_ref[:, cols] = (y * jax.nn.sigmoid(og_ref[:, cols].astype(F32))).astype(o_ref.dtype)


def _mlstm(mqk, conv_w, mvo, gates, norm_g, B, S, H, dh):
    T = mqk.shape[0]
    width = conv_w.shape[0]
    L = ML_CHUNK
    nc = S // L
    group = _tile(H, 8)
    ng = H // group
    gw = group * dh
    grow = jnp.transpose(gates.reshape(B, S, 2 * H), (0, 2, 1))
    row = lambda b, h, c: (b * nc + c, h)
    row_hi = lambda b, h, c: (b * nc + c, ng + h)
    return pl.pallas_call(
        functools.partial(_mlstm_kernel, nheads=H, group=group, dh=dh),
        grid=(B, ng, nc),
        in_specs=[pl.BlockSpec((L, gw), row),
                  pl.BlockSpec((L, gw), row_hi),
                  pl.BlockSpec((width, gw), lambda b, h, c: (0, h)),
                  pl.BlockSpec((width, gw), lambda b, h, c: (0, ng + h)),
                  pl.BlockSpec((L, gw), row),
                  pl.BlockSpec((L, gw), row_hi),
                  pl.BlockSpec((L, 2 * H), lambda b, h, c: (b * nc + c, 0)),
                  pl.BlockSpec((None, 2 * H, L), lambda b, h, c: (b, 0, c)),
                  pl.BlockSpec((group, 1, dh), lambda b, h, c: (h, 0, 0))],
        out_specs=pl.BlockSpec((L, gw), row),
        out_shape=jax.ShapeDtypeStruct((T, H * dh), BF16),
        scratch_shapes=[pltpu.VMEM((group, dh, dh), F32), pltpu.VMEM((group, 1, dh), F32),
                        pltpu.VMEM((group, 8, LANES), F32), pltpu.VMEM((L + 8, gw), F32),
                        pltpu.VMEM((L + 8, gw), F32)],
        compiler_params=_cparams("arbitrary", "arbitrary", "arbitrary"),
        name="mlstm",
    )(mqk, mqk, conv_w, conv_w, mvo, mvo, gates, grow, norm_g.reshape(H, 1, dh))


def _merge_kernel(a_ref, m_ref, wa_ref, wm_ref, ga_ref, gm_ref, o_ref):
    pa = jnp.dot(a_ref[...], wa_ref[...].astype(BF16), preferred_element_type=F32)
    pm = jnp.dot(m_ref[...], wm_ref[...].astype(BF16), preferred_element_type=F32)
    out = (jax.nn.sigmoid(ga_ref[...].astype(F32)) * pa + jax.nn.sigmoid(gm_ref[...].astype(F32)) * pm)
    o_ref[...] = out.astype(o_ref.dtype)


def _merge(attn, hm, w_a, w_m, gates_am):
    T, Ka = attn.shape
    Km = hm.shape[1]
    D = w_a.shape[1]
    tm = _tile(T, 1024)
    tn = _tile(D, 256)
    nj = D // tn
    return pl.pallas_call(
        _merge_kernel,
        grid=(T // tm, nj),
        in_specs=[pl.BlockSpec((tm, Ka), lambda i, j: (i, 0)),
                  pl.BlockSpec((tm, Km), lambda i, j: (i, 0)),
                  pl.BlockSpec((Ka, tn), lambda i, j: (0, j)),
                  pl.BlockSpec((Km, tn), lambda i, j: (0, j)),
                  pl.BlockSpec((tm, tn), lambda i, j: (i, j)),
                  pl.BlockSpec((tm, tn), lambda i, j: (i, nj + j))],
        out_specs=pl.BlockSpec((tm, tn), lambda i, j: (i, j)),
        out_shape=jax.ShapeDtypeStruct((T, D), BF16),
        compiler_params=_cparams("arbitrary", "arbitrary"),
        name="merge",
    )(attn, hm, w_a, w_m, gates_am, gates_am)


def _out_kernel(a_ref, w_ref, x_ref, mod_ref, o_ref):
    p = jnp.dot(a_ref[...], w_ref[...].astype(BF16), preferred_element_type=F32)
    o_ref[...] = x_ref[...] + mod_ref[2:3, :] * p


def _outproj(merged, w_out, x2, mod, S):
    T, K = merged.shape
    D = w_out.shape[1]
    tm = _tile(S, 1024)
    tn = _tile(D, 512)
    per_b = S // tm
    return pl.pallas_call(
        _out_kernel,
        grid=(T // tm, D // tn),
        in_specs=[pl.BlockSpec((tm, K), lambda i, j: (i, 0)),
                  pl.BlockSpec((K, tn), lambda i, j: (0, j)),
                  pl.BlockSpec((tm, tn), lambda i, j: (i, j)),
                  pl.BlockSpec((None, 6, tn), lambda i, j: (i // per_b, 0, j))],
        out_specs=pl.BlockSpec((tm, tn), lambda i, j: (i, j)),
        out_shape=jax.ShapeDtypeStruct((T, D), F32),
        compiler_params=_cparams("arbitrary", "arbitrary"),
        name="outproj",
    )(merged, w_out, x2, mod)


def _router_kernel(x_ref, mod_ref, g_ref, whi_ref, wlo_ref, b_ref, h_ref, r_ref, cnt_ref, run_sc,
                   *, n_groups, per_group):
    @pl.when(jnp.logical_and(pl.program_id(0) == 0, pl.program_id(1) == 0))
    def _():
        run_sc[...] = jnp.zeros(run_sc.shape, F32)

    x = x_ref[...]
    y = x * lax.rsqrt(jnp.mean(x * x, axis=-1, keepdims=True) + EPS) * g_ref[...]
    h2 = y * (1.0 + mod_ref[4:5, :]) + mod_ref[3:4, :]
    h_ref[...] = _pack_halves(h2)
    hi = h2.astype(BF16)
    lo = (h2 - hi.astype(F32)).astype(BF16)
    logits = (jnp.dot(hi, whi_ref[...], preferred_element_type=F32)
              + jnp.dot(hi, wlo_ref[...], preferred_element_type=F32)
              + jnp.dot(lo, whi_ref[...], preferred_element_type=F32)) + b_ref[...]
    lane = lax.broadcasted_iota(jnp.int32, logits.shape, 1)
    big = jnp.int32(1 << 20)

    def top(vals):
        mx = jnp.max(vals, axis=1, keepdims=True)
        idx = jnp.min(jnp.where(vals == mx, lane, big), axis=1, keepdims=True)
        return mx, idx

    gl = jnp.where(lane < n_groups, logits, -jnp.inf)
    gmax, gidx = top(gl)
    g_p = 1.0 / jnp.sum(jnp.exp(gl - gmax), axis=1, keepdims=True)
    lo_lane = n_groups + per_group * gidx
    el = jnp.where((lane >= lo_lane) & (lane < lo_lane + per_group), logits, -jnp.inf)
    e1, i1 = top(el)
    e2, i2 = top(jnp.where(lane == i1, -jnp.inf, el))
    r = jnp.exp(e2 - e1)
    gate1 = g_p / (1.0 + r)
    gate2 = g_p * r / (1.0 + r)
    ex1 = i1 - n_groups
    ex2 = i2 - n_groups
    oh1 = (lane == ex1).astype(F32)
    oh2 = (lane == ex2).astype(F32)
    both = oh1 + oh2
    tm = x.shape[0]
    earlier = (lax.broadcasted_iota(jnp.int32, (tm, tm), 1)
               < lax.broadcasted_iota(jnp.int32, (tm, tm), 0)).astype(BF16)
    prefix = jnp.dot(earlier, both.astype(BF16), preferred_element_type=F32) + run_sc[0:1, :]
    rank1 = jnp.sum(oh1 * prefix, axis=1, keepdims=True)
    rank2 = jnp.sum(oh2 * prefix, axis=1, keepdims=True)
    run_sc[...] = run_sc[...] + jnp.sum(both, axis=0, keepdims=True)
    cnt_ref[...] = run_sc[...]

    cols = (ex1.astype(F32), ex2.astype(F32), gate1, gate2, rank1, rank2)
    out = jnp.zeros(logits.shape, F32)
    for n, col in enumerate(cols):
        out = jnp.where(lane == n, col, out)
    r_ref[...] = out


def _router(x1, mod, g, w_group, b_group, w_expert, b_expert, B, S):
    T, D = x1.shape
    NG = w_group.shape[1]
    NE = w_expert.shape[1]
    assert NG + NE <= LANES
    pad = LANES - NG - NE
    w = jnp.concatenate([w_group, w_expert, jnp.zeros((D, pad), F32)], axis=1)
    w_hi = w.astype(BF16)
    w_lo = (w - w_hi.astype(F32)).astype(BF16)
    bias = jnp.concatenate([b_group, b_expert, jnp.full((pad,), -jnp.inf, F32)]).reshape(1, LANES)
    tm = _tile(S, 256)
    nb = S // tm
    kern = functools.partial(_router_kernel, n_groups=NG, per_group=NE // NG)
    return pl.pallas_call(
        kern,
        grid=(B, nb),
        in_specs=[pl.BlockSpec((tm, D), lambda b, i: (b * nb + i, 0)),
                  pl.BlockSpec((None, 6, D), lambda b, i: (b, 0, 0)),
                  pl.BlockSpec((1, D), lambda b, i: (0, 0)),
                  pl.BlockSpec((D, LANES), lambda b, i: (0, 0)),
                  pl.BlockSpec((D, LANES), lambda b, i: (0, 0)),
                  pl.BlockSpec((1, LANES), lambda b, i: (0, 0))],
        out_specs=[pl.BlockSpec((tm, D // 2), lambda b, i: (b * nb + i, 0)),
                   pl.BlockSpec((tm, LANES), lambda b, i: (b * nb + i, 0)),
                   pl.BlockSpec((8, LANES), lambda b, i: (0, 0))],
        out_shape=[jax.ShapeDtypeStruct((T, D // 2), jnp.uint32), jax.ShapeDtypeStruct((T, LANES), F32),
                   jax.ShapeDtypeStruct((8, LANES), F32)],
        scratch_shapes=[pltpu.VMEM((8, LANES), F32)],
        compiler_params=_cparams("arbitrary", "arbitrary"),
        name="router",
    )(x1, mod, g.reshape(1, D), w_hi, w_lo, bias)


def _gather_rows(idx_ref, base, src_hbm, dst_ref, sem, n):
    def body(r, carry):
        tok = idx_ref[base + r]
        pltpu.make_async_copy(src_hbm.at[pl.ds(tok, 1), :], dst_ref.at[pl.ds(r, 1), :], sem).start()
        return carry
    lax.fori_loop(0, n, body, 0, unroll=DMA_UNROLL)


def _wait_rows(src_hbm, dst_ref, sem, n):
    def body(r, carry):
        pltpu.make_async_copy(src_hbm.at[pl.ds(0, 1), :], dst_ref.at[pl.ds(r, 1), :], sem).wait()
        return carry
    lax.fori_loop(0, n, body, 0, unroll=DMA_UNROLL)


def _scatter_kernel(dest_ref, pend_ref, h_ref, xs_hbm, zbuf, sem, zsem, *, tm, tb, n_experts, n_blocks):
    i = pl.program_id(0)

    def zero_block(row0):
        return pltpu.make_async_copy(zbuf, xs_hbm.at[pl.ds(pl.multiple_of(row0, tb), tb), :], zsem)

    def zero_copy(e):
        return zero_block(jnp.maximum(pend_ref[e] - tb, 0))

    @pl.when(i == 0)
    def _():
        zbuf[...] = jnp.zeros(zbuf.shape, zbuf.dtype)

        def start(e, carry):
            @pl.when(pend_ref[e] > 0)
            def _():
                zero_copy(e).start()
            return carry

        def wait(e, carry):
            @pl.when(pend_ref[e] > 0)
            def _():
                zero_copy(e).wait()
            return carry

        def start_tail(b, carry):
            zero_block(b * tb).start()
            return carry

        def wait_tail(b, carry):
            zero_block(b * tb).wait()
            return carry

        first_unused = pend_ref[n_experts - 1] // tb
        lax.fori_loop(0, n_experts, start, 0)
        lax.fori_loop(first_unused, n_blocks, start_tail, 0)
        lax.fori_loop(0, n_experts, wait, 0)
        lax.fori_loop(first_unused, n_blocks, wait_tail, 0)

    def row_copy(r, choice):
        slot = dest_ref[2 * (i * tm + r) + choice]
        return pltpu.make_async_copy(h_ref.at[pl.ds(r, 1), :], xs_hbm.at[pl.ds(slot, 1), :], sem)

    def start_row(r, carry):
        row_copy(r, 0).start()
        row_copy(r, 1).start()
        return carry

    def wait_row(r, carry):
        row_copy(r, 0).wait()
        row_copy(r, 1).wait()
        return carry

    lax.fori_loop(0, tm, start_row, 0, unroll=DMA_UNROLL)
    lax.fori_loop(0, tm, wait_row, 0, unroll=DMA_UNROLL)


def _scatter(h2, dest, pend, n_slots, tb):
    T, D = h2.shape
    tm = _tile(T, 256)
    grid_spec = pltpu.PrefetchScalarGridSpec(
        num_scalar_prefetch=2,
        grid=(T // tm,),
        in_specs=[pl.BlockSpec((tm, D), lambda i, de, pe: (i, 0))],
        out_specs=pl.BlockSpec(memory_space=pl.ANY),
        scratch_shapes=[pltpu.VMEM((tb, D), h2.dtype), pltpu.SemaphoreType.DMA(()),
                        pltpu.SemaphoreType.DMA(())],
    )
    return pl.pallas_call(
        functools.partial(_scatter_kernel, tm=tm, tb=tb, n_experts=pend.shape[0],
                          n_blocks=n_slots // tb),
        grid_spec=grid_spec,
        out_shape=jax.ShapeDtypeStruct((n_slots, D), h2.dtype),
        compiler_params=_cparams("arbitrary"),
        name="scatter",
    )(dest, pend, h2)


def _moe_kernel(be_ref, first_ref, nused_ref, x_ref, wg_ref, wu_ref, wd_ref, y_ref,
                cg_sc, cu_sc, cd_sc, acc_sc, *, nk):
    i = pl.program_id(0)
    k = pl.program_id(1)
    nused = nused_ref[0]
    live = i < nused
    first = first_ref[jnp.minimum(i, nused - 1)] == 1
    half = x_ref.shape[1]
    dk = wd_ref.shape[0]

    def x_halves():
        x_lo, x_hi = _unpack_halves(x_ref[...])
        return x_lo.astype(BF16), x_hi.astype(BF16)

    def expert(cols):
        x_lo, x_hi = x_halves()
        g = (jnp.dot(x_lo, cg_sc[:half, cols], preferred_element_type=F32)
             + jnp.dot(x_hi, cg_sc[half:, cols], preferred_element_type=F32))
        u = (jnp.dot(x_lo, cu_sc[:half, cols], preferred_element_type=F32)
             + jnp.dot(x_hi, cu_sc[half:, cols], preferred_element_type=F32))
        act = (g * jax.nn.sigmoid(g) * u).astype(BF16)
        return jnp.dot(act, cd_sc[cols, :], preferred_element_type=F32)

    for s in range(nk):
        @pl.when(jnp.logical_and(jnp.logical_and(live, first), k == s))
        def _():
            cols = slice(s * dk, (s + 1) * dk)
            cg_sc[:, cols] = wg_ref[...].astype(BF16)
            cu_sc[:, cols] = wu_ref[...].astype(BF16)
            cd_sc[cols, :] = wd_ref[...].astype(BF16)
            part = expert(cols)
            if s == 0:
                acc_sc[...] = part
            elif s < nk - 1:
                acc_sc[...] += part
            else:
                y_ref[...] = _pack_halves(acc_sc[...] + part)

    @pl.when(jnp.logical_and(live, jnp.logical_and(jnp.logical_not(first), k == nk - 1)))
    def _():
        y_ref[...] = _pack_halves(expert(slice(None)))

    @pl.when(jnp.logical_and(jnp.logical_not(live), k == 0))
    def _():
        y_ref[...] = jnp.zeros(y_ref.shape, y_ref.dtype)


def _moe(xs, block_e, first, nused, wgu, wd, tb):
    n_slots, hw = xs.shape
    E, D, de2 = wgu.shape
    de = de2 // 2
    nk = MOE_WEIGHT_SLICES
    dk = de // nk
    n_blocks = n_slots // tb

    def blk(i, nu):
        return jnp.minimum(i, nu[0] - 1)

    def piece(i, k, fi, nu):
        streaming = jnp.logical_and(i < nu[0], fi[blk(i, nu)] == 1)
        return jnp.where(streaming, k, nk - 1)

    grid_spec = pltpu.PrefetchScalarGridSpec(
        num_scalar_prefetch=3,
        grid=(n_blocks, nk),
        in_specs=[pl.BlockSpec((tb, hw), lambda i, k, be, fi, nu: (blk(i, nu), 0)),
                  pl.BlockSpec((None, D, dk), lambda i, k, be, fi, nu: (be[blk(i, nu)], 0, piece(i, k, fi, nu))),
                  pl.BlockSpec((None, D, dk),
                               lambda i, k, be, fi, nu: (be[blk(i, nu)], 0, nk + piece(i, k, fi, nu))),
                  pl.BlockSpec((None, dk, D), lambda i, k, be, fi, nu: (be[blk(i, nu)], piece(i, k, fi, nu), 0))],
        out_specs=pl.BlockSpec((tb, hw), lambda i, k, be, fi, nu: (i, 0)),
        scratch_shapes=[pltpu.VMEM((D, de), BF16), pltpu.VMEM((D, de), BF16), pltpu.VMEM((de, D), BF16),
                        pltpu.VMEM((tb, D), F32)],
    )
    return pl.pallas_call(
        functools.partial(_moe_kernel, nk=nk),
        grid_spec=grid_spec,
        out_shape=jax.ShapeDtypeStruct((n_slots, hw), jnp.uint32),
        compiler_params=_cparams("arbitrary", "arbitrary"),
        name="moe",
    )(block_e, first, nused, xs, wgu, wgu, wd)


def _final_kernel(pos_ref, y_hbm, x_ref, r_ref, mod_ref, g_ref, o_ref, ybuf, sem, *, tm, nsteps):
    i = pl.program_id(0)
    slot = lax.rem(i, 2)

    @pl.when(i == 0)
    def _():
        _gather_rows(pos_ref, 0, y_hbm, ybuf.at[0], sem.at[0], 2 * tm)

    @pl.when(i + 1 < nsteps)
    def _():
        _gather_rows(pos_ref, (i + 1) * 2 * tm, y_hbm, ybuf.at[1 - slot], sem.at[1 - slot], 2 * tm)

    _wait_rows(y_hbm, ybuf.at[slot], sem.at[slot], 2 * tm)
    y_lo, y_hi = _unpack_halves(ybuf[slot])
    r = r_ref[...]
    lane = lax.broadcasted_iota(jnp.int32, r.shape, 1)
    gate1 = jnp.sum(jnp.where(lane == 2, r, 0.0), axis=1, keepdims=True)
    gate2 = jnp.sum(jnp.where(lane == 3, r, 0.0), axis=1, keepdims=True)
    half = y_lo.shape[1]
    x_lo = x_ref[:, :half] + mod_ref[5:6, :half] * (gate1 * y_lo[:tm] + gate2 * y_lo[tm:])
    x_hi = x_ref[:, half:] + mod_ref[5:6, half:] * (gate1 * y_hi[:tm] + gate2 * y_hi[tm:])
    ms = (jnp.sum(x_lo * x_lo, axis=-1, keepdims=True)
          + jnp.sum(x_hi * x_hi, axis=-1, keepdims=True)) / (2 * half)
    inv = lax.rsqrt(ms + EPS)
    o_ref[:, :half] = x_lo * inv * g_ref[:, :half]
    o_ref[:, half:] = x_hi * inv * g_ref[:, half:]


def _final(y_slots, dest, route, x1, mod, g, S):
    T, D = x1.shape
    tm = _tile(S, 128)
    per_b = S // tm
    nsteps = T // tm
    pos_tiled = jnp.transpose(dest.reshape(nsteps, tm, 2), (0, 2, 1)).reshape(-1)
    grid_spec = pltpu.PrefetchScalarGridSpec(
        num_scalar_prefetch=1,
        grid=(nsteps,),
        in_specs=[pl.BlockSpec(memory_space=pl.ANY),
                  pl.BlockSpec((tm, D), lambda i, p: (i, 0)),
                  pl.BlockSpec((tm, LANES), lambda i, p: (i, 0)),
                  pl.BlockSpec((None, 6, D), lambda i, p: (i // per_b, 0, 0)),
                  pl.BlockSpec((1, D), lambda i, p: (0, 0))],
        out_specs=pl.BlockSpec((tm, D), lambda i, p: (i, 0)),
        scratch_shapes=[pltpu.VMEM((2, 2 * tm, D // 2), y_slots.dtype), pltpu.SemaphoreType.DMA((2,))],
    )
    return pl.pallas_call(
        functools.partial(_final_kernel, tm=tm, nsteps=nsteps),
        grid_spec=grid_spec,
        out_shape=jax.ShapeDtypeStruct((T, D), F32),
        compiler_params=_cparams("arbitrary"),
        name="final",
    )(pos_tiled, y_slots, x1, route, mod, g.reshape(1, D))


def _dispatch(route, counts, n_experts, tb):
    T = route.shape[0]
    ids = route[:, 0:2].astype(jnp.int32)
    rank = route[:, 4:6].astype(jnp.int32)
    counts = counts[0, :n_experts].astype(jnp.int32)
    padded = (counts + tb - 1) // tb * tb
    pend = jnp.cumsum(padded).astype(jnp.int32)
    pstart = pend - padded
    onehot = ids[:, :, None] == jnp.arange(n_experts, dtype=jnp.int32)
    dest = (rank + jnp.sum(jnp.where(onehot, pstart, 0), axis=-1)).reshape(-1)
    n_blocks = -(-2 * T // tb) + n_experts
    first_row = jnp.arange(n_blocks, dtype=jnp.int32) * tb
    block_e = jnp.minimum(jnp.sum(pend[None, :] <= first_row[:, None], axis=1), n_experts - 1)
    nused = (pend[-1] // tb).reshape(1)
    block_e = block_e.astype(jnp.int32)
    first = jnp.concatenate([jnp.ones((1,), jnp.int32), (block_e[1:] != block_e[:-1]).astype(jnp.int32)])
    return dest, pend, block_e, first, nused, n_blocks * tb


def _layer(x2, c, lidx, B, S, w_ada, b_ada, norm1_g, w_in, conv_qk, b_if, da_lambda, da_norm_g, rel_bias,
           ml_norm_g, w_o_attn, w_o_mlstm, w_out, norm2_g, w_group, b_group, w_expert, b_expert,
           w_gate_up, w_down):
    D = x2.shape[1]
    H = rel_bias.shape[1]
    d = da_lambda.shape[1]
    da_w = H * 2 * d
    MH, dh = ml_norm_g.shape
    ml_w = MH * dh
    n_experts = w_expert.shape[1]

    mod = _ada(c, w_ada, b_ada)

    off_mlq = 3 * da_w
    off_mlv = off_mlq + 2 * ml_w
    off_if = off_mlv + 2 * ml_w
    off_g = off_if + 2 * MH
    w_in_t = w_in.T
    h, gates_if = _norm1(x2, mod, norm1_g, w_in_t[off_if:off_g], b_if, B, S)
    q_scale = jnp.concatenate([jnp.full((da_w,), d ** -0.5 * math.log2(math.e), F32),
                               jnp.ones((2 * da_w,), F32)])
    da_qkv = _proj(h, w_in_t, 0, 3 * da_w, BF16, "proj_da", col_scale=q_scale)
    ml_qk = _proj(h, w_in_t, off_mlq, 2 * ml_w, F32, "proj_mlqk")
    ml_vo = _proj(h, w_in_t, off_mlv, 2 * ml_w, BF16, "proj_mlvo")
    gates_am = _proj(h, w_in_t, off_g, 2 * D, BF16, "proj_gates")

    lambda_init = 0.8 - 0.6 * math.exp(-0.3 * lidx)
    lamp = da_lambda.astype(F32)
    lam = jnp.exp(jnp.sum(lamp[0] * lamp[1])) - jnp.exp(jnp.sum(lamp[2] * lamp[3])) + lambda_init
    attn = _attn(da_qkv, lam, rel_bias, da_norm_g, B, S, H, d, lambda_init)

    hm = _mlstm(ml_qk, conv_qk, ml_vo, gates_if, ml_norm_g, B, S, MH, dh)

    merged = _merge(attn, hm, w_o_attn, w_o_mlstm, gates_am)
    x1 = _outproj(merged, w_out, x2, mod, S)

    h2, route, counts = _router(x1, mod, norm2_g, w_group, b_group, w_expert, b_expert, B, S)
    tb = 256
    dest, pend, block_e, first, nused, n_slots = _dispatch(route, counts, n_experts, tb)
    xs = _scatter(h2, dest, pend, n_slots, tb)
    y_slots = _moe(xs, block_e, first, nused, w_gate_up, w_down, tb)
    return y_slots, dest, route, x1, mod


def kernel(x, c, w_ada, b_ada, norm1_g, w_in, conv_qk, b_if, da_lambda, da_norm_g, rel_bias, ml_norm_g,
           w_o_attn, w_o_mlstm, w_out, norm2_g, w_group, b_group, w_expert, b_expert, w_gate_up, w_down,
           normf_g):
    B, S, D = x.shape
    assert w_ada.shape[0] == 1, "the final rmsnorm is fused into the layer's last kernel: one layer only"
    l = 0
    y_slots, dest, route, x1, mod = _layer(
        x.reshape(B * S, D), c, l, B, S, w_ada[l], b_ada[l], norm1_g[l], w_in[l], conv_qk[l], b_if[l],
        da_lambda[l], da_norm_g[l], rel_bias, ml_norm_g[l], w_o_attn[l], w_o_mlstm[l], w_out[l],
        norm2_g[l], w_group[l], b_group[l], w_expert[l], b_expert[l], w_gate_up[l], w_down[l])
    return _final(y_slots, dest, route, x1, mod, normf_g, S).reshape(B, S, D)
```

```python
import functools
import math

import jax
import jax.numpy as jnp
from jax import lax
from jax.experimental import pallas as pl
from jax.experimental.pallas import tpu as pltpu

F32 = jnp.float32
BF16 = jnp.bfloat16
EPS = 1e-6
REL_MAX_DIST = 128
ML_CHUNK = 128
NEG_BIG = -1e30
V7X_VMEM_LIMIT = 56 * 1024 * 1024
LANES = 128
SUBLANES = 8
DMA_UNROLL = 8


def _cparams(*sem):
    return pltpu.CompilerParams(dimension_semantics=sem, vmem_limit_bytes=V7X_VMEM_LIMIT)


def _tile(n, pref):
    t = min(n, pref)
    while n % t:
        t //= 2
    return t


def _nt_dot(a, b):
    return lax.dot_general(a, b, (((1,), (1,)), ((), ())), preferred_element_type=F32)


def _tn_dot(a, b):
    return lax.dot_general(a, b, (((0,), (0,)), ((), ())), preferred_element_type=F32)


def _pack_halves(x):
    n = x.shape[1] // 2
    bits = pltpu.bitcast(x.astype(BF16).astype(F32), jnp.uint32)
    return bits[:, n:] | (bits[:, :n] >> 16)


def _unpack_halves(w):
    lo = pltpu.bitcast(w << 16, F32)
    hi = pltpu.bitcast(w & jnp.uint32(0xFFFF0000), F32)
    return lo, hi


def _ada_kernel(c_ref, w_ref, b_ref, o_ref):
    c = c_ref[...]
    s = (c * jax.nn.sigmoid(c)).astype(BF16)
    o_ref[...] = jnp.dot(s, w_ref[...].astype(BF16), preferred_element_type=F32) + b_ref[...]


def _ada(c, w_ada, b_ada):
    B, D = c.shape
    N = w_ada.shape[1]
    rows = 8
    c8 = jnp.zeros((rows, D), F32).at[:B].set(c)
    tn = _tile(N, 512)
    mod = pl.pallas_call(
        _ada_kernel,
        grid=(N // tn,),
        in_specs=[pl.BlockSpec((rows, D), lambda j: (0, 0)),
                  pl.BlockSpec((D, tn), lambda j: (0, j)),
                  pl.BlockSpec((1, tn), lambda j: (0, j))],
        out_specs=pl.BlockSpec((rows, tn), lambda j: (0, j)),
        out_shape=jax.ShapeDtypeStruct((rows, N), F32),
        compiler_params=_cparams("arbitrary"),
        name="ada",
    )(c8, w_ada, b_ada.reshape(1, N))
    return mod[:B].reshape(B, 6, D)


def _norm1_kernel(x_ref, mod_ref, g_ref, wif_ref, bif_ref, h_ref, gate_ref):
    x = x_ref[...]
    y = x * lax.rsqrt(jnp.mean(x * x, axis=-1, keepdims=True) + EPS) * g_ref[...]
    h = (y * (1.0 + mod_ref[1:2, :]) + mod_ref[0:1, :]).astype(BF16)
    h_ref[...] = h
    gate_ref[...] = _nt_dot(h, wif_ref[...].astype(BF16)) + bif_ref[...]


def _norm1(x2, mod, g, w_if_t, b_if, B, S):
    T, D = x2.shape
    G = w_if_t.shape[0]
    tm = _tile(S, 256)
    nb = S // tm
    return pl.pallas_call(
        _norm1_kernel,
        grid=(B, nb),
        in_specs=[pl.BlockSpec((tm, D), lambda b, i: (b * nb + i, 0)),
                  pl.BlockSpec((None, 6, D), lambda b, i: (b, 0, 0)),
                  pl.BlockSpec((1, D), lambda b, i: (0, 0)),
                  pl.BlockSpec((G, D), lambda b, i: (0, 0)),
                  pl.BlockSpec((1, G), lambda b, i: (0, 0))],
        out_specs=[pl.BlockSpec((tm, D), lambda b, i: (b * nb + i, 0)),
                   pl.BlockSpec((tm, G), lambda b, i: (b * nb + i, 0))],
        out_shape=[jax.ShapeDtypeStruct((T, D), BF16), jax.ShapeDtypeStruct((T, G), F32)],
        compiler_params=_cparams("arbitrary", "arbitrary"),
        name="norm1",
    )(x2, mod, g.reshape(1, D), w_if_t, b_if.reshape(1, G))


def _proj_kernel(h_ref, w_ref, o_ref):
    o_ref[...] = _nt_dot(h_ref[...], w_ref[...].astype(BF16)).astype(o_ref.dtype)


def _proj_scaled_kernel(h_ref, w_ref, s_ref, o_ref):
    acc = _nt_dot(h_ref[...], w_ref[...].astype(BF16))
    o_ref[...] = (acc * s_ref[...]).astype(o_ref.dtype)


def _proj(h, w_t, col0, ncols, out_dtype, name, col_scale=None):
    M, K = h.shape
    tm = _tile(M, 1024)
    tn = _tile(ncols, 512)
    assert col0 % SUBLANES == 0
    in_specs = [pl.BlockSpec((tm, K), lambda i, j: (i, 0)),
                pl.BlockSpec((pl.Element(tn), pl.Element(K)),
                             lambda i, j: (pl.multiple_of(col0 + j * tn, SUBLANES), 0))]
    args = (h, w_t)
    body = _proj_kernel
    if col_scale is not None:
        in_specs.append(pl.BlockSpec((1, tn), lambda i, j: (0, j)))
        args = (h, w_t, col_scale.reshape(1, ncols))
        body = _proj_scaled_kernel
    return pl.pallas_call(
        body,
        grid=(M // tm, ncols // tn),
        in_specs=in_specs,
        out_specs=pl.BlockSpec((tm, tn), lambda i, j: (i, j)),
        out_shape=jax.ShapeDtypeStruct((M, ncols), out_dtype),
        compiler_params=_cparams("arbitrary", "arbitrary"),
        name=name,
    )(*args)


def _attn_kernel(lam_ref, q_ref, k_ref, v_ref, bd_ref, bp_ref, g_ref, o_ref,
                 m_sc, l_sc, acc_sc, s_sc, p_sc, a_sc, *, d, t, out_scale):
    i = pl.program_id(2)
    q = q_ref[...]
    ngrp = t // LANES
    m_sc[...] = jnp.full(m_sc.shape, NEG_BIG, F32)
    l_sc[...] = jnp.zeros(l_sc.shape, F32)
    acc_sc[...] = jnp.zeros(acc_sc.shape, F32)
    spare = pl.ds(1 - lax.rem(i, 2), 1)
    p_sc[spare] = jnp.zeros((1,) + p_sc.shape[1:], BF16)
    a_sc[spare] = jnp.ones((1,) + a_sc.shape[1:], F32)

    def rows(j):
        return pl.ds(pl.multiple_of(j * t, t), t)

    def scores(j):
        k = k_ref[rows(j), :]
        for m in range(2):
            s_sc[m] = _nt_dot(q[:, m * d:(m + 1) * d], k[:, m * d:(m + 1) * d])

    def accumulate(j, buf):
        v = v_ref[rows(j), :]
        for m in range(2):
            alpha = a_sc[buf, m]
            pv = jnp.dot(p_sc[buf, m], v, preferred_element_type=F32)
            acc_sc[m] = jnp.concatenate([alpha] * (2 * d // LANES), axis=1) * acc_sc[m] + pv

    def softmax(bias_ref, buf):
        for m in range(2):
            s = s_sc[m]
            if bias_ref is not None:
                s = s + bias_ref[...]
            grp = [s[:, g * LANES:(g + 1) * LANES] for g in range(ngrp)]
            m_old = m_sc[m]
            row_max = jnp.max(functools.reduce(jnp.maximum, grp), axis=1, keepdims=True)
            m_new = jnp.maximum(m_old, row_max)
            alpha = jnp.exp2(m_old - m_new)
            p = [jnp.exp2(x - m_new) for x in grp]
            l_sc[m] = alpha * l_sc[m] + functools.reduce(jnp.add, p)
            p_sc[buf, m] = jnp.concatenate([x.astype(BF16) for x in p], axis=1)
            a_sc[buf, m] = alpha
            m_sc[m] = m_new

    def stage(j, bias_ref, buf, last=False):
        accumulate(jnp.maximum(j - 1, 0), 1 - buf)
        softmax(bias_ref, buf)
        if not last:
            scores(j + 1)

    n_far = jnp.maximum(i - 1, 0)
    odd = lax.rem(n_far, 2)

    def far_pair(jj, carry):
        j = odd + 2 * jj
        stage(j, None, 1)
        stage(j + 1, None, 0)
        return carry

    scores(0)

    @pl.when(odd == 1)
    def _():
        stage(0, None, 0)

    lax.fori_loop(0, n_far // 2, far_pair, 0)

    @pl.when(i > 0)
    def _():
        stage(i - 1, bp_ref, 1)

    stage(i, bd_ref, 0, last=True)
    accumulate(i, 0)

    l0 = jnp.sum(l_sc[0], axis=1, keepdims=True)
    l1 = jnp.sum(l_sc[1], axis=1, keepdims=True)
    a = acc_sc[0] / l0 - lam_ref[0] * (acc_sc[1] / l1)
    y = a * lax.rsqrt(jnp.mean(a * a, axis=-1, keepdims=True) + EPS) * g_ref[...]
    o_ref[...] = (y * out_scale).astype(o_ref.dtype)


def _rel_bucket(n, n_buckets):
    max_exact = n_buckets // 2
    nf = jnp.maximum(n, 1).astype(F32)
    large = max_exact + (jnp.log(nf / max_exact) / math.log(REL_MAX_DIST / max_exact)
                         * (n_buckets - max_exact)).astype(jnp.int32)
    large = jnp.minimum(large, n_buckets - 1)
    return jnp.where(n < max_exact, n, large)


def _bias_table(rel, dist, n_buckets):
    bucket = _rel_bucket(dist, n_buckets)
    out = jnp.zeros((rel.shape[1],) + dist.shape, F32)
    for b in range(n_buckets):
        out = jnp.where((bucket == b)[None], rel[b][:, None, None], out)
    return out


def _attn(qkv, lam, rel_bias, norm_g, B, S, H, d, lambda_init):
    T = qkv.shape[0]
    t = _tile(S, 512)
    assert t >= REL_MAX_DIST, "key blocks two or more tiles back must sit in the saturated bucket"
    assert d % LANES == 0
    nq = S // t
    n_buckets = rel_bias.shape[0]
    rel = rel_bias.astype(F32)
    rel = (rel - rel[n_buckets - 1]) * math.log2(math.e)
    dist_d = jnp.arange(t)[:, None] - jnp.arange(t)[None, :]
    bd = jnp.where((dist_d >= 0)[None], _bias_table(rel, jnp.maximum(dist_d, 0), n_buckets), NEG_BIG)
    bp = _bias_table(rel, dist_d + t, n_buckets)
    kern = functools.partial(_attn_kernel, d=d, t=t, out_scale=1.0 - lambda_init)
    return pl.pallas_call(
        kern,
        grid=(B, H, nq),
        in_specs=[pl.BlockSpec(memory_space=pltpu.SMEM),
                  pl.BlockSpec((t, 2 * d), lambda b, h, i: (b * nq + i, h)),
                  pl.BlockSpec((S, 2 * d), lambda b, h, i: (b, H + h)),
                  pl.BlockSpec((S, 2 * d), lambda b, h, i: (b, 2 * H + h)),
                  pl.BlockSpec((None, t, t), lambda b, h, i: (h, 0, 0)),
                  pl.BlockSpec((None, t, t), lambda b, h, i: (h, 0, 0)),
                  pl.BlockSpec((1, 2 * d), lambda b, h, i: (0, 0))],
        out_specs=pl.BlockSpec((t, 2 * d), lambda b, h, i: (b * nq + i, h)),
        out_shape=jax.ShapeDtypeStruct((T, H * 2 * d), BF16),
        scratch_shapes=[pltpu.VMEM((2, t, LANES), F32), pltpu.VMEM((2, t, LANES), F32),
                        pltpu.VMEM((2, t, 2 * d), F32), pltpu.VMEM((2, t, t), F32),
                        pltpu.VMEM((2, 2, t, t), BF16), pltpu.VMEM((2, 2, t, LANES), F32)],
        compiler_params=_cparams("arbitrary", "arbitrary", "arbitrary"),
        name="attn",
    )(lam.reshape(1), qkv, qkv, qkv, bd, bp, norm_g.reshape(1, 2 * d))


def _causal_conv_silu(u_ref, w_ref, ext_ref, scale):
    L = u_ref.shape[0]
    width = w_ref.shape[0]
    u = u_ref[...]
    ext_ref[8:, :] = u
    acc = u * w_ref[width - 1:width, :]
    for t in range(1, width):
        acc = acc + ext_ref[8 - t:8 - t + L, :] * w_ref[width - 1 - t:width - t, :]
    ext_ref[0:8, :] = u[L - 8:, :]
    return (acc * jax.nn.sigmoid(acc) * scale).astype(BF16)


def _mlstm_kernel(q_ref, k_ref, wq_ref, wk_ref, v_ref, og_ref, gcol_ref, grow_ref, g_ref, o_ref,
                  c_sc, n_sc, m_sc, extq_sc, extk_sc, *, nheads, group, dh):
    L = q_ref.shape[0]

    @pl.when(pl.program_id(2) == 0)
    def _():
        c_sc[...] = jnp.zeros(c_sc.shape, F32)
        n_sc[...] = jnp.zeros(n_sc.shape, F32)
        m_sc[...] = jnp.zeros(m_sc.shape, F32)
        extq_sc[0:8, :] = jnp.zeros((8, extq_sc.shape[1]), F32)
        extk_sc[0:8, :] = jnp.zeros((8, extk_sc.shape[1]), F32)

    q_all = _causal_conv_silu(q_ref, wq_ref, extq_sc, 1.0)
    k_all = _causal_conv_silu(k_ref, wk_ref, extk_sc, dh ** -0.5)

    gcol = gcol_ref[...]
    lane = lax.broadcasted_iota(jnp.int32, gcol.shape, 1)
    grow = grow_ref[...]
    sub = lax.broadcasted_iota(jnp.int32, grow.shape, 0)
    jj = lax.broadcasted_iota(jnp.int32, (L, L), 0)
    ss = lax.broadcasted_iota(jnp.int32, (L, L), 1)
    tril = ss <= jj

    for hx in range(group):
        h = pl.program_id(1) * group + hx
        cols = slice(hx * dh, (hx + 1) * dh)
        i_col = jnp.sum(jnp.where(lane == h, gcol, 0.0), axis=1, keepdims=True)
        f_col = jax.nn.log_sigmoid(jnp.sum(jnp.where(lane == h + nheads, gcol, 0.0), axis=1, keepdims=True))
        i_row = jnp.sum(jnp.where(sub == h, grow, 0.0), axis=0, keepdims=True)
        f_row = jax.nn.log_sigmoid(jnp.sum(jnp.where(sub == h + nheads, grow, 0.0), axis=0, keepdims=True))

        b_col = jnp.sum(jnp.where(tril, f_row, 0.0), axis=1, keepdims=True)
        b_row = jnp.sum(jnp.where(jj <= ss, f_col, 0.0), axis=0, keepdims=True)
        u_row = i_row - b_row
        u_col = i_col - b_col

        m_prev = m_sc[hx, 0:1, 0:1]
        mm_col = jnp.maximum(m_prev, jnp.max(jnp.where(tril, u_row, NEG_BIG), axis=1, keepdims=True))
        w = jnp.exp(jnp.where(tril, u_row - mm_col, NEG_BIG))
        a_inter = jnp.exp(m_prev - mm_col)

        q = q_all[:, cols]
        k = k_all[:, cols]
        v = v_ref[:, cols]
        c_old = c_sc[hx]
        n_old = n_sc[hx]
        s_qk = _nt_dot(q, k) * w
        num = (a_inter * _nt_dot(q, c_old.astype(BF16))
               + jnp.dot(s_qk.astype(BF16), v, preferred_element_type=F32))
        den = (a_inter * jnp.sum(q.astype(F32) * n_old, axis=1, keepdims=True)
               + jnp.sum(s_qk, axis=1, keepdims=True))
        hh = num / jnp.maximum(jnp.abs(den), jnp.exp(-(b_col + mm_col)))

        mm_last = mm_col[L - 1:L, :]
        w_s = jnp.exp(u_col - mm_last)
        a_state = jnp.exp(m_prev - mm_last)
        c_sc[hx] = a_state * c_old + _tn_dot((v.astype(F32) * w_s).astype(BF16), k)
        n_sc[hx] = a_state * n_old + jnp.sum(k.astype(F32) * w_s, axis=0, keepdims=True)
        m_sc[hx] = jnp.broadcast_to(b_col[L - 1:L, :] + mm_last, m_sc.shape[1:])

        y = hh * lax.rsqrt(jnp.mean(hh * hh, axis=-1, keepdims=True) + EPS) * g_ref[hx]
        o_ref[:, cols] = (y * jax.nn.sigmoid(og_ref[:, cols].astype(F32))).astype(o_ref.dtype)


def _mlstm(mqk, conv_w, mvo, gates, norm_g, B, S, H, dh):
    T = mqk.shape[0]
    width = conv_w.shape[0]
    L = ML_CHUNK
    nc = S // L
    group = _tile(H, 4)
    ng = H // group
    gw = group * dh
    grow = jnp.transpose(gates.reshape(B, S, 2 * H), (0, 2, 1))
    row = lambda b, h, c: (b * nc + c, h)
    row_hi = lambda b, h, c: (b * nc + c, ng + h)
    return pl.pallas_call(
        functools.partial(_mlstm_kernel, nheads=H, group=group, dh=dh),
        grid=(B, ng, nc),
        in_specs=[pl.BlockSpec((L, gw), row),
                  pl.BlockSpec((L, gw), row_hi),
                  pl.BlockSpec((width, gw), lambda b, h, c: (0, h)),
                  pl.BlockSpec((width, gw), lambda b, h, c: (0, ng + h)),
                  pl.BlockSpec((L, gw), row),
                  pl.BlockSpec((L, gw), row_hi),
                  pl.BlockSpec((L, 2 * H), lambda b, h, c: (b * nc + c, 0)),
                  pl.BlockSpec((None, 2 * H, L), lambda b, h, c: (b, 0, c)),
                  pl.BlockSpec((group, 1, dh), lambda b, h, c: (h, 0, 0))],
        out_specs=pl.BlockSpec((L, gw), row),
        out_shape=jax.ShapeDtypeStruct((T, H * dh), BF16),
        scratch_shapes=[pltpu.VMEM((group, dh, dh), F32), pltpu.VMEM((group, 1, dh), F32),
                        pltpu.VMEM((group, 8, LANES), F32), pltpu.VMEM((L + 8, gw), F32),
                        pltpu.VMEM((L + 8, gw), F32)],
        compiler_params=_cparams("arbitrary", "arbitrary", "arbitrary"),
        name="mlstm",
    )(mqk, mqk, conv_w, conv_w, mvo, mvo, gates, grow, norm_g.reshape(H, 1, dh))


def _merge_kernel(a_ref, m_ref, wa_ref, wm_ref, ga_ref, gm_ref, o_ref):
    pa = jnp.dot(a_ref[...], wa_ref[...].astype(BF16), preferred_element_type=F32)
    pm = jnp.dot(m_ref[...], wm_ref[...].astype(BF16), preferred_element_type=F32)
    out = (jax.nn.sigmoid(ga_ref[...].astype(F32)) * pa + jax.nn.sigmoid(gm_ref[...].astype(F32)) * pm)
    o_ref[...] = out.astype(o_ref.dtype)


def _merge(attn, hm, w_a, w_m, gates_am):
    T, Ka = attn.shape
    Km = hm.shape[1]
    D = w_a.shape[1]
    tm = _tile(T, 1024)
    tn = _tile(D, 256)
    nj = D // tn
    return pl.pallas_call(
        _merge_kernel,
        grid=(T // tm, nj),
        in_specs=[pl.BlockSpec((tm, Ka), lambda i, j: (i, 0)),
                  pl.BlockSpec((tm, Km), lambda i, j: (i, 0)),
                  pl.BlockSpec((Ka, tn), lambda i, j: (0, j)),
                  pl.BlockSpec((Km, tn), lambda i, j: (0, j)),
                  pl.BlockSpec((tm, tn), lambda i, j: (i, j)),
                  pl.BlockSpec((tm, tn), lambda i, j: (i, nj + j))],
        out_specs=pl.BlockSpec((tm, tn), lambda i, j: (i, j)),
        out_shape=jax.ShapeDtypeStruct((T, D), BF16),
        compiler_params=_cparams("arbitrary", "arbitrary"),
        name="merge",
    )(attn, hm, w_a, w_m, gates_am, gates_am)


def _out_kernel(a_ref, w_ref, x_ref, mod_ref, o_ref):
    p = jnp.dot(a_ref[...], w_ref[...].astype(BF16), preferred_element_type=F32)
    o_ref[...] = x_ref[...] + mod_ref[2:3, :] * p


def _outproj(merged, w_out, x2, mod, S):
    T, K = merged.shape
    D = w_out.shape[1]
    tm = _tile(S, 1024)
    tn = _tile(D, 512)
    per_b = S // tm
    return pl.pallas_call(
        _out_kernel,
        grid=(T // tm, D // tn),
        in_specs=[pl.BlockSpec((tm, K), lambda i, j: (i, 0)),
                  pl.BlockSpec((K, tn), lambda i, j: (0, j)),
                  pl.BlockSpec((tm, tn), lambda i, j: (i, j)),
                  pl.BlockSpec((None, 6, tn), lambda i, j: (i // per_b, 0, j))],
        out_specs=pl.BlockSpec((tm, tn), lambda i, j: (i, j)),
        out_shape=jax.ShapeDtypeStruct((T, D), F32),
        compiler_params=_cparams("arbitrary", "arbitrary"),
        name="outproj",
    )(merged, w_out, x2, mod)


def _router_kernel(x_ref, mod_ref, g_ref, whi_ref, wlo_ref, b_ref, h_ref, r_ref, cnt_ref, run_sc,
                   *, n_groups, per_group):
    @pl.when(jnp.logical_and(pl.program_id(0) == 0, pl.program_id(1) == 0))
    def _():
        run_sc[...] = jnp.zeros(run_sc.shape, F32)

    x = x_ref[...]
    y = x * lax.rsqrt(jnp.mean(x * x, axis=-1, keepdims=True) + EPS) * g_ref[...]
    h2 = y * (1.0 + mod_ref[4:5, :]) + mod_ref[3:4, :]
    h_ref[...] = _pack_halves(h2)
    hi = h2.astype(BF16)
    lo = (h2 - hi.astype(F32)).astype(BF16)
    logits = (jnp.dot(hi, whi_ref[...], preferred_element_type=F32)
              + jnp.dot(hi, wlo_ref[...], preferred_element_type=F32)
              + jnp.dot(lo, whi_ref[...], preferred_element_type=F32)) + b_ref[...]
    lane = lax.broadcasted_iota(jnp.int32, logits.shape, 1)
    big = jnp.int32(1 << 20)

    def top(vals):
        mx = jnp.max(vals, axis=1, keepdims=True)
        idx = jnp.min(jnp.where(vals == mx, lane, big), axis=1, keepdims=True)
        return mx, idx

    gl = jnp.where(lane < n_groups, logits, -jnp.inf)
    gmax, gidx = top(gl)
    g_p = 1.0 / jnp.sum(jnp.exp(gl - gmax), axis=1, keepdims=True)
    lo_lane = n_groups + per_group * gidx
    el = jnp.where((lane >= lo_lane) & (lane < lo_lane + per_group), logits, -jnp.inf)
    e1, i1 = top(el)
    e2, i2 = top(jnp.where(lane == i1, -jnp.inf, el))
    r = jnp.exp(e2 - e1)
    gate1 = g_p / (1.0 + r)
    gate2 = g_p * r / (1.0 + r)
    ex1 = i1 - n_groups
    ex2 = i2 - n_groups
    oh1 = (lane == ex1).astype(F32)
    oh2 = (lane == ex2).astype(F32)
    both = oh1 + oh2
    tm = x.shape[0]
    earlier = (lax.broadcasted_iota(jnp.int32, (tm, tm), 1)
               < lax.broadcasted_iota(jnp.int32, (tm, tm), 0)).astype(BF16)
    prefix = jnp.dot(earlier, both.astype(BF16), preferred_element_type=F32) + run_sc[0:1, :]
    rank1 = jnp.sum(oh1 * prefix, axis=1, keepdims=True)
    rank2 = jnp.sum(oh2 * prefix, axis=1, keepdims=True)
    run_sc[...] = run_sc[...] + jnp.sum(both, axis=0, keepdims=True)
    cnt_ref[...] = run_sc[...]

    cols = (ex1.astype(F32), ex2.astype(F32), gate1, gate2, rank1, rank2)
    out = jnp.zeros(logits.shape, F32)
    for n, col in enumerate(cols):
        out = jnp.where(lane == n, col, out)
    r_ref[...] = out


def _router(x1, mod, g, w_group, b_group, w_expert, b_expert, B, S):
    T, D = x1.shape
    NG = w_group.shape[1]
    NE = w_expert.shape[1]
    assert NG + NE <= LANES
    pad = LANES - NG - NE
    w = jnp.concatenate([w_group, w_expert, jnp.zeros((D, pad), F32)], axis=1)
    w_hi = w.astype(BF16)
    w_lo = (w - w_hi.astype(F32)).astype(BF16)
    bias = jnp.concatenate([b_group, b_expert, jnp.full((pad,), -jnp.inf, F32)]).reshape(1, LANES)
    tm = _tile(S, 256)
    nb = S // tm
    kern = functools.partial(_router_kernel, n_groups=NG, per_group=NE // NG)
    return pl.pallas_call(
        kern,
        grid=(B, nb),
        in_specs=[pl.BlockSpec((tm, D), lambda b, i: (b * nb + i, 0)),
                  pl.BlockSpec((None, 6, D), lambda b, i: (b, 0, 0)),
                  pl.BlockSpec((1, D), lambda b, i: (0, 0)),
                  pl.BlockSpec((D, LANES), lambda b, i: (0, 0)),
                  pl.BlockSpec((D, LANES), lambda b, i: (0, 0)),
                  pl.BlockSpec((1, LANES), lambda b, i: (0, 0))],
        out_specs=[pl.BlockSpec((tm, D // 2), lambda b, i: (b * nb + i, 0)),
                   pl.BlockSpec((tm, LANES), lambda b, i: (b * nb + i, 0)),
                   pl.BlockSpec((8, LANES), lambda b, i: (0, 0))],
        out_shape=[jax.ShapeDtypeStruct((T, D // 2), jnp.uint32), jax.ShapeDtypeStruct((T, LANES), F32),
                   jax.ShapeDtypeStruct((8, LANES), F32)],
        scratch_shapes=[pltpu.VMEM((8, LANES), F32)],
        compiler_params=_cparams("arbitrary", "arbitrary"),
        name="router",
    )(x1, mod, g.reshape(1, D), w_hi, w_lo, bias)


def _gather_rows(idx_ref, base, src_hbm, dst_ref, sem, n):
    def body(r, carry):
        tok = idx_ref[base + r]
        pltpu.make_async_copy(src_hbm.at[pl.ds(tok, 1), :], dst_ref.at[pl.ds(r, 1), :], sem).start()
        return carry
    lax.fori_loop(0, n, body, 0, unroll=DMA_UNROLL)


def _wait_rows(src_hbm, dst_ref, sem, n):
    def body(r, carry):
        pltpu.make_async_copy(src_hbm.at[pl.ds(0, 1), :], dst_ref.at[pl.ds(r, 1), :], sem).wait()
        return carry
    lax.fori_loop(0, n, body, 0, unroll=DMA_UNROLL)


def _scatter_kernel(dest_ref, pend_ref, h_ref, xs_hbm, zbuf, sem, zsem, *, tm, tb, n_experts, n_blocks):
    i = pl.program_id(0)

    def zero_block(row0):
        return pltpu.make_async_copy(zbuf, xs_hbm.at[pl.ds(pl.multiple_of(row0, tb), tb), :], zsem)

    def zero_copy(e):
        return zero_block(jnp.maximum(pend_ref[e] - tb, 0))

    @pl.when(i == 0)
    def _():
        zbuf[...] = jnp.zeros(zbuf.shape, zbuf.dtype)

        def start(e, carry):
            @pl.when(pend_ref[e] > 0)
            def _():
                zero_copy(e).start()
            return carry

        def wait(e, carry):
            @pl.when(pend_ref[e] > 0)
            def _():
                zero_copy(e).wait()
            return carry

        def start_tail(b, carry):
            zero_block(b * tb).start()
            return carry

        def wait_tail(b, carry):
            zero_block(b * tb).wait()
            return carry

        first_unused = pend_ref[n_experts - 1] // tb
        lax.fori_loop(0, n_experts, start, 0)
        lax.fori_loop(first_unused, n_blocks, start_tail, 0)
        lax.fori_loop(0, n_experts, wait, 0)
        lax.fori_loop(first_unused, n_blocks, wait_tail, 0)

    def row_copy(r, choice):
        slot = dest_ref[2 * (i * tm + r) + choice]
        return pltpu.make_async_copy(h_ref.at[pl.ds(r, 1), :], xs_hbm.at[pl.ds(slot, 1), :], sem)

    def start_row(r, carry):
        row_copy(r, 0).start()
        row_copy(r, 1).start()
        return carry

    def wait_row(r, carry):
        row_copy(r, 0).wait()
        row_copy(r, 1).wait()
        return carry

    lax.fori_loop(0, tm, start_row, 0, unroll=DMA_UNROLL)
    lax.fori_loop(0, tm, wait_row, 0, unroll=DMA_UNROLL)


def _scatter(h2, dest, pend, n_slots, tb):
    T, D = h2.shape
    tm = _tile(T, 256)
    grid_spec = pltpu.PrefetchScalarGridSpec(
        num_scalar_prefetch=2,
        grid=(T // tm,),
        in_specs=[pl.BlockSpec((tm, D), lambda i, de, pe: (i, 0))],
        out_specs=pl.BlockSpec(memory_space=pl.ANY),
        scratch_shapes=[pltpu.VMEM((tb, D), h2.dtype), pltpu.SemaphoreType.DMA(()),
                        pltpu.SemaphoreType.DMA(())],
    )
    return pl.pallas_call(
        functools.partial(_scatter_kernel, tm=tm, tb=tb, n_experts=pend.shape[0],
                          n_blocks=n_slots // tb),
        grid_spec=grid_spec,
        out_shape=jax.ShapeDtypeStruct((n_slots, D), h2.dtype),
        compiler_params=_cparams("arbitrary"),
        name="scatter",
    )(dest, pend, h2)


def _moe_kernel(be_ref, first_ref, nused_ref, x_ref, wgu_ref, wd_ref, y_ref, cgu_sc, cd_sc, gu_sc):
    i = pl.program_id(0)
    k = pl.program_id(1)
    nused = nused_ref[0]
    live = i < nused
    first = first_ref[jnp.minimum(i, nused - 1)] == 1
    half = x_ref.shape[1]
    dk = wd_ref.shape[0]
    de = cd_sc.shape[0]

    def x_halves():
        x_lo, x_hi = _unpack_halves(x_ref[...])
        return x_lo.astype(BF16), x_hi.astype(BF16)

    def gate_up_lo(x_lo):
        return jnp.dot(x_lo, cgu_sc[:half, :], preferred_element_type=F32)

    def finish(gu_lo, x_hi):
        gu = gu_lo + jnp.dot(x_hi, cgu_sc[half:, :], preferred_element_type=F32)
        g = gu[:, :de]
        u = gu[:, de:]
        act = (g * jax.nn.sigmoid(g) * u).astype(BF16)
        y_ref[...] = _pack_halves(jnp.dot(act, cd_sc[...], preferred_element_type=F32))

    streaming = jnp.logical_and(live, first)

    @pl.when(jnp.logical_and(streaming, k == 0))
    def _():
        cgu_sc[:half, :] = wgu_ref[...].astype(BF16)
        cd_sc[:dk, :] = wd_ref[...].astype(BF16)
        gu_sc[...] = gate_up_lo(x_halves()[0])

    @pl.when(jnp.logical_and(streaming, k == 1))
    def _():
        cgu_sc[half:, :] = wgu_ref[...].astype(BF16)
        cd_sc[dk:, :] = wd_ref[...].astype(BF16)
        finish(gu_sc[...], x_halves()[1])

    @pl.when(jnp.logical_and(live, jnp.logical_and(jnp.logical_not(first), k == 1)))
    def _():
        x_lo, x_hi = x_halves()
        finish(gate_up_lo(x_lo), x_hi)

    @pl.when(jnp.logical_and(jnp.logical_not(live), k == 0))
    def _():
        y_ref[...] = jnp.zeros(y_ref.shape, y_ref.dtype)


def _moe(xs, block_e, first, nused, wgu, wd, tb):
    n_slots, hw = xs.shape
    E, D, de2 = wgu.shape
    de = de2 // 2
    nk = 2
    dk = de // nk
    n_blocks = n_slots // tb

    def blk(i, nu):
        return jnp.minimum(i, nu[0] - 1)

    def piece(i, k, fi, nu):
        streaming = jnp.logical_and(i < nu[0], fi[blk(i, nu)] == 1)
        return jnp.where(streaming, k, nk - 1)

    grid_spec = pltpu.PrefetchScalarGridSpec(
        num_scalar_prefetch=3,
        grid=(n_blocks, nk),
        in_specs=[pl.BlockSpec((tb, hw), lambda i, k, be, fi, nu: (blk(i, nu), 0)),
                  pl.BlockSpec((None, hw, de2), lambda i, k, be, fi, nu: (be[blk(i, nu)], piece(i, k, fi, nu), 0)),
                  pl.BlockSpec((None, dk, D), lambda i, k, be, fi, nu: (be[blk(i, nu)], piece(i, k, fi, nu), 0))],
        out_specs=pl.BlockSpec((tb, hw), lambda i, k, be, fi, nu: (i, 0)),
        scratch_shapes=[pltpu.VMEM((D, de2), BF16), pltpu.VMEM((de, D), BF16), pltpu.VMEM((tb, de2), F32)],
    )
    return pl.pallas_call(
        _moe_kernel,
        grid_spec=grid_spec,
        out_shape=jax.ShapeDtypeStruct((n_slots, hw), jnp.uint32),
        compiler_params=_cparams("arbitrary", "arbitrary"),
        name="moe",
    )(block_e, first, nused, xs, wgu, wd)


def _final_kernel(pos_ref, y_hbm, x_ref, r_ref, mod_ref, g_ref, o_ref, ybuf, sem, *, tm, nsteps):
    i = pl.program_id(0)
    slot = lax.rem(i, 2)

    @pl.when(i == 0)
    def _():
        _gather_rows(pos_ref, 0, y_hbm, ybuf.at[0], sem.at[0], 2 * tm)

    @pl.when(i + 1 < nsteps)
    def _():
        _gather_rows(pos_ref, (i + 1) * 2 * tm, y_hbm, ybuf.at[1 - slot], sem.at[1 - slot], 2 * tm)

    _wait_rows(y_hbm, ybuf.at[slot], sem.at[slot], 2 * tm)
    y_lo, y_hi = _unpack_halves(ybuf[slot])
    r = r_ref[...]
    lane = lax.broadcasted_iota(jnp.int32, r.shape, 1)
    gate1 = jnp.sum(jnp.where(lane == 2, r, 0.0), axis=1, keepdims=True)
    gate2 = jnp.sum(jnp.where(lane == 3, r, 0.0), axis=1, keepdims=True)
    half = y_lo.shape[1]
    x_lo = x_ref[:, :half] + mod_ref[5:6, :half] * (gate1 * y_lo[:tm] + gate2 * y_lo[tm:])
    x_hi = x_ref[:, half:] + mod_ref[5:6, half:] * (gate1 * y_hi[:tm] + gate2 * y_hi[tm:])
    ms = (jnp.sum(x_lo * x_lo, axis=-1, keepdims=True)
          + jnp.sum(x_hi * x_hi, axis=-1, keepdims=True)) / (2 * half)
    inv = lax.rsqrt(ms + EPS)
    o_ref[:, :half] = x_lo * inv * g_ref[:, :half]
    o_ref[:, half:] = x_hi * inv * g_ref[:, half:]


def _final(y_slots, dest, route, x1, mod, g, S):
    T, D = x1.shape
    tm = _tile(S, 128)
    per_b = S // tm
    nsteps = T // tm
    pos_tiled = jnp.transpose(dest.reshape(nsteps, tm, 2), (0, 2, 1)).reshape(-1)
    grid_spec = pltpu.PrefetchScalarGridSpec(
        num_scalar_prefetch=1,
        grid=(nsteps,),
        in_specs=[pl.BlockSpec(memory_space=pl.ANY),
                  pl.BlockSpec((tm, D), lambda i, p: (i, 0)),
                  pl.BlockSpec((tm, LANES), lambda i, p: (i, 0)),
                  pl.BlockSpec((None, 6, D), lambda i, p: (i // per_b, 0, 0)),
                  pl.BlockSpec((1, D), lambda i, p: (0, 0))],
        out_specs=pl.BlockSpec((tm, D), lambda i, p: (i, 0)),
        scratch_shapes=[pltpu.VMEM((2, 2 * tm, D // 2), y_slots.dtype), pltpu.SemaphoreType.DMA((2,))],
    )
    return pl.pallas_call(
        functools.partial(_final_kernel, tm=tm, nsteps=nsteps),
        grid_spec=grid_spec,
        out_shape=jax.ShapeDtypeStruct((T, D), F32),
        compiler_params=_cparams("arbitrary"),
        name="final",
    )(pos_tiled, y_slots, x1, route, mod, g.reshape(1, D))


def _dispatch(route, counts, n_experts, tb):
    T = route.shape[0]
    ids = route[:, 0:2].astype(jnp.int32)
    rank = route[:, 4:6].astype(jnp.int32)
    counts = counts[0, :n_experts].astype(jnp.int32)
    padded = (counts + tb - 1) // tb * tb
    pend = jnp.cumsum(padded).astype(jnp.int32)
    pstart = pend - padded
    onehot = ids[:, :, None] == jnp.arange(n_experts, dtype=jnp.int32)
    dest = (rank + jnp.sum(jnp.where(onehot, pstart, 0), axis=-1)).reshape(-1)
    n_blocks = -(-2 * T // tb) + n_experts
    first_row = jnp.arange(n_blocks, dtype=jnp.int32) * tb
    block_e = jnp.minimum(jnp.sum(pend[None, :] <= first_row[:, None], axis=1), n_experts - 1)
    nused = (pend[-1] // tb).reshape(1)
    block_e = block_e.astype(jnp.int32)
    first = jnp.concatenate([jnp.ones((1,), jnp.int32), (block_e[1:] != block_e[:-1]).astype(jnp.int32)])
    return dest, pend, block_e, first, nused, n_blocks * tb


def _layer(x2, c, lidx, B, S, w_ada, b_ada, norm1_g, w_in, conv_qk, b_if, da_lambda, da_norm_g, rel_bias,
           ml_norm_g, w_o_attn, w_o_mlstm, w_out, norm2_g, w_group, b_group, w_expert, b_expert,
           w_gate_up, w_down):
    D = x2.shape[1]
    H = rel_bias.shape[1]
    d = da_lambda.shape[1]
    da_w = H * 2 * d
    MH, dh = ml_norm_g.shape
    ml_w = MH * dh
    n_experts = w_expert.shape[1]

    mod = _ada(c, w_ada, b_ada)

    off_mlq = 3 * da_w
    off_mlv = off_mlq + 2 * ml_w
    off_if = off_mlv + 2 * ml_w
    off_g = off_if + 2 * MH
    w_in_t = w_in.T
    h, gates_if = _norm1(x2, mod, norm1_g, w_in_t[off_if:off_g], b_if, B, S)
    q_scale = jnp.concatenate([jnp.full((da_w,), d ** -0.5 * math.log2(math.e), F32),
                               jnp.ones((2 * da_w,), F32)])
    da_qkv = _proj(h, w_in_t, 0, 3 * da_w, BF16, "proj_da", col_scale=q_scale)
    ml_qk = _proj(h, w_in_t, off_mlq, 2 * ml_w, F32, "proj_mlqk")
    ml_vo = _proj(h, w_in_t, off_mlv, 2 * ml_w, BF16, "proj_mlvo")
    gates_am = _proj(h, w_in_t, off_g, 2 * D, BF16, "proj_gates")

    lambda_init = 0.8 - 0.6 * math.exp(-0.3 * lidx)
    lamp = da_lambda.astype(F32)
    lam = jnp.exp(jnp.sum(lamp[0] * lamp[1])) - jnp.exp(jnp.sum(lamp[2] * lamp[3])) + lambda_init
    attn = _attn(da_qkv, lam, rel_bias, da_norm_g, B, S, H, d, lambda_init)

    hm = _mlstm(ml_qk, conv_qk, ml_vo, gates_if, ml_norm_g, B, S, MH, dh)

    merged = _merge(attn, hm, w_o_attn, w_o_mlstm, gates_am)
    x1 = _outproj(merged, w_out, x2, mod, S)

    h2, route, counts = _router(x1, mod, norm2_g, w_group, b_group, w_expert, b_expert, B, S)
    tb = 256
    dest, pend, block_e, first, nused, n_slots = _dispatch(route, counts, n_experts, tb)
    xs = _scatter(h2, dest, pend, n_slots, tb)
    y_slots = _moe(xs, block_e, first, nused, w_gate_up, w_down, tb)
    return y_slots, dest, route, x1, mod


def kernel(x, c, w_ada, b_ada, norm1_g, w_in, conv_qk, b_if, da_lambda, da_norm_g, rel_bias, ml_norm_g,
           w_o_attn, w_o_mlstm, w_out, norm2_g, w_group, b_group, w_expert, b_expert, w_gate_up, w_down,
           normf_g):
    B, S, D = x.shape
    assert w_ada.shape[0] == 1, "the final rmsnorm is fused into the layer's last kernel: one layer only"
    l = 0
    y_slots, dest, route, x1, mod = _layer(
        x.reshape(B * S, D), c, l, B, S, w_ada[l], b_ada[l], norm1_g[l], w_in[l], conv_qk[l], b_if[l],
        da_lambda[l], da_norm_g[l], rel_bias, ml_norm_g[l], w_o_attn[l], w_o_mlstm[l], w_out[l],
        norm2_g[l], w_group[l], b_group[l], w_expert[l], b_expert[l], w_gate_up[l], w_down[l])
    return _final(y_slots, dest, route, x1, mod, normf_g, S).reshape(B, S, D)
```

```python
import functools
import math

import jax
import jax.numpy as jnp
from jax import lax
from jax.experimental import pallas as pl
from jax.experimental.pallas import tpu as pltpu

F32 = jnp.float32
BF16 = jnp.bfloat16
EPS = 1e-6
REL_MAX_DIST = 128
ML_CHUNK = 128
NEG_BIG = -1e30
V7X_VMEM_LIMIT = 56 * 1024 * 1024
LANES = 128
SUBLANES = 8
DMA_UNROLL = 8
MOE_WEIGHT_SLICES = 2


def _cparams(*sem):
    return pltpu.CompilerParams(dimension_semantics=sem, vmem_limit_bytes=V7X_VMEM_LIMIT)


def _tile(n, pref):
    t = min(n, pref)
    while n % t:
        t //= 2
    return t


def _nt_dot(a, b):
    return lax.dot_general(a, b, (((1,), (1,)), ((), ())), preferred_element_type=F32)


def _tn_dot(a, b):
    return lax.dot_general(a, b, (((0,), (0,)), ((), ())), preferred_element_type=F32)


def _pack_halves(x):
    n = x.shape[1] // 2
    bits = pltpu.bitcast(x.astype(BF16).astype(F32), jnp.uint32)
    return bits[:, n:] | (bits[:, :n] >> 16)


def _unpack_halves(w):
    lo = pltpu.bitcast(w << 16, F32)
    hi = pltpu.bitcast(w & jnp.uint32(0xFFFF0000), F32)
    return lo, hi


def _ada_kernel(c_ref, w_ref, b_ref, o_ref):
    c = c_ref[...]
    s = (c * jax.nn.sigmoid(c)).astype(BF16)
    o_ref[...] = jnp.dot(s, w_ref[...].astype(BF16), preferred_element_type=F32) + b_ref[...]


def _ada(c, w_ada, b_ada):
    B, D = c.shape
    N = w_ada.shape[1]
    rows = 8
    c8 = jnp.zeros((rows, D), F32).at[:B].set(c)
    tn = _tile(N, 512)
    mod = pl.pallas_call(
        _ada_kernel,
        grid=(N // tn,),
        in_specs=[pl.BlockSpec((rows, D), lambda j: (0, 0)),
                  pl.BlockSpec((D, tn), lambda j: (0, j)),
                  pl.BlockSpec((1, tn), lambda j: (0, j))],
        out_specs=pl.BlockSpec((rows, tn), lambda j: (0, j)),
        out_shape=jax.ShapeDtypeStruct((rows, N), F32),
        compiler_params=_cparams("arbitrary"),
        name="ada",
    )(c8, w_ada, b_ada.reshape(1, N))
    return mod[:B].reshape(B, 6, D)


def _norm1_kernel(x_ref, mod_ref, g_ref, wif_ref, bif_ref, h_ref, gate_ref):
    x = x_ref[...]
    y = x * lax.rsqrt(jnp.mean(x * x, axis=-1, keepdims=True) + EPS) * g_ref[...]
    h = (y * (1.0 + mod_ref[1:2, :]) + mod_ref[0:1, :]).astype(BF16)
    h_ref[...] = h
    gate_ref[...] = _nt_dot(h, wif_ref[...].astype(BF16)) + bif_ref[...]


def _norm1(x2, mod, g, w_if_t, b_if, B, S):
    T, D = x2.shape
    G = w_if_t.shape[0]
    tm = _tile(S, 512)
    nb = S // tm
    return pl.pallas_call(
        _norm1_kernel,
        grid=(B, nb),
        in_specs=[pl.BlockSpec((tm, D), lambda b, i: (b * nb + i, 0)),
                  pl.BlockSpec((None, 6, D), lambda b, i: (b, 0, 0)),
                  pl.BlockSpec((1, D), lambda b, i: (0, 0)),
                  pl.BlockSpec((G, D), lambda b, i: (0, 0)),
                  pl.BlockSpec((1, G), lambda b, i: (0, 0))],
        out_specs=[pl.BlockSpec((tm, D), lambda b, i: (b * nb + i, 0)),
                   pl.BlockSpec((tm, G), lambda b, i: (b * nb + i, 0))],
        out_shape=[jax.ShapeDtypeStruct((T, D), BF16), jax.ShapeDtypeStruct((T, G), F32)],
        compiler_params=_cparams("arbitrary", "arbitrary"),
        name="norm1",
    )(x2, mod, g.reshape(1, D), w_if_t, b_if.reshape(1, G))


def _proj_kernel(h_ref, w_ref, o_ref):
    o_ref[...] = _nt_dot(h_ref[...], w_ref[...].astype(BF16)).astype(o_ref.dtype)


def _proj_scaled_kernel(h_ref, w_ref, s_ref, o_ref):
    acc = _nt_dot(h_ref[...], w_ref[...].astype(BF16))
    o_ref[...] = (acc * s_ref[...]).astype(o_ref.dtype)


def _proj(h, w_t, col0, ncols, out_dtype, name, col_scale=None):
    M, K = h.shape
    tm = _tile(M, 1024)
    tn = _tile(ncols, 512)
    assert col0 % SUBLANES == 0
    in_specs = [pl.BlockSpec((tm, K), lambda i, j: (i, 0)),
                pl.BlockSpec((pl.Element(tn), pl.Element(K)),
                             lambda i, j: (pl.multiple_of(col0 + j * tn, SUBLANES), 0))]
    args = (h, w_t)
    body = _proj_kernel
    if col_scale is not None:
        in_specs.append(pl.BlockSpec((1, tn), lambda i, j: (0, j)))
        args = (h, w_t, col_scale.reshape(1, ncols))
        body = _proj_scaled_kernel
    return pl.pallas_call(
        body,
        grid=(M // tm, ncols // tn),
        in_specs=in_specs,
        out_specs=pl.BlockSpec((tm, tn), lambda i, j: (i, j)),
        out_shape=jax.ShapeDtypeStruct((M, ncols), out_dtype),
        compiler_params=_cparams("arbitrary", "arbitrary"),
        name=name,
    )(*args)


def _attn_kernel(lam_ref, q_ref, k_ref, v_ref, bd_ref, bp_ref, g_ref, o_ref,
                 m_sc, l_sc, acc_sc, s_sc, p_sc, a_sc, *, d, t, out_scale):
    i = pl.program_id(2)
    q = q_ref[...]
    ngrp = t // LANES
    m_sc[...] = jnp.full(m_sc.shape, NEG_BIG, F32)
    l_sc[...] = jnp.zeros(l_sc.shape, F32)
    acc_sc[...] = jnp.zeros(acc_sc.shape, F32)
    spare = pl.ds(1 - lax.rem(i, 2), 1)
    p_sc[spare] = jnp.zeros((1,) + p_sc.shape[1:], BF16)
    a_sc[spare] = jnp.ones((1,) + a_sc.shape[1:], F32)

    def rows(j):
        return pl.ds(pl.multiple_of(j * t, t), t)

    def scores(j):
        k = k_ref[rows(j), :]
        for m in range(2):
            s_sc[m] = _nt_dot(q[:, m * d:(m + 1) * d], k[:, m * d:(m + 1) * d])

    def accumulate(j, buf):
        v = v_ref[rows(j), :]
        for m in range(2):
            alpha = a_sc[buf, m]
            pv = jnp.dot(p_sc[buf, m], v, preferred_element_type=F32)
            acc_sc[m] = jnp.concatenate([alpha] * (2 * d // LANES), axis=1) * acc_sc[m] + pv

    def softmax(bias_ref, buf):
        for m in range(2):
            s = s_sc[m]
            if bias_ref is not None:
                s = s + bias_ref[...]
            grp = [s[:, g * LANES:(g + 1) * LANES] for g in range(ngrp)]
            m_old = m_sc[m]
            row_max = jnp.max(functools.reduce(jnp.maximum, grp), axis=1, keepdims=True)
            m_new = jnp.maximum(m_old, row_max)
            alpha = jnp.exp2(m_old - m_new)
            p = [jnp.exp2(x - m_new) for x in grp]
            l_sc[m] = alpha * l_sc[m] + functools.reduce(jnp.add, p)
            p_sc[buf, m] = jnp.concatenate([x.astype(BF16) for x in p], axis=1)
            a_sc[buf, m] = alpha
            m_sc[m] = m_new

    def stage(j, bias_ref, buf, last=False):
        accumulate(jnp.maximum(j - 1, 0), 1 - buf)
        softmax(bias_ref, buf)
        if not last:
            scores(j + 1)

    n_far = jnp.maximum(i - 1, 0)
    odd = lax.rem(n_far, 2)

    def far_pair(jj, carry):
        j = odd + 2 * jj
        stage(j, None, 1)
        stage(j + 1, None, 0)
        return carry

    scores(0)

    @pl.when(odd == 1)
    def _():
        stage(0, None, 0)

    lax.fori_loop(0, n_far // 2, far_pair, 0)

    @pl.when(i > 0)
    def _():
        stage(i - 1, bp_ref, 1)

    stage(i, bd_ref, 0, last=True)
    accumulate(i, 0)

    l0 = jnp.sum(l_sc[0], axis=1, keepdims=True)
    l1 = jnp.sum(l_sc[1], axis=1, keepdims=True)
    a = acc_sc[0] / l0 - lam_ref[0] * (acc_sc[1] / l1)
    y = a * lax.rsqrt(jnp.mean(a * a, axis=-1, keepdims=True) + EPS) * g_ref[...]
    o_ref[...] = (y * out_scale).astype(o_ref.dtype)


def _rel_bucket(n, n_buckets):
    max_exact = n_buckets // 2
    nf = jnp.maximum(n, 1).astype(F32)
    large = max_exact + (jnp.log(nf / max_exact) / math.log(REL_MAX_DIST / max_exact)
                         * (n_buckets - max_exact)).astype(jnp.int32)
    large = jnp.minimum(large, n_buckets - 1)
    return jnp.where(n < max_exact, n, large)


def _bias_table(rel, dist, n_buckets):
    bucket = _rel_bucket(dist, n_buckets)
    out = jnp.zeros((rel.shape[1],) + dist.shape, F32)
    for b in range(n_buckets):
        out = jnp.where((bucket == b)[None], rel[b][:, None, None], out)
    return out


def _attn(qkv, lam, rel_bias, norm_g, B, S, H, d, lambda_init):
    T = qkv.shape[0]
    t = _tile(S, 512)
    assert t >= REL_MAX_DIST, "key blocks two or more tiles back must sit in the saturated bucket"
    assert d % LANES == 0
    nq = S // t
    n_buckets = rel_bias.shape[0]
    rel = rel_bias.astype(F32)
    rel = (rel - rel[n_buckets - 1]) * math.log2(math.e)
    dist_d = jnp.arange(t)[:, None] - jnp.arange(t)[None, :]
    bd = jnp.where((dist_d >= 0)[None], _bias_table(rel, jnp.maximum(dist_d, 0), n_buckets), NEG_BIG)
    bp = _bias_table(rel, dist_d + t, n_buckets)
    kern = functools.partial(_attn_kernel, d=d, t=t, out_scale=1.0 - lambda_init)
    return pl.pallas_call(
        kern,
        grid=(B, H, nq),
        in_specs=[pl.BlockSpec(memory_space=pltpu.SMEM),
                  pl.BlockSpec((t, 2 * d), lambda b, h, i: (b * nq + i, h)),
                  pl.BlockSpec((S, 2 * d), lambda b, h, i: (b, H + h)),
                  pl.BlockSpec((S, 2 * d), lambda b, h, i: (b, 2 * H + h)),
                  pl.BlockSpec((None, t, t), lambda b, h, i: (h, 0, 0)),
                  pl.BlockSpec((None, t, t), lambda b, h, i: (h, 0, 0)),
                  pl.BlockSpec((1, 2 * d), lambda b, h, i: (0, 0))],
        out_specs=pl.BlockSpec((t, 2 * d), lambda b, h, i: (b * nq + i, h)),
        out_shape=jax.ShapeDtypeStruct((T, H * 2 * d), BF16),
        scratch_shapes=[pltpu.VMEM((2, t, LANES), F32), pltpu.VMEM((2, t, LANES), F32),
                        pltpu.VMEM((2, t, 2 * d), F32), pltpu.VMEM((2, t, t), F32),
                        pltpu.VMEM((2, 2, t, t), BF16), pltpu.VMEM((2, 2, t, LANES), F32)],
        compiler_params=_cparams("arbitrary", "arbitrary", "arbitrary"),
        name="attn",
    )(lam.reshape(1), qkv, qkv, qkv, bd, bp, norm_g.reshape(1, 2 * d))


def _causal_conv_silu(u_ref, w_ref, ext_ref, scale):
    L = u_ref.shape[0]
    width = w_ref.shape[0]
    u = u_ref[...]
    ext_ref[8:, :] = u
    acc = u * w_ref[width - 1:width, :]
    for t in range(1, width):
        acc = acc + ext_ref[8 - t:8 - t + L, :] * w_ref[width - 1 - t:width - t, :]
    ext_ref[0:8, :] = u[L - 8:, :]
    return (acc * jax.nn.sigmoid(acc) * scale).astype(BF16)


def _mlstm_kernel(q_ref, k_ref, wq_ref, wk_ref, v_ref, og_ref, gcol_ref, grow_ref, g_ref, o_ref,
                  c_sc, n_sc, m_sc, extq_sc, extk_sc, *, nheads, group, dh):
    L = q_ref.shape[0]

    @pl.when(pl.program_id(2) == 0)
    def _():
        c_sc[...] = jnp.zeros(c_sc.shape, F32)
        n_sc[...] = jnp.zeros(n_sc.shape, F32)
        m_sc[...] = jnp.zeros(m_sc.shape, F32)
        extq_sc[0:8, :] = jnp.zeros((8, extq_sc.shape[1]), F32)
        extk_sc[0:8, :] = jnp.zeros((8, extk_sc.shape[1]), F32)

    q_all = _causal_conv_silu(q_ref, wq_ref, extq_sc, 1.0)
    k_all = _causal_conv_silu(k_ref, wk_ref, extk_sc, dh ** -0.5)

    gcol = gcol_ref[...]
    lane = lax.broadcasted_iota(jnp.int32, gcol.shape, 1)
    grow = grow_ref[...]
    sub = lax.broadcasted_iota(jnp.int32, grow.shape, 0)
    jj = lax.broadcasted_iota(jnp.int32, (L, L), 0)
    ss = lax.broadcasted_iota(jnp.int32, (L, L), 1)
    tril = ss <= jj

    for hx in range(group):
        h = pl.program_id(1) * group + hx
        cols = slice(hx * dh, (hx + 1) * dh)
        i_col = jnp.sum(jnp.where(lane == h, gcol, 0.0), axis=1, keepdims=True)
        f_col = jax.nn.log_sigmoid(jnp.sum(jnp.where(lane == h + nheads, gcol, 0.0), axis=1, keepdims=True))
        i_row = jnp.sum(jnp.where(sub == h, grow, 0.0), axis=0, keepdims=True)
        f_row = jax.nn.log_sigmoid(jnp.sum(jnp.where(sub == h + nheads, grow, 0.0), axis=0, keepdims=True))

        b_col = jnp.sum(jnp.where(tril, f_row, 0.0), axis=1, keepdims=True)
        b_row = jnp.sum(jnp.where(jj <= ss, f_col, 0.0), axis=0, keepdims=True)
        u_row = i_row - b_row
        u_col = i_col - b_col

        m_prev = m_sc[hx, 0:1, 0:1]
        mm_col = jnp.maximum(m_prev, jnp.max(jnp.where(tril, u_row, NEG_BIG), axis=1, keepdims=True))
        w = jnp.exp(jnp.where(tril, u_row - mm_col, NEG_BIG))
        a_inter = jnp.exp(m_prev - mm_col)

        q = q_all[:, cols]
        k = k_all[:, cols]
        v = v_ref[:, cols]
        c_old = c_sc[hx]
        n_old = n_sc[hx]
        s_qk = _nt_dot(q, k) * w
        num = (a_inter * _nt_dot(q, c_old.astype(BF16))
               + jnp.dot(s_qk.astype(BF16), v, preferred_element_type=F32))
        den = (a_inter * jnp.sum(q.astype(F32) * n_old, axis=1, keepdims=True)
               + jnp.sum(s_qk, axis=1, keepdims=True))
        hh = num / jnp.maximum(jnp.abs(den), jnp.exp(-(b_col + mm_col)))

        mm_last = mm_col[L - 1:L, :]
        w_s = jnp.exp(u_col - mm_last)
        a_state = jnp.exp(m_prev - mm_last)
        c_sc[hx] = a_state * c_old + _tn_dot((v.astype(F32) * w_s).astype(BF16), k)
        n_sc[hx] = a_state * n_old + jnp.sum(k.astype(F32) * w_s, axis=0, keepdims=True)
        m_sc[hx] = jnp.broadcast_to(b_col[L - 1:L, :] + mm_last, m_sc.shape[1:])

        y = hh * lax.rsqrt(jnp.mean(hh * hh, axis=-1, keepdims=True) + EPS) * g_ref[hx]
        o_ref[:, cols] = (y * jax.nn.sigmoid(og_ref[:, cols].astype(F32))).astype(o_ref.dtype)


def _mlstm(mqk, conv_w, mvo, gates, norm_g, B, S, H, dh):
    T = mqk.shape[0]
    width = conv_w.shape[0]
    L = ML_CHUNK
    nc = S // L
    group = _tile(H, 4)
    ng = H // group
    gw = group * dh
    grow = jnp.transpose(gates.reshape(B, S, 2 * H), (0, 2, 1))
    row = lambda b, h, c: (b * nc + c, h)
    row_hi = lambda b, h, c: (b * nc + c, ng + h)
    return pl.pallas_call(
        functools.partial(_mlstm_kernel, nheads=H, group=group, dh=dh),
        grid=(B, ng, nc),
        in_specs=[pl.BlockSpec((L, gw), row),
                  pl.BlockSpec((L, gw), row_hi),
                  pl.BlockSpec((width, gw), lambda b, h, c: (0, h)),
                  pl.BlockSpec((width, gw), lambda b, h, c: (0, ng + h)),
                  pl.BlockSpec((L, gw), row),
                  pl.BlockSpec((L, gw), row_hi),
                  pl.BlockSpec((L, 2 * H), lambda b, h, c: (b * nc + c, 0)),
                  pl.BlockSpec((None, 2 * H, L), lambda b, h, c: (b, 0, c)),
                  pl.BlockSpec((group, 1, dh), lambda b, h, c: (h, 0, 0))],
        out_specs=pl.BlockSpec((L, gw), row),
        out_shape=jax.ShapeDtypeStruct((T, H * dh), BF16),
        scratch_shapes=[pltpu.VMEM((group, dh, dh), F32), pltpu.VMEM((group, 1, dh), F32),
                        pltpu.VMEM((group, 8, LANES), F32), pltpu.VMEM((L + 8, gw), F32),
                        pltpu.VMEM((L + 8, gw), F32)],
        compiler_params=_cparams("arbitrary", "arbitrary", "arbitrary"),
        name="mlstm",
    )(mqk, mqk, conv_w, conv_w, mvo, mvo, gates, grow, norm_g.reshape(H, 1, dh))


def _merge_kernel(a_ref, m_ref, wa_ref, wm_ref, ga_ref, gm_ref, o_ref):
    pa = jnp.dot(a_ref[...], wa_ref[...].astype(BF16), preferred_element_type=F32)
    pm = jnp.dot(m_ref[...], wm_ref[...].astype(BF16), preferred_element_type=F32)
    out = (jax.nn.sigmoid(ga_ref[...].astype(F32)) * pa + jax.nn.sigmoid(gm_ref[...].astype(F32)) * pm)
    o_ref[...] = out.astype(o_ref.dtype)


def _merge(attn, hm, w_a, w_m, gates_am):
    T, Ka = attn.shape
    Km = hm.shape[1]
    D = w_a.shape[1]
    tm = _tile(T, 1024)
    tn = _tile(D, 256)
    nj = D // tn
    return pl.pallas_call(
        _merge_kernel,
        grid=(T // tm, nj),
        in_specs=[pl.BlockSpec((tm, Ka), lambda i, j: (i, 0)),
                  pl.BlockSpec((tm, Km), lambda i, j: (i, 0)),
                  pl.BlockSpec((Ka, tn), lambda i, j: (0, j)),
                  pl.BlockSpec((Km, tn), lambda i, j: (0, j)),
                  pl.BlockSpec((tm, tn), lambda i, j: (i, j)),
                  pl.BlockSpec((tm, tn), lambda i, j: (i, nj + j))],
        out_specs=pl.BlockSpec((tm, tn), lambda i, j: (i, j)),
        out_shape=jax.ShapeDtypeStruct((T, D), BF16),
        compiler_params=_cparams("arbitrary", "arbitrary"),
        name="merge",
    )(attn, hm, w_a, w_m, gates_am, gates_am)


def _out_kernel(a_ref, w_ref, x_ref, mod_ref, o_ref):
    p = jnp.dot(a_ref[...], w_ref[...].astype(BF16), preferred_element_type=F32)
    o_ref[...] = x_ref[...] + mod_ref[2:3, :] * p


def _outproj(merged, w_out, x2, mod, S):
    T, K = merged.shape
    D = w_out.shape[1]
    tm = _tile(S, 1024)
    tn = _tile(D, 512)
    per_b = S // tm
    return pl.pallas_call(
        _out_kernel,
        grid=(T // tm, D // tn),
        in_specs=[pl.BlockSpec((tm, K), lambda i, j: (i, 0)),
                  pl.BlockSpec((K, tn), lambda i, j: (0, j)),
                  pl.BlockSpec((tm, tn), lambda i, j: (i, j)),
                  pl.BlockSpec((None, 6, tn), lambda i, j: (i // per_b, 0, j))],
        out_specs=pl.BlockSpec((tm, tn), lambda i, j: (i, j)),
        out_shape=jax.ShapeDtypeStruct((T, D), F32),
        compiler_params=_cparams("arbitrary", "arbitrary"),
        name="outproj",
    )(merged, w_out, x2, mod)


def _router_kernel(x_ref, mod_ref, g_ref, whi_ref, wlo_ref, b_ref, h_ref, r_ref, cnt_ref, run_sc,
                   *, n_groups, per_group):
    @pl.when(jnp.logical_and(pl.program_id(0) == 0, pl.program_id(1) == 0))
    def _():
        run_sc[...] = jnp.zeros(run_sc.shape, F32)

    x = x_ref[...]
    y = x * lax.rsqrt(jnp.mean(x * x, axis=-1, keepdims=True) + EPS) * g_ref[...]
    h2 = y * (1.0 + mod_ref[4:5, :]) + mod_ref[3:4, :]
    h_ref[...] = _pack_halves(h2)
    hi = h2.astype(BF16)
    lo = (h2 - hi.astype(F32)).astype(BF16)
    logits = (jnp.dot(hi, whi_ref[...], preferred_element_type=F32)
              + jnp.dot(hi, wlo_ref[...], preferred_element_type=F32)
              + jnp.dot(lo, whi_ref[...], preferred_element_type=F32)) + b_ref[...]
    lane = lax.broadcasted_iota(jnp.int32, logits.shape, 1)
    big = jnp.int32(1 << 20)

    def top(vals):
        mx = jnp.max(vals, axis=1, keepdims=True)
        idx = jnp.min(jnp.where(vals == mx, lane, big), axis=1, keepdims=True)
        return mx, idx

    gl = jnp.where(lane < n_groups, logits, -jnp.inf)
    gmax, gidx = top(gl)
    g_p = 1.0 / jnp.sum(jnp.exp(gl - gmax), axis=1, keepdims=True)
    lo_lane = n_groups + per_group * gidx
    el = jnp.where((lane >= lo_lane) & (lane < lo_lane + per_group), logits, -jnp.inf)
    e1, i1 = top(el)
    e2, i2 = top(jnp.where(lane == i1, -jnp.inf, el))
    r = jnp.exp(e2 - e1)
    gate1 = g_p / (1.0 + r)
    gate2 = g_p * r / (1.0 + r)
    ex1 = i1 - n_groups
    ex2 = i2 - n_groups
    oh1 = (lane == ex1).astype(F32)
    oh2 = (lane == ex2).astype(F32)
    both = oh1 + oh2
    tm = x.shape[0]
    earlier = (lax.broadcasted_iota(jnp.int32, (tm, tm), 1)
               < lax.broadcasted_iota(jnp.int32, (tm, tm), 0)).astype(BF16)
    prefix = jnp.dot(earlier, both.astype(BF16), preferred_element_type=F32) + run_sc[0:1, :]
    rank1 = jnp.sum(oh1 * prefix, axis=1, keepdims=True)
    rank2 = jnp.sum(oh2 * prefix, axis=1, keepdims=True)
    run_sc[...] = run_sc[...] + jnp.sum(both, axis=0, keepdims=True)
    cnt_ref[...] = run_sc[...]

    cols = (ex1.astype(F32), ex2.astype(F32), gate1, gate2, rank1, rank2)
    out = jnp.zeros(logits.shape, F32)
    for n, col in enumerate(cols):
        out = jnp.where(lane == n, col, out)
    r_ref[...] = out


def _router(x1, mod, g, w_group, b_group, w_expert, b_expert, B, S):
    T, D = x1.shape
    NG = w_group.shape[1]
    NE = w_expert.shape[1]
    assert NG + NE <= LANES
    pad = LANES - NG - NE
    w = jnp.concatenate([w_group, w_expert, jnp.zeros((D, pad), F32)], axis=1)
    w_hi = w.astype(BF16)
    w_lo = (w - w_hi.astype(F32)).astype(BF16)
    bias = jnp.concatenate([b_group, b_expert, jnp.full((pad,), -jnp.inf, F32)]).reshape(1, LANES)
    tm = _tile(S, 256)
    nb = S // tm
    kern = functools.partial(_router_kernel, n_groups=NG, per_group=NE // NG)
    return pl.pallas_call(
        kern,
        grid=(B, nb),
        in_specs=[pl.BlockSpec((tm, D), lambda b, i: (b * nb + i, 0)),
                  pl.BlockSpec((None, 6, D), lambda b, i: (b, 0, 0)),
                  pl.BlockSpec((1, D), lambda b, i: (0, 0)),
                  pl.BlockSpec((D, LANES), lambda b, i: (0, 0)),
                  pl.BlockSpec((D, LANES), lambda b, i: (0, 0)),
                  pl.BlockSpec((1, LANES), lambda b, i: (0, 0))],
        out_specs=[pl.BlockSpec((tm, D // 2), lambda b, i: (b * nb + i, 0)),
                   pl.BlockSpec((tm, LANES), lambda b, i: (b * nb + i, 0)),
                   pl.BlockSpec((8, LANES), lambda b, i: (0, 0))],
        out_shape=[jax.ShapeDtypeStruct((T, D // 2), jnp.uint32), jax.ShapeDtypeStruct((T, LANES), F32),
                   jax.ShapeDtypeStruct((8, LANES), F32)],
        scratch_shapes=[pltpu.VMEM((8, LANES), F32)],
        compiler_params=_cparams("arbitrary", "arbitrary"),
        name="router",
    )(x1, mod, g.reshape(1, D), w_hi, w_lo, bias)


def _gather_rows(idx_ref, base, src_hbm, dst_ref, sem, n):
    def body(r, carry):
        tok = idx_ref[base + r]
        pltpu.make_async_copy(src_hbm.at[pl.ds(tok, 1), :], dst_ref.at[pl.ds(r, 1), :], sem).start()
        return carry
    lax.fori_loop(0, n, body, 0, unroll=DMA_UNROLL)


def _wait_rows(src_hbm, dst_ref, sem, n):
    def body(r, carry):
        pltpu.make_async_copy(src_hbm.at[pl.ds(0, 1), :], dst_ref.at[pl.ds(r, 1), :], sem).wait()
        return carry
    lax.fori_loop(0, n, body, 0, unroll=DMA_UNROLL)


def _scatter_kernel(dest_ref, pend_ref, h_ref, xs_hbm, zbuf, sem, zsem, *, tm, tb, n_experts, n_blocks):
    i = pl.program_id(0)

    def zero_block(row0):
        return pltpu.make_async_copy(zbuf, xs_hbm.at[pl.ds(pl.multiple_of(row0, tb), tb), :], zsem)

    def zero_copy(e):
        return zero_block(jnp.maximum(pend_ref[e] - tb, 0))

    @pl.when(i == 0)
    def _():
        zbuf[...] = jnp.zeros(zbuf.shape, zbuf.dtype)

        def start(e, carry):
            @pl.when(pend_ref[e] > 0)
            def _():
                zero_copy(e).start()
            return carry

        def wait(e, carry):
            @pl.when(pend_ref[e] > 0)
            def _():
                zero_copy(e).wait()
            return carry

        def start_tail(b, carry):
            zero_block(b * tb).start()
            return carry

        def wait_tail(b, carry):
            zero_block(b * tb).wait()
            return carry

        first_unused = pend_ref[n_experts - 1] // tb
        lax.fori_loop(0, n_experts, start, 0)
        lax.fori_loop(first_unused, n_blocks, start_tail, 0)
        lax.fori_loop(0, n_experts, wait, 0)
        lax.fori_loop(first_unused, n_blocks, wait_tail, 0)

    def row_copy(r, choice):
        slot = dest_ref[2 * (i * tm + r) + choice]
        return pltpu.make_async_copy(h_ref.at[pl.ds(r, 1), :], xs_hbm.at[pl.ds(slot, 1), :], sem)

    def start_row(r, carry):
        row_copy(r, 0).start()
        row_copy(r, 1).start()
        return carry

    def wait_row(r, carry):
        row_copy(r, 0).wait()
        row_copy(r, 1).wait()
        return carry

    lax.fori_loop(0, tm, start_row, 0, unroll=DMA_UNROLL)
    lax.fori_loop(0, tm, wait_row, 0, unroll=DMA_UNROLL)


def _scatter(h2, dest, pend, n_slots, tb):
    T, D = h2.shape
    tm = _tile(T, 256)
    grid_spec = pltpu.PrefetchScalarGridSpec(
        num_scalar_prefetch=2,
        grid=(T // tm,),
        in_specs=[pl.BlockSpec((tm, D), lambda i, de, pe: (i, 0))],
        out_specs=pl.BlockSpec(memory_space=pl.ANY),
        scratch_shapes=[pltpu.VMEM((tb, D), h2.dtype), pltpu.SemaphoreType.DMA(()),
                        pltpu.SemaphoreType.DMA(())],
    )
    return pl.pallas_call(
        functools.partial(_scatter_kernel, tm=tm, tb=tb, n_experts=pend.shape[0],
                          n_blocks=n_slots // tb),
        grid_spec=grid_spec,
        out_shape=jax.ShapeDtypeStruct((n_slots, D), h2.dtype),
        compiler_params=_cparams("arbitrary"),
        name="scatter",
    )(dest, pend, h2)


def _moe_kernel(be_ref, first_ref, nused_ref, x_ref, wg_ref, wu_ref, wd_ref, y_ref,
                cg_sc, cu_sc, cd_sc, acc_sc, *, nk):
    i = pl.program_id(0)
    k = pl.program_id(1)
    nused = nused_ref[0]
    live = i < nused
    first = first_ref[jnp.minimum(i, nused - 1)] == 1
    half = x_ref.shape[1]
    dk = wd_ref.shape[0]

    def x_halves():
        x_lo, x_hi = _unpack_halves(x_ref[...])
        return x_lo.astype(BF16), x_hi.astype(BF16)

    def expert(cols):
        x_lo, x_hi = x_halves()
        g = (jnp.dot(x_lo, cg_sc[:half, cols], preferred_element_type=F32)
             + jnp.dot(x_hi, cg_sc[half:, cols], preferred_element_type=F32))
        u = (jnp.dot(x_lo, cu_sc[:half, cols], preferred_element_type=F32)
             + jnp.dot(x_hi, cu_sc[half:, cols], preferred_element_type=F32))
        act = (g * jax.nn.sigmoid(g) * u).astype(BF16)
        return jnp.dot(act, cd_sc[cols, :], preferred_element_type=F32)

    for s in range(nk):
        @pl.when(jnp.logical_and(jnp.logical_and(live, first), k == s))
        def _():
            cols = slice(s * dk, (s + 1) * dk)
            cg_sc[:, cols] = wg_ref[...].astype(BF16)
            cu_sc[:, cols] = wu_ref[...].astype(BF16)
            cd_sc[cols, :] = wd_ref[...].astype(BF16)
            part = expert(cols)
            if s == 0:
                acc_sc[...] = part
            elif s < nk - 1:
                acc_sc[...] += part
            else:
                y_ref[...] = _pack_halves(acc_sc[...] + part)

    @pl.when(jnp.logical_and(live, jnp.logical_and(jnp.logical_not(first), k == nk - 1)))
    def _():
        y_ref[...] = _pack_halves(expert(slice(None)))

    @pl.when(jnp.logical_and(jnp.logical_not(live), k == 0))
    def _():
        y_ref[...] = jnp.zeros(y_ref.shape, y_ref.dtype)


def _moe(xs, block_e, first, nused, wgu, wd, tb):
    n_slots, hw = xs.shape
    E, D, de2 = wgu.shape
    de = de2 // 2
    nk = MOE_WEIGHT_SLICES
    dk = de // nk
    n_blocks = n_slots // tb

    def blk(i, nu):
        return jnp.minimum(i, nu[0] - 1)

    def piece(i, k, fi, nu):
        streaming = jnp.logical_and(i < nu[0], fi[blk(i, nu)] == 1)
        return jnp.where(streaming, k, nk - 1)

    grid_spec = pltpu.PrefetchScalarGridSpec(
        num_scalar_prefetch=3,
        grid=(n_blocks, nk),
        in_specs=[pl.BlockSpec((tb, hw), lambda i, k, be, fi, nu: (blk(i, nu), 0)),
                  pl.BlockSpec((None, D, dk), lambda i, k, be, fi, nu: (be[blk(i, nu)], 0, piece(i, k, fi, nu))),
                  pl.BlockSpec((None, D, dk),
                               lambda i, k, be, fi, nu: (be[blk(i, nu)], 0, nk + piece(i, k, fi, nu))),
                  pl.BlockSpec((None, dk, D), lambda i, k, be, fi, nu: (be[blk(i, nu)], piece(i, k, fi, nu), 0))],
        out_specs=pl.BlockSpec((tb, hw), lambda i, k, be, fi, nu: (i, 0)),
        scratch_shapes=[pltpu.VMEM((D, de), BF16), pltpu.VMEM((D, de), BF16), pltpu.VMEM((de, D), BF16),
                        pltpu.VMEM((tb, D), F32)],
    )
    return pl.pallas_call(
        functools.partial(_moe_kernel, nk=nk),
        grid_spec=grid_spec,
        out_shape=jax.ShapeDtypeStruct((n_slots, hw), jnp.uint32),
        compiler_params=_cparams("arbitrary", "arbitrary"),
        name="moe",
    )(block_e, first, nused, xs, wgu, wgu, wd)


def _final_kernel(pos_ref, y_hbm, x_ref, r_ref, mod_ref, g_ref, o_ref, ybuf, sem, *, tm, nsteps):
    i = pl.program_id(0)
    slot = lax.rem(i, 2)

    @pl.when(i == 0)
    def _():
        _gather_rows(pos_ref, 0, y_hbm, ybuf.at[0], sem.at[0], 2 * tm)

    @pl.when(i + 1 < nsteps)
    def _():
        _gather_rows(pos_ref, (i + 1) * 2 * tm, y_hbm, ybuf.at[1 - slot], sem.at[1 - slot], 2 * tm)

    _wait_rows(y_hbm, ybuf.at[slot], sem.at[slot], 2 * tm)
    y_lo, y_hi = _unpack_halves(ybuf[slot])
    r = r_ref[...]
    lane = lax.broadcasted_iota(jnp.int32, r.shape, 1)
    gate1 = jnp.sum(jnp.where(lane == 2, r, 0.0), axis=1, keepdims=True)
    gate2 = jnp.sum(jnp.where(lane == 3, r, 0.0), axis=1, keepdims=True)
    half = y_lo.shape[1]
    x_lo = x_ref[:, :half] + mod_ref[5:6, :half] * (gate1 * y_lo[:tm] + gate2 * y_lo[tm:])
    x_hi = x_ref[:, half:] + mod_ref[5:6, half:] * (gate1 * y_hi[:tm] + gate2 * y_hi[tm:])
    ms = (jnp.sum(x_lo * x_lo, axis=-1, keepdims=True)
          + jnp.sum(x_hi * x_hi, axis=-1, keepdims=True)) / (2 * half)
    inv = lax.rsqrt(ms + EPS)
    o_ref[:, :half] = x_lo * inv * g_ref[:, :half]
    o_ref[:, half:] = x_hi * inv * g_ref[:, half:]


def _final(y_slots, dest, route, x1, mod, g, S):
    T, D = x1.shape
    tm = _tile(S, 128)
    per_b = S // tm
    nsteps = T // tm
    pos_tiled = jnp.transpose(dest.reshape(nsteps, tm, 2), (0, 2, 1)).reshape(-1)
    grid_spec = pltpu.PrefetchScalarGridSpec(
        num_scalar_prefetch=1,
        grid=(nsteps,),
        in_specs=[pl.BlockSpec(memory_space=pl.ANY),
                  pl.BlockSpec((tm, D), lambda i, p: (i, 0)),
                  pl.BlockSpec((tm, LANES), lambda i, p: (i, 0)),
                  pl.BlockSpec((None, 6, D), lambda i, p: (i // per_b, 0, 0)),
                  pl.BlockSpec((1, D), lambda i, p: (0, 0))],
        out_specs=pl.BlockSpec((tm, D), lambda i, p: (i, 0)),
        scratch_shapes=[pltpu.VMEM((2, 2 * tm, D // 2), y_slots.dtype), pltpu.SemaphoreType.DMA((2,))],
    )
    return pl.pallas_call(
        functools.partial(_final_kernel, tm=tm, nsteps=nsteps),
        grid_spec=grid_spec,
        out_shape=jax.ShapeDtypeStruct((T, D), F32),
        compiler_params=_cparams("arbitrary"),
        name="final",
    )(pos_tiled, y_slots, x1, route, mod, g.reshape(1, D))


def _dispatch(route, counts, n_experts, tb):
    T = route.shape[0]
    ids = route[:, 0:2].astype(jnp.int32)
    rank = route[:, 4:6].astype(jnp.int32)
    counts = counts[0, :n_experts].astype(jnp.int32)
    padded = (counts + tb - 1) // tb * tb
    pend = jnp.cumsum(padded).astype(jnp.int32)
    pstart = pend - padded
    onehot = ids[:, :, None] == jnp.arange(n_experts, dtype=jnp.int32)
    dest = (rank + jnp.sum(jnp.where(onehot, pstart, 0), axis=-1)).reshape(-1)
    n_blocks = -(-2 * T // tb) + n_experts
    first_row = jnp.arange(n_blocks, dtype=jnp.int32) * tb
    block_e = jnp.minimum(jnp.sum(pend[None, :] <= first_row[:, None], axis=1), n_experts - 1)
    nused = (pend[-1] // tb).reshape(1)
    block_e = block_e.astype(jnp.int32)
    first = jnp.concatenate([jnp.ones((1,), jnp.int32), (block_e[1:] != block_e[:-1]).astype(jnp.int32)])
    return dest, pend, block_e, first, nused, n_blocks * tb


def _layer(x2, c, lidx, B, S, w_ada, b_ada, norm1_g, w_in, conv_qk, b_if, da_lambda, da_norm_g, rel_bias,
           ml_norm_g, w_o_attn, w_o_mlstm, w_out, norm2_g, w_group, b_group, w_expert, b_expert,
           w_gate_up, w_down):
    D = x2.shape[1]
    H = rel_bias.shape[1]
    d = da_lambda.shape[1]
    da_w = H * 2 * d
    MH, dh = ml_norm_g.shape
    ml_w = MH * dh
    n_experts = w_expert.shape[1]

    mod = _ada(c, w_ada, b_ada)

    off_mlq = 3 * da_w
    off_mlv = off_mlq + 2 * ml_w
    off_if = off_mlv + 2 * ml_w
    off_g = off_if + 2 * MH
    w_in_t = w_in.T
    h, gates_if = _norm1(x2, mod, norm1_g, w_in_t[off_if:off_g], b_if, B, S)
    q_scale = jnp.concatenate([jnp.full((da_w,), d ** -0.5 * math.log2(math.e), F32),
                               jnp.ones((2 * da_w,), F32)])
    da_qkv = _proj(h, w_in_t, 0, 3 * da_w, BF16, "proj_da", col_scale=q_scale)
    ml_qk = _proj(h, w_in_t, off_mlq, 2 * ml_w, F32, "proj_mlqk")
    ml_vo = _proj(h, w_in_t, off_mlv, 2 * ml_w, BF16, "proj_mlvo")
    gates_am = _proj(h, w_in_t, off_g, 2 * D, BF16, "proj_gates")

    lambda_init = 0.8 - 0.6 * math.exp(-0.3 * lidx)
    lamp = da_lambda.astype(F32)
    lam = jnp.exp(jnp.sum(lamp[0] * lamp[1])) - jnp.exp(jnp.sum(lamp[2] * lamp[3])) + lambda_init
    attn = _attn(da_qkv, lam, rel_bias, da_norm_g, B, S, H, d, lambda_init)

    hm = _mlstm(ml_qk, conv_qk, ml_vo, gates_if, ml_norm_g, B, S, MH, dh)

    merged = _merge(attn, hm, w_o_attn, w_o_mlstm, gates_am)
    x1 = _outproj(merged, w_out, x2, mod, S)

    h2, route, counts = _router(x1, mod, norm2_g, w_group, b_group, w_expert, b_expert, B, S)
    tb = 256
    dest, pend, block_e, first, nused, n_slots = _dispatch(route, counts, n_experts, tb)
    xs = _scatter(h2, dest, pend, n_slots, tb)
    y_slots = _moe(xs, block_e, first, nused, w_gate_up, w_down, tb)
    return y_slots, dest, route, x1, mod


def kernel(x, c, w_ada, b_ada, norm1_g, w_in, conv_qk, b_if, da_lambda, da_norm_g, rel_bias, ml_norm_g,
           w_o_attn, w_o_mlstm, w_out, norm2_g, w_group, b_group, w_expert, b_expert, w_gate_up, w_down,
           normf_g):
    B, S, D = x.shape
    assert w_ada.shape[0] == 1, "the final rmsnorm is fused into the layer's last kernel: one layer only"
    l = 0
    y_slots, dest, route, x1, mod = _layer(
        x.reshape(B * S, D), c, l, B, S, w_ada[l], b_ada[l], norm1_g[l], w_in[l], conv_qk[l], b_if[l],
        da_lambda[l], da_norm_g[l], rel_bias, ml_norm_g[l], w_o_attn[l], w_o_mlstm[l], w_out[l],
        norm2_g[l], w_group[l], b_group[l], w_expert[l], b_expert[l], w_gate_up[l], w_down[l])
    return _final(y_slots, dest, route, x1, mod, normf_g, S).reshape(B, S, D)
```

```python
import functools
import math

import jax
import jax.numpy as jnp
from jax import lax
from jax.experimental import pallas as pl
from jax.experimental.pallas import tpu as pltpu

F32 = jnp.float32
BF16 = jnp.bfloat16
EPS = 1e-6
REL_MAX_DIST = 128
ML_CHUNK = 128
NEG_BIG = -1e30
V7X_VMEM_LIMIT = 56 * 1024 * 1024
LANES = 128
SUBLANES = 8
HALO = SUBLANES
DMA_UNROLL = 8
MOE_WEIGHT_SLICES = 2


def _cparams(*sem):
    return pltpu.CompilerParams(dimension_semantics=sem, vmem_limit_bytes=V7X_VMEM_LIMIT)


def _tile(n, pref):
    t = min(n, pref)
    while n % t:
        t //= 2
    return t


def _nt_dot(a, b):
    return lax.dot_general(a, b, (((1,), (1,)), ((), ())), preferred_element_type=F32)


def _tn_dot(a, b):
    return lax.dot_general(a, b, (((0,), (0,)), ((), ())), preferred_element_type=F32)


def _pack_halves(x):
    n = x.shape[1] // 2
    bits = pltpu.bitcast(x.astype(BF16).astype(F32), jnp.uint32)
    return bits[:, n:] | (bits[:, :n] >> 16)


def _unpack_halves(w):
    lo = pltpu.bitcast(w << 16, F32)
    hi = pltpu.bitcast(w & jnp.uint32(0xFFFF0000), F32)
    return lo, hi


def _ada_kernel(c_ref, w_ref, b_ref, o_ref):
    c = c_ref[...]
    s = (c * jax.nn.sigmoid(c)).astype(BF16)
    o_ref[...] = jnp.dot(s, w_ref[...].astype(BF16), preferred_element_type=F32) + b_ref[...]


def _ada(c, w_ada, b_ada):
    B, D = c.shape
    N = w_ada.shape[1]
    rows = SUBLANES * pl.cdiv(B, SUBLANES)
    c8 = jnp.zeros((rows, D), F32).at[:B].set(c)
    tn = _tile(N, 512)
    mod = pl.pallas_call(
        _ada_kernel,
        grid=(N // tn,),
        in_specs=[pl.BlockSpec((rows, D), lambda j: (0, 0)),
                  pl.BlockSpec((D, tn), lambda j: (0, j)),
                  pl.BlockSpec((1, tn), lambda j: (0, j))],
        out_specs=pl.BlockSpec((rows, tn), lambda j: (0, j)),
        out_shape=jax.ShapeDtypeStruct((rows, N), F32),
        compiler_params=_cparams("arbitrary"),
        name="ada",
    )(c8, w_ada, b_ada.reshape(1, N))
    return mod[:B].reshape(B, 6, D)


def _norm1_kernel(x_ref, mod_ref, g_ref, wif_ref, bif_ref, h_ref, gate_ref):
    x = x_ref[...]
    y = x * lax.rsqrt(jnp.mean(x * x, axis=-1, keepdims=True) + EPS) * g_ref[...]
    h = (y * (1.0 + mod_ref[1:2, :]) + mod_ref[0:1, :]).astype(BF16)
    h_ref[...] = h
    gate_ref[...] = _nt_dot(h, wif_ref[...].astype(BF16)) + bif_ref[...]


def _norm1(x2, mod, g, w_if_t, b_if, B, S):
    T, D = x2.shape
    G = w_if_t.shape[0]
    tm = _tile(S, 512)
    nb = S // tm
    return pl.pallas_call(
        _norm1_kernel,
        grid=(B, nb),
        in_specs=[pl.BlockSpec((tm, D), lambda b, i: (b * nb + i, 0)),
                  pl.BlockSpec((None, 6, D), lambda b, i: (b, 0, 0)),
                  pl.BlockSpec((1, D), lambda b, i: (0, 0)),
                  pl.BlockSpec((G, D), lambda b, i: (0, 0)),
                  pl.BlockSpec((1, G), lambda b, i: (0, 0))],
        out_specs=[pl.BlockSpec((tm, D), lambda b, i: (b * nb + i, 0)),
                   pl.BlockSpec((tm, G), lambda b, i: (b * nb + i, 0))],
        out_shape=[jax.ShapeDtypeStruct((T, D), BF16), jax.ShapeDtypeStruct((T, G), F32)],
        compiler_params=_cparams("arbitrary", "arbitrary"),
        name="norm1",
    )(x2, mod, g.reshape(1, D), w_if_t, b_if.reshape(1, G))


def _proj_kernel(h_ref, w_ref, o_ref):
    o_ref[...] = _nt_dot(h_ref[...], w_ref[...].astype(BF16)).astype(o_ref.dtype)


def _proj_scaled_kernel(h_ref, w_ref, s_ref, o_ref):
    acc = _nt_dot(h_ref[...], w_ref[...].astype(BF16))
    o_ref[...] = (acc * s_ref[...]).astype(o_ref.dtype)


def _proj(h, w_t, col0, ncols, out_dtype, name, col_scale=None):
    M, K = h.shape
    tm = _tile(M, 1024)
    tn = _tile(ncols, 512)
    assert col0 % SUBLANES == 0
    in_specs = [pl.BlockSpec((tm, K), lambda i, j: (i, 0)),
                pl.BlockSpec((pl.Element(tn), pl.Element(K)),
                             lambda i, j: (pl.multiple_of(col0 + j * tn, SUBLANES), 0))]
    args = (h, w_t)
    body = _proj_kernel
    if col_scale is not None:
        in_specs.append(pl.BlockSpec((1, tn), lambda i, j: (0, j)))
        args = (h, w_t, col_scale.reshape(1, ncols))
        body = _proj_scaled_kernel
    return pl.pallas_call(
        body,
        grid=(M // tm, ncols // tn),
        in_specs=in_specs,
        out_specs=pl.BlockSpec((tm, tn), lambda i, j: (i, j)),
        out_shape=jax.ShapeDtypeStruct((M, ncols), out_dtype),
        compiler_params=_cparams("arbitrary", "arbitrary"),
        name=name,
    )(*args)


def _attn_kernel(lam_ref, q_ref, k_ref, v_ref, bd_ref, bp_ref, g_ref, o_ref,
                 m_sc, l_sc, acc_sc, s_sc, p_sc, a_sc, *, d, t, out_scale):
    i = pl.program_id(2)
    q = q_ref[...]
    ngrp = t // LANES
    m_sc[...] = jnp.full(m_sc.shape, NEG_BIG, F32)
    l_sc[...] = jnp.zeros(l_sc.shape, F32)
    acc_sc[...] = jnp.zeros(acc_sc.shape, F32)
    spare = pl.ds(1 - lax.rem(i, 2), 1)
    p_sc[spare] = jnp.zeros((1,) + p_sc.shape[1:], BF16)
    a_sc[spare] = jnp.ones((1,) + a_sc.shape[1:], F32)

    def rows(j):
        return pl.ds(pl.multiple_of(j * t, t), t)

    def scores(j):
        k = k_ref[rows(j), :]
        for m in range(2):
            s_sc[m] = _nt_dot(q[:, m * d:(m + 1) * d], k[:, m * d:(m + 1) * d])

    def accumulate(j, buf):
        v = v_ref[rows(j), :]
        for m in range(2):
            alpha = a_sc[buf, m]
            pv = jnp.dot(p_sc[buf, m], v, preferred_element_type=F32)
            acc_sc[m] = jnp.concatenate([alpha] * (2 * d // LANES), axis=1) * acc_sc[m] + pv

    def softmax(bias_ref, buf):
        for m in range(2):
            s = s_sc[m]
            if bias_ref is not None:
                s = s + bias_ref[...]
            grp = [s[:, g * LANES:(g + 1) * LANES] for g in range(ngrp)]
            m_old = m_sc[m]
            row_max = jnp.max(functools.reduce(jnp.maximum, grp), axis=1, keepdims=True)
            m_new = jnp.maximum(m_old, row_max)
            alpha = jnp.exp2(m_old - m_new)
            p = [jnp.exp2(x - m_new) for x in grp]
            l_sc[m] = alpha * l_sc[m] + functools.reduce(jnp.add, p)
            p_sc[buf, m] = jnp.concatenate([x.astype(BF16) for x in p], axis=1)
            a_sc[buf, m] = alpha
            m_sc[m] = m_new

    def stage(j, bias_ref, buf, last=False):
        accumulate(jnp.maximum(j - 1, 0), 1 - buf)
        softmax(bias_ref, buf)
        if not last:
            scores(j + 1)

    n_far = jnp.maximum(i - 1, 0)
    odd = lax.rem(n_far, 2)

    def far_pair(jj, carry):
        j = odd + 2 * jj
        stage(j, None, 1)
        stage(j + 1, None, 0)
        return carry

    scores(0)

    @pl.when(odd == 1)
    def _():
        stage(0, None, 0)

    lax.fori_loop(0, n_far // 2, far_pair, 0)

    @pl.when(i > 0)
    def _():
        stage(i - 1, bp_ref, 1)

    stage(i, bd_ref, 0, last=True)
    accumulate(i, 0)

    l0 = jnp.sum(l_sc[0], axis=1, keepdims=True)
    l1 = jnp.sum(l_sc[1], axis=1, keepdims=True)
    a = acc_sc[0] / l0 - lam_ref[0] * (acc_sc[1] / l1)
    y = a * lax.rsqrt(jnp.mean(a * a, axis=-1, keepdims=True) + EPS) * g_ref[...]
    o_ref[...] = (y * out_scale).astype(o_ref.dtype)


def _rel_bucket(n, n_buckets):
    max_exact = n_buckets // 2
    nf = jnp.maximum(n, 1).astype(F32)
    large = max_exact + (jnp.log(nf / max_exact) / math.log(REL_MAX_DIST / max_exact)
                         * (n_buckets - max_exact)).astype(jnp.int32)
    large = jnp.minimum(large, n_buckets - 1)
    return jnp.where(n < max_exact, n, large)


def _bias_table(rel, dist, n_buckets):
    bucket = _rel_bucket(dist, n_buckets)
    out = jnp.zeros((rel.shape[1],) + dist.shape, F32)
    for b in range(n_buckets):
        out = jnp.where((bucket == b)[None], rel[b][:, None, None], out)
    return out


def _attn(qkv, lam, rel_bias, norm_g, B, S, H, d, lambda_init):
    T = qkv.shape[0]
    t = _tile(S, 512)
    assert t >= REL_MAX_DIST, "key blocks two or more tiles back must sit in the saturated bucket"
    assert d % LANES == 0
    nq = S // t
    n_buckets = rel_bias.shape[0]
    rel = rel_bias.astype(F32)
    rel = (rel - rel[n_buckets - 1]) * math.log2(math.e)
    dist_d = jnp.arange(t)[:, None] - jnp.arange(t)[None, :]
    bd = jnp.where((dist_d >= 0)[None], _bias_table(rel, jnp.maximum(dist_d, 0), n_buckets), NEG_BIG)
    bp = _bias_table(rel, dist_d + t, n_buckets)
    kern = functools.partial(_attn_kernel, d=d, t=t, out_scale=1.0 - lambda_init)
    return pl.pallas_call(
        kern,
        grid=(B, H, nq),
        in_specs=[pl.BlockSpec(memory_space=pltpu.SMEM),
                  pl.BlockSpec((t, 2 * d), lambda b, h, i: (b * nq + i, h)),
                  pl.BlockSpec((S, 2 * d), lambda b, h, i: (b, H + h)),
                  pl.BlockSpec((S, 2 * d), lambda b, h, i: (b, 2 * H + h)),
                  pl.BlockSpec((None, t, t), lambda b, h, i: (h, 0, 0)),
                  pl.BlockSpec((None, t, t), lambda b, h, i: (h, 0, 0)),
                  pl.BlockSpec((1, 2 * d), lambda b, h, i: (0, 0))],
        out_specs=pl.BlockSpec((t, 2 * d), lambda b, h, i: (b * nq + i, h)),
        out_shape=jax.ShapeDtypeStruct((T, H * 2 * d), BF16),
        scratch_shapes=[pltpu.VMEM((2, t, LANES), F32), pltpu.VMEM((2, t, LANES), F32),
                        pltpu.VMEM((2, t, 2 * d), F32), pltpu.VMEM((2, t, t), F32),
                        pltpu.VMEM((2, 2, t, t), BF16), pltpu.VMEM((2, 2, t, LANES), F32)],
        compiler_params=_cparams("arbitrary", "arbitrary", "arbitrary"),
        name="attn",
    )(lam.reshape(1), qkv, qkv, qkv, bd, bp, norm_g.reshape(1, 2 * d))


def _causal_conv_silu(u_ref, w_ref, ext_ref, scale):
    L = u_ref.shape[0]
    width = w_ref.shape[0]
    assert width - 1 <= HALO
    u = u_ref[...]
    ext_ref[HALO:, :] = u
    acc = u * w_ref[width - 1:width, :]
    for t in range(1, width):
        acc = acc + ext_ref[HALO - t:HALO - t + L, :] * w_ref[width - 1 - t:width - t, :]
    ext_ref[0:HALO, :] = u[L - HALO:, :]
    return (acc * jax.nn.sigmoid(acc) * scale).astype(BF16)


def _mlstm_kernel(q_ref, k_ref, wq_ref, wk_ref, v_ref, og_ref, gcol_ref, grow_ref, g_ref, o_ref,
                  c_sc, n_sc, m_sc, extq_sc, extk_sc, *, nheads, group, dh):
    L = q_ref.shape[0]

    @pl.when(pl.program_id(2) == 0)
    def _():
        c_sc[...] = jnp.zeros(c_sc.shape, F32)
        n_sc[...] = jnp.zeros(n_sc.shape, F32)
        m_sc[...] = jnp.zeros(m_sc.shape, F32)
        extq_sc[0:HALO, :] = jnp.zeros((HALO, extq_sc.shape[1]), F32)
        extk_sc[0:HALO, :] = jnp.zeros((HALO, extk_sc.shape[1]), F32)

    q_all = _causal_conv_silu(q_ref, wq_ref, extq_sc, 1.0)
    k_all = _causal_conv_silu(k_ref, wk_ref, extk_sc, dh ** -0.5)

    gcol = gcol_ref[...]
    lane = lax.broadcasted_iota(jnp.int32, gcol.shape, 1)
    grow = grow_ref[...]
    sub = lax.broadcasted_iota(jnp.int32, grow.shape, 0)
    jj = lax.broadcasted_iota(jnp.int32, (L, L), 0)
    ss = lax.broadcasted_iota(jnp.int32, (L, L), 1)
    tril = ss <= jj

    for hx in range(group):
        h = pl.program_id(1) * group + hx
        cols = slice(hx * dh, (hx + 1) * dh)
        i_col = jnp.sum(jnp.where(lane == h, gcol, 0.0), axis=1, keepdims=True)
        f_col = jax.nn.log_sigmoid(jnp.sum(jnp.where(lane == h + nheads, gcol, 0.0), axis=1, keepdims=True))
        i_row = jnp.sum(jnp.where(sub == h, grow, 0.0), axis=0, keepdims=True)
        f_row = jax.nn.log_sigmoid(jnp.sum(jnp.where(sub == h + nheads, grow, 0.0), axis=0, keepdims=True))

        b_col = jnp.sum(jnp.where(tril, f_row, 0.0), axis=1, keepdims=True)
        b_row = jnp.sum(jnp.where(jj <= ss, f_col, 0.0), axis=0, keepdims=True)
        u_row = i_row - b_row
        u_col = i_col - b_col

        m_prev = m_sc[hx, 0:1, 0:1]
        mm_col = jnp.maximum(m_prev, jnp.max(jnp.where(tril, u_row, NEG_BIG), axis=1, keepdims=True))
        w = jnp.exp(jnp.where(tril, u_row - mm_col, NEG_BIG))
        a_inter = jnp.exp(m_prev - mm_col)

        q = q_all[:, cols]
        k = k_all[:, cols]
        v = v_ref[:, cols]
        c_old = c_sc[hx]
        n_old = n_sc[hx]
        s_qk = _nt_dot(q, k) * w
        num = (a_inter * _nt_dot(q, c_old.astype(BF16))
               + jnp.dot(s_qk.astype(BF16), v, preferred_element_type=F32))
        den = (a_inter * jnp.sum(q.astype(F32) * n_old, axis=1, keepdims=True)
               + jnp.sum(s_qk, axis=1, keepdims=True))
        hh = num / jnp.maximum(jnp.abs(den), jnp.exp(-(b_col + mm_col)))

        mm_last = mm_col[L - 1:L, :]
        w_s = jnp.exp(u_col - mm_last)
        a_state = jnp.exp(m_prev - mm_last)
        c_sc[hx] = a_state * c_old + _tn_dot((v.astype(F32) * w_s).astype(BF16), k)
        n_sc[hx] = a_state * n_old + jnp.sum(k.astype(F32) * w_s, axis=0, keepdims=True)
        m_sc[hx] = jnp.broadcast_to(b_col[L - 1:L, :] + mm_last, m_sc.shape[1:])

        y = hh * lax.rsqrt(jnp.mean(hh * hh, axis=-1, keepdims=True) + EPS) * g_ref[hx]
        o_ref[:, cols] = (y * jax.nn.sigmoid(og_ref[:, cols].astype(F32))).astype(o_ref.dtype)


def _mlstm(mqk, conv_w, mvo, gates, norm_g, B, S, H, dh):
    T = mqk.shape[0]
    width = conv_w.shape[0]
    L = ML_CHUNK
    nc = S // L
    group = _tile(H, 4)
    ng = H // group
    gw = group * dh
    grow = jnp.transpose(gates.reshape(B, S, 2 * H), (0, 2, 1))
    row = lambda b, h, c: (b * nc + c, h)
    row_hi = lambda b, h, c: (b * nc + c, ng + h)
    return pl.pallas_call(
        functools.partial(_mlstm_kernel, nheads=H, group=group, dh=dh),
        grid=(B, ng, nc),
        in_specs=[pl.BlockSpec((L, gw), row),
                  pl.BlockSpec((L, gw), row_hi),
                  pl.BlockSpec((width, gw), lambda b, h, c: (0, h)),
                  pl.BlockSpec((width, gw), lambda b, h, c: (0, ng + h)),
                  pl.BlockSpec((L, gw), row),
                  pl.BlockSpec((L, gw), row_hi),
                  pl.BlockSpec((L, 2 * H), lambda b, h, c: (b * nc + c, 0)),
                  pl.BlockSpec((None, 2 * H, L), lambda b, h, c: (b, 0, c)),
                  pl.BlockSpec((group, 1, dh), lambda b, h, c: (h, 0, 0))],
        out_specs=pl.BlockSpec((L, gw), row),
        out_shape=jax.ShapeDtypeStruct((T, H * dh), BF16),
        scratch_shapes=[pltpu.VMEM((group, dh, dh), F32), pltpu.VMEM((group, 1, dh), F32),
                        pltpu.VMEM((group, SUBLANES, LANES), F32), pltpu.VMEM((L + HALO, gw), F32),
                        pltpu.VMEM((L + HALO, gw), F32)],
        compiler_params=_cparams("arbitrary", "arbitrary", "arbitrary"),
        name="mlstm",
    )(mqk, mqk, conv_w, conv_w, mvo, mvo, gates, grow, norm_g.reshape(H, 1, dh))


def _merge_kernel(a_ref, m_ref, wa_ref, wm_ref, ga_ref, gm_ref, o_ref):
    pa = jnp.dot(a_ref[...], wa_ref[...].astype(BF16), preferred_element_type=F32)
    pm = jnp.dot(m_ref[...], wm_ref[...].astype(BF16), preferred_element_type=F32)
    out = (jax.nn.sigmoid(ga_ref[...].astype(F32)) * pa + jax.nn.sigmoid(gm_ref[...].astype(F32)) * pm)
    o_ref[...] = out.astype(o_ref.dtype)


def _merge(attn, hm, w_a, w_m, gates_am):
    T, Ka = attn.shape
    Km = hm.shape[1]
    D = w_a.shape[1]
    tm = _tile(T, 1024)
    tn = _tile(D, 256)
    nj = D // tn
    return pl.pallas_call(
        _merge_kernel,
        grid=(T // tm, nj),
        in_specs=[pl.BlockSpec((tm, Ka), lambda i, j: (i, 0)),
                  pl.BlockSpec((tm, Km), lambda i, j: (i, 0)),
                  pl.BlockSpec((Ka, tn), lambda i, j: (0, j)),
                  pl.BlockSpec((Km, tn), lambda i, j: (0, j)),
                  pl.BlockSpec((tm, tn), lambda i, j: (i, j)),
                  pl.BlockSpec((tm, tn), lambda i, j: (i, nj + j))],
        out_specs=pl.BlockSpec((tm, tn), lambda i, j: (i, j)),
        out_shape=jax.ShapeDtypeStruct((T, D), BF16),
        compiler_params=_cparams("arbitrary", "arbitrary"),
        name="merge",
    )(attn, hm, w_a, w_m, gates_am, gates_am)


def _out_kernel(a_ref, w_ref, x_ref, mod_ref, o_ref):
    p = jnp.dot(a_ref[...], w_ref[...].astype(BF16), preferred_element_type=F32)
    o_ref[...] = x_ref[...] + mod_ref[2:3, :] * p


def _outproj(merged, w_out, x2, mod, S):
    T, K = merged.shape
    D = w_out.shape[1]
    tm = _tile(S, 1024)
    tn = _tile(D, 512)
    per_b = S // tm
    return pl.pallas_call(
        _out_kernel,
        grid=(T // tm, D // tn),
        in_specs=[pl.BlockSpec((tm, K), lambda i, j: (i, 0)),
                  pl.BlockSpec((K, tn), lambda i, j: (0, j)),
                  pl.BlockSpec((tm, tn), lambda i, j: (i, j)),
                  pl.BlockSpec((None, 6, tn), lambda i, j: (i // per_b, 0, j))],
        out_specs=pl.BlockSpec((tm, tn), lambda i, j: (i, j)),
        out_shape=jax.ShapeDtypeStruct((T, D), F32),
        compiler_params=_cparams("arbitrary", "arbitrary"),
        name="outproj",
    )(merged, w_out, x2, mod)


def _router_kernel(x_ref, mod_ref, g_ref, whi_ref, wlo_ref, b_ref, h_ref, r_ref, cnt_ref, run_sc,
                   *, n_groups, per_group):
    @pl.when(jnp.logical_and(pl.program_id(0) == 0, pl.program_id(1) == 0))
    def _():
        run_sc[...] = jnp.zeros(run_sc.shape, F32)

    x = x_ref[...]
    y = x * lax.rsqrt(jnp.mean(x * x, axis=-1, keepdims=True) + EPS) * g_ref[...]
    h2 = y * (1.0 + mod_ref[4:5, :]) + mod_ref[3:4, :]
    h_ref[...] = _pack_halves(h2)
    hi = h2.astype(BF16)
    lo = (h2 - hi.astype(F32)).astype(BF16)
    logits = (jnp.dot(hi, whi_ref[...], preferred_element_type=F32)
              + jnp.dot(hi, wlo_ref[...], preferred_element_type=F32)
              + jnp.dot(lo, whi_ref[...], preferred_element_type=F32)) + b_ref[...]
    lane = lax.broadcasted_iota(jnp.int32, logits.shape, 1)
    big = jnp.int32(1 << 20)

    def top(vals):
        mx = jnp.max(vals, axis=1, keepdims=True)
        idx = jnp.min(jnp.where(vals == mx, lane, big), axis=1, keepdims=True)
        return mx, idx

    gl = jnp.where(lane < n_groups, logits, -jnp.inf)
    gmax, gidx = top(gl)
    g_p = 1.0 / jnp.sum(jnp.exp(gl - gmax), axis=1, keepdims=True)
    lo_lane = n_groups + per_group * gidx
    el = jnp.where((lane >= lo_lane) & (lane < lo_lane + per_group), logits, -jnp.inf)
    e1, i1 = top(el)
    e2, i2 = top(jnp.where(lane == i1, -jnp.inf, el))
    r = jnp.exp(e2 - e1)
    gate1 = g_p / (1.0 + r)
    gate2 = g_p * r / (1.0 + r)
    ex1 = i1 - n_groups
    ex2 = i2 - n_groups
    oh1 = (lane == ex1).astype(F32)
    oh2 = (lane == ex2).astype(F32)
    both = oh1 + oh2
    tm = x.shape[0]
    earlier = (lax.broadcasted_iota(jnp.int32, (tm, tm), 1)
               < lax.broadcasted_iota(jnp.int32, (tm, tm), 0)).astype(BF16)
    prefix = jnp.dot(earlier, both.astype(BF16), preferred_element_type=F32) + run_sc[0:1, :]
    rank1 = jnp.sum(oh1 * prefix, axis=1, keepdims=True)
    rank2 = jnp.sum(oh2 * prefix, axis=1, keepdims=True)
    run_sc[...] = run_sc[...] + jnp.sum(both, axis=0, keepdims=True)
    cnt_ref[...] = run_sc[...]

    cols = (ex1.astype(F32), ex2.astype(F32), gate1, gate2, rank1, rank2)
    out = jnp.zeros(logits.shape, F32)
    for n, col in enumerate(cols):
        out = jnp.where(lane == n, col, out)
    r_ref[...] = out


def _router(x1, mod, g, w_group, b_group, w_expert, b_expert, B, S):
    T, D = x1.shape
    NG = w_group.shape[1]
    NE = w_expert.shape[1]
    assert NG + NE <= LANES
    pad = LANES - NG - NE
    w = jnp.concatenate([w_group, w_expert, jnp.zeros((D, pad), F32)], axis=1)
    w_hi = w.astype(BF16)
    w_lo = (w - w_hi.astype(F32)).astype(BF16)
    bias = jnp.concatenate([b_group, b_expert, jnp.full((pad,), -jnp.inf, F32)]).reshape(1, LANES)
    tm = _tile(S, 256)
    nb = S // tm
    kern = functools.partial(_router_kernel, n_groups=NG, per_group=NE // NG)
    return pl.pallas_call(
        kern,
        grid=(B, nb),
        in_specs=[pl.BlockSpec((tm, D), lambda b, i: (b * nb + i, 0)),
                  pl.BlockSpec((None, 6, D), lambda b, i: (b, 0, 0)),
                  pl.BlockSpec((1, D), lambda b, i: (0, 0)),
                  pl.BlockSpec((D, LANES), lambda b, i: (0, 0)),
                  pl.BlockSpec((D, LANES), lambda b, i: (0, 0)),
                  pl.BlockSpec((1, LANES), lambda b, i: (0, 0))],
        out_specs=[pl.BlockSpec((tm, D // 2), lambda b, i: (b * nb + i, 0)),
                   pl.BlockSpec((tm, LANES), lambda b, i: (b * nb + i, 0)),
                   pl.BlockSpec((SUBLANES, LANES), lambda b, i: (0, 0))],
        out_shape=[jax.ShapeDtypeStruct((T, D // 2), jnp.uint32), jax.ShapeDtypeStruct((T, LANES), F32),
                   jax.ShapeDtypeStruct((SUBLANES, LANES), F32)],
        scratch_shapes=[pltpu.VMEM((SUBLANES, LANES), F32)],
        compiler_params=_cparams("arbitrary", "arbitrary"),
        name="router",
    )(x1, mod, g.reshape(1, D), w_hi, w_lo, bias)


def _gather_rows(idx_ref, base, src_hbm, dst_ref, sem, n):
    def body(r, carry):
        tok = idx_ref[base + r]
        pltpu.make_async_copy(src_hbm.at[pl.ds(tok, 1), :], dst_ref.at[pl.ds(r, 1), :], sem).start()
        return carry
    lax.fori_loop(0, n, body, 0, unroll=DMA_UNROLL)


def _wait_rows(src_hbm, dst_ref, sem, n):
    def body(r, carry):
        pltpu.make_async_copy(src_hbm.at[pl.ds(0, 1), :], dst_ref.at[pl.ds(r, 1), :], sem).wait()
        return carry
    lax.fori_loop(0, n, body, 0, unroll=DMA_UNROLL)


def _scatter_kernel(dest_ref, pend_ref, h_ref, xs_hbm, zbuf, sem, zsem, *, tm, tb, n_experts, n_blocks):
    i = pl.program_id(0)

    def zero_block(row0):
        return pltpu.make_async_copy(zbuf, xs_hbm.at[pl.ds(pl.multiple_of(row0, tb), tb), :], zsem)

    def zero_copy(e):
        return zero_block(jnp.maximum(pend_ref[e] - tb, 0))

    @pl.when(i == 0)
    def _():
        zbuf[...] = jnp.zeros(zbuf.shape, zbuf.dtype)

        def start(e, carry):
            @pl.when(pend_ref[e] > 0)
            def _():
                zero_copy(e).start()
            return carry

        def wait(e, carry):
            @pl.when(pend_ref[e] > 0)
            def _():
                zero_copy(e).wait()
            return carry

        def start_tail(b, carry):
            zero_block(b * tb).start()
            return carry

        def wait_tail(b, carry):
            zero_block(b * tb).wait()
            return carry

        first_unused = pend_ref[n_experts - 1] // tb
        lax.fori_loop(0, n_experts, start, 0)
        lax.fori_loop(first_unused, n_blocks, start_tail, 0)
        lax.fori_loop(0, n_experts, wait, 0)
        lax.fori_loop(first_unused, n_blocks, wait_tail, 0)

    def row_copy(r, choice):
        slot = dest_ref[2 * (i * tm + r) + choice]
        return pltpu.make_async_copy(h_ref.at[pl.ds(r, 1), :], xs_hbm.at[pl.ds(slot, 1), :], sem)

    def start_row(r, carry):
        row_copy(r, 0).start()
        row_copy(r, 1).start()
        return carry

    def wait_row(r, carry):
        row_copy(r, 0).wait()
        row_copy(r, 1).wait()
        return carry

    lax.fori_loop(0, tm, start_row, 0, unroll=DMA_UNROLL)
    lax.fori_loop(0, tm, wait_row, 0, unroll=DMA_UNROLL)


def _scatter(h2, dest, pend, n_slots, tb):
    T, D = h2.shape
    tm = _tile(T, 256)
    grid_spec = pltpu.PrefetchScalarGridSpec(
        num_scalar_prefetch=2,
        grid=(T // tm,),
        in_specs=[pl.BlockSpec((tm, D), lambda i, de, pe: (i, 0))],
        out_specs=pl.BlockSpec(memory_space=pl.ANY),
        scratch_shapes=[pltpu.VMEM((tb, D), h2.dtype), pltpu.SemaphoreType.DMA(()),
                        pltpu.SemaphoreType.DMA(())],
    )
    return pl.pallas_call(
        functools.partial(_scatter_kernel, tm=tm, tb=tb, n_experts=pend.shape[0],
                          n_blocks=n_slots // tb),
        grid_spec=grid_spec,
        out_shape=jax.ShapeDtypeStruct((n_slots, D), h2.dtype),
        compiler_params=_cparams("arbitrary"),
        name="scatter",
    )(dest, pend, h2)


def _moe_kernel(be_ref, first_ref, nused_ref, x_ref, wg_ref, wu_ref, wd_ref, y_ref,
                cg_sc, cu_sc, cd_sc, acc_sc, *, nk):
    i = pl.program_id(0)
    k = pl.program_id(1)
    nused = nused_ref[0]
    live = i < nused
    first = first_ref[jnp.minimum(i, nused - 1)] == 1
    half = x_ref.shape[1]
    dk = wd_ref.shape[0]

    def x_halves():
        x_lo, x_hi = _unpack_halves(x_ref[...])
        return x_lo.astype(BF16), x_hi.astype(BF16)

    def expert(cols):
        x_lo, x_hi = x_halves()
        g = (jnp.dot(x_lo, cg_sc[:half, cols], preferred_element_type=F32)
             + jnp.dot(x_hi, cg_sc[half:, cols], preferred_element_type=F32))
        u = (jnp.dot(x_lo, cu_sc[:half, cols], preferred_element_type=F32)
             + jnp.dot(x_hi, cu_sc[half:, cols], preferred_element_type=F32))
        act = (g * jax.nn.sigmoid(g) * u).astype(BF16)
        return jnp.dot(act, cd_sc[cols, :], preferred_element_type=F32)

    for s in range(nk):
        @pl.when(jnp.logical_and(jnp.logical_and(live, first), k == s))
        def _():
            cols = slice(s * dk, (s + 1) * dk)
            cg_sc[:, cols] = wg_ref[...].astype(BF16)
            cu_sc[:, cols] = wu_ref[...].astype(BF16)
            cd_sc[cols, :] = wd_ref[...].astype(BF16)
            part = expert(cols)
            if s == 0:
                acc_sc[...] = part
            elif s < nk - 1:
                acc_sc[...] += part
            else:
                y_ref[...] = _pack_halves(acc_sc[...] + part)

    @pl.when(jnp.logical_and(live, jnp.logical_and(jnp.logical_not(first), k == nk - 1)))
    def _():
        y_ref[...] = _pack_halves(expert(slice(None)))

    @pl.when(jnp.logical_and(jnp.logical_not(live), k == 0))
    def _():
        y_ref[...] = jnp.zeros(y_ref.shape, y_ref.dtype)


def _moe(xs, block_e, first, nused, wgu, wd, tb):
    n_slots, hw = xs.shape
    E, D, de2 = wgu.shape
    de = de2 // 2
    nk = MOE_WEIGHT_SLICES
    dk = de // nk
    n_blocks = n_slots // tb

    def blk(i, nu):
        return jnp.minimum(i, nu[0] - 1)

    def piece(i, k, fi, nu):
        streaming = jnp.logical_and(i < nu[0], fi[blk(i, nu)] == 1)
        return jnp.where(streaming, k, nk - 1)

    grid_spec = pltpu.PrefetchScalarGridSpec(
        num_scalar_prefetch=3,
        grid=(n_blocks, nk),
        in_specs=[pl.BlockSpec((tb, hw), lambda i, k, be, fi, nu: (blk(i, nu), 0)),
                  pl.BlockSpec((None, D, dk), lambda i, k, be, fi, nu: (be[blk(i, nu)], 0, piece(i, k, fi, nu))),
                  pl.BlockSpec((None, D, dk),
                               lambda i, k, be, fi, nu: (be[blk(i, nu)], 0, nk + piece(i, k, fi, nu))),
                  pl.BlockSpec((None, dk, D), lambda i, k, be, fi, nu: (be[blk(i, nu)], piece(i, k, fi, nu), 0))],
        out_specs=pl.BlockSpec((tb, hw), lambda i, k, be, fi, nu: (i, 0)),
        scratch_shapes=[pltpu.VMEM((D, de), BF16), pltpu.VMEM((D, de), BF16), pltpu.VMEM((de, D), BF16),
                        pltpu.VMEM((tb, D), F32)],
    )
    return pl.pallas_call(
        functools.partial(_moe_kernel, nk=nk),
        grid_spec=grid_spec,
        out_shape=jax.ShapeDtypeStruct((n_slots, hw), jnp.uint32),
        compiler_params=_cparams("arbitrary", "arbitrary"),
        name="moe",
    )(block_e, first, nused, xs, wgu, wgu, wd)


def _final_kernel(pos_ref, y_hbm, x_ref, r_ref, mod_ref, g_ref, o_ref, ybuf, sem, *, tm, nsteps):
    i = pl.program_id(0)
    slot = lax.rem(i, 2)

    @pl.when(i == 0)
    def _():
        _gather_rows(pos_ref, 0, y_hbm, ybuf.at[0], sem.at[0], 2 * tm)

    @pl.when(i + 1 < nsteps)
    def _():
        _gather_rows(pos_ref, (i + 1) * 2 * tm, y_hbm, ybuf.at[1 - slot], sem.at[1 - slot], 2 * tm)

    _wait_rows(y_hbm, ybuf.at[slot], sem.at[slot], 2 * tm)
    y_lo, y_hi = _unpack_halves(ybuf[slot])
    r = r_ref[...]
    lane = lax.broadcasted_iota(jnp.int32, r.shape, 1)
    gate1 = jnp.sum(jnp.where(lane == 2, r, 0.0), axis=1, keepdims=True)
    gate2 = jnp.sum(jnp.where(lane == 3, r, 0.0), axis=1, keepdims=True)
    half = y_lo.shape[1]
    x_lo = x_ref[:, :half] + mod_ref[5:6, :half] * (gate1 * y_lo[:tm] + gate2 * y_lo[tm:])
    x_hi = x_ref[:, half:] + mod_ref[5:6, half:] * (gate1 * y_hi[:tm] + gate2 * y_hi[tm:])
    ms = (jnp.sum(x_lo * x_lo, axis=-1, keepdims=True)
          + jnp.sum(x_hi * x_hi, axis=-1, keepdims=True)) / (2 * half)
    inv = lax.rsqrt(ms + EPS)
    o_ref[:, :half] = x_lo * inv * g_ref[:, :half]
    o_ref[:, half:] = x_hi * inv * g_ref[:, half:]


def _final(y_slots, dest, route, x1, mod, g, S):
    T, D = x1.shape
    tm = _tile(S, 128)
    per_b = S // tm
    nsteps = T // tm
    pos_tiled = jnp.transpose(dest.reshape(nsteps, tm, 2), (0, 2, 1)).reshape(-1)
    grid_spec = pltpu.PrefetchScalarGridSpec(
        num_scalar_prefetch=1,
        grid=(nsteps,),
        in_specs=[pl.BlockSpec(memory_space=pl.ANY),
                  pl.BlockSpec((tm, D), lambda i, p: (i, 0)),
                  pl.BlockSpec((tm, LANES), lambda i, p: (i, 0)),
                  pl.BlockSpec((None, 6, D), lambda i, p: (i // per_b, 0, 0)),
                  pl.BlockSpec((1, D), lambda i, p: (0, 0))],
        out_specs=pl.BlockSpec((tm, D), lambda i, p: (i, 0)),
        scratch_shapes=[pltpu.VMEM((2, 2 * tm, D // 2), y_slots.dtype), pltpu.SemaphoreType.DMA((2,))],
    )
    return pl.pallas_call(
        functools.partial(_final_kernel, tm=tm, nsteps=nsteps),
        grid_spec=grid_spec,
        out_shape=jax.ShapeDtypeStruct((T, D), F32),
        compiler_params=_cparams("arbitrary"),
        name="final",
    )(pos_tiled, y_slots, x1, route, mod, g.reshape(1, D))


def _dispatch(route, counts, n_experts, tb):
    T = route.shape[0]
    ids = route[:, 0:2].astype(jnp.int32)
    rank = route[:, 4:6].astype(jnp.int32)
    counts = counts[0, :n_experts].astype(jnp.int32)
    padded = (counts + tb - 1) // tb * tb
    pend = jnp.cumsum(padded).astype(jnp.int32)
    pstart = pend - padded
    onehot = ids[:, :, None] == jnp.arange(n_experts, dtype=jnp.int32)
    dest = (rank + jnp.sum(jnp.where(onehot, pstart, 0), axis=-1)).reshape(-1)
    n_blocks = -(-2 * T // tb) + n_experts
    first_row = jnp.arange(n_blocks, dtype=jnp.int32) * tb
    block_e = jnp.minimum(jnp.sum(pend[None, :] <= first_row[:, None], axis=1), n_experts - 1)
    nused = (pend[-1] // tb).reshape(1)
    block_e = block_e.astype(jnp.int32)
    first = jnp.concatenate([jnp.ones((1,), jnp.int32), (block_e[1:] != block_e[:-1]).astype(jnp.int32)])
    return dest, pend, block_e, first, nused, n_blocks * tb


def _layer(x2, c, lidx, B, S, w_ada, b_ada, norm1_g, w_in, conv_qk, b_if, da_lambda, da_norm_g, rel_bias,
           ml_norm_g, w_o_attn, w_o_mlstm, w_out, norm2_g, w_group, b_group, w_expert, b_expert,
           w_gate_up, w_down):
    D = x2.shape[1]
    H = rel_bias.shape[1]
    d = da_lambda.shape[1]
    da_w = H * 2 * d
    MH, dh = ml_norm_g.shape
    ml_w = MH * dh
    n_experts = w_expert.shape[1]

    mod = _ada(c, w_ada, b_ada)

    off_mlq = 3 * da_w
    off_mlv = off_mlq + 2 * ml_w
    off_if = off_mlv + 2 * ml_w
    off_g = off_if + 2 * MH
    w_in_t = w_in.T
    h, gates_if = _norm1(x2, mod, norm1_g, w_in_t[off_if:off_g], b_if, B, S)
    q_scale = jnp.concatenate([jnp.full((da_w,), d ** -0.5 * math.log2(math.e), F32),
                               jnp.ones((2 * da_w,), F32)])
    da_qkv = _proj(h, w_in_t, 0, 3 * da_w, BF16, "proj_da", col_scale=q_scale)
    ml_qk = _proj(h, w_in_t, off_mlq, 2 * ml_w, F32, "proj_mlqk")
    ml_vo = _proj(h, w_in_t, off_mlv, 2 * ml_w, BF16, "proj_mlvo")
    gates_am = _proj(h, w_in_t, off_g, 2 * D, BF16, "proj_gates")

    lambda_init = 0.8 - 0.6 * math.exp(-0.3 * lidx)
    lamp = da_lambda.astype(F32)
    lam = jnp.exp(jnp.sum(lamp[0] * lamp[1])) - jnp.exp(jnp.sum(lamp[2] * lamp[3])) + lambda_init
    attn = _attn(da_qkv, lam, rel_bias, da_norm_g, B, S, H, d, lambda_init)

    hm = _mlstm(ml_qk, conv_qk, ml_vo, gates_if, ml_norm_g, B, S, MH, dh)

    merged = _merge(attn, hm, w_o_attn, w_o_mlstm, gates_am)
    x1 = _outproj(merged, w_out, x2, mod, S)

    h2, route, counts = _router(x1, mod, norm2_g, w_group, b_group, w_expert, b_expert, B, S)
    tb = 256
    dest, pend, block_e, first, nused, n_slots = _dispatch(route, counts, n_experts, tb)
    xs = _scatter(h2, dest, pend, n_slots, tb)
    y_slots = _moe(xs, block_e, first, nused, w_gate_up, w_down, tb)
    return y_slots, dest, route, x1, mod


def kernel(x, c, w_ada, b_ada, norm1_g, w_in, conv_qk, b_if, da_lambda, da_norm_g, rel_bias, ml_norm_g,
           w_o_attn, w_o_mlstm, w_out, norm2_g, w_group, b_group, w_expert, b_expert, w_gate_up, w_down,
           normf_g):
    B, S, D = x.shape
    assert w_ada.shape[0] == 1, "the final rmsnorm is fused into the layer's last kernel: one layer only"
    l = 0
    y_slots, dest, route, x1, mod = _layer(
        x.reshape(B * S, D), c, l, B, S, w_ada[l], b_ada[l], norm1_g[l], w_in[l], conv_qk[l], b_if[l],
        da_lambda[l], da_norm_g[l], rel_bias, ml_norm_g[l], w_o_attn[l], w_o_mlstm[l], w_out[l],
        norm2_g[l], w_group[l], b_group[l], w_expert[l], b_expert[l], w_gate_up[l], w_down[l])
    return _final(y_slots, dest, route, x1, mod, normf_g, S).reshape(B, S, D)
```

```python
import functools
import math

import jax
import jax.numpy as jnp
from jax import lax
from jax.experimental import pallas as pl
from jax.experimental.pallas import tpu as pltpu

F32 = jnp.float32
BF16 = jnp.bfloat16
EPS = 1e-6
REL_MAX_DIST = 128
ML_CHUNK = 128
NEG_BIG = -1e30
V7X_VMEM_LIMIT = 60 * 1024 * 1024
LANES = 128
SUBLANES = 8
HALO = SUBLANES
DMA_UNROLL = 8
MOE_WEIGHT_SLICES = 2


def _cparams(*sem):
    return pltpu.CompilerParams(dimension_semantics=sem, vmem_limit_bytes=V7X_VMEM_LIMIT)


def _tile(n, pref):
    t = min(n, pref)
    while n % t:
        t //= 2
    return t


def _nt_dot(a, b):
    return lax.dot_general(a, b, (((1,), (1,)), ((), ())), preferred_element_type=F32)


def _tn_dot(a, b):
    return lax.dot_general(a, b, (((0,), (0,)), ((), ())), preferred_element_type=F32)


def _pack_halves(x):
    n = x.shape[1] // 2
    bits = pltpu.bitcast(x.astype(BF16).astype(F32), jnp.uint32)
    return bits[:, n:] | (bits[:, :n] >> 16)


def _unpack_halves(w):
    lo = pltpu.bitcast(w << 16, F32)
    hi = pltpu.bitcast(w & jnp.uint32(0xFFFF0000), F32)
    return lo, hi


def _ada_kernel(c_ref, w_ref, b_ref, o_ref):
    c = c_ref[...]
    s = (c * jax.nn.sigmoid(c)).astype(BF16)
    o_ref[...] = jnp.dot(s, w_ref[...].astype(BF16), preferred_element_type=F32) + b_ref[...]


def _ada(c, w_ada, b_ada):
    B, D = c.shape
    N = w_ada.shape[1]
    rows = SUBLANES * pl.cdiv(B, SUBLANES)
    c8 = jnp.zeros((rows, D), F32).at[:B].set(c)
    tn = _tile(N, 512)
    mod = pl.pallas_call(
        _ada_kernel,
        grid=(N // tn,),
        in_specs=[pl.BlockSpec((rows, D), lambda j: (0, 0)),
                  pl.BlockSpec((D, tn), lambda j: (0, j)),
                  pl.BlockSpec((1, tn), lambda j: (0, j))],
        out_specs=pl.BlockSpec((rows, tn), lambda j: (0, j)),
        out_shape=jax.ShapeDtypeStruct((rows, N), F32),
        compiler_params=_cparams("arbitrary"),
        name="ada",
    )(c8, w_ada, b_ada.reshape(1, N))
    return mod[:B].reshape(B, 6, D)


def _norm1_kernel(x_ref, mod_ref, g_ref, wif_ref, bif_ref, h_ref, gate_ref):
    x = x_ref[...]
    y = x * lax.rsqrt(jnp.mean(x * x, axis=-1, keepdims=True) + EPS) * g_ref[...]
    h = (y * (1.0 + mod_ref[1:2, :]) + mod_ref[0:1, :]).astype(BF16)
    h_ref[...] = h
    gate_ref[...] = _nt_dot(h, wif_ref[...].astype(BF16)) + bif_ref[...]


def _norm1(x2, mod, g, w_if_t, b_if, B, S):
    T, D = x2.shape
    G = w_if_t.shape[0]
    tm = _tile(S, 512)
    nb = S // tm
    return pl.pallas_call(
        _norm1_kernel,
        grid=(B, nb),
        in_specs=[pl.BlockSpec((tm, D), lambda b, i: (b * nb + i, 0)),
                  pl.BlockSpec((None, 6, D), lambda b, i: (b, 0, 0)),
                  pl.BlockSpec((1, D), lambda b, i: (0, 0)),
                  pl.BlockSpec((G, D), lambda b, i: (0, 0)),
                  pl.BlockSpec((1, G), lambda b, i: (0, 0))],
        out_specs=[pl.BlockSpec((tm, D), lambda b, i: (b * nb + i, 0)),
                   pl.BlockSpec((tm, G), lambda b, i: (b * nb + i, 0))],
        out_shape=[jax.ShapeDtypeStruct((T, D), BF16), jax.ShapeDtypeStruct((T, G), F32)],
        compiler_params=_cparams("arbitrary", "arbitrary"),
        name="norm1",
    )(x2, mod, g.reshape(1, D), w_if_t, b_if.reshape(1, G))


def _proj_kernel(h_ref, w_ref, o_ref):
    o_ref[...] = _nt_dot(h_ref[...], w_ref[...].astype(BF16)).astype(o_ref.dtype)


def _proj_scaled_kernel(h_ref, w_ref, s_ref, o_ref):
    acc = _nt_dot(h_ref[...], w_ref[...].astype(BF16))
    o_ref[...] = (acc * s_ref[...]).astype(o_ref.dtype)


def _proj(h, w_t, col0, ncols, out_dtype, name, col_scale=None):
    M, K = h.shape
    tm = _tile(M, 1024)
    tn = _tile(ncols, 512)
    assert col0 % SUBLANES == 0
    in_specs = [pl.BlockSpec((tm, K), lambda i, j: (i, 0)),
                pl.BlockSpec((pl.Element(tn), pl.Element(K)),
                             lambda i, j: (pl.multiple_of(col0 + j * tn, SUBLANES), 0))]
    args = (h, w_t)
    body = _proj_kernel
    if col_scale is not None:
        in_specs.append(pl.BlockSpec((1, tn), lambda i, j: (0, j)))
        args = (h, w_t, col_scale.reshape(1, ncols))
        body = _proj_scaled_kernel
    return pl.pallas_call(
        body,
        grid=(M // tm, ncols // tn),
        in_specs=in_specs,
        out_specs=pl.BlockSpec((tm, tn), lambda i, j: (i, j)),
        out_shape=jax.ShapeDtypeStruct((M, ncols), out_dtype),
        compiler_params=_cparams("arbitrary", "arbitrary"),
        name=name,
    )(*args)


def _attn_kernel(lam_ref, q_ref, k_ref, v_ref, bd_ref, bp_ref, g_ref, o_ref,
                 m_sc, l_sc, acc_sc, s_sc, p_sc, a_sc, *, d, t, out_scale):
    i = pl.program_id(2)
    q = q_ref[...]
    ngrp = t // LANES
    m_sc[...] = jnp.full(m_sc.shape, NEG_BIG, F32)
    l_sc[...] = jnp.zeros(l_sc.shape, F32)
    acc_sc[...] = jnp.zeros(acc_sc.shape, F32)
    spare = pl.ds(1 - lax.rem(i, 2), 1)
    p_sc[spare] = jnp.zeros((1,) + p_sc.shape[1:], BF16)
    a_sc[spare] = jnp.ones((1,) + a_sc.shape[1:], F32)

    def rows(j):
        return pl.ds(pl.multiple_of(j * t, t), t)

    def scores(j):
        k = k_ref[rows(j), :]
        for m in range(2):
            s_sc[m] = _nt_dot(q[:, m * d:(m + 1) * d], k[:, m * d:(m + 1) * d])

    def accumulate(j, buf):
        v = v_ref[rows(j), :]
        for m in range(2):
            alpha = a_sc[buf, m]
            pv = jnp.dot(p_sc[buf, m], v, preferred_element_type=F32)
            acc_sc[m] = jnp.concatenate([alpha] * (2 * d // LANES), axis=1) * acc_sc[m] + pv

    def softmax(bias_ref, buf):
        for m in range(2):
            s = s_sc[m]
            if bias_ref is not None:
                s = s + bias_ref[...]
            grp = [s[:, g * LANES:(g + 1) * LANES] for g in range(ngrp)]
            m_old = m_sc[m]
            row_max = jnp.max(functools.reduce(jnp.maximum, grp), axis=1, keepdims=True)
            m_new = jnp.maximum(m_old, row_max)
            alpha = jnp.exp2(m_old - m_new)
            p = [jnp.exp2(x - m_new) for x in grp]
            l_sc[m] = alpha * l_sc[m] + functools.reduce(jnp.add, p)
            p_sc[buf, m] = jnp.concatenate([x.astype(BF16) for x in p], axis=1)
            a_sc[buf, m] = alpha
            m_sc[m] = m_new

    def stage(j, bias_ref, buf, last=False):
        accumulate(jnp.maximum(j - 1, 0), 1 - buf)
        softmax(bias_ref, buf)
        if not last:
            scores(j + 1)

    n_far = jnp.maximum(i - 1, 0)
    odd = lax.rem(n_far, 2)

    def far_pair(jj, carry):
        j = odd + 2 * jj
        stage(j, None, 1)
        stage(j + 1, None, 0)
        return carry

    scores(0)

    @pl.when(odd == 1)
    def _():
        stage(0, None, 0)

    lax.fori_loop(0, n_far // 2, far_pair, 0)

    @pl.when(i > 0)
    def _():
        stage(i - 1, bp_ref, 1)

    stage(i, bd_ref, 0, last=True)
    accumulate(i, 0)

    l0 = jnp.sum(l_sc[0], axis=1, keepdims=True)
    l1 = jnp.sum(l_sc[1], axis=1, keepdims=True)
    a = acc_sc[0] / l0 - lam_ref[0] * (acc_sc[1] / l1)
    y = a * lax.rsqrt(jnp.mean(a * a, axis=-1, keepdims=True) + EPS) * g_ref[...]
    o_ref[...] = (y * out_scale).astype(o_ref.dtype)


def _rel_bucket(n, n_buckets):
    max_exact = n_buckets // 2
    nf = jnp.maximum(n, 1).astype(F32)
    large = max_exact + (jnp.log(nf / max_exact) / math.log(REL_MAX_DIST / max_exact)
                         * (n_buckets - max_exact)).astype(jnp.int32)
    large = jnp.minimum(large, n_buckets - 1)
    return jnp.where(n < max_exact, n, large)


def _bias_table(rel, dist, n_buckets):
    bucket = _rel_bucket(dist, n_buckets)
    out = jnp.zeros((rel.shape[1],) + dist.shape, F32)
    for b in range(n_buckets):
        out = jnp.where((bucket == b)[None], rel[b][:, None, None], out)
    return out


def _attn(qkv, lam, rel_bias, norm_g, B, S, H, d, lambda_init):
    T = qkv.shape[0]
    t = _tile(S, 512)
    assert t >= REL_MAX_DIST, "key blocks two or more tiles back must sit in the saturated bucket"
    assert d % LANES == 0
    nq = S // t
    n_buckets = rel_bias.shape[0]
    rel = rel_bias.astype(F32)
    rel = (rel - rel[n_buckets - 1]) * math.log2(math.e)
    dist_d = jnp.arange(t)[:, None] - jnp.arange(t)[None, :]
    bd = jnp.where((dist_d >= 0)[None], _bias_table(rel, jnp.maximum(dist_d, 0), n_buckets), NEG_BIG)
    bp = _bias_table(rel, dist_d + t, n_buckets)
    kern = functools.partial(_attn_kernel, d=d, t=t, out_scale=1.0 - lambda_init)
    return pl.pallas_call(
        kern,
        grid=(B, H, nq),
        in_specs=[pl.BlockSpec(memory_space=pltpu.SMEM),
                  pl.BlockSpec((t, 2 * d), lambda b, h, i: (b * nq + i, h)),
                  pl.BlockSpec((S, 2 * d), lambda b, h, i: (b, H + h)),
                  pl.BlockSpec((S, 2 * d), lambda b, h, i: (b, 2 * H + h)),
                  pl.BlockSpec((None, t, t), lambda b, h, i: (h, 0, 0)),
                  pl.BlockSpec((None, t, t), lambda b, h, i: (h, 0, 0)),
                  pl.BlockSpec((1, 2 * d), lambda b, h, i: (0, 0))],
        out_specs=pl.BlockSpec((t, 2 * d), lambda b, h, i: (b * nq + i, h)),
        out_shape=jax.ShapeDtypeStruct((T, H * 2 * d), BF16),
        scratch_shapes=[pltpu.VMEM((2, t, LANES), F32), pltpu.VMEM((2, t, LANES), F32),
                        pltpu.VMEM((2, t, 2 * d), F32), pltpu.VMEM((2, t, t), F32),
                        pltpu.VMEM((2, 2, t, t), BF16), pltpu.VMEM((2, 2, t, LANES), F32)],
        compiler_params=_cparams("arbitrary", "arbitrary", "arbitrary"),
        name="attn",
    )(lam.reshape(1), qkv, qkv, qkv, bd, bp, norm_g.reshape(1, 2 * d))


def _causal_conv_silu(u_ref, w_ref, ext_ref, scale):
    L = u_ref.shape[0]
    width = w_ref.shape[0]
    assert width - 1 <= HALO
    u = u_ref[...]
    ext_ref[HALO:, :] = u
    acc = u * w_ref[width - 1:width, :]
    for t in range(1, width):
        acc = acc + ext_ref[HALO - t:HALO - t + L, :] * w_ref[width - 1 - t:width - t, :]
    ext_ref[0:HALO, :] = u[L - HALO:, :]
    return (acc * jax.nn.sigmoid(acc) * scale).astype(BF16)


def _mlstm_kernel(q_ref, k_ref, wq_ref, wk_ref, v_ref, og_ref, gcol_ref, grow_ref, g_ref, o_ref,
                  c_sc, n_sc, m_sc, extq_sc, extk_sc, *, nheads, group, dh):
    L = q_ref.shape[0]

    @pl.when(pl.program_id(2) == 0)
    def _():
        c_sc[...] = jnp.zeros(c_sc.shape, F32)
        n_sc[...] = jnp.zeros(n_sc.shape, F32)
        m_sc[...] = jnp.zeros(m_sc.shape, F32)
        extq_sc[0:HALO, :] = jnp.zeros((HALO, extq_sc.shape[1]), F32)
        extk_sc[0:HALO, :] = jnp.zeros((HALO, extk_sc.shape[1]), F32)

    q_all = _causal_conv_silu(q_ref, wq_ref, extq_sc, 1.0)
    k_all = _causal_conv_silu(k_ref, wk_ref, extk_sc, dh ** -0.5)

    gcol = gcol_ref[...]
    lane = lax.broadcasted_iota(jnp.int32, gcol.shape, 1)
    grow = grow_ref[...]
    sub = lax.broadcasted_iota(jnp.int32, grow.shape, 0)
    jj = lax.broadcasted_iota(jnp.int32, (L, L), 0)
    ss = lax.broadcasted_iota(jnp.int32, (L, L), 1)
    tril = ss <= jj

    for hx in range(group):
        h = pl.program_id(1) * group + hx
        cols = slice(hx * dh, (hx + 1) * dh)
        i_col = jnp.sum(jnp.where(lane == h, gcol, 0.0), axis=1, keepdims=True)
        f_col = jax.nn.log_sigmoid(jnp.sum(jnp.where(lane == h + nheads, gcol, 0.0), axis=1, keepdims=True))
        i_row = jnp.sum(jnp.where(sub == h, grow, 0.0), axis=0, keepdims=True)
        f_row = jax.nn.log_sigmoid(jnp.sum(jnp.where(sub == h + nheads, grow, 0.0), axis=0, keepdims=True))

        b_col = jnp.sum(jnp.where(tril, f_row, 0.0), axis=1, keepdims=True)
        b_row = jnp.sum(jnp.where(jj <= ss, f_col, 0.0), axis=0, keepdims=True)
        u_row = i_row - b_row
        u_col = i_col - b_col

        m_prev = m_sc[hx, 0:1, 0:1]
        mm_col = jnp.maximum(m_prev, jnp.max(jnp.where(tril, u_row, NEG_BIG), axis=1, keepdims=True))
        w = jnp.exp(jnp.where(tril, u_row - mm_col, NEG_BIG))
        a_inter = jnp.exp(m_prev - mm_col)

        q = q_all[:, cols]
        k = k_all[:, cols]
        v = v_ref[:, cols]
        c_old = c_sc[hx]
        n_old = n_sc[hx]
        s_qk = _nt_dot(q, k) * w
        num = (a_inter * _nt_dot(q, c_old.astype(BF16))
               + jnp.dot(s_qk.astype(BF16), v, preferred_element_type=F32))
        den = (a_inter * jnp.sum(q.astype(F32) * n_old, axis=1, keepdims=True)
               + jnp.sum(s_qk, axis=1, keepdims=True))
        hh = num / jnp.maximum(jnp.abs(den), jnp.exp(-(b_col + mm_col)))

        mm_last = mm_col[L - 1:L, :]
        w_s = jnp.exp(u_col - mm_last)
        a_state = jnp.exp(m_prev - mm_last)
        c_sc[hx] = a_state * c_old + _tn_dot((v.astype(F32) * w_s).astype(BF16), k)
        n_sc[hx] = a_state * n_old + jnp.sum(k.astype(F32) * w_s, axis=0, keepdims=True)
        m_sc[hx] = jnp.broadcast_to(b_col[L - 1:L, :] + mm_last, m_sc.shape[1:])

        y = hh * lax.rsqrt(jnp.mean(hh * hh, axis=-1, keepdims=True) + EPS) * g_ref[hx]
        o_ref[:, cols] = (y * jax.nn.sigmoid(og_ref[:, cols].astype(F32))).astype(o_ref.dtype)


def _mlstm(mqk, conv_w, mvo, gates, norm_g, B, S, H, dh):
    T = mqk.shape[0]
    width = conv_w.shape[0]
    L = ML_CHUNK
    nc = S // L
    group = _tile(H, 4)
    ng = H // group
    gw = group * dh
    grow = jnp.transpose(gates.reshape(B, S, 2 * H), (0, 2, 1))
    row = lambda b, h, c: (b * nc + c, h)
    row_hi = lambda b, h, c: (b * nc + c, ng + h)
    return pl.pallas_call(
        functools.partial(_mlstm_kernel, nheads=H, group=group, dh=dh),
        grid=(B, ng, nc),
        in_specs=[pl.BlockSpec((L, gw), row),
                  pl.BlockSpec((L, gw), row_hi),
                  pl.BlockSpec((width, gw), lambda b, h, c: (0, h)),
                  pl.BlockSpec((width, gw), lambda b, h, c: (0, ng + h)),
                  pl.BlockSpec((L, gw), row),
                  pl.BlockSpec((L, gw), row_hi),
                  pl.BlockSpec((L, 2 * H), lambda b, h, c: (b * nc + c, 0)),
                  pl.BlockSpec((None, 2 * H, L), lambda b, h, c: (b, 0, c)),
                  pl.BlockSpec((group, 1, dh), lambda b, h, c: (h, 0, 0))],
        out_specs=pl.BlockSpec((L, gw), row),
        out_shape=jax.ShapeDtypeStruct((T, H * dh), BF16),
        scratch_shapes=[pltpu.VMEM((group, dh, dh), F32), pltpu.VMEM((group, 1, dh), F32),
                        pltpu.VMEM((group, SUBLANES, LANES), F32), pltpu.VMEM((L + HALO, gw), F32),
                        pltpu.VMEM((L + HALO, gw), F32)],
        compiler_params=_cparams("arbitrary", "arbitrary", "arbitrary"),
        name="mlstm",
    )(mqk, mqk, conv_w, conv_w, mvo, mvo, gates, grow, norm_g.reshape(H, 1, dh))


def _merge_kernel(a_ref, m_ref, wa_ref, wm_ref, ga_ref, gm_ref, o_ref):
    pa = jnp.dot(a_ref[...], wa_ref[...].astype(BF16), preferred_element_type=F32)
    pm = jnp.dot(m_ref[...], wm_ref[...].astype(BF16), preferred_element_type=F32)
    out = (jax.nn.sigmoid(ga_ref[...].astype(F32)) * pa + jax.nn.sigmoid(gm_ref[...].astype(F32)) * pm)
    o_ref[...] = out.astype(o_ref.dtype)


def _merge(attn, hm, w_a, w_m, gates_am):
    T, Ka = attn.shape
    Km = hm.shape[1]
    D = w_a.shape[1]
    tm = _tile(T, 1024)
    tn = _tile(D, 512)
    nj = D // tn
    return pl.pallas_call(
        _merge_kernel,
        grid=(T // tm, nj),
        in_specs=[pl.BlockSpec((tm, Ka), lambda i, j: (i, 0)),
                  pl.BlockSpec((tm, Km), lambda i, j: (i, 0)),
                  pl.BlockSpec((Ka, tn), lambda i, j: (0, j)),
                  pl.BlockSpec((Km, tn), lambda i, j: (0, j)),
                  pl.BlockSpec((tm, tn), lambda i, j: (i, j)),
                  pl.BlockSpec((tm, tn), lambda i, j: (i, nj + j))],
        out_specs=pl.BlockSpec((tm, tn), lambda i, j: (i, j)),
        out_shape=jax.ShapeDtypeStruct((T, D), BF16),
        compiler_params=_cparams("arbitrary", "arbitrary"),
        name="merge",
    )(attn, hm, w_a, w_m, gates_am, gates_am)


def _out_kernel(a_ref, w_ref, x_ref, mod_ref, o_ref):
    p = jnp.dot(a_ref[...], w_ref[...].astype(BF16), preferred_element_type=F32)
    o_ref[...] = x_ref[...] + mod_ref[2:3, :] * p


def _outproj(merged, w_out, x2, mod, S):
    T, K = merged.shape
    D = w_out.shape[1]
    tm = _tile(S, 1024)
    tn = _tile(D, 512)
    per_b = S // tm
    return pl.pallas_call(
        _out_kernel,
        grid=(T // tm, D // tn),
        in_specs=[pl.BlockSpec((tm, K), lambda i, j: (i, 0)),
                  pl.BlockSpec((K, tn), lambda i, j: (0, j)),
                  pl.BlockSpec((tm, tn), lambda i, j: (i, j)),
                  pl.BlockSpec((None, 6, tn), lambda i, j: (i // per_b, 0, j))],
        out_specs=pl.BlockSpec((tm, tn), lambda i, j: (i, j)),
        out_shape=jax.ShapeDtypeStruct((T, D), F32),
        compiler_params=_cparams("arbitrary", "arbitrary"),
        name="outproj",
    )(merged, w_out, x2, mod)


def _router_kernel(x_ref, mod_ref, g_ref, whi_ref, wlo_ref, b_ref, h_ref, r_ref, cnt_ref, run_sc,
                   *, n_groups, per_group):
    @pl.when(jnp.logical_and(pl.program_id(0) == 0, pl.program_id(1) == 0))
    def _():
        run_sc[...] = jnp.zeros(run_sc.shape, F32)

    x = x_ref[...]
    y = x * lax.rsqrt(jnp.mean(x * x, axis=-1, keepdims=True) + EPS) * g_ref[...]
    h2 = y * (1.0 + mod_ref[4:5, :]) + mod_ref[3:4, :]
    h_ref[...] = _pack_halves(h2)
    hi = h2.astype(BF16)
    lo = (h2 - hi.astype(F32)).astype(BF16)
    logits = (jnp.dot(hi, whi_ref[...], preferred_element_type=F32)
              + jnp.dot(hi, wlo_ref[...], preferred_element_type=F32)
              + jnp.dot(lo, whi_ref[...], preferred_element_type=F32)) + b_ref[...]
    lane = lax.broadcasted_iota(jnp.int32, logits.shape, 1)
    big = jnp.int32(1 << 20)

    def top(vals):
        mx = jnp.max(vals, axis=1, keepdims=True)
        idx = jnp.min(jnp.where(vals == mx, lane, big), axis=1, keepdims=True)
        return mx, idx

    gl = jnp.where(lane < n_groups, logits, -jnp.inf)
    gmax, gidx = top(gl)
    g_p = 1.0 / jnp.sum(jnp.exp(gl - gmax), axis=1, keepdims=True)
    lo_lane = n_groups + per_group * gidx
    el = jnp.where((lane >= lo_lane) & (lane < lo_lane + per_group), logits, -jnp.inf)
    e1, i1 = top(el)
    e2, i2 = top(jnp.where(lane == i1, -jnp.inf, el))
    r = jnp.exp(e2 - e1)
    gate1 = g_p / (1.0 + r)
    gate2 = g_p * r / (1.0 + r)
    ex1 = i1 - n_groups
    ex2 = i2 - n_groups
    oh1 = (lane == ex1).astype(F32)
    oh2 = (lane == ex2).astype(F32)
    both = oh1 + oh2
    tm = x.shape[0]
    earlier = (lax.broadcasted_iota(jnp.int32, (tm, tm), 1)
               < lax.broadcasted_iota(jnp.int32, (tm, tm), 0)).astype(BF16)
    prefix = jnp.dot(earlier, both.astype(BF16), preferred_element_type=F32) + run_sc[0:1, :]
    rank1 = jnp.sum(oh1 * prefix, axis=1, keepdims=True)
    rank2 = jnp.sum(oh2 * prefix, axis=1, keepdims=True)
    run_sc[...] = run_sc[...] + jnp.sum(both, axis=0, keepdims=True)
    cnt_ref[...] = run_sc[...]

    cols = (ex1.astype(F32), ex2.astype(F32), gate1, gate2, rank1, rank2)
    out = jnp.zeros(logits.shape, F32)
    for n, col in enumerate(cols):
        out = jnp.where(lane == n, col, out)
    r_ref[...] = out


def _router(x1, mod, g, w_group, b_group, w_expert, b_expert, B, S):
    T, D = x1.shape
    NG = w_group.shape[1]
    NE = w_expert.shape[1]
    assert NG + NE <= LANES
    pad = LANES - NG - NE
    w = jnp.concatenate([w_group, w_expert, jnp.zeros((D, pad), F32)], axis=1)
    w_hi = w.astype(BF16)
    w_lo = (w - w_hi.astype(F32)).astype(BF16)
    bias = jnp.concatenate([b_group, b_expert, jnp.full((pad,), -jnp.inf, F32)]).reshape(1, LANES)
    tm = _tile(S, 256)
    nb = S // tm
    kern = functools.partial(_router_kernel, n_groups=NG, per_group=NE // NG)
    return pl.pallas_call(
        kern,
        grid=(B, nb),
        in_specs=[pl.BlockSpec((tm, D), lambda b, i: (b * nb + i, 0)),
                  pl.BlockSpec((None, 6, D), lambda b, i: (b, 0, 0)),
                  pl.BlockSpec((1, D), lambda b, i: (0, 0)),
                  pl.BlockSpec((D, LANES), lambda b, i: (0, 0)),
                  pl.BlockSpec((D, LANES), lambda b, i: (0, 0)),
                  pl.BlockSpec((1, LANES), lambda b, i: (0, 0))],
        out_specs=[pl.BlockSpec((tm, D // 2), lambda b, i: (b * nb + i, 0)),
                   pl.BlockSpec((tm, LANES), lambda b, i: (b * nb + i, 0)),
                   pl.BlockSpec((SUBLANES, LANES), lambda b, i: (0, 0))],
        out_shape=[jax.ShapeDtypeStruct((T, D // 2), jnp.uint32), jax.ShapeDtypeStruct((T, LANES), F32),
                   jax.ShapeDtypeStruct((SUBLANES, LANES), F32)],
        scratch_shapes=[pltpu.VMEM((SUBLANES, LANES), F32)],
        compiler_params=_cparams("arbitrary", "arbitrary"),
        name="router",
    )(x1, mod, g.reshape(1, D), w_hi, w_lo, bias)


def _gather_rows(idx_ref, base, src_hbm, dst_ref, sem, n):
    def body(r, carry):
        tok = idx_ref[base + r]
        pltpu.make_async_copy(src_hbm.at[pl.ds(tok, 1), :], dst_ref.at[pl.ds(r, 1), :], sem).start()
        return carry
    lax.fori_loop(0, n, body, 0, unroll=DMA_UNROLL)


def _wait_rows(src_hbm, dst_ref, sem, n):
    def body(r, carry):
        pltpu.make_async_copy(src_hbm.at[pl.ds(0, 1), :], dst_ref.at[pl.ds(r, 1), :], sem).wait()
        return carry
    lax.fori_loop(0, n, body, 0, unroll=DMA_UNROLL)


def _scatter_kernel(dest_ref, pend_ref, h_ref, xs_hbm, zbuf, sem, zsem, *, tm, tb, n_experts, n_blocks):
    i = pl.program_id(0)

    def zero_block(row0):
        return pltpu.make_async_copy(zbuf, xs_hbm.at[pl.ds(pl.multiple_of(row0, tb), tb), :], zsem)

    def zero_copy(e):
        return zero_block(jnp.maximum(pend_ref[e] - tb, 0))

    @pl.when(i == 0)
    def _():
        zbuf[...] = jnp.zeros(zbuf.shape, zbuf.dtype)

        def start(e, carry):
            @pl.when(pend_ref[e] > 0)
            def _():
                zero_copy(e).start()
            return carry

        def wait(e, carry):
            @pl.when(pend_ref[e] > 0)
            def _():
                zero_copy(e).wait()
            return carry

        def start_tail(b, carry):
            zero_block(b * tb).start()
            return carry

        def wait_tail(b, carry):
            zero_block(b * tb).wait()
            return carry

        first_unused = pend_ref[n_experts - 1] // tb
        lax.fori_loop(0, n_experts, start, 0)
        lax.fori_loop(first_unused, n_blocks, start_tail, 0)
        lax.fori_loop(0, n_experts, wait, 0)
        lax.fori_loop(first_unused, n_blocks, wait_tail, 0)

    def row_copy(r, choice):
        slot = dest_ref[2 * (i * tm + r) + choice]
        return pltpu.make_async_copy(h_ref.at[pl.ds(r, 1), :], xs_hbm.at[pl.ds(slot, 1), :], sem)

    def start_row(r, carry):
        row_copy(r, 0).start()
        row_copy(r, 1).start()
        return carry

    def wait_row(r, carry):
        row_copy(r, 0).wait()
        row_copy(r, 1).wait()
        return carry

    lax.fori_loop(0, tm, start_row, 0, unroll=DMA_UNROLL)
    lax.fori_loop(0, tm, wait_row, 0, unroll=DMA_UNROLL)


def _scatter(h2, dest, pend, n_slots, tb):
    T, D = h2.shape
    tm = _tile(T, 256)
    grid_spec = pltpu.PrefetchScalarGridSpec(
        num_scalar_prefetch=2,
        grid=(T // tm,),
        in_specs=[pl.BlockSpec((tm, D), lambda i, de, pe: (i, 0))],
        out_specs=pl.BlockSpec(memory_space=pl.ANY),
        scratch_shapes=[pltpu.VMEM((tb, D), h2.dtype), pltpu.SemaphoreType.DMA(()),
                        pltpu.SemaphoreType.DMA(())],
    )
    return pl.pallas_call(
        functools.partial(_scatter_kernel, tm=tm, tb=tb, n_experts=pend.shape[0],
                          n_blocks=n_slots // tb),
        grid_spec=grid_spec,
        out_shape=jax.ShapeDtypeStruct((n_slots, D), h2.dtype),
        compiler_params=_cparams("arbitrary"),
        name="scatter",
    )(dest, pend, h2)


def _moe_kernel(be_ref, first_ref, nused_ref, x_ref, wg_ref, wu_ref, wd_ref, y_ref,
                cg_sc, cu_sc, cd_sc, acc_sc, *, nk):
    i = pl.program_id(0)
    k = pl.program_id(1)
    nused = nused_ref[0]
    live = i < nused
    first = first_ref[jnp.minimum(i, nused - 1)] == 1
    half = x_ref.shape[1]
    dk = wd_ref.shape[0]

    def x_halves():
        x_lo, x_hi = _unpack_halves(x_ref[...])
        return x_lo.astype(BF16), x_hi.astype(BF16)

    def expert(cols):
        x_lo, x_hi = x_halves()
        g = (jnp.dot(x_lo, cg_sc[:half, cols], preferred_element_type=F32)
             + jnp.dot(x_hi, cg_sc[half:, cols], preferred_element_type=F32))
        u = (jnp.dot(x_lo, cu_sc[:half, cols], preferred_element_type=F32)
             + jnp.dot(x_hi, cu_sc[half:, cols], preferred_element_type=F32))
        act = (g * jax.nn.sigmoid(g) * u).astype(BF16)
        return jnp.dot(act, cd_sc[cols, :], preferred_element_type=F32)

    for s in range(nk):
        @pl.when(jnp.logical_and(jnp.logical_and(live, first), k == s))
        def _():
            cols = slice(s * dk, (s + 1) * dk)
            cg_sc[:, cols] = wg_ref[...].astype(BF16)
            cu_sc[:, cols] = wu_ref[...].astype(BF16)
            cd_sc[cols, :] = wd_ref[...].astype(BF16)
            part = expert(cols)
            if s == 0:
                acc_sc[...] = part
            elif s < nk - 1:
                acc_sc[...] += part
            else:
                y_ref[...] = _pack_halves(acc_sc[...] + part)

    @pl.when(jnp.logical_and(live, jnp.logical_and(jnp.logical_not(first), k == nk - 1)))
    def _():
        y_ref[...] = _pack_halves(expert(slice(None)))

    @pl.when(jnp.logical_and(jnp.logical_not(live), k == 0))
    def _():
        y_ref[...] = jnp.zeros(y_ref.shape, y_ref.dtype)


def _moe(xs, block_e, first, nused, wgu, wd, tb):
    n_slots, hw = xs.shape
    E, D, de2 = wgu.shape
    de = de2 // 2
    nk = MOE_WEIGHT_SLICES
    dk = de // nk
    n_blocks = n_slots // tb

    def blk(i, nu):
        return jnp.minimum(i, nu[0] - 1)

    def piece(i, k, fi, nu):
        streaming = jnp.logical_and(i < nu[0], fi[blk(i, nu)] == 1)
        return jnp.where(streaming, k, nk - 1)

    grid_spec = pltpu.PrefetchScalarGridSpec(
        num_scalar_prefetch=3,
        grid=(n_blocks, nk),
        in_specs=[pl.BlockSpec((tb, hw), lambda i, k, be, fi, nu: (blk(i, nu), 0)),
                  pl.BlockSpec((None, D, dk), lambda i, k, be, fi, nu: (be[blk(i, nu)], 0, piece(i, k, fi, nu))),
                  pl.BlockSpec((None, D, dk),
                               lambda i, k, be, fi, nu: (be[blk(i, nu)], 0, nk + piece(i, k, fi, nu))),
                  pl.BlockSpec((None, dk, D), lambda i, k, be, fi, nu: (be[blk(i, nu)], piece(i, k, fi, nu), 0))],
        out_specs=pl.BlockSpec((tb, hw), lambda i, k, be, fi, nu: (i, 0)),
        scratch_shapes=[pltpu.VMEM((D, de), BF16), pltpu.VMEM((D, de), BF16), pltpu.VMEM((de, D), BF16),
                        pltpu.VMEM((tb, D), F32)],
    )
    return pl.pallas_call(
        functools.partial(_moe_kernel, nk=nk),
        grid_spec=grid_spec,
        out_shape=jax.ShapeDtypeStruct((n_slots, hw), jnp.uint32),
        compiler_params=_cparams("arbitrary", "arbitrary"),
        name="moe",
    )(block_e, first, nused, xs, wgu, wgu, wd)


def _final_kernel(pos_ref, y_hbm, x_ref, r_ref, mod_ref, g_ref, o_ref, ybuf, sem, *, tm, nsteps):
    i = pl.program_id(0)
    slot = lax.rem(i, 2)

    @pl.when(i == 0)
    def _():
        _gather_rows(pos_ref, 0, y_hbm, ybuf.at[0], sem.at[0], 2 * tm)

    @pl.when(i + 1 < nsteps)
    def _():
        _gather_rows(pos_ref, (i + 1) * 2 * tm, y_hbm, ybuf.at[1 - slot], sem.at[1 - slot], 2 * tm)

    _wait_rows(y_hbm, ybuf.at[slot], sem.at[slot], 2 * tm)
    y_lo, y_hi = _unpack_halves(ybuf[slot])
    r = r_ref[...]
    lane = lax.broadcasted_iota(jnp.int32, r.shape, 1)
    gate1 = jnp.sum(jnp.where(lane == 2, r, 0.0), axis=1, keepdims=True)
    gate2 = jnp.sum(jnp.where(lane == 3, r, 0.0), axis=1, keepdims=True)
    half = y_lo.shape[1]
    x_lo = x_ref[:, :half] + mod_ref[5:6, :half] * (gate1 * y_lo[:tm] + gate2 * y_lo[tm:])
    x_hi = x_ref[:, half:] + mod_ref[5:6, half:] * (gate1 * y_hi[:tm] + gate2 * y_hi[tm:])
    ms = (jnp.sum(x_lo * x_lo, axis=-1, keepdims=True)
          + jnp.sum(x_hi * x_hi, axis=-1, keepdims=True)) / (2 * half)
    inv = lax.rsqrt(ms + EPS)
    o_ref[:, :half] = x_lo * inv * g_ref[:, :half]
    o_ref[:, half:] = x_hi * inv * g_ref[:, half:]


def _final(y_slots, dest, route, x1, mod, g, S):
    T, D = x1.shape
    tm = _tile(S, 128)
    per_b = S // tm
    nsteps = T // tm
    pos_tiled = jnp.transpose(dest.reshape(nsteps, tm, 2), (0, 2, 1)).reshape(-1)
    grid_spec = pltpu.PrefetchScalarGridSpec(
        num_scalar_prefetch=1,
        grid=(nsteps,),
        in_specs=[pl.BlockSpec(memory_space=pl.ANY),
                  pl.BlockSpec((tm, D), lambda i, p: (i, 0)),
                  pl.BlockSpec((tm, LANES), lambda i, p: (i, 0)),
                  pl.BlockSpec((None, 6, D), lambda i, p: (i // per_b, 0, 0)),
                  pl.BlockSpec((1, D), lambda i, p: (0, 0))],
        out_specs=pl.BlockSpec((tm, D), lambda i, p: (i, 0)),
        scratch_shapes=[pltpu.VMEM((2, 2 * tm, D // 2), y_slots.dtype), pltpu.SemaphoreType.DMA((2,))],
    )
    return pl.pallas_call(
        functools.partial(_final_kernel, tm=tm, nsteps=nsteps),
        grid_spec=grid_spec,
        out_shape=jax.ShapeDtypeStruct((T, D), F32),
        compiler_params=_cparams("arbitrary"),
        name="final",
    )(pos_tiled, y_slots, x1, route, mod, g.reshape(1, D))


def _dispatch(route, counts, n_experts, tb):
    T = route.shape[0]
    ids = route[:, 0:2].astype(jnp.int32)
    rank = route[:, 4:6].astype(jnp.int32)
    counts = counts[0, :n_experts].astype(jnp.int32)
    padded = (counts + tb - 1) // tb * tb
    pend = jnp.cumsum(padded).astype(jnp.int32)
    pstart = pend - padded
    onehot = ids[:, :, None] == jnp.arange(n_experts, dtype=jnp.int32)
    dest = (rank + jnp.sum(jnp.where(onehot, pstart, 0), axis=-1)).reshape(-1)
    n_blocks = -(-2 * T // tb) + n_experts
    first_row = jnp.arange(n_blocks, dtype=jnp.int32) * tb
    block_e = jnp.minimum(jnp.sum(pend[None, :] <= first_row[:, None], axis=1), n_experts - 1)
    nused = (pend[-1] // tb).reshape(1)
    block_e = block_e.astype(jnp.int32)
    first = jnp.concatenate([jnp.ones((1,), jnp.int32), (block_e[1:] != block_e[:-1]).astype(jnp.int32)])
    return dest, pend, block_e, first, nused, n_blocks * tb


def _layer(x2, c, lidx, B, S, w_ada, b_ada, norm1_g, w_in, conv_qk, b_if, da_lambda, da_norm_g, rel_bias,
           ml_norm_g, w_o_attn, w_o_mlstm, w_out, norm2_g, w_group, b_group, w_expert, b_expert,
           w_gate_up, w_down):
    D = x2.shape[1]
    H = rel_bias.shape[1]
    d = da_lambda.shape[1]
    da_w = H * 2 * d
    MH, dh = ml_norm_g.shape
    ml_w = MH * dh
    n_experts = w_expert.shape[1]

    mod = _ada(c, w_ada, b_ada)

    off_mlq = 3 * da_w
    off_mlv = off_mlq + 2 * ml_w
    off_if = off_mlv + 2 * ml_w
    off_g = off_if + 2 * MH
    w_in_t = w_in.T
    h, gates_if = _norm1(x2, mod, norm1_g, w_in_t[off_if:off_g], b_if, B, S)
    q_scale = jnp.concatenate([jnp.full((da_w,), d ** -0.5 * math.log2(math.e), F32),
                               jnp.ones((2 * da_w,), F32)])
    da_qkv = _proj(h, w_in_t, 0, 3 * da_w, BF16, "proj_da", col_scale=q_scale)
    ml_qk = _proj(h, w_in_t, off_mlq, 2 * ml_w, F32, "proj_mlqk")
    ml_vo = _proj(h, w_in_t, off_mlv, 2 * ml_w, BF16, "proj_mlvo")
    gates_am = _proj(h, w_in_t, off_g, 2 * D, BF16, "proj_gates")

    lambda_init = 0.8 - 0.6 * math.exp(-0.3 * lidx)
    lamp = da_lambda.astype(F32)
    lam = jnp.exp(jnp.sum(lamp[0] * lamp[1])) - jnp.exp(jnp.sum(lamp[2] * lamp[3])) + lambda_init
    attn = _attn(da_qkv, lam, rel_bias, da_norm_g, B, S, H, d, lambda_init)

    hm = _mlstm(ml_qk, conv_qk, ml_vo, gates_if, ml_norm_g, B, S, MH, dh)

    merged = _merge(attn, hm, w_o_attn, w_o_mlstm, gates_am)
    x1 = _outproj(merged, w_out, x2, mod, S)

    h2, route, counts = _router(x1, mod, norm2_g, w_group, b_group, w_expert, b_expert, B, S)
    tb = 256
    dest, pend, block_e, first, nused, n_slots = _dispatch(route, counts, n_experts, tb)
    xs = _scatter(h2, dest, pend, n_slots, tb)
    y_slots = _moe(xs, block_e, first, nused, w_gate_up, w_down, tb)
    return y_slots, dest, route, x1, mod


def kernel(x, c, w_ada, b_ada, norm1_g, w_in, conv_qk, b_if, da_lambda, da_norm_g, rel_bias, ml_norm_g,
           w_o_attn, w_o_mlstm, w_out, norm2_g, w_group, b_group, w_expert, b_expert, w_gate_up, w_down,
           normf_g):
    B, S, D = x.shape
    assert w_ada.shape[0] == 1, "the final rmsnorm is fused into the layer's last kernel: one layer only"
    l = 0
    y_slots, dest, route, x1, mod = _layer(
        x.reshape(B * S, D), c, l, B, S, w_ada[l], b_ada[l], norm1_g[l], w_in[l], conv_qk[l], b_if[l],
        da_lambda[l], da_norm_g[l], rel_bias, ml_norm_g[l], w_o_attn[l], w_o_mlstm[l], w_out[l],
        norm2_g[l], w_group[l], b_group[l], w_expert[l], b_expert[l], w_gate_up[l], w_down[l])
    return _final(y_slots, dest, route, x1, mod, normf_g, S).reshape(B, S, D)
```

```python
import functools
import math

import jax
import jax.numpy as jnp
from jax import lax
from jax.experimental import pallas as pl
from jax.experimental.pallas import tpu as pltpu

F32 = jnp.float32
BF16 = jnp.bfloat16
EPS = 1e-6
REL_MAX_DIST = 128
ML_CHUNK = 128
NEG_BIG = -1e30
V7X_VMEM_LIMIT = 60 * 1024 * 1024
LANES = 128
SUBLANES = 8
HALO = SUBLANES
DMA_UNROLL = 8
MOE_WEIGHT_SLICES = 2


def _cparams(*sem):
    return pltpu.CompilerParams(dimension_semantics=sem, vmem_limit_bytes=V7X_VMEM_LIMIT)


def _tile(n, pref):
    t = min(n, pref)
    while n % t:
        t //= 2
    return t


def _nt_dot(a, b):
    return lax.dot_general(a, b, (((1,), (1,)), ((), ())), preferred_element_type=F32)


def _tn_dot(a, b):
    return lax.dot_general(a, b, (((0,), (0,)), ((), ())), preferred_element_type=F32)


def _pack_halves(x):
    n = x.shape[1] // 2
    bits = pltpu.bitcast(x.astype(BF16).astype(F32), jnp.uint32)
    return bits[:, n:] | (bits[:, :n] >> 16)


def _unpack_halves(w):
    lo = pltpu.bitcast(w << 16, F32)
    hi = pltpu.bitcast(w & jnp.uint32(0xFFFF0000), F32)
    return lo, hi


def _ada_kernel(c_ref, w_ref, b_ref, o_ref):
    c = c_ref[...]
    s = (c * jax.nn.sigmoid(c)).astype(BF16)
    o_ref[...] = jnp.dot(s, w_ref[...].astype(BF16), preferred_element_type=F32) + b_ref[...]


def _ada(c, w_ada, b_ada):
    B, D = c.shape
    N = w_ada.shape[1]
    rows = SUBLANES * pl.cdiv(B, SUBLANES)
    c8 = jnp.zeros((rows, D), F32).at[:B].set(c)
    tn = _tile(N, 512)
    mod = pl.pallas_call(
        _ada_kernel,
        grid=(N // tn,),
        in_specs=[pl.BlockSpec((rows, D), lambda j: (0, 0)),
                  pl.BlockSpec((D, tn), lambda j: (0, j)),
                  pl.BlockSpec((1, tn), lambda j: (0, j))],
        out_specs=pl.BlockSpec((rows, tn), lambda j: (0, j)),
        out_shape=jax.ShapeDtypeStruct((rows, N), F32),
        compiler_params=_cparams("arbitrary"),
        name="ada",
    )(c8, w_ada, b_ada.reshape(1, N))
    return mod[:B].reshape(B, 6, D)


def _norm1_kernel(x_ref, mod_ref, g_ref, wif_ref, bif_ref, h_ref, gate_ref):
    x = x_ref[...]
    y = x * lax.rsqrt(jnp.mean(x * x, axis=-1, keepdims=True) + EPS) * g_ref[...]
    h = (y * (1.0 + mod_ref[1:2, :]) + mod_ref[0:1, :]).astype(BF16)
    h_ref[...] = h
    gate_ref[...] = _nt_dot(h, wif_ref[...].astype(BF16)) + bif_ref[...]


def _norm1(x2, mod, g, w_if_t, b_if, B, S):
    T, D = x2.shape
    G = w_if_t.shape[0]
    tm = _tile(S, 512)
    nb = S // tm
    return pl.pallas_call(
        _norm1_kernel,
        grid=(B, nb),
        in_specs=[pl.BlockSpec((tm, D), lambda b, i: (b * nb + i, 0)),
                  pl.BlockSpec((None, 6, D), lambda b, i: (b, 0, 0)),
                  pl.BlockSpec((1, D), lambda b, i: (0, 0)),
                  pl.BlockSpec((G, D), lambda b, i: (0, 0)),
                  pl.BlockSpec((1, G), lambda b, i: (0, 0))],
        out_specs=[pl.BlockSpec((tm, D), lambda b, i: (b * nb + i, 0)),
                   pl.BlockSpec((tm, G), lambda b, i: (b * nb + i, 0))],
        out_shape=[jax.ShapeDtypeStruct((T, D), BF16), jax.ShapeDtypeStruct((T, G), F32)],
        compiler_params=_cparams("arbitrary", "arbitrary"),
        name="norm1",
    )(x2, mod, g.reshape(1, D), w_if_t, b_if.reshape(1, G))


def _proj_kernel(h_ref, w_ref, o_ref):
    o_ref[...] = _nt_dot(h_ref[...], w_ref[...].astype(BF16)).astype(o_ref.dtype)


def _proj_scaled_kernel(h_ref, w_ref, s_ref, o_ref):
    acc = _nt_dot(h_ref[...], w_ref[...].astype(BF16))
    o_ref[...] = (acc * s_ref[...]).astype(o_ref.dtype)


def _proj(h, w_t, col0, ncols, out_dtype, name, col_scale=None):
    M, K = h.shape
    tm = _tile(M, 1024)
    tn = _tile(ncols, 512)
    assert col0 % SUBLANES == 0
    in_specs = [pl.BlockSpec((tm, K), lambda i, j: (i, 0)),
                pl.BlockSpec((pl.Element(tn), pl.Element(K)),
                             lambda i, j: (pl.multiple_of(col0 + j * tn, SUBLANES), 0))]
    args = (h, w_t)
    body = _proj_kernel
    if col_scale is not None:
        in_specs.append(pl.BlockSpec((1, tn), lambda i, j: (0, j)))
        args = (h, w_t, col_scale.reshape(1, ncols))
        body = _proj_scaled_kernel
    return pl.pallas_call(
        body,
        grid=(M // tm, ncols // tn),
        in_specs=in_specs,
        out_specs=pl.BlockSpec((tm, tn), lambda i, j: (i, j)),
        out_shape=jax.ShapeDtypeStruct((M, ncols), out_dtype),
        compiler_params=_cparams("arbitrary", "arbitrary"),
        name=name,
    )(*args)


def _attn_kernel(lam_ref, q_ref, k_ref, v_ref, bd_ref, bp_ref, g_ref, o_ref,
                 m_sc, l_sc, acc_sc, s_sc, p_sc, a_sc, *, d, t, out_scale):
    i = pl.program_id(2)
    q = q_ref[...]
    ngrp = t // LANES
    m_sc[...] = jnp.full(m_sc.shape, NEG_BIG, F32)
    l_sc[...] = jnp.zeros(l_sc.shape, F32)
    acc_sc[...] = jnp.zeros(acc_sc.shape, F32)
    spare = pl.ds(1 - lax.rem(i, 2), 1)
    p_sc[spare] = jnp.zeros((1,) + p_sc.shape[1:], BF16)
    a_sc[spare] = jnp.ones((1,) + a_sc.shape[1:], F32)

    def rows(j):
        return pl.ds(pl.multiple_of(j * t, t), t)

    def scores(j):
        k = k_ref[rows(j), :]
        for m in range(2):
            s_sc[m] = _nt_dot(q[:, m * d:(m + 1) * d], k[:, m * d:(m + 1) * d])

    def accumulate(j, buf):
        v = v_ref[rows(j), :]
        for m in range(2):
            alpha = a_sc[buf, m]
            pv = jnp.dot(p_sc[buf, m], v, preferred_element_type=F32)
            acc_sc[m] = jnp.concatenate([alpha] * (2 * d // LANES), axis=1) * acc_sc[m] + pv

    def softmax(bias_ref, buf):
        for m in range(2):
            s = s_sc[m]
            if bias_ref is not None:
                s = s + bias_ref[...]
            grp = [s[:, g * LANES:(g + 1) * LANES] for g in range(ngrp)]
            m_old = m_sc[m]
            row_max = jnp.max(functools.reduce(jnp.maximum, grp), axis=1, keepdims=True)
            m_new = jnp.maximum(m_old, row_max)
            alpha = jnp.exp2(m_old - m_new)
            p = [jnp.exp2(x - m_new) for x in grp]
            l_sc[m] = alpha * l_sc[m] + functools.reduce(jnp.add, p)
            p_sc[buf, m] = jnp.concatenate([x.astype(BF16) for x in p], axis=1)
            a_sc[buf, m] = alpha
            m_sc[m] = m_new

    def stage(j, bias_ref, buf, last=False):
        accumulate(jnp.maximum(j - 1, 0), 1 - buf)
        softmax(bias_ref, buf)
        if not last:
            scores(j + 1)

    n_far = jnp.maximum(i - 1, 0)
    odd = lax.rem(n_far, 2)

    def far_pair(jj, carry):
        j = odd + 2 * jj
        stage(j, None, 1)
        stage(j + 1, None, 0)
        return carry

    scores(0)

    @pl.when(odd == 1)
    def _():
        stage(0, None, 0)

    lax.fori_loop(0, n_far // 2, far_pair, 0)

    @pl.when(i > 0)
    def _():
        stage(i - 1, bp_ref, 1)

    stage(i, bd_ref, 0, last=True)
    accumulate(i, 0)

    l0 = jnp.sum(l_sc[0], axis=1, keepdims=True)
    l1 = jnp.sum(l_sc[1], axis=1, keepdims=True)
    a = acc_sc[0] / l0 - lam_ref[0] * (acc_sc[1] / l1)
    y = a * lax.rsqrt(jnp.mean(a * a, axis=-1, keepdims=True) + EPS) * g_ref[...]
    o_ref[...] = (y * out_scale).astype(o_ref.dtype)


def _rel_bucket(n, n_buckets):
    max_exact = n_buckets // 2
    nf = jnp.maximum(n, 1).astype(F32)
    large = max_exact + (jnp.log(nf / max_exact) / math.log(REL_MAX_DIST / max_exact)
                         * (n_buckets - max_exact)).astype(jnp.int32)
    large = jnp.minimum(large, n_buckets - 1)
    return jnp.where(n < max_exact, n, large)


def _bias_table(rel, dist, n_buckets):
    bucket = _rel_bucket(dist, n_buckets)
    out = jnp.zeros((rel.shape[1],) + dist.shape, F32)
    for b in range(n_buckets):
        out = jnp.where((bucket == b)[None], rel[b][:, None, None], out)
    return out


def _attn(qkv, lam, rel_bias, norm_g, B, S, H, d, lambda_init):
    T = qkv.shape[0]
    t = _tile(S, 512)
    assert t >= REL_MAX_DIST, "key blocks two or more tiles back must sit in the saturated bucket"
    assert d % LANES == 0
    nq = S // t
    n_buckets = rel_bias.shape[0]
    rel = rel_bias.astype(F32)
    rel = (rel - rel[n_buckets - 1]) * math.log2(math.e)
    dist_d = jnp.arange(t)[:, None] - jnp.arange(t)[None, :]
    bd = jnp.where((dist_d >= 0)[None], _bias_table(rel, jnp.maximum(dist_d, 0), n_buckets), NEG_BIG)
    bp = _bias_table(rel, dist_d + t, n_buckets)
    kern = functools.partial(_attn_kernel, d=d, t=t, out_scale=1.0 - lambda_init)
    return pl.pallas_call(
        kern,
        grid=(B, H, nq),
        in_specs=[pl.BlockSpec(memory_space=pltpu.SMEM),
                  pl.BlockSpec((t, 2 * d), lambda b, h, i: (b * nq + i, h)),
                  pl.BlockSpec((S, 2 * d), lambda b, h, i: (b, H + h)),
                  pl.BlockSpec((S, 2 * d), lambda b, h, i: (b, 2 * H + h)),
                  pl.BlockSpec((None, t, t), lambda b, h, i: (h, 0, 0)),
                  pl.BlockSpec((None, t, t), lambda b, h, i: (h, 0, 0)),
                  pl.BlockSpec((1, 2 * d), lambda b, h, i: (0, 0))],
        out_specs=pl.BlockSpec((t, 2 * d), lambda b, h, i: (b * nq + i, h)),
        out_shape=jax.ShapeDtypeStruct((T, H * 2 * d), BF16),
        scratch_shapes=[pltpu.VMEM((2, t, LANES), F32), pltpu.VMEM((2, t, LANES), F32),
                        pltpu.VMEM((2, t, 2 * d), F32), pltpu.VMEM((2, t, t), F32),
                        pltpu.VMEM((2, 2, t, t), BF16), pltpu.VMEM((2, 2, t, LANES), F32)],
        compiler_params=_cparams("arbitrary", "arbitrary", "arbitrary"),
        name="attn",
    )(lam.reshape(1), qkv, qkv, qkv, bd, bp, norm_g.reshape(1, 2 * d))


def _causal_conv_silu(u_ref, w_ref, ext_ref, scale):
    L = u_ref.shape[0]
    width = w_ref.shape[0]
    assert width - 1 <= HALO
    u = u_ref[...]
    ext_ref[HALO:, :] = u
    acc = u * w_ref[width - 1:width, :]
    for t in range(1, width):
        acc = acc + ext_ref[HALO - t:HALO - t + L, :] * w_ref[width - 1 - t:width - t, :]
    ext_ref[0:HALO, :] = u[L - HALO:, :]
    return (acc * jax.nn.sigmoid(acc) * scale).astype(BF16)


def _mlstm_kernel(q_ref, k_ref, wq_ref, wk_ref, v_ref, og_ref, gcol_ref, grow_ref, g_ref, o_ref,
                  c_sc, n_sc, m_sc, extq_sc, extk_sc, *, nheads, group, dh):
    L = q_ref.shape[0]

    @pl.when(pl.program_id(2) == 0)
    def _():
        c_sc[...] = jnp.zeros(c_sc.shape, F32)
        n_sc[...] = jnp.zeros(n_sc.shape, F32)
        m_sc[...] = jnp.zeros(m_sc.shape, F32)
        extq_sc[0:HALO, :] = jnp.zeros((HALO, extq_sc.shape[1]), F32)
        extk_sc[0:HALO, :] = jnp.zeros((HALO, extk_sc.shape[1]), F32)

    q_all = _causal_conv_silu(q_ref, wq_ref, extq_sc, 1.0)
    k_all = _causal_conv_silu(k_ref, wk_ref, extk_sc, dh ** -0.5)

    gcol = gcol_ref[...]
    lane = lax.broadcasted_iota(jnp.int32, gcol.shape, 1)
    grow = grow_ref[...]
    sub = lax.broadcasted_iota(jnp.int32, grow.shape, 0)
    jj = lax.broadcasted_iota(jnp.int32, (L, L), 0)
    ss = lax.broadcasted_iota(jnp.int32, (L, L), 1)
    tril = ss <= jj

    for hx in range(group):
        h = pl.program_id(1) * group + hx
        cols = slice(hx * dh, (hx + 1) * dh)
        i_col = jnp.sum(jnp.where(lane == h, gcol, 0.0), axis=1, keepdims=True)
        f_col = jax.nn.log_sigmoid(jnp.sum(jnp.where(lane == h + nheads, gcol, 0.0), axis=1, keepdims=True))
        i_row = jnp.sum(jnp.where(sub == h, grow, 0.0), axis=0, keepdims=True)
        f_row = jax.nn.log_sigmoid(jnp.sum(jnp.where(sub == h + nheads, grow, 0.0), axis=0, keepdims=True))

        b_col = jnp.sum(jnp.where(tril, f_row, 0.0), axis=1, keepdims=True)
        b_row = jnp.sum(jnp.where(jj <= ss, f_col, 0.0), axis=0, keepdims=True)
        u_row = i_row - b_row
        u_col = i_col - b_col

        m_prev = m_sc[hx, 0:1, 0:1]
        mm_col = jnp.maximum(m_prev, jnp.max(jnp.where(tril, u_row, NEG_BIG), axis=1, keepdims=True))
        w = jnp.exp(jnp.where(tril, u_row - mm_col, NEG_BIG))
        a_inter = jnp.exp(m_prev - mm_col)

        q = q_all[:, cols]
        k = k_all[:, cols]
        v = v_ref[:, cols]
        c_old = c_sc[hx]
        n_old = n_sc[hx]
        s_qk = _nt_dot(q, k) * w
        num = (a_inter * _nt_dot(q, c_old.astype(BF16))
               + jnp.dot(s_qk.astype(BF16), v, preferred_element_type=F32))
        den = (a_inter * jnp.sum(q.astype(F32) * n_old, axis=1, keepdims=True)
               + jnp.sum(s_qk, axis=1, keepdims=True))
        hh = num / jnp.maximum(jnp.abs(den), jnp.exp(-(b_col + mm_col)))

        mm_last = mm_col[L - 1:L, :]
        w_s = jnp.exp(u_col - mm_last)
        a_state = jnp.exp(m_prev - mm_last)
        c_sc[hx] = a_state * c_old + _tn_dot((v.astype(F32) * w_s).astype(BF16), k)
        n_sc[hx] = a_state * n_old + jnp.sum(k.astype(F32) * w_s, axis=0, keepdims=True)
        m_sc[hx] = jnp.broadcast_to(b_col[L - 1:L, :] + mm_last, m_sc.shape[1:])

        y = hh * lax.rsqrt(jnp.mean(hh * hh, axis=-1, keepdims=True) + EPS) * g_ref[hx]
        o_ref[:, cols] = (y * jax.nn.sigmoid(og_ref[:, cols].astype(F32))).astype(o_ref.dtype)


def _mlstm(mqk, conv_w, mvo, gates, norm_g, B, S, H, dh):
    T = mqk.shape[0]
    width = conv_w.shape[0]
    L = ML_CHUNK
    nc = S // L
    group = _tile(H, 4)
    ng = H // group
    gw = group * dh
    grow = jnp.transpose(gates.reshape(B, S, 2 * H), (0, 2, 1))
    row = lambda b, h, c: (b * nc + c, h)
    row_hi = lambda b, h, c: (b * nc + c, ng + h)
    return pl.pallas_call(
        functools.partial(_mlstm_kernel, nheads=H, group=group, dh=dh),
        grid=(B, ng, nc),
        in_specs=[pl.BlockSpec((L, gw), row),
                  pl.BlockSpec((L, gw), row_hi),
                  pl.BlockSpec((width, gw), lambda b, h, c: (0, h)),
                  pl.BlockSpec((width, gw), lambda b, h, c: (0, ng + h)),
                  pl.BlockSpec((L, gw), row),
                  pl.BlockSpec((L, gw), row_hi),
                  pl.BlockSpec((L, 2 * H), lambda b, h, c: (b * nc + c, 0)),
                  pl.BlockSpec((None, 2 * H, L), lambda b, h, c: (b, 0, c)),
                  pl.BlockSpec((group, 1, dh), lambda b, h, c: (h, 0, 0))],
        out_specs=pl.BlockSpec((L, gw), row),
        out_shape=jax.ShapeDtypeStruct((T, H * dh), BF16),
        scratch_shapes=[pltpu.VMEM((group, dh, dh), F32), pltpu.VMEM((group, 1, dh), F32),
                        pltpu.VMEM((group, SUBLANES, LANES), F32), pltpu.VMEM((L + HALO, gw), F32),
                        pltpu.VMEM((L + HALO, gw), F32)],
        compiler_params=_cparams("arbitrary", "arbitrary", "arbitrary"),
        name="mlstm",
    )(mqk, mqk, conv_w, conv_w, mvo, mvo, gates, grow, norm_g.reshape(H, 1, dh))


def _merge_kernel(a_ref, m_ref, wa_ref, wm_ref, ga_ref, gm_ref, o_ref):
    pa = jnp.dot(a_ref[...], wa_ref[...].astype(BF16), preferred_element_type=F32)
    pm = jnp.dot(m_ref[...], wm_ref[...].astype(BF16), preferred_element_type=F32)
    out = (jax.nn.sigmoid(ga_ref[...].astype(F32)) * pa + jax.nn.sigmoid(gm_ref[...].astype(F32)) * pm)
    o_ref[...] = out.astype(o_ref.dtype)


def _merge(attn, hm, w_a, w_m, gates_am):
    T, Ka = attn.shape
    Km = hm.shape[1]
    D = w_a.shape[1]
    tm = _tile(T, 1024)
    tn = _tile(D, 512)
    nj = D // tn
    return pl.pallas_call(
        _merge_kernel,
        grid=(T // tm, nj),
        in_specs=[pl.BlockSpec((tm, Ka), lambda i, j: (i, 0)),
                  pl.BlockSpec((tm, Km), lambda i, j: (i, 0)),
                  pl.BlockSpec((Ka, tn), lambda i, j: (0, j)),
                  pl.BlockSpec((Km, tn), lambda i, j: (0, j)),
                  pl.BlockSpec((tm, tn), lambda i, j: (i, j)),
                  pl.BlockSpec((tm, tn), lambda i, j: (i, nj + j))],
        out_specs=pl.BlockSpec((tm, tn), lambda i, j: (i, j)),
        out_shape=jax.ShapeDtypeStruct((T, D), BF16),
        compiler_params=_cparams("arbitrary", "arbitrary"),
        name="merge",
    )(attn, hm, w_a, w_m, gates_am, gates_am)


def _out_kernel(a_ref, w_ref, x_ref, mod_ref, o_ref):
    p = jnp.dot(a_ref[...], w_ref[...].astype(BF16), preferred_element_type=F32)
    o_ref[...] = x_ref[...] + mod_ref[2:3, :] * p


def _outproj(merged, w_out, x2, mod, S):
    T, K = merged.shape
    D = w_out.shape[1]
    tm = _tile(S, 1024)
    tn = _tile(D, 512)
    per_b = S // tm
    return pl.pallas_call(
        _out_kernel,
        grid=(T // tm, D // tn),
        in_specs=[pl.BlockSpec((tm, K), lambda i, j: (i, 0)),
                  pl.BlockSpec((K, tn), lambda i, j: (0, j)),
                  pl.BlockSpec((tm, tn), lambda i, j: (i, j)),
                  pl.BlockSpec((None, 6, tn), lambda i, j: (i // per_b, 0, j))],
        out_specs=pl.BlockSpec((tm, tn), lambda i, j: (i, j)),
        out_shape=jax.ShapeDtypeStruct((T, D), F32),
        compiler_params=_cparams("arbitrary", "arbitrary"),
        name="outproj",
    )(merged, w_out, x2, mod)


def _router_kernel(x_ref, mod_ref, g_ref, whi_ref, wlo_ref, b_ref, h_ref, r_ref, cnt_ref, run_sc,
                   *, n_groups, per_group):
    @pl.when(jnp.logical_and(pl.program_id(0) == 0, pl.program_id(1) == 0))
    def _():
        run_sc[...] = jnp.zeros(run_sc.shape, F32)

    x = x_ref[...]
    y = x * lax.rsqrt(jnp.mean(x * x, axis=-1, keepdims=True) + EPS) * g_ref[...]
    h2 = y * (1.0 + mod_ref[4:5, :]) + mod_ref[3:4, :]
    h_ref[...] = _pack_halves(h2)
    hi = h2.astype(BF16)
    lo = (h2 - hi.astype(F32)).astype(BF16)
    logits = (jnp.dot(hi, whi_ref[...], preferred_element_type=F32)
              + jnp.dot(hi, wlo_ref[...], preferred_element_type=F32)
              + jnp.dot(lo, whi_ref[...], preferred_element_type=F32)) + b_ref[...]
    lane = lax.broadcasted_iota(jnp.int32, logits.shape, 1)
    big = jnp.int32(1 << 20)

    def top(vals):
        mx = jnp.max(vals, axis=1, keepdims=True)
        idx = jnp.min(jnp.where(vals == mx, lane, big), axis=1, keepdims=True)
        return mx, idx

    gl = jnp.where(lane < n_groups, logits, -jnp.inf)
    gmax, gidx = top(gl)
    g_p = 1.0 / jnp.sum(jnp.exp(gl - gmax), axis=1, keepdims=True)
    lo_lane = n_groups + per_group * gidx
    el = jnp.where((lane >= lo_lane) & (lane < lo_lane + per_group), logits, -jnp.inf)
    e1, i1 = top(el)
    e2, i2 = top(jnp.where(lane == i1, -jnp.inf, el))
    r = jnp.exp(e2 - e1)
    gate1 = g_p / (1.0 + r)
    gate2 = g_p * r / (1.0 + r)
    ex1 = i1 - n_groups
    ex2 = i2 - n_groups
    oh1 = (lane == ex1).astype(F32)
    oh2 = (lane == ex2).astype(F32)
    both = oh1 + oh2
    tm = x.shape[0]
    earlier = (lax.broadcasted_iota(jnp.int32, (tm, tm), 1)
               < lax.broadcasted_iota(jnp.int32, (tm, tm), 0)).astype(BF16)
    prefix = jnp.dot(earlier, both.astype(BF16), preferred_element_type=F32) + run_sc[0:1, :]
    rank1 = jnp.sum(oh1 * prefix, axis=1, keepdims=True)
    rank2 = jnp.sum(oh2 * prefix, axis=1, keepdims=True)
    run_sc[...] = run_sc[...] + jnp.sum(both, axis=0, keepdims=True)
    cnt_ref[...] = run_sc[...]

    cols = (ex1.astype(F32), ex2.astype(F32), gate1, gate2, rank1, rank2)
    out = jnp.zeros(logits.shape, F32)
    for n, col in enumerate(cols):
        out = jnp.where(lane == n, col, out)
    r_ref[...] = out


def _router(x1, mod, g, w_group, b_group, w_expert, b_expert, B, S):
    T, D = x1.shape
    NG = w_group.shape[1]
    NE = w_expert.shape[1]
    assert NG + NE <= LANES
    pad = LANES - NG - NE
    w = jnp.concatenate([w_group, w_expert, jnp.zeros((D, pad), F32)], axis=1)
    w_hi = w.astype(BF16)
    w_lo = (w - w_hi.astype(F32)).astype(BF16)
    bias = jnp.concatenate([b_group, b_expert, jnp.full((pad,), -jnp.inf, F32)]).reshape(1, LANES)
    tm = _tile(S, 256)
    nb = S // tm
    kern = functools.partial(_router_kernel, n_groups=NG, per_group=NE // NG)
    return pl.pallas_call(
        kern,
        grid=(B, nb),
        in_specs=[pl.BlockSpec((tm, D), lambda b, i: (b * nb + i, 0)),
                  pl.BlockSpec((None, 6, D), lambda b, i: (b, 0, 0)),
                  pl.BlockSpec((1, D), lambda b, i: (0, 0)),
                  pl.BlockSpec((D, LANES), lambda b, i: (0, 0)),
                  pl.BlockSpec((D, LANES), lambda b, i: (0, 0)),
                  pl.BlockSpec((1, LANES), lambda b, i: (0, 0))],
        out_specs=[pl.BlockSpec((tm, D // 2), lambda b, i: (b * nb + i, 0)),
                   pl.BlockSpec((tm, LANES), lambda b, i: (b * nb + i, 0)),
                   pl.BlockSpec((SUBLANES, LANES), lambda b, i: (0, 0))],
        out_shape=[jax.ShapeDtypeStruct((T, D // 2), jnp.uint32), jax.ShapeDtypeStruct((T, LANES), F32),
                   jax.ShapeDtypeStruct((SUBLANES, LANES), F32)],
        scratch_shapes=[pltpu.VMEM((SUBLANES, LANES), F32)],
        compiler_params=_cparams("arbitrary", "arbitrary"),
        name="router",
    )(x1, mod, g.reshape(1, D), w_hi, w_lo, bias)


def _gather_rows(idx_ref, base, src_hbm, dst_ref, sem, n):
    def body(p, carry):
        for prio in range(2):
            r = 2 * p + prio
            tok = idx_ref[base + r]
            pltpu.make_async_copy(src_hbm.at[pl.ds(tok, 1), :], dst_ref.at[pl.ds(r, 1), :],
                                  sem).start(priority=prio)
        return carry
    assert n % 2 == 0
    lax.fori_loop(0, n // 2, body, 0, unroll=DMA_UNROLL // 2)


def _wait_rows(src_hbm, dst_ref, sem, n):
    def body(r, carry):
        pltpu.make_async_copy(src_hbm.at[pl.ds(0, 1), :], dst_ref.at[pl.ds(r, 1), :], sem).wait()
        return carry
    lax.fori_loop(0, n, body, 0, unroll=DMA_UNROLL)


def _scatter_kernel(dest_ref, pend_ref, h_ref, xs_hbm, zbuf, sem, zsem, *, tm, tb, n_experts, n_blocks):
    i = pl.program_id(0)

    def zero_block(row0):
        return pltpu.make_async_copy(zbuf, xs_hbm.at[pl.ds(pl.multiple_of(row0, tb), tb), :], zsem)

    def zero_copy(e):
        return zero_block(jnp.maximum(pend_ref[e] - tb, 0))

    @pl.when(i == 0)
    def _():
        zbuf[...] = jnp.zeros(zbuf.shape, zbuf.dtype)

        def start(e, carry):
            @pl.when(pend_ref[e] > 0)
            def _():
                zero_copy(e).start()
            return carry

        def wait(e, carry):
            @pl.when(pend_ref[e] > 0)
            def _():
                zero_copy(e).wait()
            return carry

        def start_tail(b, carry):
            zero_block(b * tb).start()
            return carry

        def wait_tail(b, carry):
            zero_block(b * tb).wait()
            return carry

        first_unused = pend_ref[n_experts - 1] // tb
        lax.fori_loop(0, n_experts, start, 0)
        lax.fori_loop(first_unused, n_blocks, start_tail, 0)
        lax.fori_loop(0, n_experts, wait, 0)
        lax.fori_loop(first_unused, n_blocks, wait_tail, 0)

    def row_copy(r, choice):
        slot = dest_ref[2 * (i * tm + r) + choice]
        return pltpu.make_async_copy(h_ref.at[pl.ds(r, 1), :], xs_hbm.at[pl.ds(slot, 1), :], sem)

    def start_row(r, carry):
        row_copy(r, 0).start(priority=0)
        row_copy(r, 1).start(priority=1)
        return carry

    def wait_row(r, carry):
        row_copy(r, 0).wait()
        row_copy(r, 1).wait()
        return carry

    lax.fori_loop(0, tm, start_row, 0, unroll=DMA_UNROLL)
    lax.fori_loop(0, tm, wait_row, 0, unroll=DMA_UNROLL)


def _scatter(h2, dest, pend, n_slots, tb):
    T, D = h2.shape
    tm = _tile(T, 256)
    grid_spec = pltpu.PrefetchScalarGridSpec(
        num_scalar_prefetch=2,
        grid=(T // tm,),
        in_specs=[pl.BlockSpec((tm, D), lambda i, de, pe: (i, 0))],
        out_specs=pl.BlockSpec(memory_space=pl.ANY),
        scratch_shapes=[pltpu.VMEM((tb, D), h2.dtype), pltpu.SemaphoreType.DMA(()),
                        pltpu.SemaphoreType.DMA(())],
    )
    return pl.pallas_call(
        functools.partial(_scatter_kernel, tm=tm, tb=tb, n_experts=pend.shape[0],
                          n_blocks=n_slots // tb),
        grid_spec=grid_spec,
        out_shape=jax.ShapeDtypeStruct((n_slots, D), h2.dtype),
        compiler_params=_cparams("arbitrary"),
        name="scatter",
    )(dest, pend, h2)


def _moe_kernel(be_ref, first_ref, nused_ref, x_ref, wg_ref, wu_ref, wd_ref, y_ref,
                cg_sc, cu_sc, cd_sc, acc_sc, *, nk):
    i = pl.program_id(0)
    k = pl.program_id(1)
    nused = nused_ref[0]
    live = i < nused
    first = first_ref[jnp.minimum(i, nused - 1)] == 1
    half = x_ref.shape[1]
    dk = wd_ref.shape[0]

    def x_halves():
        x_lo, x_hi = _unpack_halves(x_ref[...])
        return x_lo.astype(BF16), x_hi.astype(BF16)

    def expert(cols):
        x_lo, x_hi = x_halves()
        g = (jnp.dot(x_lo, cg_sc[:half, cols], preferred_element_type=F32)
             + jnp.dot(x_hi, cg_sc[half:, cols], preferred_element_type=F32))
        u = (jnp.dot(x_lo, cu_sc[:half, cols], preferred_element_type=F32)
             + jnp.dot(x_hi, cu_sc[half:, cols], preferred_element_type=F32))
        act = (g * jax.nn.sigmoid(g) * u).astype(BF16)
        return jnp.dot(act, cd_sc[cols, :], preferred_element_type=F32)

    for s in range(nk):
        @pl.when(jnp.logical_and(jnp.logical_and(live, first), k == s))
        def _():
            cols = slice(s * dk, (s + 1) * dk)
            cg_sc[:, cols] = wg_ref[...].astype(BF16)
            cu_sc[:, cols] = wu_ref[...].astype(BF16)
            cd_sc[cols, :] = wd_ref[...].astype(BF16)
            part = expert(cols)
            if s == 0:
                acc_sc[...] = part
            elif s < nk - 1:
                acc_sc[...] += part
            else:
                y_ref[...] = _pack_halves(acc_sc[...] + part)

    @pl.when(jnp.logical_and(live, jnp.logical_and(jnp.logical_not(first), k == nk - 1)))
    def _():
        y_ref[...] = _pack_halves(expert(slice(None)))

    @pl.when(jnp.logical_and(jnp.logical_not(live), k == 0))
    def _():
        y_ref[...] = jnp.zeros(y_ref.shape, y_ref.dtype)


def _moe(xs, block_e, first, nused, wgu, wd, tb):
    n_slots, hw = xs.shape
    E, D, de2 = wgu.shape
    de = de2 // 2
    nk = MOE_WEIGHT_SLICES
    dk = de // nk
    n_blocks = n_slots // tb

    def blk(i, nu):
        return jnp.minimum(i, nu[0] - 1)

    def piece(i, k, fi, nu):
        streaming = jnp.logical_and(i < nu[0], fi[blk(i, nu)] == 1)
        return jnp.where(streaming, k, nk - 1)

    grid_spec = pltpu.PrefetchScalarGridSpec(
        num_scalar_prefetch=3,
        grid=(n_blocks, nk),
        in_specs=[pl.BlockSpec((tb, hw), lambda i, k, be, fi, nu: (blk(i, nu), 0)),
                  pl.BlockSpec((None, D, dk), lambda i, k, be, fi, nu: (be[blk(i, nu)], 0, piece(i, k, fi, nu))),
                  pl.BlockSpec((None, D, dk),
                               lambda i, k, be, fi, nu: (be[blk(i, nu)], 0, nk + piece(i, k, fi, nu))),
                  pl.BlockSpec((None, dk, D), lambda i, k, be, fi, nu: (be[blk(i, nu)], piece(i, k, fi, nu), 0))],
        out_specs=pl.BlockSpec((tb, hw), lambda i, k, be, fi, nu: (i, 0)),
        scratch_shapes=[pltpu.VMEM((D, de), BF16), pltpu.VMEM((D, de), BF16), pltpu.VMEM((de, D), BF16),
                        pltpu.VMEM((tb, D), F32)],
    )
    return pl.pallas_call(
        functools.partial(_moe_kernel, nk=nk),
        grid_spec=grid_spec,
        out_shape=jax.ShapeDtypeStruct((n_slots, hw), jnp.uint32),
        compiler_params=_cparams("arbitrary", "arbitrary"),
        name="moe",
    )(block_e, first, nused, xs, wgu, wgu, wd)


def _final_kernel(pos_ref, y_hbm, x_ref, r_ref, mod_ref, g_ref, o_ref, ybuf, sem, *, tm, nsteps):
    i = pl.program_id(0)
    slot = lax.rem(i, 2)

    @pl.when(i == 0)
    def _():
        _gather_rows(pos_ref, 0, y_hbm, ybuf.at[0], sem.at[0], 2 * tm)

    @pl.when(i + 1 < nsteps)
    def _():
        _gather_rows(pos_ref, (i + 1) * 2 * tm, y_hbm, ybuf.at[1 - slot], sem.at[1 - slot], 2 * tm)

    _wait_rows(y_hbm, ybuf.at[slot], sem.at[slot], 2 * tm)
    y_lo, y_hi = _unpack_halves(ybuf[slot])
    r = r_ref[...]
    lane = lax.broadcasted_iota(jnp.int32, r.shape, 1)
    gate1 = jnp.sum(jnp.where(lane == 2, r, 0.0), axis=1, keepdims=True)
    gate2 = jnp.sum(jnp.where(lane == 3, r, 0.0), axis=1, keepdims=True)
    half = y_lo.shape[1]
    x_lo = x_ref[:, :half] + mod_ref[5:6, :half] * (gate1 * y_lo[:tm] + gate2 * y_lo[tm:])
    x_hi = x_ref[:, half:] + mod_ref[5:6, half:] * (gate1 * y_hi[:tm] + gate2 * y_hi[tm:])
    ms = (jnp.sum(x_lo * x_lo, axis=-1, keepdims=True)
          + jnp.sum(x_hi * x_hi, axis=-1, keepdims=True)) / (2 * half)
    inv = lax.rsqrt(ms + EPS)
    o_ref[:, :half] = x_lo * inv * g_ref[:, :half]
    o_ref[:, half:] = x_hi * inv * g_ref[:, half:]


def _final(y_slots, dest, route, x1, mod, g, S):
    T, D = x1.shape
    tm = _tile(S, 128)
    per_b = S // tm
    nsteps = T // tm
    pos_tiled = jnp.transpose(dest.reshape(nsteps, tm, 2), (0, 2, 1)).reshape(-1)
    grid_spec = pltpu.PrefetchScalarGridSpec(
        num_scalar_prefetch=1,
        grid=(nsteps,),
        in_specs=[pl.BlockSpec(memory_space=pl.ANY),
                  pl.BlockSpec((tm, D), lambda i, p: (i, 0)),
                  pl.BlockSpec((tm, LANES), lambda i, p: (i, 0)),
                  pl.BlockSpec((None, 6, D), lambda i, p: (i // per_b, 0, 0)),
                  pl.BlockSpec((1, D), lambda i, p: (0, 0))],
        out_specs=pl.BlockSpec((tm, D), lambda i, p: (i, 0)),
        scratch_shapes=[pltpu.VMEM((2, 2 * tm, D // 2), y_slots.dtype), pltpu.SemaphoreType.DMA((2,))],
    )
    return pl.pallas_call(
        functools.partial(_final_kernel, tm=tm, nsteps=nsteps),
        grid_spec=grid_spec,
        out_shape=jax.ShapeDtypeStruct((T, D), F32),
        compiler_params=_cparams("arbitrary"),
        name="final",
    )(pos_tiled, y_slots, x1, route, mod, g.reshape(1, D))


def _dispatch(route, counts, n_experts, tb):
    T = route.shape[0]
    ids = route[:, 0:2].astype(jnp.int32)
    rank = route[:, 4:6].astype(jnp.int32)
    counts = counts[0, :n_experts].astype(jnp.int32)
    padded = (counts + tb - 1) // tb * tb
    pend = jnp.cumsum(padded).astype(jnp.int32)
    pstart = pend - padded
    onehot = ids[:, :, None] == jnp.arange(n_experts, dtype=jnp.int32)
    dest = (rank + jnp.sum(jnp.where(onehot, pstart, 0), axis=-1)).reshape(-1)
    n_blocks = -(-2 * T // tb) + n_experts
    first_row = jnp.arange(n_blocks, dtype=jnp.int32) * tb
    block_e = jnp.minimum(jnp.sum(pend[None, :] <= first_row[:, None], axis=1), n_experts - 1)
    nused = (pend[-1] // tb).reshape(1)
    block_e = block_e.astype(jnp.int32)
    first = jnp.concatenate([jnp.ones((1,), jnp.int32), (block_e[1:] != block_e[:-1]).astype(jnp.int32)])
    return dest, pend, block_e, first, nused, n_blocks * tb


def _layer(x2, c, lidx, B, S, w_ada, b_ada, norm1_g, w_in, conv_qk, b_if, da_lambda, da_norm_g, rel_bias,
           ml_norm_g, w_o_attn, w_o_mlstm, w_out, norm2_g, w_group, b_group, w_expert, b_expert,
           w_gate_up, w_down):
    D = x2.shape[1]
    H = rel_bias.shape[1]
    d = da_lambda.shape[1]
    da_w = H * 2 * d
    MH, dh = ml_norm_g.shape
    ml_w = MH * dh
    n_experts = w_expert.shape[1]

    mod = _ada(c, w_ada, b_ada)

    off_mlq = 3 * da_w
    off_mlv = off_mlq + 2 * ml_w
    off_if = off_mlv + 2 * ml_w
    off_g = off_if + 2 * MH
    w_in_t = w_in.T
    h, gates_if = _norm1(x2, mod, norm1_g, w_in_t[off_if:off_g], b_if, B, S)
    q_scale = jnp.concatenate([jnp.full((da_w,), d ** -0.5 * math.log2(math.e), F32),
                               jnp.ones((2 * da_w,), F32)])
    da_qkv = _proj(h, w_in_t, 0, 3 * da_w, BF16, "proj_da", col_scale=q_scale)
    ml_qk = _proj(h, w_in_t, off_mlq, 2 * ml_w, F32, "proj_mlqk")
    ml_vo = _proj(h, w_in_t, off_mlv, 2 * ml_w, BF16, "proj_mlvo")
    gates_am = _proj(h, w_in_t, off_g, 2 * D, BF16, "proj_gates")

    lambda_init = 0.8 - 0.6 * math.exp(-0.3 * lidx)
    lamp = da_lambda.astype(F32)
    lam = jnp.exp(jnp.sum(lamp[0] * lamp[1])) - jnp.exp(jnp.sum(lamp[2] * lamp[3])) + lambda_init
    attn = _attn(da_qkv, lam, rel_bias, da_norm_g, B, S, H, d, lambda_init)

    hm = _mlstm(ml_qk, conv_qk, ml_vo, gates_if, ml_norm_g, B, S, MH, dh)

    merged = _merge(attn, hm, w_o_attn, w_o_mlstm, gates_am)
    x1 = _outproj(merged, w_out, x2, mod, S)

    h2, route, counts = _router(x1, mod, norm2_g, w_group, b_group, w_expert, b_expert, B, S)
    tb = 256
    dest, pend, block_e, first, nused, n_slots = _dispatch(route, counts, n_experts, tb)
    xs = _scatter(h2, dest, pend, n_slots, tb)
    y_slots = _moe(xs, block_e, first, nused, w_gate_up, w_down, tb)
    return y_slots, dest, route, x1, mod


def kernel(x, c, w_ada, b_ada, norm1_g, w_in, conv_qk, b_if, da_lambda, da_norm_g, rel_bias, ml_norm_g,
           w_o_attn, w_o_mlstm, w_out, norm2_g, w_group, b_group, w_expert, b_expert, w_gate_up, w_down,
           normf_g):
    B, S, D = x.shape
    assert w_ada.shape[0] == 1, "the final rmsnorm is fused into the layer's last kernel: one layer only"
    l = 0
    y_slots, dest, route, x1, mod = _layer(
        x.reshape(B * S, D), c, l, B, S, w_ada[l], b_ada[l], norm1_g[l], w_in[l], conv_qk[l], b_if[l],
        da_lambda[l], da_norm_g[l], rel_bias, ml_norm_g[l], w_o_attn[l], w_o_mlstm[l], w_out[l],
        norm2_g[l], w_group[l], b_group[l], w_expert[l], b_expert[l], w_gate_up[l], w_down[l])
    return _final(y_slots, dest, route, x1, mod, normf_g, S).reshape(B, S, D)
```
